```python
import jax, jax.numpy as jnp
from jax import lax
import numpy as np

D_MODEL = 1024
BATCH = 8
SEQ = 4096
DEPTH = 4

N_EVEN = (DEPTH + 1) // 2
N_ODD = DEPTH // 2
EPS = 1e-6
ROPE_THETA = 10000.0
ROPE_DIM = 64
Q_BLOCK = 128

MLA_HEADS = 8
MLA_Q_LORA = 384
MLA_KV_LORA = 256
MLA_NOPE = 128
MLA_ROPE = ROPE_DIM
MLA_V = 128

SWA_Q_HEADS = 16
SWA_KV_HEADS = 2
SWA_HEAD_DIM = ROPE_DIM
SWA_RADIUS = 128

DIL_PATTERN = ((128, 1), (512, 4), (2048, 16))
DIL_GROUPS = len(DIL_PATTERN)
DIL_HEADS = 16
DIL_HEAD_DIM = ROPE_DIM

N_EXPERTS = 32
TOP_K = 4
D_FF = D_MODEL
SWIGLU_LIMIT = 7.0
SWIGLU_ALPHA = 1.702
MOE_BLOCK = 256

AB_SPLITS = (MLA_Q_LORA, MLA_KV_LORA, MLA_ROPE, SWA_Q_HEADS * SWA_HEAD_DIM,
             SWA_KV_HEADS * SWA_HEAD_DIM, SWA_KV_HEADS * SWA_HEAD_DIM)
AB_IN = sum(AB_SPLITS)
AB_MIX = MLA_HEADS * MLA_V + SWA_Q_HEADS * SWA_HEAD_DIM
C_IN = DIL_GROUPS * 3 * DIL_HEADS * DIL_HEAD_DIM
C_MIX = DIL_HEADS * DIL_HEAD_DIM

kernel_name = "hybrid_mla_swa_dilated_moe_encoder"


def rmsnorm(x, gain):
    xf = x.astype(jnp.float32)
    y = xf * lax.rsqrt(jnp.mean(xf * xf, axis=-1, keepdims=True) + EPS)
    return (y * gain.astype(jnp.float32)).astype(x.dtype)


def ada_norm(x, gain, shift, scale):
    h = rmsnorm(x, gain).astype(jnp.float32)
    return (h * (1.0 + scale[:, None, :]) + shift[:, None, :]).astype(x.dtype)


def rope_tables(positions):
    inv = ROPE_THETA ** (-jnp.arange(0, ROPE_DIM, 2, dtype=jnp.float32) / ROPE_DIM)
    ang = positions.astype(jnp.float32)[..., None] * inv
    return jnp.cos(ang), jnp.sin(ang)


def apply_rope(x, cos, sin):
    xf = x.astype(jnp.float32)
    x1, x2 = jnp.split(xf, 2, axis=-1)
    c = cos[:, :, None, :]
    s = sin[:, :, None, :]
    return jnp.concatenate([x1 * c - x2 * s, x2 * c + x1 * s], axis=-1).astype(x.dtype)


def dense_attention(q, k, v):
    B, S, H, dq = q.shape
    scale = dq ** -0.5
    qb = q.reshape(B, S // Q_BLOCK, Q_BLOCK, H, dq).swapaxes(0, 1)

    def one_block(qblk):
        s = jnp.einsum('bqhd,bkhd->bhqk', qblk, k, preferred_element_type=jnp.float32) * scale
        p = jax.nn.softmax(s, axis=-1).astype(v.dtype)
        return jnp.einsum('bhqk,bkhd->bqhd', p, v, preferred_element_type=jnp.float32).astype(q.dtype)

    o = lax.map(one_block, qb)
    return o.swapaxes(0, 1).reshape(B, S, H, v.shape[-1])


def banded_attention(q, k, v, radius, sink=None, return_lse=False):
    N, L, Hq, hd = q.shape
    Hkv = k.shape[2]
    G = Hq // Hkv
    blk = radius
    nb = -(-L // blk)
    Lp = nb * blk
    pad = Lp - L
    qb = jnp.pad(q, ((0, 0), (0, pad), (0, 0), (0, 0))).reshape(N, nb, blk, Hkv, G, hd)

    def windows(t):
        tp = jnp.pad(t, ((0, 0), (blk, pad + blk), (0, 0), (0, 0))).reshape(N, nb + 2, blk, Hkv, hd)
        return jnp.concatenate([tp[:, :-2], tp[:, 1:-1], tp[:, 2:]], axis=2)

    kw, vw = windows(k), windows(v)
    s = jnp.einsum('nbqhgd,nbjhd->nbhgqj', qb, kw, preferred_element_type=jnp.float32) * (hd ** -0.5)
    qpos = jnp.arange(nb)[:, None] * blk + jnp.arange(blk)[None, :]
    kpos = (jnp.arange(nb)[:, None] - 1) * blk + jnp.arange(3 * blk)[None, :]
    valid = ((jnp.abs(qpos[:, :, None] - kpos[:, None, :]) <= radius)
             & (kpos[:, None, :] >= 0) & (kpos[:, None, :] < L))
    s = jnp.where(valid[None, :, None, None], s, -jnp.inf)
    m = jnp.max(s, axis=-1, keepdims=True)
    if sink is not None:
        sk = sink.astype(jnp.float32).reshape(Hkv, G)[None, None, :, :, None, None]
        m = jnp.maximum(m, sk)
        p = jnp.exp(s - m)
        den = jnp.sum(p, axis=-1, keepdims=True) + jnp.exp(sk - m)
    else:
        p = jnp.exp(s - m)
        den = jnp.sum(p, axis=-1, keepdims=True)
    o = jnp.einsum('nbhgqj,nbjhd->nbqhgd', (p / den).astype(v.dtype), vw,
                   preferred_element_type=jnp.float32)
    o = o.reshape(N, Lp, Hq, hd)[:, :L].astype(q.dtype)
    if return_lse:
        lse = (m + jnp.log(den))[..., 0]
        lse = lse.transpose(0, 1, 4, 2, 3).reshape(N, Lp, Hq)[:, :L]
        return o, lse
    return o


def to_strided(t, dil):
    B, S = t.shape[:2]
    t = t.reshape((B, S // dil, dil) + t.shape[2:])
    t = jnp.moveaxis(t, 2, 1)
    return t.reshape((B * dil, S // dil) + t.shape[3:])


def from_strided(t, dil, B):
    L = t.shape[1]
    t = t.reshape((B, dil, L) + t.shape[2:])
    t = jnp.moveaxis(t, 1, 2)
    return t.reshape((B, L * dil) + t.shape[3:])


def mixer_mla_swa(h, cos, sin, w_in, g_q, w_qb, g_kv, w_kvb, sink, w_out):
    B, S, _ = h.shape
    proj = h @ w_in
    cq, ckv, kr, qs, ks, vs = jnp.split(proj, [int(i) for i in np.cumsum(AB_SPLITS)[:-1]], axis=-1)
    qa = (rmsnorm(cq, g_q) @ w_qb).reshape(B, S, MLA_HEADS, MLA_NOPE + MLA_ROPE)
    q_rope = apply_rope(qa[..., MLA_NOPE:], cos, sin)
    kv = (rmsnorm(ckv, g_kv) @ w_kvb).reshape(B, S, MLA_HEADS, MLA_NOPE + MLA_V)
    k_nope, v_a = kv[..., :MLA_NOPE], kv[..., MLA_NOPE:]
    k_rope = apply_rope(kr.reshape(B, S, 1, MLA_ROPE), cos, sin)
    q_a = jnp.concatenate([qa[..., :MLA_NOPE], q_rope], axis=-1)
    k_a = jnp.concatenate([k_nope, jnp.broadcast_to(k_rope, (B, S, MLA_HEADS, MLA_ROPE))], axis=-1)
    o_a = dense_attention(q_a, k_a, v_a)
    q_b = apply_rope(qs.reshape(B, S, SWA_Q_HEADS, SWA_HEAD_DIM), cos, sin)
    k_b = apply_rope(ks.reshape(B, S, SWA_KV_HEADS, SWA_HEAD_DIM), cos, sin)
    v_b = vs.reshape(B, S, SWA_KV_HEADS, SWA_HEAD_DIM)
    o_b = banded_attention(q_b, k_b, v_b, SWA_RADIUS, sink=sink)
    o = jnp.concatenate([o_a.reshape(B, S, -1), o_b.reshape(B, S, -1)], axis=-1)
    return o @ w_out


def mixer_dilated(h, cos, sin, w_in, w_out):
    B, S, _ = h.shape
    proj = (h @ w_in).reshape(B, S, DIL_GROUPS, 3, DIL_HEADS, DIL_HEAD_DIM)
    outs, lses = [], []
    for g, (window, dil) in enumerate(DIL_PATTERN):
        q = apply_rope(proj[:, :, g, 0], cos, sin)
        k = apply_rope(proj[:, :, g, 1], cos, sin)
        v = proj[:, :, g, 2]
        radius = window // (2 * dil)
        o, lse = banded_attention(to_strided(q, dil), to_strided(k, dil), to_strided(v, dil),
                                  radius, return_lse=True)
        outs.append(from_strided(o, dil, B))
        lses.append(from_strided(lse, dil, B))
    wts = jax.nn.softmax(jnp.stack(lses, axis=0), axis=0)
    o = jnp.sum(wts[..., None] * jnp.stack(outs, axis=0).astype(jnp.float32), axis=0).astype(h.dtype)
    return o.reshape(B, S, C_MIX) @ w_out


def moe_ffn(h, w_router, b_router, w_gu, b_gu, w_down, b_down):
    Bs, S, D = h.shape
    T = Bs * S
    ht = h.reshape(T, D)
    logits = jnp.dot(ht, w_router, preferred_element_type=jnp.float32) + b_router.astype(jnp.float32)
    top_val, top_idx = lax.top_k(logits, TOP_K)
    gate = jax.nn.softmax(top_val, axis=-1)
    e_flat = top_idx.reshape(-1).astype(jnp.int32)
    tok_flat = jnp.arange(T * TOP_K, dtype=jnp.int32) // TOP_K
    order = jnp.argsort(e_flat)
    e_sorted = e_flat[order]
    tok_sorted = tok_flat[order]
    g_sorted = gate.reshape(-1)[order]
    counts = jnp.bincount(e_flat, length=N_EXPERTS).astype(jnp.int32)
    padded = (counts + MOE_BLOCK - 1) // MOE_BLOCK * MOE_BLOCK
    start = jnp.cumsum(counts) - counts
    pend = jnp.cumsum(padded)
    pstart = pend - padded
    rank = jnp.arange(T * TOP_K, dtype=jnp.int32) - start[e_sorted]
    dest = pstart[e_sorted] + rank
    n_rows = T * TOP_K + N_EXPERTS * MOE_BLOCK
    n_blocks = n_rows // MOE_BLOCK
    row_tok = jnp.full((n_rows,), T, dtype=jnp.int32).at[dest].set(tok_sorted)
    ht_pad = jnp.concatenate([ht, jnp.zeros((1, D), ht.dtype)], axis=0)
    xb = ht_pad[row_tok].reshape(n_blocks, MOE_BLOCK, D)
    blk_start = jnp.arange(n_blocks, dtype=jnp.int32) * MOE_BLOCK
    blk_expert = jnp.minimum(jnp.searchsorted(pend, blk_start, side='right'), N_EXPERTS - 1)

    def expert_block(args):
        xblk, e = args
        gu = jnp.dot(xblk, w_gu[e], preferred_element_type=jnp.float32) + b_gu[e].astype(jnp.float32)
        glu = jnp.minimum(gu[:, :D_FF], SWIGLU_LIMIT)
        lin = jnp.clip(gu[:, D_FF:], -SWIGLU_LIMIT, SWIGLU_LIMIT)
        act = (glu * jax.nn.sigmoid(SWIGLU_ALPHA * glu) * (lin + 1.0)).astype(xblk.dtype)
        out = jnp.dot(act, w_down[e], preferred_element_type=jnp.float32) + b_down[e].astype(jnp.float32)
        return out.astype(xblk.dtype)

    yb = lax.map(expert_block, (xb, blk_expert)).reshape(n_rows, D)
    contrib = yb[dest] * g_sorted[:, None].astype(h.dtype)
    y = jnp.zeros((T, D), h.dtype).at[tok_sorted].add(contrib)
    return y.reshape(Bs, S, D)


def setup_inputs(seed: int = 0) -> dict:
    key = jax.random.key(seed)
    ks = jax.random.split(key, 24)
    f32 = jnp.float32

    def nrm(k, shape, fan_in, gain=1.0):
        return jax.random.normal(k, shape, f32) * (gain * fan_in ** -0.5)

    def gain_init(k, shape):
        return 1.0 + 0.05 * jax.random.normal(k, shape, f32)

    def small(k, shape, s):
        return s * jax.random.normal(k, shape, f32)

    D = D_MODEL
    x = jax.random.normal(ks[0], (BATCH, SEQ, D), f32)
    c = jax.random.normal(ks[1], (BATCH, D), f32)
    offs = jax.random.randint(ks[2], (BATCH, 1), 0, SEQ, dtype=jnp.int32)
    positions = jnp.arange(SEQ, dtype=jnp.int32)[None, :] + offs
    return {
        "x": x,
        "c": c,
        "positions": positions,
        "w_mod": nrm(ks[3], (DEPTH, D, 6 * D), D, 0.5),
        "b_mod": small(ks[4], (DEPTH, 6 * D), 0.02),
        "g_norm_mix": gain_init(ks[5], (DEPTH, D)),
        "g_norm_ffn": gain_init(ks[6], (DEPTH, D)),
        "w_in_ab": nrm(ks[7], (N_EVEN, D, AB_IN), D),
        "mla_g_q": gain_init(ks[8], (N_EVEN, MLA_Q_LORA)),
        "mla_w_qb": nrm(ks[9], (N_EVEN, MLA_Q_LORA, MLA_HEADS * (MLA_NOPE + MLA_ROPE)), MLA_Q_LORA),
        "mla_g_kv": gain_init(ks[10], (N_EVEN, MLA_KV_LORA)),
        "mla_w_kvb": nrm(ks[11], (N_EVEN, MLA_KV_LORA, MLA_HEADS * (MLA_NOPE + MLA_V)), MLA_KV_LORA),
        "swa_sink": small(ks[12], (N_EVEN, SWA_Q_HEADS), 0.5),
        "w_out_ab": nrm(ks[13], (N_EVEN, AB_MIX, D), AB_MIX),
        "w_in_c": nrm(ks[14], (N_ODD, D, C_IN), D),
        "w_out_c": nrm(ks[15], (N_ODD, C_MIX, D), C_MIX),
        "w_router": nrm(ks[16], (DEPTH, D, N_EXPERTS), D),
        "b_router": small(ks[17], (DEPTH, N_EXPERTS), 0.01),
        "w_gu": nrm(ks[18], (DEPTH, N_EXPERTS, D, 2 * D_FF), D),
        "b_gu": small(ks[19], (DEPTH, N_EXPERTS, 2 * D_FF), 0.02),
        "w_down": nrm(ks[20], (DEPTH, N_EXPERTS, D_FF, D), D_FF),
        "b_down": small(ks[21], (DEPTH, N_EXPERTS, D), 0.02),
        "g_final": gain_init(ks[22], (D,)),
    }


def reference(x, c, positions, w_mod, b_mod, g_norm_mix, g_norm_ffn,
              w_in_ab, mla_g_q, mla_w_qb, mla_g_kv, mla_w_kvb, swa_sink, w_out_ab,
              w_in_c, w_out_c, w_router, b_router, w_gu, b_gu, w_down, b_down, g_final):
    cos, sin = rope_tables(positions)
    c_act = jax.nn.silu(c)
    for layer in range(DEPTH):
        mod = c_act @ w_mod[layer] + b_mod[layer]
        sh1, sc1, g1, sh2, sc2, g2 = jnp.split(mod, 6, axis=-1)
        h = ada_norm(x, g_norm_mix[layer], sh1, sc1)
        li = layer // 2
        if layer % 2 == 0:
            mix = mixer_mla_swa(h, cos, sin, w_in_ab[li], mla_g_q[li], mla_w_qb[li],
                                mla_g_kv[li], mla_w_kvb[li], swa_sink[li], w_out_ab[li])
        else:
            mix = mixer_dilated(h, cos, sin, w_in_c[li], w_out_c[li])
        x = x + g1[:, None, :] * mix
        h = ada_norm(x, g_norm_ffn[layer], sh2, sc2)
        ffn = moe_ffn(h, w_router[layer], b_router[layer], w_gu[layer], b_gu[layer],
                      w_down[layer], b_down[layer])
        x = x + g2[:, None, :] * ffn
    return rmsnorm(x, g_final)
```

```python
import functools

import jax
import jax.numpy as jnp
import numpy as np
from jax import lax
from jax.experimental import pallas as pl
from jax.experimental.pallas import tpu as pltpu

F32 = jnp.float32
BF16 = jnp.bfloat16

EPS = 1e-6
ROPE_THETA = 10000.0
LANES = 128
HEAD_DIM = 64
ROPE_HALF = HEAD_DIM // 2

MLA_HEADS = 8
MLA_Q_LORA = 384
MLA_KV_LORA = 256
MLA_NOPE = 128
MLA_ROPE = 64
MLA_V = 128
SWA_Q_HEADS = 16
SWA_KV_HEADS = 2
SWA_RADIUS = 128
DIL_PATTERN = ((128, 1), (512, 4), (2048, 16))
DIL_HEADS = 16
N_EXPERTS = 32
TOP_K = 4
SWIGLU_LIMIT = 7.0
SWIGLU_ALPHA = 1.702
MOE_ROWS = 256

VMEM_LIMIT = 48 * 1024 * 1024


def _cparams(sem):
    return pltpu.CompilerParams(dimension_semantics=sem, vmem_limit_bytes=VMEM_LIMIT)


def _mod_kernel(c_ref, w_ref, b_ref, o_ref):
    c = c_ref[...]
    ca = c * (1.0 / (1.0 + jnp.exp(-c)))
    o_ref[...] = jnp.dot(ca, w_ref[...], preferred_element_type=F32,
                         precision=lax.Precision.HIGHEST) + b_ref[...]


def _modulation(c, w_mod, b_mod):
    depth, d, n = w_mod.shape
    bsz = c.shape[0]
    tn = 1536
    return pl.pallas_call(
        _mod_kernel,
        grid=(depth, n // tn),
        in_specs=[pl.BlockSpec((bsz, d), lambda l, j: (0, 0)),
                  pl.BlockSpec((None, d, tn), lambda l, j: (l, 0, j)),
                  pl.BlockSpec((None, 1, tn), lambda l, j: (l, 0, j))],
        out_specs=pl.BlockSpec((None, bsz, tn), lambda l, j: (l, 0, j)),
        out_shape=jax.ShapeDtypeStruct((depth, bsz, n), F32),
        compiler_params=_cparams(("parallel", "parallel")),
        name="modulation",
    )(c, w_mod, b_mod.reshape(depth, 1, n))


def _rope_table_kernel(pos_ref, inv_ref, sign_ref, cos_ref, sin_ref):
    ang = inv_ref[...] * pos_ref[...].astype(F32)
    cos_ref[...] = jnp.transpose(jnp.cos(ang))
    sin_ref[...] = jnp.transpose(jnp.sin(ang) * sign_ref[...])


def _rope_tables(positions):
    bsz, s = positions.shape
    inv = ROPE_THETA ** (-jnp.arange(0, HEAD_DIM, 2, dtype=F32) / HEAD_DIM)
    inv128 = jnp.tile(inv, LANES // ROPE_HALF).reshape(LANES, 1)
    sign = jnp.where((jnp.arange(LANES) % HEAD_DIM) < ROPE_HALF, -1.0, 1.0).astype(F32).reshape(LANES, 1)
    out = jax.ShapeDtypeStruct((bsz * s, LANES), F32)
    return pl.pallas_call(
        _rope_table_kernel,
        grid=(bsz,),
        in_specs=[pl.BlockSpec((None, 1, s), lambda b: (b, 0, 0)),
                  pl.BlockSpec((LANES, 1), lambda b: (0, 0)),
                  pl.BlockSpec((LANES, 1), lambda b: (0, 0))],
        out_specs=[pl.BlockSpec((s, LANES), lambda b: (b, 0)),
                   pl.BlockSpec((s, LANES), lambda b: (b, 0))],
        out_shape=[out, out],
        compiler_params=_cparams(("parallel",)),
        name="rope_tables",
    )(positions.reshape(bsz, 1, s), inv128, sign)


def _rms(x):
    return x * lax.rsqrt(jnp.mean(x * x, axis=-1, keepdims=True) + EPS)


def _ada_norm(x, g, sc, sh):
    return (_rms(x) * g) * (1.0 + sc) + sh


def _rope_group(a, cos, sin, first_half):
    rot = jnp.where(first_half, pltpu.roll(a, LANES - ROPE_HALF, 1), pltpu.roll(a, ROPE_HALF, 1))
    return a * cos + rot * sin


def _proj_kernel(*refs, prologue, n_lhs, rope_pattern, out_scale, residual):
    pos = 0
    if prologue == "ada":
        x_ref, g_ref, sc_ref, sh_ref = refs[:4]
        pos = 4
    elif prologue == "rms":
        x_ref, g_ref = refs[:2]
        pos = 2
    elif prologue == "dil":
        o_refs = refs[0:3]
        l_refs = refs[3:6]
        pos = 6
    else:
        x_refs = refs[:n_lhs]
        pos = n_lhs
    w_refs = refs[pos:pos + n_lhs]
    pos += n_lhs
    if rope_pattern is not None:
        cos_ref, sin_ref = refs[pos:pos + 2]
        pos += 2
    if residual:
        res_ref, gate_ref = refs[pos:pos + 2]
        pos += 2
    o_ref = refs[pos]
    pos += 1
    h_ref = refs[pos] if prologue != "plain" else None

    if prologue != "plain":
        @pl.when(pl.program_id(1) == 0)
        def _():
            if prologue == "ada":
                h = _ada_norm(x_ref[...], g_ref[...], sc_ref[...], sh_ref[...])
            elif prologue == "rms":
                h = _rms(x_ref[...]) * g_ref[...]
            else:
                l0, l1, l2 = (r[...] for r in l_refs)
                m = jnp.maximum(jnp.maximum(l0, l1), l2)
                e0, e1, e2 = jnp.exp(l0 - m), jnp.exp(l1 - m), jnp.exp(l2 - m)
                num = (e0 * o_refs[0][...].astype(F32) + e1 * o_refs[1][...].astype(F32)
                       + e2 * o_refs[2][...].astype(F32))
                h = num / (e0 + e1 + e2)
            h_ref[...] = h.astype(BF16)
        acc = jnp.dot(h_ref[...], w_refs[0][...], preferred_element_type=F32)
    else:
        acc = jnp.dot(x_refs[0][...], w_refs[0][...], preferred_element_type=F32)
        for xr, wr in zip(x_refs[1:], w_refs[1:]):
            acc = acc + jnp.dot(xr[...], wr[...], preferred_element_type=F32)

    if out_scale != 1.0:
        acc = acc * out_scale
    if residual:
        o_ref[...] = (res_ref[...] + gate_ref[...] * acc).astype(o_ref.dtype)
    elif rope_pattern is not None:
        cos = cos_ref[...]
        sin = sin_ref[...]
        lane = lax.broadcasted_iota(jnp.int32, cos.shape, 1)
        first_half = (lane % HEAD_DIM) < ROPE_HALF
        for c, flag in enumerate(rope_pattern):
            a = acc[:, c * LANES:(c + 1) * LANES]
            if flag:
                a = _rope_group(a, cos, sin, first_half)
            o_ref[:, c * LANES:(c + 1) * LANES] = a.astype(o_ref.dtype)
    else:
        o_ref[...] = acc.astype(o_ref.dtype)


def _proj(lhs, ws, *, prologue, seq, tm, tn, out_dtype, norm=None, lhs_cols=None, rope=None,
          rope_pattern=None, out_scale=1.0, residual=None, name):
    t = lhs[0].shape[0]
    k, n = ws[0].shape
    assert t % tm == 0 and n % tn == 0 and seq % tm == 0
    bpb = seq // tm
    n_lhs = len(ws)
    args, specs = [], []
    row = lambda i, j: (i, 0)
    if prologue in ("ada", "rms"):
        kw, kc = lhs_cols if lhs_cols is not None else (k, 0)
        args.append(lhs[0])
        specs.append(pl.BlockSpec((tm, kw), lambda i, j: (i, kc)))
        args.append(norm[0])
        specs.append(pl.BlockSpec((1, k), lambda i, j: (0, 0)))
        if prologue == "ada":
            for v in norm[1:3]:
                args.append(v)
                specs.append(pl.BlockSpec((None, 1, k), lambda i, j: (i // bpb, 0, 0)))
    elif prologue == "dil":
        for a in lhs:
            args.append(a)
            specs.append(pl.BlockSpec((tm, k), row))
    else:
        for a, w in zip(lhs, ws):
            args.append(a)
            specs.append(pl.BlockSpec((tm, w.shape[0]), row))
    for w in ws:
        args.append(w)
        specs.append(pl.BlockSpec((w.shape[0], tn), lambda i, j: (0, j)))
    if rope_pattern is not None:
        assert len(rope_pattern) == tn // LANES
        for tab in rope:
            args.append(tab)
            specs.append(pl.BlockSpec((tm, LANES), row))
    if residual is not None:
        res, gate = residual
        args.append(res)
        specs.append(pl.BlockSpec((tm, tn), lambda i, j: (i, j)))
        args.append(gate)
        specs.append(pl.BlockSpec((None, 1, tn), lambda i, j: (i // bpb, 0, j)))
    scratch = [] if prologue == "plain" else [pltpu.VMEM((tm, k), BF16)]
    kern = functools.partial(_proj_kernel, prologue=prologue, n_lhs=n_lhs, rope_pattern=rope_pattern,
                             out_scale=out_scale, residual=residual is not None)
    return pl.pallas_call(
        kern,
        grid=(t // tm, n // tn),
        in_specs=specs,
        out_specs=pl.BlockSpec((tm, tn), lambda i, j: (i, j)),
        out_shape=jax.ShapeDtypeStruct((t, n), out_dtype),
        scratch_shapes=scratch,
        compiler_params=_cparams(("parallel", "arbitrary")),
        name=name,
    )(*args)


def _mla_kernel(q_ref, kn_ref, kr_ref, v_ref, o_ref):
    k = jnp.concatenate([kn_ref[...], kr_ref[...]], axis=1)
    s = lax.dot_general(q_ref[...], k, (((1,), (1,)), ((), ())), preferred_element_type=F32)
    m = jnp.max(s, axis=1, keepdims=True)
    p = jnp.exp(s - m)
    den = jnp.sum(p, axis=1, keepdims=True)
    o = jnp.dot(p.astype(BF16), v_ref[...], preferred_element_type=F32)
    o_ref[...] = (o / den).astype(o_ref.dtype)


def _mla_attention(q_cat, kv, qkv_b, kr_col, *, bsz, seq, tq):
    t = bsz * seq
    nq = seq // tq
    return pl.pallas_call(
        _mla_kernel,
        grid=(bsz, MLA_HEADS, nq),
        in_specs=[pl.BlockSpec((tq, 2 * LANES), lambda b, h, i: (b * nq + i, h)),
                  pl.BlockSpec((seq, LANES), lambda b, h, i: (b, h)),
                  pl.BlockSpec((seq, LANES), lambda b, h, i: (b, kr_col)),
                  pl.BlockSpec((seq, LANES), lambda b, h, i: (b, MLA_HEADS + h))],
        out_specs=pl.BlockSpec((tq, LANES), lambda b, h, i: (b * nq + i, h)),
        out_shape=jax.ShapeDtypeStruct((t, MLA_HEADS * MLA_V), BF16),
        compiler_params=_cparams(("parallel", "parallel", "parallel")),
        name="mla_attention",
    )(q_cat, kv, qkv_b, kv)


def _banded_kernel(*refs, radius, tq, hb, length, n_pairs, pairs_per_kv, has_sink, want_lse):
    pos = 0
    if has_sink:
        sink_ref = refs[0]
        pos = 1
    q_ref, kp_ref, kc_ref, kn_ref, vp_ref, vc_ref, vn_ref = refs[pos:pos + 7]
    pos += 7
    o_ref = refs[pos]
    lse_ref = refs[pos + 1] if want_lse else None

    j = pl.program_id(2)
    tk = tq + 2 * hb
    kw = jnp.concatenate([kp_ref[...], kc_ref[...], kn_ref[...]], axis=0)
    vw = jnp.concatenate([vp_ref[...], vc_ref[...], vn_ref[...]], axis=0)
    qpos = j * tq + lax.broadcasted_iota(jnp.int32, (tq, tk), 0)
    kpos = j * tq - hb + lax.broadcasted_iota(jnp.int32, (tq, tk), 1)
    valid = (jnp.abs(qpos - kpos) <= radius) & (kpos >= 0) & (kpos < length)
    lane = lax.broadcasted_iota(jnp.int32, (tq, LANES), 1)
    low = lane < HEAD_DIM
    scale = HEAD_DIM ** -0.5

    for p in range(n_pairs):
        c = p // pairs_per_kv
        qp = q_ref[:, p * LANES:(p + 1) * LANES] * scale
        kp = kw[:, c * LANES:(c + 1) * LANES]
        vp = vw[:, c * LANES:(c + 1) * LANES]
        halves = []
        for half in range(2):
            qh = jnp.where(low if half == 0 else ~low, qp, jnp.zeros_like(qp))
            s = lax.dot_general(qh, kp, (((1,), (1,)), ((), ())), preferred_element_type=F32)
            s = jnp.where(valid, s, -jnp.inf)
            m = jnp.max(s, axis=1, keepdims=True)
            if has_sink:
                sk = sink_ref[2 * p + half]
                m = jnp.maximum(m, sk)
                e = jnp.exp(s - m)
                den = jnp.sum(e, axis=1, keepdims=True) + jnp.exp(sk - m)
            else:
                e = jnp.exp(s - m)
                den = jnp.sum(e, axis=1, keepdims=True)
            o = jnp.dot(e.astype(BF16), vp, preferred_element_type=F32) / den
            halves.append((o, m + jnp.log(den) if want_lse else None))
        o_ref[:, p * LANES:(p + 1) * LANES] = jnp.where(low, halves[0][0], halves[1][0]).astype(o_ref.dtype)
        if want_lse:
            lse_ref[:, p * LANES:(p + 1) * LANES] = jnp.where(
                low, jnp.broadcast_to(halves[0][1], (tq, LANES)), jnp.broadcast_to(halves[1][1], (tq, LANES)))


def _banded_attention(q_arr, k_arr, v_arr, *, bsz, seq, dil, radius, q_col, k_col, v_col, kv_width,
                      pairs_per_kv, sink=None, want_lse, name):
    length = seq // dil
    hb = radius
    tq = min(256, length)
    assert tq % hb == 0 and length % tq == 0
    qw = DIL_HEADS * HEAD_DIM
    n_pairs = qw // LANES
    nq_blocks = q_arr.shape[1] // qw
    nk_blocks = k_arr.shape[1] // kv_width
    nv_blocks = v_arr.shape[1] // kv_width
    rep = tq // hb
    last_hb = length // hb - 1
    q3 = q_arr.reshape(bsz, length, dil * q_arr.shape[1])
    k3 = k_arr.reshape(bsz, length, dil * k_arr.shape[1])
    v3 = v_arr.reshape(bsz, length, dil * v_arr.shape[1])

    def prev_spec(nb, col):
        return pl.BlockSpec((None, hb, kv_width),
                            lambda b, r, j: (b, jnp.maximum(j * rep - 1, 0), r * nb + col))

    def cur_spec(nb, col):
        return pl.BlockSpec((None, tq, kv_width), lambda b, r, j: (b, j, r * nb + col))

    def next_spec(nb, col):
        return pl.BlockSpec((None, hb, kv_width),
                            lambda b, r, j: (b, jnp.minimum((j + 1) * rep, last_hb), r * nb + col))

    in_specs = [pl.BlockSpec((None, tq, qw), lambda b, r, j: (b, j, r * nq_blocks + q_col)),
                prev_spec(nk_blocks, k_col), cur_spec(nk_blocks, k_col), next_spec(nk_blocks, k_col),
                prev_spec(nv_blocks, v_col), cur_spec(nv_blocks, v_col), next_spec(nv_blocks, v_col)]
    args = [q3, k3, k3, k3, v3, v3, v3]
    if sink is not None:
        in_specs = [pl.BlockSpec(memory_space=pltpu.SMEM)] + in_specs
        args = [sink] + args
    out_spec = pl.BlockSpec((None, tq, qw), lambda b, r, j: (b, j, r))
    out_specs = [out_spec]
    out_shape = [jax.ShapeDtypeStruct((bsz, length, dil * qw), BF16)]
    if want_lse:
        out_specs.append(out_spec)
        out_shape.append(jax.ShapeDtypeStruct((bsz, length, dil * qw), F32))
    kern = functools.partial(_banded_kernel, radius=radius, tq=tq, hb=hb, length=length, n_pairs=n_pairs,
                             pairs_per_kv=pairs_per_kv, has_sink=sink is not None, want_lse=want_lse)
    outs = pl.pallas_call(
        kern,
        grid=(bsz, dil, length // tq),
        in_specs=in_specs,
        out_specs=out_specs,
        out_shape=out_shape,
        compiler_params=_cparams(("parallel", "parallel", "parallel")),
        name=name,
    )(*args)
    return [o.reshape(bsz * seq, qw) for o in outs]


def _router_kernel(x_ref, g_ref, sc_ref, sh_ref, wt_ref, b_ref, tri_ref,
                   idx_ref, gate_ref, rank_ref, cnt_ref, carry_ref):
    @pl.when(pl.program_id(0) == 0)
    def _():
        carry_ref[...] = jnp.zeros_like(carry_ref)

    h = _ada_norm(x_ref[...], g_ref[...], sc_ref[...], sh_ref[...])
    lt = lax.dot_general(wt_ref[...], h, (((1,), (1,)), ((), ())), preferred_element_type=F32,
                         precision=lax.Precision.HIGHEST) + b_ref[...]
    n_e, tm = lt.shape
    e_iota = lax.broadcasted_iota(jnp.int32, (n_e, tm), 0)
    vals, sels = [], []
    for k in range(TOP_K):
        m = jnp.max(lt, axis=0, keepdims=True)
        idx = jnp.min(jnp.where(lt == m, e_iota, n_e), axis=0, keepdims=True)
        sel = e_iota == idx
        idx_ref[k:k + 1, :] = idx
        vals.append(m)
        sels.append(sel)
        lt = jnp.where(sel, -jnp.inf, lt)
    exps = [jnp.exp(v - vals[0]) for v in vals]
    den = exps[0] + exps[1] + exps[2] + exps[3]
    for k in range(TOP_K):
        gate_ref[k:k + 1, :] = exps[k] / den

    onehot = jnp.where(sels[0] | sels[1] | sels[2] | sels[3], 1.0, 0.0)
    carry = carry_ref[:, 0:1]
    before = jnp.dot(onehot.astype(BF16), tri_ref[...], preferred_element_type=F32) + carry
    for k in range(TOP_K):
        rank_ref[k:k + 1, :] = jnp.sum(jnp.where(sels[k], before, 0.0), axis=0, keepdims=True).astype(jnp.int32)
    total = carry + jnp.sum(onehot, axis=1, keepdims=True)
    carry_ref[...] = jnp.broadcast_to(total, carry_ref.shape)
    cnt_ref[...] = jnp.broadcast_to(total, cnt_ref.shape)


def _router(x, norm, w_router, b_router, *, seq, tm=512):
    t, d = x.shape
    n_e = w_router.shape[1]
    bpb = seq // tm
    tri = (jnp.arange(tm)[:, None] < jnp.arange(tm)[None, :]).astype(BF16)
    g, sc, sh = norm
    out_i = jax.ShapeDtypeStruct((TOP_K, t), jnp.int32)
    tok = pl.BlockSpec((TOP_K, tm), lambda i: (0, i))
    return pl.pallas_call(
        _router_kernel,
        grid=(t // tm,),
        in_specs=[pl.BlockSpec((tm, d), lambda i: (i, 0)),
                  pl.BlockSpec((1, d), lambda i: (0, 0)),
                  pl.BlockSpec((None, 1, d), lambda i: (i // bpb, 0, 0)),
                  pl.BlockSpec((None, 1, d), lambda i: (i // bpb, 0, 0)),
                  pl.BlockSpec((n_e, d), lambda i: (0, 0)),
                  pl.BlockSpec((n_e, 1), lambda i: (0, 0)),
                  pl.BlockSpec((tm, tm), lambda i: (0, 0))],
        out_specs=[tok, tok, tok, pl.BlockSpec((n_e, LANES), lambda i: (0, 0))],
        out_shape=[out_i, jax.ShapeDtypeStruct((TOP_K, t), F32), out_i,
                   jax.ShapeDtypeStruct((n_e, LANES), F32)],
        scratch_shapes=[pltpu.VMEM((n_e, LANES), F32)],
        compiler_params=_cparams(("arbitrary",)),
        name="router",
    )(x, g, sc, sh, w_router.T, b_router.reshape(n_e, 1), tri)


ROW_TOKENS = 256
WAIT_UNROLL = 8


def _wait_rows(make_copy, n):
    def body(_, carry):
        for _u in range(WAIT_UNROLL):
            make_copy().wait()
        return carry
    lax.fori_loop(0, n // WAIT_UNROLL, body, 0)


def _dispatch_kernel(dest_hbm, x_ref, g_ref, sc_ref, sh_ref, xs_hbm, idx_smem, h_buf, idx_sem, row_sem):
    i = pl.program_id(0)
    n = pl.num_programs(0)
    slot = i % 2

    def idx_copy(step, sl):
        return pltpu.make_async_copy(dest_hbm.at[step], idx_smem.at[sl], idx_sem.at[sl])

    @pl.when(i == 0)
    def _():
        idx_copy(0, 0).start()

    @pl.when(i + 1 < n)
    def _():
        idx_copy(i + 1, 1 - slot).start()

    h_buf[...] = _ada_norm(x_ref[...], g_ref[...], sc_ref[...], sh_ref[...])
    idx_copy(i, slot).wait()

    def row_copy(t, d):
        return pltpu.make_async_copy(h_buf.at[pl.ds(t, 1)], xs_hbm.at[pl.ds(d, 1)], row_sem)

    def issue(t, carry):
        for k in range(TOP_K):
            row_copy(t, idx_smem[slot, k * ROW_TOKENS + t]).start()
        return carry
    lax.fori_loop(0, ROW_TOKENS, issue, 0)
    _wait_rows(lambda: row_copy(0, 0), ROW_TOKENS * TOP_K)


def _dispatch(x, norm, dest_blocks, n_rows, *, seq):
    t, d = x.shape
    bpb = seq // ROW_TOKENS
    g, sc, sh = norm
    return pl.pallas_call(
        _dispatch_kernel,
        grid=(t // ROW_TOKENS,),
        in_specs=[pl.BlockSpec(memory_space=pl.ANY),
                  pl.BlockSpec((ROW_TOKENS, d), lambda i: (i, 0)),
                  pl.BlockSpec((1, d), lambda i: (0, 0)),
                  pl.BlockSpec((None, 1, d), lambda i: (i // bpb, 0, 0)),
                  pl.BlockSpec((None, 1, d), lambda i: (i // bpb, 0, 0))],
        out_specs=pl.BlockSpec(memory_space=pl.ANY),
        out_shape=jax.ShapeDtypeStruct((n_rows, d), F32),
        scratch_shapes=[pltpu.SMEM((2, ROW_TOKENS * TOP_K), jnp.int32),
                        pltpu.VMEM((ROW_TOKENS, d), F32),
                        pltpu.SemaphoreType.DMA((2,)),
                        pltpu.SemaphoreType.DMA(())],
        compiler_params=_cparams(("arbitrary",)),
        name="moe_dispatch",
    )(dest_blocks, x, g, sc, sh)


def _expert_kernel(be_ref, nu_ref, xs_ref, wgu_ref, bgu_ref, wd_ref, bd_ref, y_ref):
    @pl.when(pl.program_id(0) < nu_ref[0])
    def _():
        d_ff = wd_ref.shape[0]
        gu = jnp.dot(xs_ref[...].astype(BF16), wgu_ref[...], preferred_element_type=F32) + bgu_ref[...]
        glu = jnp.minimum(gu[:, :d_ff], SWIGLU_LIMIT)
        lin = jnp.clip(gu[:, d_ff:], -SWIGLU_LIMIT, SWIGLU_LIMIT)
        act = glu * (1.0 / (1.0 + jnp.exp(-SWIGLU_ALPHA * glu))) * (lin + 1.0)
        y_ref[...] = jnp.dot(act.astype(BF16), wd_ref[...], preferred_element_type=F32) + bd_ref[...]


def _experts(xs, blk_expert, n_used, w_gu, b_gu, w_down, b_down):
    n_rows, d = xs.shape
    n_e, _, f2 = w_gu.shape
    n_blocks = n_rows // MOE_ROWS

    def blk(i, be, nu):
        return jnp.minimum(i, nu[0] - 1)

    grid_spec = pltpu.PrefetchScalarGridSpec(
        num_scalar_prefetch=2,
        grid=(n_blocks,),
        in_specs=[pl.BlockSpec((MOE_ROWS, d), lambda i, be, nu: (blk(i, be, nu), 0)),
                  pl.BlockSpec((None, d, f2), lambda i, be, nu: (be[blk(i, be, nu)], 0, 0)),
                  pl.BlockSpec((None, 1, f2), lambda i, be, nu: (be[blk(i, be, nu)], 0, 0)),
                  pl.BlockSpec((None, f2 // 2, d), lambda i, be, nu: (be[blk(i, be, nu)], 0, 0)),
                  pl.BlockSpec((None, 1, d), lambda i, be, nu: (be[blk(i, be, nu)], 0, 0))],
        out_specs=pl.BlockSpec((MOE_ROWS, d), lambda i, be, nu: (i, 0)),
    )
    return pl.pallas_call(
        _expert_kernel,
        grid_spec=grid_spec,
        out_shape=jax.ShapeDtypeStruct((n_rows, d), F32),
        compiler_params=_cparams(("arbitrary",)),
        name="moe_experts",
    )(blk_expert, n_used, xs, w_gu, b_gu.reshape(n_e, 1, f2), w_down, b_down.reshape(n_e, 1, d))


def _combine_kernel(dest_hbm, yb_hbm, gate_ref, x_ref, g2_ref, o_ref, idx_smem, buf, idx_sem, row_sem):
    i = pl.program_id(0)
    n = pl.num_programs(0)
    slot = i % 2

    def idx_copy(step, sl):
        return pltpu.make_async_copy(dest_hbm.at[step], idx_smem.at[sl], idx_sem.at[sl])

    @pl.when(i == 0)
    def _():
        idx_copy(0, 0).start()

    @pl.when(i + 1 < n)
    def _():
        idx_copy(i + 1, 1 - slot).start()

    idx_copy(i, slot).wait()

    def row_copy(k, t, d):
        return pltpu.make_async_copy(yb_hbm.at[pl.ds(d, 1)], buf.at[k, pl.ds(t, 1)], row_sem)

    def issue(t, carry):
        for k in range(TOP_K):
            row_copy(k, t, idx_smem[slot, k * ROW_TOKENS + t]).start()
        return carry
    lax.fori_loop(0, ROW_TOKENS, issue, 0)
    _wait_rows(lambda: row_copy(0, 0, 0), ROW_TOKENS * TOP_K)

    gate = gate_ref[...]
    y = gate[:, 0:1] * buf[0]
    for k in range(1, TOP_K):
        y = y + gate[:, k:k + 1] * buf[k]
    o_ref[...] = x_ref[...] + g2_ref[...] * y


def _combine(yb, dest_blocks, gate, x, g2, *, seq):
    t, d = x.shape
    bpb = seq // ROW_TOKENS
    return pl.pallas_call(
        _combine_kernel,
        grid=(t // ROW_TOKENS,),
        in_specs=[pl.BlockSpec(memory_space=pl.ANY),
                  pl.BlockSpec(memory_space=pl.ANY),
                  pl.BlockSpec((ROW_TOKENS, TOP_K), lambda i: (i, 0)),
                  pl.BlockSpec((ROW_TOKENS, d), lambda i: (i, 0)),
                  pl.BlockSpec((None, 1, d), lambda i: (i // bpb, 0, 0))],
        out_specs=pl.BlockSpec((ROW_TOKENS, d), lambda i: (i, 0)),
        out_shape=jax.ShapeDtypeStruct((t, d), F32),
        scratch_shapes=[pltpu.SMEM((2, ROW_TOKENS * TOP_K), jnp.int32),
                        pltpu.VMEM((TOP_K, ROW_TOKENS, d), F32),
                        pltpu.SemaphoreType.DMA((2,)),
                        pltpu.SemaphoreType.DMA(())],
        compiler_params=_cparams(("arbitrary",)),
        name="moe_combine",
    )(dest_blocks, yb, gate, x, g2)


def _moe(x, norm, g2, w_router, b_router, w_gu, b_gu, w_down, b_down, *, seq):
    t, d = x.shape
    n_e = w_router.shape[1]
    idx_t, gate_t, rank_t, counts = _router(x, norm, w_router, b_router, seq=seq)
    counts = counts[:, 0].astype(jnp.int32)
    padded = (counts + MOE_ROWS - 1) // MOE_ROWS * MOE_ROWS
    pend = jnp.cumsum(padded)
    pstart = pend - padded
    n_rows = t * TOP_K + n_e * MOE_ROWS
    n_blocks = n_rows // MOE_ROWS
    onehot = idx_t[:, :, None] == jnp.arange(n_e, dtype=jnp.int32)
    dest_t = jnp.sum(jnp.where(onehot, pstart, 0), axis=-1) + rank_t
    dest_blocks = (dest_t.reshape(TOP_K, t // ROW_TOKENS, ROW_TOKENS).transpose(1, 0, 2)
                   .reshape(t // ROW_TOKENS, TOP_K * ROW_TOKENS))
    blk_start = jnp.arange(n_blocks, dtype=jnp.int32) * MOE_ROWS
    blk_expert = jnp.minimum(jnp.sum(blk_start[:, None] >= pend[None, :], axis=1), n_e - 1).astype(jnp.int32)
    n_used = (pend[-1:] // MOE_ROWS).astype(jnp.int32)

    xs = _dispatch(x, norm, dest_blocks, n_rows, seq=seq)
    yb = _experts(xs, blk_expert, n_used, w_gu, b_gu, w_down, b_down)
    return _combine(yb, dest_blocks, gate_t.T, x, g2, seq=seq)


def _final_norm_kernel(x_ref, g_ref, o_ref):
    o_ref[...] = _rms(x_ref[...]) * g_ref[...]


def _final_norm(x, g, tm=1024):
    t, d = x.shape
    return pl.pallas_call(
        _final_norm_kernel,
        grid=(t // tm,),
        in_specs=[pl.BlockSpec((tm, d), lambda i: (i, 0)), pl.BlockSpec((1, d), lambda i: (0, 0))],
        out_specs=pl.BlockSpec((tm, d), lambda i: (i, 0)),
        out_shape=jax.ShapeDtypeStruct((t, d), F32),
        compiler_params=_cparams(("parallel",)),
        name="final_norm",
    )(x, g.reshape(1, d))


def _dup_heads(w, n_heads):
    d = w.shape[0]
    return jnp.repeat(w.reshape(d, n_heads, 1, HEAD_DIM), 2, axis=2).reshape(d, n_heads * LANES)


def _mixer_mla_swa(x, norm, g1, cos, sin, w_in, g_q, w_qb, g_kv, w_kvb, sink, w_out, *, bsz, seq):
    d = x.shape[1]
    o = np.cumsum((0, MLA_Q_LORA, MLA_KV_LORA, MLA_ROPE, SWA_Q_HEADS * HEAD_DIM,
                   SWA_KV_HEADS * HEAD_DIM, SWA_KV_HEADS * HEAD_DIM))
    w_cq, w_ckv, w_kr, w_qs, w_ks, w_vs = (w_in[:, o[i]:o[i + 1]] for i in range(6))
    w_lat = jnp.concatenate([w_cq, jnp.zeros((d, LANES), F32), w_ckv], axis=1).astype(BF16)
    lat = _proj([x], [w_lat], prologue="ada", norm=norm, seq=seq, tm=512, tn=w_lat.shape[1],
                out_dtype=F32, name="proj_latent")
    w_b = jnp.concatenate([w_qs, _dup_heads(w_ks, SWA_KV_HEADS), _dup_heads(w_vs, SWA_KV_HEADS),
                           w_kr, jnp.zeros((d, LANES - MLA_ROPE), F32)], axis=1).astype(BF16)
    qkv_b = _proj([x], [w_b], prologue="ada", norm=norm, seq=seq, tm=512, tn=w_b.shape[1], out_dtype=BF16,
                  rope=(cos, sin), rope_pattern=(1,) * 10 + (0, 0) + (1,), name="proj_swa")
    kr_col = (w_b.shape[1] - LANES) // LANES
    qd = MLA_NOPE + MLA_ROPE
    w_q = w_qb.reshape(MLA_Q_LORA, MLA_HEADS, qd)
    w_q = jnp.concatenate([w_q, jnp.zeros((MLA_Q_LORA, MLA_HEADS, 2 * LANES - qd), F32)], axis=2)
    w_q = w_q.reshape(MLA_Q_LORA, MLA_HEADS * 2 * LANES).astype(BF16)
    q_cat = _proj([lat], [w_q], prologue="rms", norm=(g_q.reshape(1, -1),), lhs_cols=(MLA_Q_LORA, 0),
                  seq=seq, tm=512, tn=w_q.shape[1], out_dtype=BF16, rope=(cos, sin),
                  rope_pattern=(0, 1) * MLA_HEADS, out_scale=qd ** -0.5, name="proj_mla_q")
    w_kv = w_kvb.reshape(MLA_KV_LORA, MLA_HEADS, MLA_NOPE + MLA_V)
    w_kv = jnp.concatenate([w_kv[:, :, :MLA_NOPE].reshape(MLA_KV_LORA, -1),
                            w_kv[:, :, MLA_NOPE:].reshape(MLA_KV_LORA, -1)], axis=1).astype(BF16)
    kv = _proj([lat], [w_kv], prologue="rms", norm=(g_kv.reshape(1, -1),), lhs_cols=(MLA_KV_LORA, 2),
               seq=seq, tm=512, tn=w_kv.shape[1], out_dtype=BF16, name="proj_mla_kv")
    o_a = _mla_attention(q_cat, kv, qkv_b, kr_col, bsz=bsz, seq=seq, tq=256)
    (o_b,) = _banded_attention(qkv_b, qkv_b, qkv_b, bsz=bsz, seq=seq, dil=1, radius=SWA_RADIUS,
                               q_col=0, k_col=4, v_col=5, kv_width=2 * LANES,
                               pairs_per_kv=SWA_Q_HEADS // SWA_KV_HEADS // 2, sink=sink, want_lse=False,
                               name="swa_attention")
    na = MLA_HEADS * MLA_V
    return _proj([o_a, o_b], [w_out[:na].astype(BF16), w_out[na:].astype(BF16)], prologue="plain",
                 seq=seq, tm=512, tn=d, out_dtype=F32, residual=(x, g1), name="proj_out_ab")


def _mixer_dilated(x, norm, g1, cos, sin, w_in, w_out, *, bsz, seq):
    d = x.shape[1]
    qw = DIL_HEADS * HEAD_DIM
    ng = len(DIL_PATTERN)
    w5 = w_in.reshape(d, ng, 3, qw)
    w_qk = w5[:, :, 0:2].reshape(d, ng * 2 * qw).astype(BF16)
    w_v = w5[:, :, 2].reshape(d, ng * qw).astype(BF16)
    qk = _proj([x], [w_qk], prologue="ada", norm=norm, seq=seq, tm=1024, tn=qw, out_dtype=BF16,
               rope=(cos, sin), rope_pattern=(1,) * (qw // LANES), name="proj_dil_qk")
    v = _proj([x], [w_v], prologue="ada", norm=norm, seq=seq, tm=1024, tn=qw, out_dtype=BF16,
              name="proj_dil_v")
    outs, lses = [], []
    for g, (window, dil) in enumerate(DIL_PATTERN):
        o, lse = _banded_attention(qk, qk, v, bsz=bsz, seq=seq, dil=dil, radius=window // (2 * dil),
                                   q_col=2 * g, k_col=2 * g + 1, v_col=g, kv_width=qw, pairs_per_kv=1,
                                   want_lse=True, name=f"dilated_attention_{dil}")
        outs.append(o)
        lses.append(lse)
    return _proj(outs + lses, [w_out.astype(BF16)], prologue="dil", seq=seq, tm=512, tn=d, out_dtype=F32,
                 residual=(x, g1), name="proj_out_c")


def kernel(x, c, positions, w_mod, b_mod, g_norm_mix, g_norm_ffn, w_in_ab, mla_g_q, mla_w_qb, mla_g_kv,
           mla_w_kvb, swa_sink, w_out_ab, w_in_c, w_out_c, w_router, b_router, w_gu, b_gu, w_down, b_down,
           g_final):
    bsz, seq, d = x.shape
    depth = w_mod.shape[0]
    cos, sin = _rope_tables(positions)
    mod = _modulation(c, w_mod, b_mod)
    xt = x.reshape(bsz * seq, d)
    for layer in range(depth):
        sh1, sc1, g1, sh2, sc2, g2 = (mod[layer, :, i * d:(i + 1) * d].reshape(bsz, 1, d) for i in range(6))
        li = layer // 2
        norm = (g_norm_mix[layer].reshape(1, d), sc1, sh1)
        if layer % 2 == 0:
            xt = _mixer_mla_swa(xt, norm, g1, cos, sin, w_in_ab[li], mla_g_q[li], mla_w_qb[li], mla_g_kv[li],
                                mla_w_kvb[li], swa_sink[li], w_out_ab[li], bsz=bsz, seq=seq)
        else:
            xt = _mixer_dilated(xt, norm, g1, cos, sin, w_in_c[li], w_out_c[li], bsz=bsz, seq=seq)
        norm = (g_norm_ffn[layer].reshape(1, d), sc2, sh2)
        xt = _moe(xt, norm, g2, w_router[layer], b_router[layer], w_gu[layer].astype(BF16), b_gu[layer],
                  w_down[layer].astype(BF16), b_down[layer], seq=seq)
    return _final_norm(xt, g_final).reshape(bsz, seq, d)
```

```python
import functools

import jax
import jax.numpy as jnp
import numpy as np
from jax import lax
from jax.experimental import pallas as pl
from jax.experimental.pallas import tpu as pltpu

F32 = jnp.float32
BF16 = jnp.bfloat16

EPS = 1e-6
ROPE_THETA = 10000.0
LANES = 128
HEAD_DIM = 64
ROPE_HALF = HEAD_DIM // 2

MLA_HEADS = 8
MLA_Q_LORA = 384
MLA_KV_LORA = 256
MLA_NOPE = 128
MLA_ROPE = 64
MLA_V = 128
SWA_Q_HEADS = 16
SWA_KV_HEADS = 2
SWA_RADIUS = 128
DIL_PATTERN = ((128, 1), (512, 4), (2048, 16))
DIL_HEADS = 16
N_EXPERTS = 32
TOP_K = 4
SWIGLU_LIMIT = 7.0
SWIGLU_ALPHA = 1.702
MOE_ROWS = 256

VMEM_LIMIT = 48 * 1024 * 1024
EXPERT_VMEM_LIMIT = 56 * 1024 * 1024


def _cparams(sem):
    return pltpu.CompilerParams(dimension_semantics=sem, vmem_limit_bytes=VMEM_LIMIT)


def _mod_kernel(c_ref, w_ref, b_ref, o_ref):
    c = c_ref[...]
    ca = c * (1.0 / (1.0 + jnp.exp(-c)))
    o_ref[...] = jnp.dot(ca, w_ref[...], preferred_element_type=F32,
                         precision=lax.Precision.HIGHEST) + b_ref[...]


def _modulation(c, w_mod, b_mod):
    depth, d, n = w_mod.shape
    bsz = c.shape[0]
    tn = 1536
    return pl.pallas_call(
        _mod_kernel,
        grid=(depth, n // tn),
        in_specs=[pl.BlockSpec((bsz, d), lambda l, j: (0, 0)),
                  pl.BlockSpec((None, d, tn), lambda l, j: (l, 0, j)),
                  pl.BlockSpec((None, 1, tn), lambda l, j: (l, 0, j))],
        out_specs=pl.BlockSpec((None, bsz, tn), lambda l, j: (l, 0, j)),
        out_shape=jax.ShapeDtypeStruct((depth, bsz, n), F32),
        compiler_params=_cparams(("parallel", "parallel")),
        name="modulation",
    )(c, w_mod, b_mod.reshape(depth, 1, n))


def _rope_table_kernel(pos_ref, inv_ref, sign_ref, cos_ref, sin_ref):
    ang = inv_ref[...] * pos_ref[...].astype(F32)
    cos_ref[...] = jnp.transpose(jnp.cos(ang))
    sin_ref[...] = jnp.transpose(jnp.sin(ang) * sign_ref[...])


def _rope_tables(positions):
    bsz, s = positions.shape
    inv = ROPE_THETA ** (-jnp.arange(0, HEAD_DIM, 2, dtype=F32) / HEAD_DIM)
    inv128 = jnp.tile(inv, LANES // ROPE_HALF).reshape(LANES, 1)
    sign = jnp.where((jnp.arange(LANES) % HEAD_DIM) < ROPE_HALF, -1.0, 1.0).astype(F32).reshape(LANES, 1)
    out = jax.ShapeDtypeStruct((bsz * s, LANES), F32)
    return pl.pallas_call(
        _rope_table_kernel,
        grid=(bsz,),
        in_specs=[pl.BlockSpec((None, 1, s), lambda b: (b, 0, 0)),
                  pl.BlockSpec((LANES, 1), lambda b: (0, 0)),
                  pl.BlockSpec((LANES, 1), lambda b: (0, 0))],
        out_specs=[pl.BlockSpec((s, LANES), lambda b: (b, 0)),
                   pl.BlockSpec((s, LANES), lambda b: (b, 0))],
        out_shape=[out, out],
        compiler_params=_cparams(("parallel",)),
        name="rope_tables",
    )(positions.reshape(bsz, 1, s), inv128, sign)


def _rms(x):
    return x * lax.rsqrt(jnp.mean(x * x, axis=-1, keepdims=True) + EPS)


def _ada_norm(x, g, sc, sh):
    return (_rms(x) * g) * (1.0 + sc) + sh


def _rope_group(a, cos, sin, first_half):
    rot = jnp.where(first_half, pltpu.roll(a, LANES - ROPE_HALF, 1), pltpu.roll(a, ROPE_HALF, 1))
    return a * cos + rot * sin


def _proj_kernel(*refs, prologue, n_lhs, rope_pattern, rope_tiles, out_scale, residual, in_dils, out_dil):
    pos = 0
    if prologue == "ada":
        x_ref, g_ref, sc_ref, sh_ref = refs[:4]
        pos = 4
    elif prologue == "rms":
        x_ref, g_ref = refs[:2]
        pos = 2
    elif prologue == "dil":
        o_refs = refs[0:3]
        l_refs = refs[3:6]
        pos = 6
    else:
        x_refs = refs[:n_lhs]
        pos = n_lhs
    w_refs = refs[pos:pos + n_lhs]
    pos += n_lhs
    if rope_pattern is not None:
        cos_ref, sin_ref = refs[pos:pos + 2]
        pos += 2
    if residual:
        res_ref, gate_ref = refs[pos:pos + 2]
        pos += 2
    o_ref = refs[pos]
    pos += 1
    h_ref = refs[pos] if prologue != "plain" else None
    slab_ref = refs[pos + 1] if (prologue == "dil" or out_dil > 1) else None

    if prologue != "plain":
        @pl.when(pl.program_id(1) == 0)
        def _():
            if prologue == "ada":
                h_ref[...] = _ada_norm(x_ref[...], g_ref[...], sc_ref[...], sh_ref[...]).astype(BF16)
            elif prologue == "rms":
                h_ref[...] = (_rms(x_ref[...]) * g_ref[...]).astype(BF16)
            else:
                tm = h_ref.shape[0]
                for c in range(h_ref.shape[1] // LANES):
                    cols = slice(c * LANES, (c + 1) * LANES)
                    vals = []
                    for i, (ref, dil) in enumerate(zip(o_refs + l_refs, in_dils + in_dils)):
                        if dil == 1:
                            vals.append(ref[0, :, cols].astype(F32))
                        else:
                            for r in range(dil):
                                slab_ref[i, pl.ds(r, tm // dil, stride=dil), :] = ref[r, :, cols].astype(F32)
                            vals.append(slab_ref[i])
                    o0, o1, o2, l0, l1, l2 = vals
                    m = jnp.maximum(jnp.maximum(l0, l1), l2)
                    e0, e1, e2 = jnp.exp(l0 - m), jnp.exp(l1 - m), jnp.exp(l2 - m)
                    h_ref[:, cols] = ((e0 * o0 + e1 * o1 + e2 * o2) / (e0 + e1 + e2)).astype(BF16)
        acc = jnp.dot(h_ref[...], w_refs[0][...], preferred_element_type=F32)
    else:
        acc = jnp.dot(x_refs[0][...], w_refs[0][...], preferred_element_type=F32)
        for xr, wr in zip(x_refs[1:], w_refs[1:]):
            acc = acc + jnp.dot(xr[...], wr[...], preferred_element_type=F32)

    if out_scale != 1.0:
        acc = acc * out_scale
    if residual:
        o_ref[...] = (res_ref[...] + gate_ref[...] * acc).astype(o_ref.dtype)
        return

    def epilogue(pattern):
        tm = acc.shape[0]
        if any(pattern):
            cos = cos_ref[...]
            sin = sin_ref[...]
            lane = lax.broadcasted_iota(jnp.int32, cos.shape, 1)
            first_half = (lane % HEAD_DIM) < ROPE_HALF
        for c, flag in enumerate(pattern):
            cols = slice(c * LANES, (c + 1) * LANES)
            a = acc[:, cols]
            if flag:
                a = _rope_group(a, cos, sin, first_half)
            if out_dil == 1:
                o_ref[:, cols] = a.astype(o_ref.dtype)
            else:
                slab_ref[0] = a
                for r in range(out_dil):
                    o_ref[r, :, cols] = slab_ref[0, pl.ds(r, tm // out_dil, stride=out_dil), :].astype(o_ref.dtype)

    n_groups = acc.shape[1] // LANES
    if rope_pattern is None:
        epilogue((0,) * n_groups)
    elif rope_tiles is None:
        epilogue(rope_pattern)
    else:
        @pl.when(pl.program_id(1) < rope_tiles)
        def _():
            epilogue(rope_pattern)

        @pl.when(pl.program_id(1) >= rope_tiles)
        def _():
            epilogue((0,) * n_groups)


def _proj(lhs, ws, *, prologue, seq, tm, tn, out_dtype, norm=None, lhs_cols=None, rope=None,
          rope_pattern=None, rope_tiles=None, out_scale=1.0, residual=None, in_dils=None, out_dil=1, name):
    k, n = ws[0].shape
    bsz_seq = lhs[0].shape[0] if prologue != "dil" else lhs[0].shape[0] * seq
    t = bsz_seq
    assert t % tm == 0 and n % tn == 0 and seq % tm == 0
    bpb = seq // tm
    n_lhs = len(ws)
    args, specs = [], []
    row = lambda i, j: (i, 0)
    if prologue in ("ada", "rms"):
        kw, kc = lhs_cols if lhs_cols is not None else (k, 0)
        args.append(lhs[0])
        specs.append(pl.BlockSpec((tm, kw), lambda i, j: (i, kc)))
        args.append(norm[0])
        specs.append(pl.BlockSpec((1, k), lambda i, j: (0, 0)))
        if prologue == "ada":
            for v in norm[1:3]:
                args.append(v)
                specs.append(pl.BlockSpec((None, 1, k), lambda i, j: (i // bpb, 0, 0)))
    elif prologue == "dil":
        for a, dil in zip(lhs, in_dils + in_dils):
            args.append(a)
            specs.append(pl.BlockSpec((None, dil, tm // dil, k), lambda i, j: (i // bpb, 0, i % bpb, 0)))
    else:
        for a, w in zip(lhs, ws):
            args.append(a)
            specs.append(pl.BlockSpec((tm, w.shape[0]), row))
    for w in ws:
        args.append(w)
        specs.append(pl.BlockSpec((w.shape[0], tn), lambda i, j: (0, j)))
    if rope_pattern is not None:
        assert len(rope_pattern) == tn // LANES
        for tab in rope:
            args.append(tab)
            specs.append(pl.BlockSpec((tm, LANES), row))
    if residual is not None:
        res, gate = residual
        args.append(res)
        specs.append(pl.BlockSpec((tm, tn), lambda i, j: (i, j)))
        args.append(gate)
        specs.append(pl.BlockSpec((None, 1, tn), lambda i, j: (i // bpb, 0, j)))
    scratch = [] if prologue == "plain" else [pltpu.VMEM((tm, k), BF16)]
    if prologue == "dil":
        scratch.append(pltpu.VMEM((2 * len(in_dils), tm, LANES), F32))
    elif out_dil > 1:
        scratch.append(pltpu.VMEM((1, tm, LANES), F32))
    if out_dil == 1:
        out_spec = pl.BlockSpec((tm, tn), lambda i, j: (i, j))
        out_shape = jax.ShapeDtypeStruct((t, n), out_dtype)
    else:
        out_spec = pl.BlockSpec((None, out_dil, tm // out_dil, tn), lambda i, j: (i // bpb, 0, i % bpb, j))
        out_shape = jax.ShapeDtypeStruct((t // seq, out_dil, seq // out_dil, n), out_dtype)
    kern = functools.partial(_proj_kernel, prologue=prologue, n_lhs=n_lhs, rope_pattern=rope_pattern,
                             rope_tiles=rope_tiles, out_scale=out_scale, residual=residual is not None,
                             in_dils=in_dils, out_dil=out_dil)
    return pl.pallas_call(
        kern,
        grid=(t // tm, n // tn),
        in_specs=specs,
        out_specs=out_spec,
        out_shape=out_shape,
        scratch_shapes=scratch,
        compiler_params=_cparams(("parallel", "arbitrary")),
        name=name,
    )(*args)


def _mla_kernel(q_ref, kn_ref, kr_ref, v_ref, o_ref):
    k = jnp.concatenate([kn_ref[...], kr_ref[...]], axis=1)
    s = lax.dot_general(q_ref[...], k, (((1,), (1,)), ((), ())), preferred_element_type=F32)
    m = jnp.max(s, axis=1, keepdims=True)
    p = jnp.exp(s - m)
    den = jnp.sum(p, axis=1, keepdims=True)
    o = jnp.dot(p.astype(BF16), v_ref[...], preferred_element_type=F32)
    o_ref[...] = (o / den).astype(o_ref.dtype)


def _mla_attention(q_cat, kv, qkv_b, kr_col, *, bsz, seq, tq):
    t = bsz * seq
    nq = seq // tq
    return pl.pallas_call(
        _mla_kernel,
        grid=(bsz, MLA_HEADS, nq),
        in_specs=[pl.BlockSpec((tq, 2 * LANES), lambda b, h, i: (b * nq + i, h)),
                  pl.BlockSpec((seq, LANES), lambda b, h, i: (b, h)),
                  pl.BlockSpec((seq, LANES), lambda b, h, i: (b, kr_col)),
                  pl.BlockSpec((seq, LANES), lambda b, h, i: (b, MLA_HEADS + h))],
        out_specs=pl.BlockSpec((tq, LANES), lambda b, h, i: (b * nq + i, h)),
        out_shape=jax.ShapeDtypeStruct((t, MLA_HEADS * MLA_V), BF16),
        compiler_params=_cparams(("parallel", "parallel", "parallel")),
        name="mla_attention",
    )(q_cat, kv, qkv_b, kv)


def _banded_kernel(*refs, radius, tq, hb, length, n_pairs, pairs_per_kv, has_sink, want_lse):
    pos = 0
    if has_sink:
        sink_ref = refs[0]
        pos = 1
    q_ref, kp_ref, kc_ref, kn_ref, vp_ref, vc_ref, vn_ref = refs[pos:pos + 7]
    pos += 7
    o_ref = refs[pos]
    lse_ref = refs[pos + 1] if want_lse else None

    j = pl.program_id(2)
    tk = tq + 2 * hb
    kw = jnp.concatenate([kp_ref[...], kc_ref[...], kn_ref[...]], axis=0)
    vw = jnp.concatenate([vp_ref[...], vc_ref[...], vn_ref[...]], axis=0)
    qpos = j * tq + lax.broadcasted_iota(jnp.int32, (tq, tk), 0)
    kpos = j * tq - hb + lax.broadcasted_iota(jnp.int32, (tq, tk), 1)
    valid = (jnp.abs(qpos - kpos) <= radius) & (kpos >= 0) & (kpos < length)
    lane = lax.broadcasted_iota(jnp.int32, (tq, LANES), 1)
    low = lane < HEAD_DIM
    scale = HEAD_DIM ** -0.5

    for p in range(n_pairs):
        c = p // pairs_per_kv
        qp = q_ref[:, p * LANES:(p + 1) * LANES] * scale
        kp = kw[:, c * LANES:(c + 1) * LANES]
        vp = vw[:, c * LANES:(c + 1) * LANES]
        halves = []
        for half in range(2):
            qh = jnp.where(low if half == 0 else ~low, qp, jnp.zeros_like(qp))
            s = lax.dot_general(qh, kp, (((1,), (1,)), ((), ())), preferred_element_type=F32)
            s = jnp.where(valid, s, -jnp.inf)
            m = jnp.max(s, axis=1, keepdims=True)
            if has_sink:
                sk = sink_ref[2 * p + half]
                m = jnp.maximum(m, sk)
                e = jnp.exp(s - m)
                den = jnp.sum(e, axis=1, keepdims=True) + jnp.exp(sk - m)
            else:
                e = jnp.exp(s - m)
                den = jnp.sum(e, axis=1, keepdims=True)
            o = jnp.dot(e.astype(BF16), vp, preferred_element_type=F32) / den
            halves.append((o, m + jnp.log(den) if want_lse else None))
        o_ref[:, p * LANES:(p + 1) * LANES] = jnp.where(low, halves[0][0], halves[1][0]).astype(o_ref.dtype)
        if want_lse:
            lse_ref[:, p * LANES:(p + 1) * LANES] = jnp.where(
                low, jnp.broadcast_to(halves[0][1], (tq, LANES)), jnp.broadcast_to(halves[1][1], (tq, LANES)))


def _banded_attention(q_arr, k_arr, v_arr, *, bsz, seq, dil, radius, q_col, k_col, v_col, kv_width,
                      pairs_per_kv, sink=None, want_lse, name):
    length = seq // dil
    hb = radius
    tq = min(256, length)
    assert tq % hb == 0 and length % tq == 0
    qw = DIL_HEADS * HEAD_DIM
    n_pairs = qw // LANES
    rep = tq // hb
    last_hb = length // hb - 1

    def prev_spec(col):
        return pl.BlockSpec((None, None, hb, kv_width),
                            lambda b, r, j: (b, r, jnp.maximum(j * rep - 1, 0), col))

    def cur_spec(col):
        return pl.BlockSpec((None, None, tq, kv_width), lambda b, r, j: (b, r, j, col))

    def next_spec(col):
        return pl.BlockSpec((None, None, hb, kv_width),
                            lambda b, r, j: (b, r, jnp.minimum((j + 1) * rep, last_hb), col))

    in_specs = [pl.BlockSpec((None, None, tq, qw), lambda b, r, j: (b, r, j, q_col)),
                prev_spec(k_col), cur_spec(k_col), next_spec(k_col),
                prev_spec(v_col), cur_spec(v_col), next_spec(v_col)]
    args = [q_arr, k_arr, k_arr, k_arr, v_arr, v_arr, v_arr]
    if sink is not None:
        in_specs = [pl.BlockSpec(memory_space=pltpu.SMEM)] + in_specs
        args = [sink] + args
    out_spec = pl.BlockSpec((None, None, tq, qw), lambda b, r, j: (b, r, j, 0))
    out_specs = [out_spec]
    out_shape = [jax.ShapeDtypeStruct((bsz, dil, length, qw), BF16)]
    if want_lse:
        out_specs.append(out_spec)
        out_shape.append(jax.ShapeDtypeStruct((bsz, dil, length, qw), F32))
    kern = functools.partial(_banded_kernel, radius=radius, tq=tq, hb=hb, length=length, n_pairs=n_pairs,
                             pairs_per_kv=pairs_per_kv, has_sink=sink is not None, want_lse=want_lse)
    return pl.pallas_call(
        kern,
        grid=(bsz, dil, length // tq),
        in_specs=in_specs,
        out_specs=out_specs,
        out_shape=out_shape,
        compiler_params=_cparams(("parallel", "parallel", "parallel")),
        name=name,
    )(*args)


def _router_kernel(x_ref, g_ref, sc_ref, sh_ref, wt_ref, b_ref, tri_ref,
                   idx_ref, gate_ref, rank_ref, cnt_ref, carry_ref):
    @pl.when(pl.program_id(0) == 0)
    def _():
        carry_ref[...] = jnp.zeros_like(carry_ref)

    h = _ada_norm(x_ref[...], g_ref[...], sc_ref[...], sh_ref[...])
    lt = lax.dot_general(wt_ref[...], h, (((1,), (1,)), ((), ())), preferred_element_type=F32,
                         precision=lax.Precision.HIGHEST) + b_ref[...]
    n_e, tm = lt.shape
    e_iota = lax.broadcasted_iota(jnp.int32, (n_e, tm), 0)
    vals, sels = [], []
    for k in range(TOP_K):
        m = jnp.max(lt, axis=0, keepdims=True)
        idx = jnp.min(jnp.where(lt == m, e_iota, n_e), axis=0, keepdims=True)
        sel = e_iota == idx
        idx_ref[k:k + 1, :] = idx
        vals.append(m)
        sels.append(sel)
        lt = jnp.where(sel, -jnp.inf, lt)
    exps = [jnp.exp(v - vals[0]) for v in vals]
    den = exps[0] + exps[1] + exps[2] + exps[3]
    for k in range(TOP_K):
        gate_ref[k:k + 1, :] = exps[k] / den

    onehot = jnp.where(sels[0] | sels[1] | sels[2] | sels[3], 1.0, 0.0)
    carry = carry_ref[:, 0:1]
    before = jnp.dot(onehot.astype(BF16), tri_ref[...], preferred_element_type=F32) + carry
    for k in range(TOP_K):
        rank_ref[k:k + 1, :] = jnp.sum(jnp.where(sels[k], before, 0.0), axis=0, keepdims=True).astype(jnp.int32)
    total = carry + jnp.sum(onehot, axis=1, keepdims=True)
    carry_ref[...] = jnp.broadcast_to(total, carry_ref.shape)
    cnt_ref[...] = jnp.broadcast_to(total, cnt_ref.shape)


def _router(x, norm, w_router, b_router, *, seq, tm=512):
    t, d = x.shape
    n_e = w_router.shape[1]
    bpb = seq // tm
    tri = (jnp.arange(tm)[:, None] < jnp.arange(tm)[None, :]).astype(BF16)
    g, sc, sh = norm
    out_i = jax.ShapeDtypeStruct((TOP_K, t), jnp.int32)
    tok = pl.BlockSpec((TOP_K, tm), lambda i: (0, i))
    return pl.pallas_call(
        _router_kernel,
        grid=(t // tm,),
        in_specs=[pl.BlockSpec((tm, d), lambda i: (i, 0)),
                  pl.BlockSpec((1, d), lambda i: (0, 0)),
                  pl.BlockSpec((None, 1, d), lambda i: (i // bpb, 0, 0)),
                  pl.BlockSpec((None, 1, d), lambda i: (i // bpb, 0, 0)),
                  pl.BlockSpec((n_e, d), lambda i: (0, 0)),
                  pl.BlockSpec((n_e, 1), lambda i: (0, 0)),
                  pl.BlockSpec((tm, tm), lambda i: (0, 0))],
        out_specs=[tok, tok, tok, pl.BlockSpec((n_e, LANES), lambda i: (0, 0))],
        out_shape=[out_i, jax.ShapeDtypeStruct((TOP_K, t), F32), out_i,
                   jax.ShapeDtypeStruct((n_e, LANES), F32)],
        scratch_shapes=[pltpu.VMEM((n_e, LANES), F32)],
        compiler_params=_cparams(("arbitrary",)),
        name="router",
    )(x, g, sc, sh, w_router.T, b_router.reshape(n_e, 1), tri)


ROW_TOKENS = 256
WAIT_UNROLL = 8


def _wait_rows(make_copy, n):
    def body(_, carry):
        for _u in range(WAIT_UNROLL):
            make_copy().wait()
        return carry
    lax.fori_loop(0, n // WAIT_UNROLL, body, 0)


def _dispatch_kernel(dest_hbm, x_ref, g_ref, sc_ref, sh_ref, xs_hbm, idx_smem, h_buf, idx_sem, row_sem):
    i = pl.program_id(0)
    n = pl.num_programs(0)
    slot = i % 2

    def idx_copy(step, sl):
        return pltpu.make_async_copy(dest_hbm.at[step], idx_smem.at[sl], idx_sem.at[sl])

    @pl.when(i == 0)
    def _():
        idx_copy(0, 0).start()

    @pl.when(i + 1 < n)
    def _():
        idx_copy(i + 1, 1 - slot).start()

    h_buf[...] = _ada_norm(x_ref[...], g_ref[...], sc_ref[...], sh_ref[...])
    idx_copy(i, slot).wait()

    def row_copy(t, d):
        return pltpu.make_async_copy(h_buf.at[pl.ds(t, 1)], xs_hbm.at[pl.ds(d, 1)], row_sem)

    def issue(t, carry):
        for k in range(TOP_K):
            row_copy(t, idx_smem[slot, k * ROW_TOKENS + t]).start()
        return carry
    lax.fori_loop(0, ROW_TOKENS, issue, 0)
    _wait_rows(lambda: row_copy(0, 0), ROW_TOKENS * TOP_K)


def _dispatch(x, norm, dest_blocks, n_rows, *, seq):
    t, d = x.shape
    bpb = seq // ROW_TOKENS
    g, sc, sh = norm
    return pl.pallas_call(
        _dispatch_kernel,
        grid=(t // ROW_TOKENS,),
        in_specs=[pl.BlockSpec(memory_space=pl.ANY),
                  pl.BlockSpec((ROW_TOKENS, d), lambda i: (i, 0)),
                  pl.BlockSpec((1, d), lambda i: (0, 0)),
                  pl.BlockSpec((None, 1, d), lambda i: (i // bpb, 0, 0)),
                  pl.BlockSpec((None, 1, d), lambda i: (i // bpb, 0, 0))],
        out_specs=pl.BlockSpec(memory_space=pl.ANY),
        out_shape=jax.ShapeDtypeStruct((n_rows, d), F32),
        scratch_shapes=[pltpu.SMEM((2, ROW_TOKENS * TOP_K), jnp.int32),
                        pltpu.VMEM((ROW_TOKENS, d), F32),
                        pltpu.SemaphoreType.DMA((2,)),
                        pltpu.SemaphoreType.DMA(())],
        compiler_params=_cparams(("arbitrary",)),
        name="moe_dispatch",
    )(dest_blocks, x, g, sc, sh)


def _expert_kernel(be_ref, nu_ref, xs_ref, wgu_ref, bgu_ref, wd_ref, bd_ref, y_ref, wgu_bf, wd_bf):
    i = pl.program_id(0)
    used = i < nu_ref[0]

    @pl.when(used & ((i == 0) | (be_ref[i] != be_ref[jnp.maximum(i - 1, 0)])))
    def _():
        wgu_bf[...] = wgu_ref[...].astype(BF16)
        wd_bf[...] = wd_ref[...].astype(BF16)

    @pl.when(used)
    def _():
        d_ff = wd_ref.shape[0]
        gu = jnp.dot(xs_ref[...].astype(BF16), wgu_bf[...], preferred_element_type=F32) + bgu_ref[...]
        glu = jnp.minimum(gu[:, :d_ff], SWIGLU_LIMIT)
        lin = jnp.clip(gu[:, d_ff:], -SWIGLU_LIMIT, SWIGLU_LIMIT)
        act = glu * (1.0 / (1.0 + jnp.exp(-SWIGLU_ALPHA * glu))) * (lin + 1.0)
        y_ref[...] = jnp.dot(act.astype(BF16), wd_bf[...], preferred_element_type=F32) + bd_ref[...]

    @pl.when(jnp.logical_not(used))
    def _():
        y_ref[...] = jnp.zeros_like(y_ref)


def _experts(xs, blk_expert, n_used, layer, w_gu, b_gu, w_down, b_down):
    n_rows, d = xs.shape
    _, n_e, _, f2 = w_gu.shape
    n_blocks = n_rows // MOE_ROWS

    def blk(i, be, nu):
        return jnp.minimum(i, nu[0] - 1)

    def wmap(i, be, nu):
        return (layer, be[blk(i, be, nu)], 0, 0)

    grid_spec = pltpu.PrefetchScalarGridSpec(
        num_scalar_prefetch=2,
        grid=(n_blocks,),
        in_specs=[pl.BlockSpec((MOE_ROWS, d), lambda i, be, nu: (blk(i, be, nu), 0)),
                  pl.BlockSpec((None, None, d, f2), wmap),
                  pl.BlockSpec((None, None, 1, f2), wmap),
                  pl.BlockSpec((None, None, f2 // 2, d), wmap),
                  pl.BlockSpec((None, None, 1, d), wmap)],
        out_specs=pl.BlockSpec((MOE_ROWS, d), lambda i, be, nu: (i, 0)),
        scratch_shapes=[pltpu.VMEM((d, f2), BF16), pltpu.VMEM((f2 // 2, d), BF16)],
    )
    depth = w_gu.shape[0]
    return pl.pallas_call(
        _expert_kernel,
        grid_spec=grid_spec,
        out_shape=jax.ShapeDtypeStruct((n_rows, d), F32),
        compiler_params=pltpu.CompilerParams(dimension_semantics=("arbitrary",),
                                             vmem_limit_bytes=EXPERT_VMEM_LIMIT),
        name="moe_experts",
    )(blk_expert, n_used, xs, w_gu, b_gu.reshape(depth, n_e, 1, f2), w_down, b_down.reshape(depth, n_e, 1, d))


def _combine_kernel(dest_hbm, yb_hbm, gate_ref, x_ref, g2_ref, o_ref, idx_smem, buf, idx_sem, row_sem):
    i = pl.program_id(0)
    n = pl.num_programs(0)
    slot = i % 2

    def idx_copy(step, sl):
        return pltpu.make_async_copy(dest_hbm.at[step], idx_smem.at[sl], idx_sem.at[sl])

    @pl.when(i == 0)
    def _():
        idx_copy(0, 0).start()

    @pl.when(i + 1 < n)
    def _():
        idx_copy(i + 1, 1 - slot).start()

    idx_copy(i, slot).wait()

    def row_copy(k, t, d):
        return pltpu.make_async_copy(yb_hbm.at[pl.ds(d, 1)], buf.at[k, pl.ds(t, 1)], row_sem)

    def issue(t, carry):
        for k in range(TOP_K):
            row_copy(k, t, idx_smem[slot, k * ROW_TOKENS + t]).start()
        return carry
    lax.fori_loop(0, ROW_TOKENS, issue, 0)
    _wait_rows(lambda: row_copy(0, 0, 0), ROW_TOKENS * TOP_K)

    gate = gate_ref[...]
    y = gate[:, 0:1] * buf[0]
    for k in range(1, TOP_K):
        y = y + gate[:, k:k + 1] * buf[k]
    o_ref[...] = x_ref[...] + g2_ref[...] * y


def _combine(yb, dest_blocks, gate, x, g2, *, seq):
    t, d = x.shape
    bpb = seq // ROW_TOKENS
    return pl.pallas_call(
        _combine_kernel,
        grid=(t // ROW_TOKENS,),
        in_specs=[pl.BlockSpec(memory_space=pl.ANY),
                  pl.BlockSpec(memory_space=pl.ANY),
                  pl.BlockSpec((ROW_TOKENS, TOP_K), lambda i: (i, 0)),
                  pl.BlockSpec((ROW_TOKENS, d), lambda i: (i, 0)),
                  pl.BlockSpec((None, 1, d), lambda i: (i // bpb, 0, 0))],
        out_specs=pl.BlockSpec((ROW_TOKENS, d), lambda i: (i, 0)),
        out_shape=jax.ShapeDtypeStruct((t, d), F32),
        scratch_shapes=[pltpu.SMEM((2, ROW_TOKENS * TOP_K), jnp.int32),
                        pltpu.VMEM((TOP_K, ROW_TOKENS, d), F32),
                        pltpu.SemaphoreType.DMA((2,)),
                        pltpu.SemaphoreType.DMA(())],
        compiler_params=_cparams(("arbitrary",)),
        name="moe_combine",
    )(dest_blocks, yb, gate, x, g2)


def _moe(x, norm, g2, w_router, b_router, layer, w_gu, b_gu, w_down, b_down, *, seq):
    t, d = x.shape
    n_e = w_router.shape[1]
    idx_t, gate_t, rank_t, counts = _router(x, norm, w_router, b_router, seq=seq)
    counts = counts[:, 0].astype(jnp.int32)
    padded = (counts + MOE_ROWS - 1) // MOE_ROWS * MOE_ROWS
    pend = jnp.cumsum(padded)
    pstart = pend - padded
    n_rows = t * TOP_K + n_e * MOE_ROWS
    n_blocks = n_rows // MOE_ROWS
    onehot = idx_t[:, :, None] == jnp.arange(n_e, dtype=jnp.int32)
    dest_t = jnp.sum(jnp.where(onehot, pstart, 0), axis=-1) + rank_t
    dest_blocks = (dest_t.reshape(TOP_K, t // ROW_TOKENS, ROW_TOKENS).transpose(1, 0, 2)
                   .reshape(t // ROW_TOKENS, TOP_K * ROW_TOKENS))
    blk_start = jnp.arange(n_blocks, dtype=jnp.int32) * MOE_ROWS
    blk_expert = jnp.minimum(jnp.sum(blk_start[:, None] >= pend[None, :], axis=1), n_e - 1).astype(jnp.int32)
    n_used = (pend[-1:] // MOE_ROWS).astype(jnp.int32)

    xs = _dispatch(x, norm, dest_blocks, n_rows, seq=seq)
    yb = _experts(xs, blk_expert, n_used, layer, w_gu, b_gu, w_down, b_down)
    return _combine(yb, dest_blocks, gate_t.T, x, g2, seq=seq)


def _final_norm_kernel(x_ref, g_ref, o_ref):
    o_ref[...] = _rms(x_ref[...]) * g_ref[...]


def _final_norm(x, g, tm=1024):
    t, d = x.shape
    return pl.pallas_call(
        _final_norm_kernel,
        grid=(t // tm,),
        in_specs=[pl.BlockSpec((tm, d), lambda i: (i, 0)), pl.BlockSpec((1, d), lambda i: (0, 0))],
        out_specs=pl.BlockSpec((tm, d), lambda i: (i, 0)),
        out_shape=jax.ShapeDtypeStruct((t, d), F32),
        compiler_params=_cparams(("parallel",)),
        name="final_norm",
    )(x, g.reshape(1, d))


def _dup_heads(w, n_heads):
    d = w.shape[0]
    return jnp.repeat(w.reshape(d, n_heads, 1, HEAD_DIM), 2, axis=2).reshape(d, n_heads * LANES)


def _mixer_mla_swa(x, norm, g1, cos, sin, w_in, g_q, w_qb, g_kv, w_kvb, sink, w_out, *, bsz, seq):
    d = x.shape[1]
    o = np.cumsum((0, MLA_Q_LORA, MLA_KV_LORA, MLA_ROPE, SWA_Q_HEADS * HEAD_DIM,
                   SWA_KV_HEADS * HEAD_DIM, SWA_KV_HEADS * HEAD_DIM))
    w_cq, w_ckv, w_kr, w_qs, w_ks, w_vs = (w_in[:, o[i]:o[i + 1]] for i in range(6))
    w_lat = jnp.concatenate([w_cq, jnp.zeros((d, LANES), F32), w_ckv], axis=1).astype(BF16)
    lat = _proj([x], [w_lat], prologue="ada", norm=norm, seq=seq, tm=512, tn=w_lat.shape[1],
                out_dtype=F32, name="proj_latent")
    w_b = jnp.concatenate([w_qs, _dup_heads(w_ks, SWA_KV_HEADS), _dup_heads(w_vs, SWA_KV_HEADS),
                           w_kr, jnp.zeros((d, LANES - MLA_ROPE), F32)], axis=1).astype(BF16)
    qkv_b = _proj([x], [w_b], prologue="ada", norm=norm, seq=seq, tm=512, tn=w_b.shape[1], out_dtype=BF16,
                  rope=(cos, sin), rope_pattern=(1,) * 10 + (0, 0) + (1,), name="proj_swa")
    kr_col = (w_b.shape[1] - LANES) // LANES
    qd = MLA_NOPE + MLA_ROPE
    w_q = w_qb.reshape(MLA_Q_LORA, MLA_HEADS, qd)
    w_q = jnp.concatenate([w_q, jnp.zeros((MLA_Q_LORA, MLA_HEADS, 2 * LANES - qd), F32)], axis=2)
    w_q = w_q.reshape(MLA_Q_LORA, MLA_HEADS * 2 * LANES).astype(BF16)
    q_cat = _proj([lat], [w_q], prologue="rms", norm=(g_q.reshape(1, -1),), lhs_cols=(MLA_Q_LORA, 0),
                  seq=seq, tm=512, tn=w_q.shape[1], out_dtype=BF16, rope=(cos, sin),
                  rope_pattern=(0, 1) * MLA_HEADS, out_scale=qd ** -0.5, name="proj_mla_q")
    w_kv = w_kvb.reshape(MLA_KV_LORA, MLA_HEADS, MLA_NOPE + MLA_V)
    w_kv = jnp.concatenate([w_kv[:, :, :MLA_NOPE].reshape(MLA_KV_LORA, -1),
                            w_kv[:, :, MLA_NOPE:].reshape(MLA_KV_LORA, -1)], axis=1).astype(BF16)
    kv = _proj([lat], [w_kv], prologue="rms", norm=(g_kv.reshape(1, -1),), lhs_cols=(MLA_KV_LORA, 2),
               seq=seq, tm=512, tn=w_kv.shape[1], out_dtype=BF16, name="proj_mla_kv")
    o_a = _mla_attention(q_cat, kv, qkv_b, kr_col, bsz=bsz, seq=seq, tq=256)
    qkv_b4 = qkv_b.reshape(bsz, 1, seq, qkv_b.shape[1])
    (o_b,) = _banded_attention(qkv_b4, qkv_b4, qkv_b4, bsz=bsz, seq=seq, dil=1, radius=SWA_RADIUS,
                               q_col=0, k_col=4, v_col=5, kv_width=2 * LANES,
                               pairs_per_kv=SWA_Q_HEADS // SWA_KV_HEADS // 2, sink=sink, want_lse=False,
                               name="swa_attention")
    o_b = o_b.reshape(bsz * seq, SWA_Q_HEADS * HEAD_DIM)
    na = MLA_HEADS * MLA_V
    return _proj([o_a, o_b], [w_out[:na].astype(BF16), w_out[na:].astype(BF16)], prologue="plain",
                 seq=seq, tm=512, tn=d, out_dtype=F32, residual=(x, g1), name="proj_out_ab")


def _mixer_dilated(x, norm, g1, cos, sin, w_in, w_out, *, bsz, seq):
    d = x.shape[1]
    qw = DIL_HEADS * HEAD_DIM
    ng = len(DIL_PATTERN)
    w_bf = w_in.astype(BF16)
    outs, lses = [], []
    for g, (window, dil) in enumerate(DIL_PATTERN):
        qkv = _proj([x], [w_bf[:, g * 3 * qw:(g + 1) * 3 * qw]], prologue="ada", norm=norm, seq=seq, tm=512,
                    tn=qw, out_dtype=BF16, rope=(cos, sin), rope_pattern=(1,) * (qw // LANES), rope_tiles=2,
                    out_dil=dil, name=f"proj_dil_{dil}")
        qkv = qkv.reshape(bsz, dil, seq // dil, 3 * qw)
        o, lse = _banded_attention(qkv, qkv, qkv, bsz=bsz, seq=seq, dil=dil, radius=window // (2 * dil),
                                   q_col=0, k_col=1, v_col=2, kv_width=qw, pairs_per_kv=1,
                                   want_lse=True, name=f"dilated_attention_{dil}")
        outs.append(o)
        lses.append(lse)
    return _proj(outs + lses, [w_out.astype(BF16)], prologue="dil", seq=seq, tm=512, tn=d, out_dtype=F32,
                 residual=(x, g1), in_dils=tuple(dil for _, dil in DIL_PATTERN), name="proj_out_c")


def kernel(x, c, positions, w_mod, b_mod, g_norm_mix, g_norm_ffn, w_in_ab, mla_g_q, mla_w_qb, mla_g_kv,
           mla_w_kvb, swa_sink, w_out_ab, w_in_c, w_out_c, w_router, b_router, w_gu, b_gu, w_down, b_down,
           g_final):
    bsz, seq, d = x.shape
    depth = w_mod.shape[0]
    cos, sin = _rope_tables(positions)
    mod = _modulation(c, w_mod, b_mod)
    xt = x.reshape(bsz * seq, d)
    for layer in range(depth):
        sh1, sc1, g1, sh2, sc2, g2 = (mod[layer, :, i * d:(i + 1) * d].reshape(bsz, 1, d) for i in range(6))
        li = layer // 2
        norm = (g_norm_mix[layer].reshape(1, d), sc1, sh1)
        if layer % 2 == 0:
            xt = _mixer_mla_swa(xt, norm, g1, cos, sin, w_in_ab[li], mla_g_q[li], mla_w_qb[li], mla_g_kv[li],
                                mla_w_kvb[li], swa_sink[li], w_out_ab[li], bsz=bsz, seq=seq)
        else:
            xt = _mixer_dilated(xt, norm, g1, cos, sin, w_in_c[li], w_out_c[li], bsz=bsz, seq=seq)
        norm = (g_norm_ffn[layer].reshape(1, d), sc2, sh2)
        xt = _moe(xt, norm, g2, w_router[layer], b_router[layer], layer, w_gu, b_gu, w_down, b_down, seq=seq)
    return _final_norm(xt, g_final).reshape(bsz, seq, d)
```

```python
import functools

import jax
import jax.numpy as jnp
import numpy as np
from jax import lax
from jax.experimental import pallas as pl
from jax.experimental.pallas import tpu as pltpu
from jax.experimental.pallas import tpu_sc as plsc

F32 = jnp.float32
BF16 = jnp.bfloat16

EPS = 1e-6
ROPE_THETA = 10000.0
LANES = 128
HEAD_DIM = 64
ROPE_HALF = HEAD_DIM // 2

MLA_HEADS = 8
MLA_Q_LORA = 384
MLA_KV_LORA = 256
MLA_NOPE = 128
MLA_ROPE = 64
MLA_V = 128
SWA_Q_HEADS = 16
SWA_KV_HEADS = 2
SWA_RADIUS = 128
DIL_PATTERN = ((128, 1), (512, 4), (2048, 16))
DIL_HEADS = 16
N_EXPERTS = 32
TOP_K = 4
SWIGLU_LIMIT = 7.0
SWIGLU_ALPHA = 1.702
MOE_ROWS = 256

VMEM_LIMIT = 48 * 1024 * 1024
EXPERT_VMEM_LIMIT = 56 * 1024 * 1024


def _cparams(sem):
    return pltpu.CompilerParams(dimension_semantics=sem, vmem_limit_bytes=VMEM_LIMIT)


def _mod_kernel(c_ref, w_ref, b_ref, o_ref):
    c = c_ref[...]
    ca = c * (1.0 / (1.0 + jnp.exp(-c)))
    o_ref[...] = jnp.dot(ca, w_ref[...], preferred_element_type=F32,
                         precision=lax.Precision.HIGHEST) + b_ref[...]


def _modulation(c, w_mod, b_mod):
    depth, d, n = w_mod.shape
    bsz = c.shape[0]
    tn = 1536
    return pl.pallas_call(
        _mod_kernel,
        grid=(depth, n // tn),
        in_specs=[pl.BlockSpec((bsz, d), lambda l, j: (0, 0)),
                  pl.BlockSpec((None, d, tn), lambda l, j: (l, 0, j)),
                  pl.BlockSpec((None, 1, tn), lambda l, j: (l, 0, j))],
        out_specs=pl.BlockSpec((None, bsz, tn), lambda l, j: (l, 0, j)),
        out_shape=jax.ShapeDtypeStruct((depth, bsz, n), F32),
        compiler_params=_cparams(("parallel", "parallel")),
        name="modulation",
    )(c, w_mod, b_mod.reshape(depth, 1, n))


def _rope_table_kernel(pos_ref, inv_ref, sign_ref, cos_ref, sin_ref):
    ang = inv_ref[...] * pos_ref[...].astype(F32)
    cos_ref[...] = jnp.transpose(jnp.cos(ang))
    sin_ref[...] = jnp.transpose(jnp.sin(ang) * sign_ref[...])


def _rope_tables(positions):
    bsz, s = positions.shape
    inv = ROPE_THETA ** (-jnp.arange(0, HEAD_DIM, 2, dtype=F32) / HEAD_DIM)
    inv128 = jnp.tile(inv, LANES // ROPE_HALF).reshape(LANES, 1)
    sign = jnp.where((jnp.arange(LANES) % HEAD_DIM) < ROPE_HALF, -1.0, 1.0).astype(F32).reshape(LANES, 1)
    out = jax.ShapeDtypeStruct((bsz * s, LANES), F32)
    return pl.pallas_call(
        _rope_table_kernel,
        grid=(bsz,),
        in_specs=[pl.BlockSpec((None, 1, s), lambda b: (b, 0, 0)),
                  pl.BlockSpec((LANES, 1), lambda b: (0, 0)),
                  pl.BlockSpec((LANES, 1), lambda b: (0, 0))],
        out_specs=[pl.BlockSpec((s, LANES), lambda b: (b, 0)),
                   pl.BlockSpec((s, LANES), lambda b: (b, 0))],
        out_shape=[out, out],
        compiler_params=_cparams(("parallel",)),
        name="rope_tables",
    )(positions.reshape(bsz, 1, s), inv128, sign)


def _rms(x):
    return x * lax.rsqrt(jnp.mean(x * x, axis=-1, keepdims=True) + EPS)


def _ada_norm(x, g, sc, sh):
    return (_rms(x) * g) * (1.0 + sc) + sh


def _rope_group(a, cos, sin, first_half):
    rot = jnp.where(first_half, pltpu.roll(a, LANES - ROPE_HALF, 1), pltpu.roll(a, ROPE_HALF, 1))
    return a * cos + rot * sin


def _proj_kernel(*refs, prologue, n_lhs, rope_pattern, rope_tiles, out_scale, residual, in_dils, out_dil):
    pos = 0
    if prologue == "ada":
        x_ref, g_ref, sc_ref, sh_ref = refs[:4]
        pos = 4
    elif prologue == "rms":
        x_ref, g_ref = refs[:2]
        pos = 2
    elif prologue == "dil":
        o_refs = refs[0:3]
        l_refs = refs[3:6]
        pos = 6
    else:
        x_refs = refs[:n_lhs]
        pos = n_lhs
    w_refs = refs[pos:pos + n_lhs]
    pos += n_lhs
    if rope_pattern is not None:
        cos_ref, sin_ref = refs[pos:pos + 2]
        pos += 2
    if residual:
        res_ref, gate_ref = refs[pos:pos + 2]
        pos += 2
    o_ref = refs[pos]
    pos += 1
    h_ref = refs[pos] if prologue != "plain" else None
    slab_ref = refs[pos + 1] if (prologue == "dil" or out_dil > 1) else None

    if prologue != "plain":
        @pl.when(pl.program_id(1) == 0)
        def _():
            if prologue == "ada":
                h_ref[...] = _ada_norm(x_ref[...], g_ref[...], sc_ref[...], sh_ref[...]).astype(BF16)
            elif prologue == "rms":
                h_ref[...] = (_rms(x_ref[...]) * g_ref[...]).astype(BF16)
            else:
                tm = h_ref.shape[0]
                for c in range(h_ref.shape[1] // LANES):
                    cols = slice(c * LANES, (c + 1) * LANES)
                    vals = []
                    for i, (ref, dil) in enumerate(zip(o_refs + l_refs, in_dils + in_dils)):
                        if dil == 1:
                            vals.append(ref[0, :, cols].astype(F32))
                        else:
                            for r in range(dil):
                                slab_ref[i, pl.ds(r, tm // dil, stride=dil), :] = ref[r, :, cols].astype(F32)
                            vals.append(slab_ref[i])
                    o0, o1, o2, l0, l1, l2 = vals
                    m = jnp.maximum(jnp.maximum(l0, l1), l2)
                    e0, e1, e2 = jnp.exp(l0 - m), jnp.exp(l1 - m), jnp.exp(l2 - m)
                    h_ref[:, cols] = ((e0 * o0 + e1 * o1 + e2 * o2) / (e0 + e1 + e2)).astype(BF16)
        acc = jnp.dot(h_ref[...], w_refs[0][...], preferred_element_type=F32)
    else:
        acc = jnp.dot(x_refs[0][...], w_refs[0][...], preferred_element_type=F32)
        for xr, wr in zip(x_refs[1:], w_refs[1:]):
            acc = acc + jnp.dot(xr[...], wr[...], preferred_element_type=F32)

    if out_scale != 1.0:
        acc = acc * out_scale
    if residual:
        o_ref[...] = (res_ref[...] + gate_ref[...] * acc).astype(o_ref.dtype)
        return

    def epilogue(pattern):
        tm = acc.shape[0]
        if any(pattern):
            cos = cos_ref[...]
            sin = sin_ref[...]
            lane = lax.broadcasted_iota(jnp.int32, cos.shape, 1)
            first_half = (lane % HEAD_DIM) < ROPE_HALF
        for c, flag in enumerate(pattern):
            cols = slice(c * LANES, (c + 1) * LANES)
            a = acc[:, cols]
            if flag:
                a = _rope_group(a, cos, sin, first_half)
            if out_dil == 1:
                o_ref[:, cols] = a.astype(o_ref.dtype)
            else:
                slab_ref[0] = a
                for r in range(out_dil):
                    o_ref[r, :, cols] = slab_ref[0, pl.ds(r, tm // out_dil, stride=out_dil), :].astype(o_ref.dtype)

    n_groups = acc.shape[1] // LANES
    if rope_pattern is None:
        epilogue((0,) * n_groups)
    elif rope_tiles is None:
        epilogue(rope_pattern)
    else:
        @pl.when(pl.program_id(1) < rope_tiles)
        def _():
            epilogue(rope_pattern)

        @pl.when(pl.program_id(1) >= rope_tiles)
        def _():
            epilogue((0,) * n_groups)


def _proj(lhs, ws, *, prologue, seq, tm, tn, out_dtype, norm=None, lhs_cols=None, rope=None,
          rope_pattern=None, rope_tiles=None, out_scale=1.0, residual=None, in_dils=None, out_dil=1, name):
    k, n = ws[0].shape
    bsz_seq = lhs[0].shape[0] if prologue != "dil" else lhs[0].shape[0] * seq
    t = bsz_seq
    assert t % tm == 0 and n % tn == 0 and seq % tm == 0
    bpb = seq // tm
    n_lhs = len(ws)
    args, specs = [], []
    row = lambda i, j: (i, 0)
    if prologue in ("ada", "rms"):
        kw, kc = lhs_cols if lhs_cols is not None else (k, 0)
        args.append(lhs[0])
        specs.append(pl.BlockSpec((tm, kw), lambda i, j: (i, kc)))
        args.append(norm[0])
        specs.append(pl.BlockSpec((1, k), lambda i, j: (0, 0)))
        if prologue == "ada":
            for v in norm[1:3]:
                args.append(v)
                specs.append(pl.BlockSpec((None, 1, k), lambda i, j: (i // bpb, 0, 0)))
    elif prologue == "dil":
        for a, dil in zip(lhs, in_dils + in_dils):
            args.append(a)
            specs.append(pl.BlockSpec((None, dil, tm // dil, k), lambda i, j: (i // bpb, 0, i % bpb, 0)))
    else:
        for a, w in zip(lhs, ws):
            args.append(a)
            specs.append(pl.BlockSpec((tm, w.shape[0]), row))
    for w in ws:
        args.append(w)
        specs.append(pl.BlockSpec((w.shape[0], tn), lambda i, j: (0, j)))
    if rope_pattern is not None:
        assert len(rope_pattern) == tn // LANES
        for tab in rope:
            args.append(tab)
            specs.append(pl.BlockSpec((tm, LANES), row))
    if residual is not None:
        res, gate = residual
        args.append(res)
        specs.append(pl.BlockSpec((tm, tn), lambda i, j: (i, j)))
        args.append(gate)
        specs.append(pl.BlockSpec((None, 1, tn), lambda i, j: (i // bpb, 0, j)))
    scratch = [] if prologue == "plain" else [pltpu.VMEM((tm, k), BF16)]
    if prologue == "dil":
        scratch.append(pltpu.VMEM((2 * len(in_dils), tm, LANES), F32))
    elif out_dil > 1:
        scratch.append(pltpu.VMEM((1, tm, LANES), F32))
    if out_dil == 1:
        out_spec = pl.BlockSpec((tm, tn), lambda i, j: (i, j))
        out_shape = jax.ShapeDtypeStruct((t, n), out_dtype)
    else:
        out_spec = pl.BlockSpec((None, out_dil, tm // out_dil, tn), lambda i, j: (i // bpb, 0, i % bpb, j))
        out_shape = jax.ShapeDtypeStruct((t // seq, out_dil, seq // out_dil, n), out_dtype)
    kern = functools.partial(_proj_kernel, prologue=prologue, n_lhs=n_lhs, rope_pattern=rope_pattern,
                             rope_tiles=rope_tiles, out_scale=out_scale, residual=residual is not None,
                             in_dils=in_dils, out_dil=out_dil)
    return pl.pallas_call(
        kern,
        grid=(t // tm, n // tn),
        in_specs=specs,
        out_specs=out_spec,
        out_shape=out_shape,
        scratch_shapes=scratch,
        compiler_params=_cparams(("parallel", "arbitrary")),
        name=name,
    )(*args)


def _mla_kernel(q_ref, kn_ref, kr_ref, v_ref, o_ref):
    k = jnp.concatenate([kn_ref[...], kr_ref[...]], axis=1)
    s = lax.dot_general(q_ref[...], k, (((1,), (1,)), ((), ())), preferred_element_type=F32)
    m = jnp.max(s, axis=1, keepdims=True)
    p = jnp.exp(s - m)
    den = jnp.sum(p, axis=1, keepdims=True)
    o = jnp.dot(p.astype(BF16), v_ref[...], preferred_element_type=F32)
    o_ref[...] = (o / den).astype(o_ref.dtype)


def _mla_attention(q_cat, kv, qkv_b, kr_col, *, bsz, seq, tq):
    t = bsz * seq
    nq = seq // tq
    return pl.pallas_call(
        _mla_kernel,
        grid=(bsz, MLA_HEADS, nq),
        in_specs=[pl.BlockSpec((tq, 2 * LANES), lambda b, h, i: (b * nq + i, h)),
                  pl.BlockSpec((seq, LANES), lambda b, h, i: (b, h)),
                  pl.BlockSpec((seq, LANES), lambda b, h, i: (b, kr_col)),
                  pl.BlockSpec((seq, LANES), lambda b, h, i: (b, MLA_HEADS + h))],
        out_specs=pl.BlockSpec((tq, LANES), lambda b, h, i: (b * nq + i, h)),
        out_shape=jax.ShapeDtypeStruct((t, MLA_HEADS * MLA_V), BF16),
        compiler_params=_cparams(("parallel", "parallel", "parallel")),
        name="mla_attention",
    )(q_cat, kv, qkv_b, kv)


def _banded_kernel(*refs, radius, tq, hb, length, n_pairs, pairs_per_kv, has_sink, want_lse):
    pos = 0
    if has_sink:
        sink_ref = refs[0]
        pos = 1
    q_ref, kp_ref, kc_ref, kn_ref, vp_ref, vc_ref, vn_ref = refs[pos:pos + 7]
    pos += 7
    o_ref = refs[pos]
    lse_ref = refs[pos + 1] if want_lse else None

    j = pl.program_id(2)
    tk = tq + 2 * hb
    kw = jnp.concatenate([kp_ref[...], kc_ref[...], kn_ref[...]], axis=0)
    vw = jnp.concatenate([vp_ref[...], vc_ref[...], vn_ref[...]], axis=0)
    qpos = j * tq + lax.broadcasted_iota(jnp.int32, (tq, tk), 0)
    kpos = j * tq - hb + lax.broadcasted_iota(jnp.int32, (tq, tk), 1)
    valid = (jnp.abs(qpos - kpos) <= radius) & (kpos >= 0) & (kpos < length)
    lane = lax.broadcasted_iota(jnp.int32, (tq, LANES), 1)
    low = lane < HEAD_DIM
    scale = HEAD_DIM ** -0.5

    for p in range(n_pairs):
        c = p // pairs_per_kv
        qp = q_ref[:, p * LANES:(p + 1) * LANES] * scale
        kp = kw[:, c * LANES:(c + 1) * LANES]
        vp = vw[:, c * LANES:(c + 1) * LANES]
        halves = []
        for half in range(2):
            qh = jnp.where(low if half == 0 else ~low, qp, jnp.zeros_like(qp))
            s = lax.dot_general(qh, kp, (((1,), (1,)), ((), ())), preferred_element_type=F32)
            s = jnp.where(valid, s, -jnp.inf)
            m = jnp.max(s, axis=1, keepdims=True)
            if has_sink:
                sk = sink_ref[2 * p + half]
                m = jnp.maximum(m, sk)
                e = jnp.exp(s - m)
                den = jnp.sum(e, axis=1, keepdims=True) + jnp.exp(sk - m)
            else:
                e = jnp.exp(s - m)
                den = jnp.sum(e, axis=1, keepdims=True)
            o = jnp.dot(e.astype(BF16), vp, preferred_element_type=F32) / den
            halves.append((o, m + jnp.log(den) if want_lse else None))
        o_ref[:, p * LANES:(p + 1) * LANES] = jnp.where(low, halves[0][0], halves[1][0]).astype(o_ref.dtype)
        if want_lse:
            lse_ref[:, p * LANES:(p + 1) * LANES] = jnp.where(
                low, jnp.broadcast_to(halves[0][1], (tq, LANES)), jnp.broadcast_to(halves[1][1], (tq, LANES)))


def _banded_attention(q_arr, k_arr, v_arr, *, bsz, seq, dil, radius, q_col, k_col, v_col, kv_width,
                      pairs_per_kv, sink=None, want_lse, name):
    length = seq // dil
    hb = radius
    tq = min(256, length)
    assert tq % hb == 0 and length % tq == 0
    qw = DIL_HEADS * HEAD_DIM
    n_pairs = qw // LANES
    rep = tq // hb
    last_hb = length // hb - 1

    def prev_spec(col):
        return pl.BlockSpec((None, None, hb, kv_width),
                            lambda b, r, j: (b, r, jnp.maximum(j * rep - 1, 0), col))

    def cur_spec(col):
        return pl.BlockSpec((None, None, tq, kv_width), lambda b, r, j: (b, r, j, col))

    def next_spec(col):
        return pl.BlockSpec((None, None, hb, kv_width),
                            lambda b, r, j: (b, r, jnp.minimum((j + 1) * rep, last_hb), col))

    in_specs = [pl.BlockSpec((None, None, tq, qw), lambda b, r, j: (b, r, j, q_col)),
                prev_spec(k_col), cur_spec(k_col), next_spec(k_col),
                prev_spec(v_col), cur_spec(v_col), next_spec(v_col)]
    args = [q_arr, k_arr, k_arr, k_arr, v_arr, v_arr, v_arr]
    if sink is not None:
        in_specs = [pl.BlockSpec(memory_space=pltpu.SMEM)] + in_specs
        args = [sink] + args
    out_spec = pl.BlockSpec((None, None, tq, qw), lambda b, r, j: (b, r, j, 0))
    out_specs = [out_spec]
    out_shape = [jax.ShapeDtypeStruct((bsz, dil, length, qw), BF16)]
    if want_lse:
        out_specs.append(out_spec)
        out_shape.append(jax.ShapeDtypeStruct((bsz, dil, length, qw), F32))
    kern = functools.partial(_banded_kernel, radius=radius, tq=tq, hb=hb, length=length, n_pairs=n_pairs,
                             pairs_per_kv=pairs_per_kv, has_sink=sink is not None, want_lse=want_lse)
    return pl.pallas_call(
        kern,
        grid=(bsz, dil, length // tq),
        in_specs=in_specs,
        out_specs=out_specs,
        out_shape=out_shape,
        compiler_params=_cparams(("parallel", "parallel", "parallel")),
        name=name,
    )(*args)


def _router_kernel(x_ref, g_ref, sc_ref, sh_ref, wt_ref, b_ref, tri_ref,
                   idx_ref, gate_ref, rank_ref, cnt_ref, h_ref, carry_ref):
    @pl.when(pl.program_id(0) == 0)
    def _():
        carry_ref[...] = jnp.zeros_like(carry_ref)

    h = _ada_norm(x_ref[...], g_ref[...], sc_ref[...], sh_ref[...])
    h_ref[...] = h
    lt = lax.dot_general(wt_ref[...], h, (((1,), (1,)), ((), ())), preferred_element_type=F32,
                         precision=lax.Precision.HIGHEST) + b_ref[...]
    n_e, tm = lt.shape
    e_iota = lax.broadcasted_iota(jnp.int32, (n_e, tm), 0)
    vals, sels = [], []
    for k in range(TOP_K):
        m = jnp.max(lt, axis=0, keepdims=True)
        idx = jnp.min(jnp.where(lt == m, e_iota, n_e), axis=0, keepdims=True)
        sel = e_iota == idx
        idx_ref[k:k + 1, :] = idx
        vals.append(m)
        sels.append(sel)
        lt = jnp.where(sel, -jnp.inf, lt)
    exps = [jnp.exp(v - vals[0]) for v in vals]
    den = exps[0] + exps[1] + exps[2] + exps[3]
    for k in range(TOP_K):
        gate_ref[k:k + 1, :] = exps[k] / den

    onehot = jnp.where(sels[0] | sels[1] | sels[2] | sels[3], 1.0, 0.0)
    carry = carry_ref[:, 0:1]
    before = jnp.dot(onehot.astype(BF16), tri_ref[...], preferred_element_type=F32) + carry
    for k in range(TOP_K):
        rank_ref[k:k + 1, :] = jnp.sum(jnp.where(sels[k], before, 0.0), axis=0, keepdims=True).astype(jnp.int32)
    total = carry + jnp.sum(onehot, axis=1, keepdims=True)
    carry_ref[...] = jnp.broadcast_to(total, carry_ref.shape)
    cnt_ref[...] = jnp.broadcast_to(total, cnt_ref.shape)


def _router(x, norm, w_router, b_router, *, seq, tm=512):
    t, d = x.shape
    n_e = w_router.shape[1]
    bpb = seq // tm
    tri = (jnp.arange(tm)[:, None] < jnp.arange(tm)[None, :]).astype(BF16)
    g, sc, sh = norm
    out_i = jax.ShapeDtypeStruct((TOP_K, t), jnp.int32)
    tok = pl.BlockSpec((TOP_K, tm), lambda i: (0, i))
    return pl.pallas_call(
        _router_kernel,
        grid=(t // tm,),
        in_specs=[pl.BlockSpec((tm, d), lambda i: (i, 0)),
                  pl.BlockSpec((1, d), lambda i: (0, 0)),
                  pl.BlockSpec((None, 1, d), lambda i: (i // bpb, 0, 0)),
                  pl.BlockSpec((None, 1, d), lambda i: (i // bpb, 0, 0)),
                  pl.BlockSpec((n_e, d), lambda i: (0, 0)),
                  pl.BlockSpec((n_e, 1), lambda i: (0, 0)),
                  pl.BlockSpec((tm, tm), lambda i: (0, 0))],
        out_specs=[tok, tok, tok, pl.BlockSpec((n_e, LANES), lambda i: (0, 0)),
                   pl.BlockSpec((tm, d), lambda i: (i, 0))],
        out_shape=[out_i, jax.ShapeDtypeStruct((TOP_K, t), F32), out_i,
                   jax.ShapeDtypeStruct((n_e, LANES), F32), jax.ShapeDtypeStruct((t, d), F32)],
        scratch_shapes=[pltpu.VMEM((n_e, LANES), F32)],
        compiler_params=_cparams(("arbitrary",)),
        name="router",
    )(x, g, sc, sh, w_router.T, b_router.reshape(n_e, 1), tri)


SC_CORES = 2
SC_SUBCORES = 16
SC_WORKERS = SC_CORES * SC_SUBCORES
SC_ROWS = 32
SC_INDEX_WINDOW = 128


def _sc_worker_id():
    return lax.axis_index("subcore") * SC_CORES + lax.axis_index("core")


def _sc_mesh():
    return plsc.VectorSubcoreMesh(core_axis_name="core", subcore_axis_name="subcore")


def _sc_dispatch(h, dest_flat, pad_rows):
    t, d = h.shape
    n_pad = pad_rows.shape[0]
    n_rows = TOP_K * t + n_pad
    tok_w = t // SC_WORKERS
    pad_w = n_pad // SC_WORKERS
    assert tok_w % SC_ROWS == 0 and pad_w % SC_ROWS == 0

    @functools.partial(
        pl.kernel, out_type=jax.ShapeDtypeStruct((n_rows, d), h.dtype), mesh=_sc_mesh(),
        scratch_types=[pltpu.VMEM((SC_ROWS,), jnp.int32), pltpu.VMEM((SC_ROWS, d), h.dtype)])
    def scatter_rows(h_hbm, dest_hbm, pad_hbm, zero_hbm, xs_hbm, idx_v, rows_v):
        wid = _sc_worker_id()

        @pl.loop(0, tok_w // SC_ROWS)
        def _(b):
            t0 = wid * tok_w + b * SC_ROWS
            pltpu.sync_copy(h_hbm.at[pl.ds(t0, SC_ROWS)], rows_v)
            for k in range(TOP_K):
                pltpu.sync_copy(dest_hbm.at[pl.ds(k * t + t0, SC_ROWS)], idx_v)
                pltpu.sync_copy(rows_v, xs_hbm.at[idx_v])

        pltpu.sync_copy(zero_hbm, rows_v)

        @pl.loop(0, pad_w // SC_ROWS)
        def _(b):
            pltpu.sync_copy(pad_hbm.at[pl.ds(wid * pad_w + b * SC_ROWS, SC_ROWS)], idx_v)
            pltpu.sync_copy(rows_v, xs_hbm.at[idx_v])

    return scatter_rows(h, dest_flat, pad_rows, jnp.zeros((SC_ROWS, d), h.dtype))


def _sc_gather(table, idx):
    n = idx.shape[0]
    d = table.shape[1]
    per_w = n // SC_WORKERS
    assert per_w % SC_INDEX_WINDOW == 0

    @functools.partial(
        pl.kernel, out_type=jax.ShapeDtypeStruct((n, d), table.dtype), mesh=_sc_mesh(),
        scratch_types=[pltpu.VMEM((SC_INDEX_WINDOW,), jnp.int32), pltpu.VMEM((SC_ROWS, d), table.dtype)])
    def gather_rows(table_hbm, idx_hbm, out_hbm, idx_v, rows_v):
        base = _sc_worker_id() * per_w

        @pl.loop(0, per_w // SC_INDEX_WINDOW)
        def _(b):
            off = base + b * SC_INDEX_WINDOW
            pltpu.sync_copy(idx_hbm.at[pl.ds(off, SC_INDEX_WINDOW)], idx_v)
            for c in range(SC_INDEX_WINDOW // SC_ROWS):
                pltpu.sync_copy(table_hbm.at[idx_v.at[pl.ds(c * SC_ROWS, SC_ROWS)]], rows_v)
                pltpu.sync_copy(rows_v, out_hbm.at[pl.ds(off + c * SC_ROWS, SC_ROWS)])

    return gather_rows(table, idx)


def _expert_kernel(be_ref, nu_ref, xs_ref, wgu_ref, bgu_ref, wd_ref, bd_ref, y_ref, wgu_bf, wd_bf):
    i = pl.program_id(0)
    used = i < nu_ref[0]

    @pl.when(used & ((i == 0) | (be_ref[i] != be_ref[jnp.maximum(i - 1, 0)])))
    def _():
        wgu_bf[...] = wgu_ref[...].astype(BF16)
        wd_bf[...] = wd_ref[...].astype(BF16)

    @pl.when(used)
    def _():
        d_ff = wd_ref.shape[0]
        gu = jnp.dot(xs_ref[...].astype(BF16), wgu_bf[...], preferred_element_type=F32) + bgu_ref[...]
        glu = jnp.minimum(gu[:, :d_ff], SWIGLU_LIMIT)
        lin = jnp.clip(gu[:, d_ff:], -SWIGLU_LIMIT, SWIGLU_LIMIT)
        act = glu * (1.0 / (1.0 + jnp.exp(-SWIGLU_ALPHA * glu))) * (lin + 1.0)
        y_ref[...] = jnp.dot(act.astype(BF16), wd_bf[...], preferred_element_type=F32) + bd_ref[...]

    @pl.when(jnp.logical_not(used))
    def _():
        y_ref[...] = jnp.zeros_like(y_ref)


def _experts(xs, blk_expert, n_used, layer, w_gu, b_gu, w_down, b_down):
    n_rows, d = xs.shape
    _, n_e, _, f2 = w_gu.shape
    n_blocks = n_rows // MOE_ROWS

    def blk(i, be, nu):
        return jnp.minimum(i, nu[0] - 1)

    def wmap(i, be, nu):
        return (layer, be[blk(i, be, nu)], 0, 0)

    grid_spec = pltpu.PrefetchScalarGridSpec(
        num_scalar_prefetch=2,
        grid=(n_blocks,),
        in_specs=[pl.BlockSpec((MOE_ROWS, d), lambda i, be, nu: (blk(i, be, nu), 0)),
                  pl.BlockSpec((None, None, d, f2), wmap),
                  pl.BlockSpec((None, None, 1, f2), wmap),
                  pl.BlockSpec((None, None, f2 // 2, d), wmap),
                  pl.BlockSpec((None, None, 1, d), wmap)],
        out_specs=pl.BlockSpec((MOE_ROWS, d), lambda i, be, nu: (i, 0)),
        scratch_shapes=[pltpu.VMEM((d, f2), BF16), pltpu.VMEM((f2 // 2, d), BF16)],
    )
    depth = w_gu.shape[0]
    return pl.pallas_call(
        _expert_kernel,
        grid_spec=grid_spec,
        out_shape=jax.ShapeDtypeStruct((n_rows, d), F32),
        compiler_params=pltpu.CompilerParams(dimension_semantics=("arbitrary",),
                                             vmem_limit_bytes=EXPERT_VMEM_LIMIT),
        name="moe_experts",
    )(blk_expert, n_used, xs, w_gu, b_gu.reshape(depth, n_e, 1, f2), w_down, b_down.reshape(depth, n_e, 1, d))


def _combine_kernel(y0_ref, y1_ref, y2_ref, y3_ref, gate_ref, x_ref, g2_ref, o_ref):
    gate = gate_ref[...]
    y = gate[:, 0:1] * y0_ref[...]
    for k, y_ref in enumerate((y1_ref, y2_ref, y3_ref), start=1):
        y = y + gate[:, k:k + 1] * y_ref[...]
    o_ref[...] = x_ref[...] + g2_ref[...] * y


def _combine(y4, gate, x, g2, *, seq, tm=512):
    t, d = x.shape
    bpb = seq // tm
    nblk = t // tm
    y_specs = [pl.BlockSpec((tm, d), functools.partial(lambda i, k: (k * nblk + i, 0), k=k)) for k in range(TOP_K)]
    return pl.pallas_call(
        _combine_kernel,
        grid=(nblk,),
        in_specs=y_specs + [pl.BlockSpec((tm, TOP_K), lambda i: (i, 0)),
                            pl.BlockSpec((tm, d), lambda i: (i, 0)),
                            pl.BlockSpec((None, 1, d), lambda i: (i // bpb, 0, 0))],
        out_specs=pl.BlockSpec((tm, d), lambda i: (i, 0)),
        out_shape=jax.ShapeDtypeStruct((t, d), F32),
        compiler_params=_cparams(("parallel",)),
        name="moe_combine",
    )(y4, y4, y4, y4, gate, x, g2)


def _moe(x, norm, g2, w_router, b_router, layer, w_gu, b_gu, w_down, b_down, *, seq):
    t, d = x.shape
    n_e = w_router.shape[1]
    idx_t, gate_t, rank_t, counts, h = _router(x, norm, w_router, b_router, seq=seq)
    e_ids = jnp.arange(n_e, dtype=jnp.int32)
    counts = counts[:, 0].astype(jnp.int32)
    padded = (counts + MOE_ROWS - 1) // MOE_ROWS * MOE_ROWS
    pend = jnp.cumsum(padded)
    pstart = pend - padded
    n_pad = n_e * MOE_ROWS
    n_rows = t * TOP_K + n_pad
    n_blocks = n_rows // MOE_ROWS
    dest_t = jnp.sum(jnp.where(idx_t[:, :, None] == e_ids, pstart, 0), axis=-1) + rank_t
    dest_flat = dest_t.reshape(TOP_K * t)
    blk_start = jnp.arange(n_blocks, dtype=jnp.int32) * MOE_ROWS
    blk_expert = jnp.minimum(jnp.sum(blk_start[:, None] >= pend[None, :], axis=1), n_e - 1).astype(jnp.int32)
    n_used = (pend[-1:] // MOE_ROWS).astype(jnp.int32)
    pad_len = padded - counts
    pad_end = jnp.cumsum(pad_len)
    j = jnp.arange(n_pad, dtype=jnp.int32)
    pe = jnp.sum(j[:, None] >= pad_end[None, :], axis=1)
    sel = jnp.minimum(pe, n_e - 1)[:, None] == e_ids
    in_expert = jnp.sum(jnp.where(sel, pstart + counts - (pad_end - pad_len), 0), axis=-1) + j
    pad_rows = jnp.where(pe < n_e, in_expert, pend[-1] + j - pad_end[-1]).astype(jnp.int32)

    xs = _sc_dispatch(h, dest_flat, pad_rows)
    yb = _experts(xs, blk_expert, n_used, layer, w_gu, b_gu, w_down, b_down)
    y4 = _sc_gather(yb, dest_flat)
    return _combine(y4, gate_t.T, x, g2, seq=seq)


def _final_norm_kernel(x_ref, g_ref, o_ref):
    o_ref[...] = _rms(x_ref[...]) * g_ref[...]


def _final_norm(x, g, tm=1024):
    t, d = x.shape
    return pl.pallas_call(
        _final_norm_kernel,
        grid=(t // tm,),
        in_specs=[pl.BlockSpec((tm, d), lambda i: (i, 0)), pl.BlockSpec((1, d), lambda i: (0, 0))],
        out_specs=pl.BlockSpec((tm, d), lambda i: (i, 0)),
        out_shape=jax.ShapeDtypeStruct((t, d), F32),
        compiler_params=_cparams(("parallel",)),
        name="final_norm",
    )(x, g.reshape(1, d))


def _dup_heads(w, n_heads):
    d = w.shape[0]
    return jnp.repeat(w.reshape(d, n_heads, 1, HEAD_DIM), 2, axis=2).reshape(d, n_heads * LANES)


def _mixer_mla_swa(x, norm, g1, cos, sin, w_in, g_q, w_qb, g_kv, w_kvb, sink, w_out, *, bsz, seq):
    d = x.shape[1]
    o = np.cumsum((0, MLA_Q_LORA, MLA_KV_LORA, MLA_ROPE, SWA_Q_HEADS * HEAD_DIM,
                   SWA_KV_HEADS * HEAD_DIM, SWA_KV_HEADS * HEAD_DIM))
    w_cq, w_ckv, w_kr, w_qs, w_ks, w_vs = (w_in[:, o[i]:o[i + 1]] for i in range(6))
    w_lat = jnp.concatenate([w_cq, jnp.zeros((d, LANES), F32), w_ckv], axis=1).astype(BF16)
    lat = _proj([x], [w_lat], prologue="ada", norm=norm, seq=seq, tm=512, tn=w_lat.shape[1],
                out_dtype=F32, name="proj_latent")
    w_b = jnp.concatenate([w_qs, _dup_heads(w_ks, SWA_KV_HEADS), _dup_heads(w_vs, SWA_KV_HEADS),
                           w_kr, jnp.zeros((d, LANES - MLA_ROPE), F32)], axis=1).astype(BF16)
    qkv_b = _proj([x], [w_b], prologue="ada", norm=norm, seq=seq, tm=512, tn=w_b.shape[1], out_dtype=BF16,
                  rope=(cos, sin), rope_pattern=(1,) * 10 + (0, 0) + (1,), name="proj_swa")
    kr_col = (w_b.shape[1] - LANES) // LANES
    qd = MLA_NOPE + MLA_ROPE
    w_q = w_qb.reshape(MLA_Q_LORA, MLA_HEADS, qd)
    w_q = jnp.concatenate([w_q, jnp.zeros((MLA_Q_LORA, MLA_HEADS, 2 * LANES - qd), F32)], axis=2)
    w_q = w_q.reshape(MLA_Q_LORA, MLA_HEADS * 2 * LANES).astype(BF16)
    q_cat = _proj([lat], [w_q], prologue="rms", norm=(g_q.reshape(1, -1),), lhs_cols=(MLA_Q_LORA, 0),
                  seq=seq, tm=512, tn=w_q.shape[1], out_dtype=BF16, rope=(cos, sin),
                  rope_pattern=(0, 1) * MLA_HEADS, out_scale=qd ** -0.5, name="proj_mla_q")
    w_kv = w_kvb.reshape(MLA_KV_LORA, MLA_HEADS, MLA_NOPE + MLA_V)
    w_kv = jnp.concatenate([w_kv[:, :, :MLA_NOPE].reshape(MLA_KV_LORA, -1),
                            w_kv[:, :, MLA_NOPE:].reshape(MLA_KV_LORA, -1)], axis=1).astype(BF16)
    kv = _proj([lat], [w_kv], prologue="rms", norm=(g_kv.reshape(1, -1),), lhs_cols=(MLA_KV_LORA, 2),
               seq=seq, tm=512, tn=w_kv.shape[1], out_dtype=BF16, name="proj_mla_kv")
    o_a = _mla_attention(q_cat, kv, qkv_b, kr_col, bsz=bsz, seq=seq, tq=256)
    qkv_b4 = qkv_b.reshape(bsz, 1, seq, qkv_b.shape[1])
    (o_b,) = _banded_attention(qkv_b4, qkv_b4, qkv_b4, bsz=bsz, seq=seq, dil=1, radius=SWA_RADIUS,
                               q_col=0, k_col=4, v_col=5, kv_width=2 * LANES,
                               pairs_per_kv=SWA_Q_HEADS // SWA_KV_HEADS // 2, sink=sink, want_lse=False,
                               name="swa_attention")
    o_b = o_b.reshape(bsz * seq, SWA_Q_HEADS * HEAD_DIM)
    na = MLA_HEADS * MLA_V
    return _proj([o_a, o_b], [w_out[:na].astype(BF16), w_out[na:].astype(BF16)], prologue="plain",
                 seq=seq, tm=512, tn=d, out_dtype=F32, residual=(x, g1), name="proj_out_ab")


def _mixer_dilated(x, norm, g1, cos, sin, w_in, w_out, *, bsz, seq):
    d = x.shape[1]
    qw = DIL_HEADS * HEAD_DIM
    ng = len(DIL_PATTERN)
    w_bf = w_in.astype(BF16)
    outs, lses = [], []
    for g, (window, dil) in enumerate(DIL_PATTERN):
        qkv = _proj([x], [w_bf[:, g * 3 * qw:(g + 1) * 3 * qw]], prologue="ada", norm=norm, seq=seq, tm=512,
                    tn=qw, out_dtype=BF16, rope=(cos, sin), rope_pattern=(1,) * (qw // LANES), rope_tiles=2,
                    out_dil=dil, name=f"proj_dil_{dil}")
        qkv = qkv.reshape(bsz, dil, seq // dil, 3 * qw)
        o, lse = _banded_attention(qkv, qkv, qkv, bsz=bsz, seq=seq, dil=dil, radius=window // (2 * dil),
                                   q_col=0, k_col=1, v_col=2, kv_width=qw, pairs_per_kv=1,
                                   want_lse=True, name=f"dilated_attention_{dil}")
        outs.append(o)
        lses.append(lse)
    return _proj(outs + lses, [w_out.astype(BF16)], prologue="dil", seq=seq, tm=512, tn=d, out_dtype=F32,
                 residual=(x, g1), in_dils=tuple(dil for _, dil in DIL_PATTERN), name="proj_out_c")


def kernel(x, c, positions, w_mod, b_mod, g_norm_mix, g_norm_ffn, w_in_ab, mla_g_q, mla_w_qb, mla_g_kv,
           mla_w_kvb, swa_sink, w_out_ab, w_in_c, w_out_c, w_router, b_router, w_gu, b_gu, w_down, b_down,
           g_final):
    bsz, seq, d = x.shape
    depth = w_mod.shape[0]
    cos, sin = _rope_tables(positions)
    mod = _modulation(c, w_mod, b_mod)
    xt = x.reshape(bsz * seq, d)
    for layer in range(depth):
        sh1, sc1, g1, sh2, sc2, g2 = (mod[layer, :, i * d:(i + 1) * d].reshape(bsz, 1, d) for i in range(6))
        li = layer // 2
        norm = (g_norm_mix[layer].reshape(1, d), sc1, sh1)
        if layer % 2 == 0:
            xt = _mixer_mla_swa(xt, norm, g1, cos, sin, w_in_ab[li], mla_g_q[li], mla_w_qb[li], mla_g_kv[li],
                                mla_w_kvb[li], swa_sink[li], w_out_ab[li], bsz=bsz, seq=seq)
        else:
            xt = _mixer_dilated(xt, norm, g1, cos, sin, w_in_c[li], w_out_c[li], bsz=bsz, seq=seq)
        norm = (g_norm_ffn[layer].reshape(1, d), sc2, sh2)
        xt = _moe(xt, norm, g2, w_router[layer], b_router[layer], layer, w_gu, b_gu, w_down, b_down, seq=seq)
    return _final_norm(xt, g_final).reshape(bsz, seq, d)
```

```python
import functools

import jax
import jax.numpy as jnp
import numpy as np
from jax import lax
from jax.experimental import pallas as pl
from jax.experimental.pallas import tpu as pltpu
from jax.experimental.pallas import tpu_sc as plsc

F32 = jnp.float32
BF16 = jnp.bfloat16

EPS = 1e-6
ROPE_THETA = 10000.0
LANES = 128
HEAD_DIM = 64
ROPE_HALF = HEAD_DIM // 2

MLA_HEADS = 8
MLA_Q_LORA = 384
MLA_KV_LORA = 256
MLA_NOPE = 128
MLA_ROPE = 64
MLA_V = 128
SWA_Q_HEADS = 16
SWA_KV_HEADS = 2
SWA_RADIUS = 128
DIL_PATTERN = ((128, 1), (512, 4), (2048, 16))
DIL_HEADS = 16
N_EXPERTS = 32
TOP_K = 4
SWIGLU_LIMIT = 7.0
SWIGLU_ALPHA = 1.702
MOE_ROWS = 512
BAND_SUB_ROWS = 128

VMEM_LIMIT = 48 * 1024 * 1024
EXPERT_VMEM_LIMIT = 56 * 1024 * 1024


def _cparams(sem):
    return pltpu.CompilerParams(dimension_semantics=sem, vmem_limit_bytes=VMEM_LIMIT)


def _mod_kernel(c_ref, w_ref, b_ref, o_ref):
    c = c_ref[...]
    ca = c * (1.0 / (1.0 + jnp.exp(-c)))
    o_ref[...] = jnp.dot(ca, w_ref[...], preferred_element_type=F32,
                         precision=lax.Precision.HIGHEST) + b_ref[...]


def _modulation(c, w_mod, b_mod):
    depth, d, n = w_mod.shape
    bsz = c.shape[0]
    tn = 1536
    return pl.pallas_call(
        _mod_kernel,
        grid=(depth, n // tn),
        in_specs=[pl.BlockSpec((bsz, d), lambda l, j: (0, 0)),
                  pl.BlockSpec((None, d, tn), lambda l, j: (l, 0, j)),
                  pl.BlockSpec((None, 1, tn), lambda l, j: (l, 0, j))],
        out_specs=pl.BlockSpec((None, bsz, tn), lambda l, j: (l, 0, j)),
        out_shape=jax.ShapeDtypeStruct((depth, bsz, n), F32),
        compiler_params=_cparams(("parallel", "parallel")),
        name="modulation",
    )(c, w_mod, b_mod.reshape(depth, 1, n))


def _rope_table_kernel(pos_ref, inv_ref, sign_ref, cos_ref, sin_ref):
    ang = inv_ref[...] * pos_ref[...].astype(F32)
    cos_ref[...] = jnp.transpose(jnp.cos(ang))
    sin_ref[...] = jnp.transpose(jnp.sin(ang) * sign_ref[...])


def _rope_tables(positions):
    bsz, s = positions.shape
    inv = ROPE_THETA ** (-jnp.arange(0, HEAD_DIM, 2, dtype=F32) / HEAD_DIM)
    inv128 = jnp.tile(inv, LANES // ROPE_HALF).reshape(LANES, 1)
    sign = jnp.where((jnp.arange(LANES) % HEAD_DIM) < ROPE_HALF, -1.0, 1.0).astype(F32).reshape(LANES, 1)
    out = jax.ShapeDtypeStruct((bsz * s, LANES), F32)
    return pl.pallas_call(
        _rope_table_kernel,
        grid=(bsz,),
        in_specs=[pl.BlockSpec((None, 1, s), lambda b: (b, 0, 0)),
                  pl.BlockSpec((LANES, 1), lambda b: (0, 0)),
                  pl.BlockSpec((LANES, 1), lambda b: (0, 0))],
        out_specs=[pl.BlockSpec((s, LANES), lambda b: (b, 0)),
                   pl.BlockSpec((s, LANES), lambda b: (b, 0))],
        out_shape=[out, out],
        compiler_params=_cparams(("parallel",)),
        name="rope_tables",
    )(positions.reshape(bsz, 1, s), inv128, sign)


def _rms(x):
    return x * lax.rsqrt(jnp.mean(x * x, axis=-1, keepdims=True) + EPS)


def _ada_norm(x, g, sc, sh):
    return (_rms(x) * g) * (1.0 + sc) + sh


def _rope_group(a, cos, sin, first_half):
    rot = jnp.where(first_half, pltpu.roll(a, LANES - ROPE_HALF, 1), pltpu.roll(a, ROPE_HALF, 1))
    return a * cos + rot * sin


def _proj_kernel(*refs, prologue, n_lhs, rope_pattern, rope_tiles, out_scale, residual, in_dils, out_dil):
    pos = 0
    if prologue == "ada":
        x_ref, g_ref, sc_ref, sh_ref = refs[:4]
        pos = 4
    elif prologue == "rms":
        x_ref, g_ref = refs[:2]
        pos = 2
    elif prologue == "dil":
        o_refs = refs[0:3]
        l_refs = refs[3:6]
        pos = 6
    else:
        x_refs = refs[:n_lhs]
        pos = n_lhs
    w_refs = refs[pos:pos + n_lhs]
    pos += n_lhs
    if rope_pattern is not None:
        cos_ref, sin_ref = refs[pos:pos + 2]
        pos += 2
    if residual:
        res_ref, gate_ref = refs[pos:pos + 2]
        pos += 2
    o_ref = refs[pos]
    pos += 1
    h_ref = refs[pos] if prologue != "plain" else None
    slab_ref = refs[pos + 1] if (prologue == "dil" or out_dil > 1) else None

    if prologue != "plain":
        @pl.when(pl.program_id(1) == 0)
        def _():
            if prologue == "ada":
                h_ref[...] = _ada_norm(x_ref[...], g_ref[...], sc_ref[...], sh_ref[...]).astype(BF16)
            elif prologue == "rms":
                h_ref[...] = (_rms(x_ref[...]) * g_ref[...]).astype(BF16)
            else:
                tm = h_ref.shape[0]
                for c in range(h_ref.shape[1] // LANES):
                    cols = slice(c * LANES, (c + 1) * LANES)
                    vals = []
                    for i, (ref, dil) in enumerate(zip(o_refs + l_refs, in_dils + in_dils)):
                        if dil == 1:
                            vals.append(ref[0, :, cols].astype(F32))
                        else:
                            for r in range(dil):
                                slab_ref[i, pl.ds(r, tm // dil, stride=dil), :] = ref[r, :, cols].astype(F32)
                            vals.append(slab_ref[i])
                    o0, o1, o2, l0, l1, l2 = vals
                    m = jnp.maximum(jnp.maximum(l0, l1), l2)
                    e0, e1, e2 = jnp.exp(l0 - m), jnp.exp(l1 - m), jnp.exp(l2 - m)
                    h_ref[:, cols] = ((e0 * o0 + e1 * o1 + e2 * o2) / (e0 + e1 + e2)).astype(BF16)
        acc = jnp.dot(h_ref[...], w_refs[0][...], preferred_element_type=F32)
    else:
        acc = jnp.dot(x_refs[0][...], w_refs[0][...], preferred_element_type=F32)
        for xr, wr in zip(x_refs[1:], w_refs[1:]):
            acc = acc + jnp.dot(xr[...], wr[...], preferred_element_type=F32)

    if out_scale != 1.0:
        acc = acc * out_scale
    if residual:
        o_ref[...] = (res_ref[...] + gate_ref[...] * acc).astype(o_ref.dtype)
        return

    def epilogue(pattern):
        tm = acc.shape[0]
        if any(pattern):
            cos = cos_ref[...]
            sin = sin_ref[...]
            lane = lax.broadcasted_iota(jnp.int32, cos.shape, 1)
            first_half = (lane % HEAD_DIM) < ROPE_HALF
        for c, flag in enumerate(pattern):
            cols = slice(c * LANES, (c + 1) * LANES)
            a = acc[:, cols]
            if flag:
                a = _rope_group(a, cos, sin, first_half)
            if out_dil == 1:
                o_ref[:, cols] = a.astype(o_ref.dtype)
            else:
                slab_ref[0] = a
                for r in range(out_dil):
                    o_ref[r, :, cols] = slab_ref[0, pl.ds(r, tm // out_dil, stride=out_dil), :].astype(o_ref.dtype)

    n_groups = acc.shape[1] // LANES
    if rope_pattern is None:
        epilogue((0,) * n_groups)
    elif rope_tiles is None:
        epilogue(rope_pattern)
    else:
        @pl.when(pl.program_id(1) < rope_tiles)
        def _():
            epilogue(rope_pattern)

        @pl.when(pl.program_id(1) >= rope_tiles)
        def _():
            epilogue((0,) * n_groups)


def _proj(lhs, ws, *, prologue, seq, tm, tn, out_dtype, norm=None, lhs_cols=None, rope=None,
          rope_pattern=None, rope_tiles=None, out_scale=1.0, residual=None, in_dils=None, out_dil=1, name):
    k, n = ws[0].shape
    bsz_seq = lhs[0].shape[0] if prologue != "dil" else lhs[0].shape[0] * seq
    t = bsz_seq
    assert t % tm == 0 and n % tn == 0 and seq % tm == 0
    bpb = seq // tm
    n_lhs = len(ws)
    args, specs = [], []
    row = lambda i, j: (i, 0)
    if prologue in ("ada", "rms"):
        kw, kc = lhs_cols if lhs_cols is not None else (k, 0)
        args.append(lhs[0])
        specs.append(pl.BlockSpec((tm, kw), lambda i, j: (i, kc)))
        args.append(norm[0])
        specs.append(pl.BlockSpec((1, k), lambda i, j: (0, 0)))
        if prologue == "ada":
            for v in norm[1:3]:
                args.append(v)
                specs.append(pl.BlockSpec((None, 1, k), lambda i, j: (i // bpb, 0, 0)))
    elif prologue == "dil":
        for a, dil in zip(lhs, in_dils + in_dils):
            args.append(a)
            specs.append(pl.BlockSpec((None, dil, tm // dil, k), lambda i, j: (i // bpb, 0, i % bpb, 0)))
    else:
        for a, w in zip(lhs, ws):
            args.append(a)
            specs.append(pl.BlockSpec((tm, w.shape[0]), row))
    for w in ws:
        args.append(w)
        specs.append(pl.BlockSpec((w.shape[0], tn), lambda i, j: (0, j)))
    if rope_pattern is not None:
        assert len(rope_pattern) == tn // LANES
        for tab in rope:
            args.append(tab)
            specs.append(pl.BlockSpec((tm, LANES), row))
    if residual is not None:
        res, gate = residual
        args.append(res)
        specs.append(pl.BlockSpec((tm, tn), lambda i, j: (i, j)))
        args.append(gate)
        specs.append(pl.BlockSpec((None, 1, tn), lambda i, j: (i // bpb, 0, j)))
    scratch = [] if prologue == "plain" else [pltpu.VMEM((tm, k), BF16)]
    if prologue == "dil":
        scratch.append(pltpu.VMEM((2 * len(in_dils), tm, LANES), F32))
    elif out_dil > 1:
        scratch.append(pltpu.VMEM((1, tm, LANES), F32))
    if out_dil == 1:
        out_spec = pl.BlockSpec((tm, tn), lambda i, j: (i, j))
        out_shape = jax.ShapeDtypeStruct((t, n), out_dtype)
    else:
        out_spec = pl.BlockSpec((None, out_dil, tm // out_dil, tn), lambda i, j: (i // bpb, 0, i % bpb, j))
        out_shape = jax.ShapeDtypeStruct((t // seq, out_dil, seq // out_dil, n), out_dtype)
    kern = functools.partial(_proj_kernel, prologue=prologue, n_lhs=n_lhs, rope_pattern=rope_pattern,
                             rope_tiles=rope_tiles, out_scale=out_scale, residual=residual is not None,
                             in_dils=in_dils, out_dil=out_dil)
    return pl.pallas_call(
        kern,
        grid=(t // tm, n // tn),
        in_specs=specs,
        out_specs=out_spec,
        out_shape=out_shape,
        scratch_shapes=scratch,
        compiler_params=_cparams(("parallel", "arbitrary")),
        name=name,
    )(*args)


MLA_HEADS_PER_STEP = 4


def _mla_scores(q_ref, kn_ref, kr, h):
    k = jnp.concatenate([kn_ref[:, h * LANES:(h + 1) * LANES], kr], axis=1)
    q = q_ref[:, h * 2 * LANES:(h + 1) * 2 * LANES]
    return lax.dot_general(q, k, (((1,), (1,)), ((), ())), preferred_element_type=F32)


def _mla_kernel(q_ref, kn_ref, kr_ref, v_ref, o_ref):
    kr = kr_ref[...]
    s_next = _mla_scores(q_ref, kn_ref, kr, 0)
    for h in range(MLA_HEADS_PER_STEP):
        s = s_next
        if h + 1 < MLA_HEADS_PER_STEP:
            s_next = _mla_scores(q_ref, kn_ref, kr, h + 1)
        m = jnp.max(s, axis=1, keepdims=True)
        p = jnp.exp(s - m)
        den = jnp.sum(p, axis=1, keepdims=True)
        o = jnp.dot(p.astype(BF16), v_ref[:, h * LANES:(h + 1) * LANES], preferred_element_type=F32)
        o_ref[:, h * LANES:(h + 1) * LANES] = (o / den).astype(o_ref.dtype)


def _mla_attention(q_cat, kv, qkv_b, kr_col, *, bsz, seq, tq):
    t = bsz * seq
    nq = seq // tq
    hps = MLA_HEADS_PER_STEP
    groups = MLA_HEADS // hps
    return pl.pallas_call(
        _mla_kernel,
        grid=(bsz, groups, nq),
        in_specs=[pl.BlockSpec((tq, hps * 2 * LANES), lambda b, h, i: (b * nq + i, h)),
                  pl.BlockSpec((seq, hps * LANES), lambda b, h, i: (b, h)),
                  pl.BlockSpec((seq, LANES), lambda b, h, i: (b, kr_col)),
                  pl.BlockSpec((seq, hps * LANES), lambda b, h, i: (b, groups + h))],
        out_specs=pl.BlockSpec((tq, hps * LANES), lambda b, h, i: (b * nq + i, h)),
        out_shape=jax.ShapeDtypeStruct((t, MLA_HEADS * MLA_V), BF16),
        compiler_params=_cparams(("parallel", "parallel", "parallel")),
        name="mla_attention",
    )(q_cat, kv, qkv_b, kv)


def _banded_kernel(*refs, radius, tq, hb, length, n_pairs, pairs_per_kv, has_sink, want_lse):
    pos = 0
    if has_sink:
        sink_ref = refs[0]
        pos = 1
    q_ref, kp_ref, kc_ref, kn_ref, vp_ref, vc_ref, vn_ref = refs[pos:pos + 7]
    pos += 7
    o_ref = refs[pos]
    lse_ref = refs[pos + 1] if want_lse else None

    j = pl.program_id(2)
    kw = jnp.concatenate([kp_ref[...], kc_ref[...], kn_ref[...]], axis=0)
    vw = jnp.concatenate([vp_ref[...], vc_ref[...], vn_ref[...]], axis=0)
    scale = HEAD_DIM ** -0.5
    qn = min(tq, BAND_SUB_ROWS)
    kn = qn + 2 * hb
    low = lax.broadcasted_iota(jnp.int32, (qn, LANES), 1) < HEAD_DIM

    for q0 in range(0, tq, qn):
        qpos = j * tq + q0 + lax.broadcasted_iota(jnp.int32, (qn, kn), 0)
        kpos = j * tq - hb + q0 + lax.broadcasted_iota(jnp.int32, (qn, kn), 1)
        valid = (jnp.abs(qpos - kpos) <= radius) & (kpos >= 0) & (kpos < length)
        for p in range(n_pairs):
            c = p // pairs_per_kv
            qp = q_ref[q0:q0 + qn, p * LANES:(p + 1) * LANES] * scale
            kp = kw[q0:q0 + kn, c * LANES:(c + 1) * LANES]
            vp = vw[q0:q0 + kn, c * LANES:(c + 1) * LANES]
            vp = jnp.concatenate([vp, jnp.ones_like(vp)], axis=1)
            halves = []
            for half in range(2):
                qh = jnp.where(low if half == 0 else ~low, qp, jnp.zeros_like(qp))
                s = lax.dot_general(qh, kp, (((1,), (1,)), ((), ())), preferred_element_type=F32)
                s = jnp.where(valid, s, -jnp.inf)
                m = jnp.max(s, axis=1, keepdims=True)
                if has_sink:
                    sk = sink_ref[2 * p + half]
                    m = jnp.maximum(m, sk)
                oe = jnp.dot(jnp.exp(s - m).astype(BF16), vp, preferred_element_type=F32)
                den = oe[:, LANES:]
                if has_sink:
                    den = den + jnp.exp(sk - m)
                halves.append((oe[:, :LANES] / den, m + jnp.log(den) if want_lse else None))
            o_ref[q0:q0 + qn, p * LANES:(p + 1) * LANES] = jnp.where(
                low, halves[0][0], halves[1][0]).astype(o_ref.dtype)
            if want_lse:
                lse_ref[q0:q0 + qn, p * LANES:(p + 1) * LANES] = jnp.where(low, halves[0][1], halves[1][1])


def _banded_attention(q_arr, k_arr, v_arr, *, bsz, seq, dil, radius, q_col, k_col, v_col, kv_width,
                      pairs_per_kv, sink=None, want_lse, name):
    length = seq // dil
    hb = radius
    tq = min(256, length)
    assert tq % hb == 0 and length % tq == 0
    qw = DIL_HEADS * HEAD_DIM
    n_pairs = qw // LANES
    rep = tq // hb
    last_hb = length // hb - 1

    def prev_spec(col):
        return pl.BlockSpec((None, None, hb, kv_width),
                            lambda b, r, j: (b, r, jnp.maximum(j * rep - 1, 0), col))

    def cur_spec(col):
        return pl.BlockSpec((None, None, tq, kv_width), lambda b, r, j: (b, r, j, col))

    def next_spec(col):
        return pl.BlockSpec((None, None, hb, kv_width),
                            lambda b, r, j: (b, r, jnp.minimum((j + 1) * rep, last_hb), col))

    in_specs = [pl.BlockSpec((None, None, tq, qw), lambda b, r, j: (b, r, j, q_col)),
                prev_spec(k_col), cur_spec(k_col), next_spec(k_col),
                prev_spec(v_col), cur_spec(v_col), next_spec(v_col)]
    args = [q_arr, k_arr, k_arr, k_arr, v_arr, v_arr, v_arr]
    if sink is not None:
        in_specs = [pl.BlockSpec(memory_space=pltpu.SMEM)] + in_specs
        args = [sink] + args
    out_spec = pl.BlockSpec((None, None, tq, qw), lambda b, r, j: (b, r, j, 0))
    out_specs = [out_spec]
    out_shape = [jax.ShapeDtypeStruct((bsz, dil, length, qw), BF16)]
    if want_lse:
        out_specs.append(out_spec)
        out_shape.append(jax.ShapeDtypeStruct((bsz, dil, length, qw), F32))
    kern = functools.partial(_banded_kernel, radius=radius, tq=tq, hb=hb, length=length, n_pairs=n_pairs,
                             pairs_per_kv=pairs_per_kv, has_sink=sink is not None, want_lse=want_lse)
    return pl.pallas_call(
        kern,
        grid=(bsz, dil, length // tq),
        in_specs=in_specs,
        out_specs=out_specs,
        out_shape=out_shape,
        compiler_params=_cparams(("parallel", "parallel", "parallel")),
        name=name,
    )(*args)


def _router_kernel(x_ref, g_ref, sc_ref, sh_ref, wt_ref, b_ref, tri_ref,
                   idx_ref, gate_ref, rank_ref, cnt_ref, h_ref, carry_ref):
    @pl.when(pl.program_id(0) == 0)
    def _():
        carry_ref[...] = jnp.zeros_like(carry_ref)

    h = _ada_norm(x_ref[...], g_ref[...], sc_ref[...], sh_ref[...])
    h_ref[...] = h
    lt = lax.dot_general(wt_ref[...], h, (((1,), (1,)), ((), ())), preferred_element_type=F32,
                         precision=lax.Precision.HIGHEST) + b_ref[...]
    n_e, tm = lt.shape
    e_iota = lax.broadcasted_iota(jnp.int32, (n_e, tm), 0)
    vals, sels = [], []
    for k in range(TOP_K):
        m = jnp.max(lt, axis=0, keepdims=True)
        idx = jnp.min(jnp.where(lt == m, e_iota, n_e), axis=0, keepdims=True)
        sel = e_iota == idx
        idx_ref[k:k + 1, :] = idx
        vals.append(m)
        sels.append(sel)
        lt = jnp.where(sel, -jnp.inf, lt)
    exps = [jnp.exp(v - vals[0]) for v in vals]
    den = exps[0] + exps[1] + exps[2] + exps[3]
    for k in range(TOP_K):
        gate_ref[k:k + 1, :] = exps[k] / den

    onehot = jnp.where(sels[0] | sels[1] | sels[2] | sels[3], 1.0, 0.0)
    carry = carry_ref[:, 0:1]
    before = jnp.dot(onehot.astype(BF16), tri_ref[...], preferred_element_type=F32) + carry
    for k in range(TOP_K):
        rank_ref[k:k + 1, :] = jnp.sum(jnp.where(sels[k], before, 0.0), axis=0, keepdims=True).astype(jnp.int32)
    total = carry + jnp.sum(onehot, axis=1, keepdims=True)
    carry_ref[...] = jnp.broadcast_to(total, carry_ref.shape)
    cnt_ref[...] = jnp.broadcast_to(total, cnt_ref.shape)


def _router(x, norm, w_router, b_router, *, seq, tm=512):
    t, d = x.shape
    n_e = w_router.shape[1]
    bpb = seq // tm
    tri = (jnp.arange(tm)[:, None] < jnp.arange(tm)[None, :]).astype(BF16)
    g, sc, sh = norm
    out_i = jax.ShapeDtypeStruct((TOP_K, t), jnp.int32)
    tok = pl.BlockSpec((TOP_K, tm), lambda i: (0, i))
    return pl.pallas_call(
        _router_kernel,
        grid=(t // tm,),
        in_specs=[pl.BlockSpec((tm, d), lambda i: (i, 0)),
                  pl.BlockSpec((1, d), lambda i: (0, 0)),
                  pl.BlockSpec((None, 1, d), lambda i: (i // bpb, 0, 0)),
                  pl.BlockSpec((None, 1, d), lambda i: (i // bpb, 0, 0)),
                  pl.BlockSpec((n_e, d), lambda i: (0, 0)),
                  pl.BlockSpec((n_e, 1), lambda i: (0, 0)),
                  pl.BlockSpec((tm, tm), lambda i: (0, 0))],
        out_specs=[tok, tok, tok, pl.BlockSpec((n_e, LANES), lambda i: (0, 0)),
                   pl.BlockSpec((tm, d), lambda i: (i, 0))],
        out_shape=[out_i, jax.ShapeDtypeStruct((TOP_K, t), F32), out_i,
                   jax.ShapeDtypeStruct((n_e, LANES), F32), jax.ShapeDtypeStruct((t, d), F32)],
        scratch_shapes=[pltpu.VMEM((n_e, LANES), F32)],
        compiler_params=_cparams(("arbitrary",)),
        name="router",
    )(x, g, sc, sh, w_router.T, b_router.reshape(n_e, 1), tri)


SC_CORES = 2
SC_SUBCORES = 16
SC_WORKERS = SC_CORES * SC_SUBCORES
SC_ROWS = 32
SC_INDEX_WINDOW = 128


def _sc_worker_id():
    return lax.axis_index("subcore") * SC_CORES + lax.axis_index("core")


def _sc_mesh():
    return plsc.VectorSubcoreMesh(core_axis_name="core", subcore_axis_name="subcore")


def _sc_dispatch(h, dest_flat, pad_rows):
    t, d = h.shape
    n_pad = pad_rows.shape[0]
    n_rows = TOP_K * t + n_pad
    tok_w = t // SC_WORKERS
    pad_w = n_pad // SC_WORKERS
    assert tok_w % SC_ROWS == 0 and pad_w % SC_ROWS == 0

    @functools.partial(
        pl.kernel, out_type=jax.ShapeDtypeStruct((n_rows, d), h.dtype), mesh=_sc_mesh(),
        scratch_types=[pltpu.VMEM((SC_ROWS,), jnp.int32), pltpu.VMEM((SC_ROWS, d), h.dtype)])
    def scatter_rows(h_hbm, dest_hbm, pad_hbm, zero_hbm, xs_hbm, idx_v, rows_v):
        wid = _sc_worker_id()

        @pl.loop(0, tok_w // SC_ROWS)
        def _(b):
            t0 = wid * tok_w + b * SC_ROWS
            pltpu.sync_copy(h_hbm.at[pl.ds(t0, SC_ROWS)], rows_v)
            for k in range(TOP_K):
                pltpu.sync_copy(dest_hbm.at[pl.ds(k * t + t0, SC_ROWS)], idx_v)
                pltpu.sync_copy(rows_v, xs_hbm.at[idx_v])

        pltpu.sync_copy(zero_hbm, rows_v)

        @pl.loop(0, pad_w // SC_ROWS)
        def _(b):
            pltpu.sync_copy(pad_hbm.at[pl.ds(wid * pad_w + b * SC_ROWS, SC_ROWS)], idx_v)
            pltpu.sync_copy(rows_v, xs_hbm.at[idx_v])

    return scatter_rows(h, dest_flat, pad_rows, jnp.zeros((SC_ROWS, d), h.dtype))


def _sc_gather(table, idx):
    n = idx.shape[0]
    d = table.shape[1]
    per_w = n // SC_WORKERS
    assert per_w % SC_INDEX_WINDOW == 0

    @functools.partial(
        pl.kernel, out_type=jax.ShapeDtypeStruct((n, d), table.dtype), mesh=_sc_mesh(),
        scratch_types=[pltpu.VMEM((SC_INDEX_WINDOW,), jnp.int32), pltpu.VMEM((SC_ROWS, d), table.dtype)])
    def gather_rows(table_hbm, idx_hbm, out_hbm, idx_v, rows_v):
        base = _sc_worker_id() * per_w

        @pl.loop(0, per_w // SC_INDEX_WINDOW)
        def _(b):
            off = base + b * SC_INDEX_WINDOW
            pltpu.sync_copy(idx_hbm.at[pl.ds(off, SC_INDEX_WINDOW)], idx_v)
            for c in range(SC_INDEX_WINDOW // SC_ROWS):
                pltpu.sync_copy(table_hbm.at[idx_v.at[pl.ds(c * SC_ROWS, SC_ROWS)]], rows_v)
                pltpu.sync_copy(rows_v, out_hbm.at[pl.ds(off + c * SC_ROWS, SC_ROWS)])

    return gather_rows(table, idx)


def _expert_kernel(be_ref, nu_ref, xs_ref, wgu_ref, bgu_ref, wd_ref, bd_ref, y_ref, wgu_bf, wd_bf):
    i = pl.program_id(0)
    used = i < nu_ref[0]

    @pl.when(used & ((i == 0) | (be_ref[i] != be_ref[jnp.maximum(i - 1, 0)])))
    def _():
        wgu_bf[...] = wgu_ref[...].astype(BF16)
        wd_bf[...] = wd_ref[...].astype(BF16)

    @pl.when(used)
    def _():
        d_ff = wd_ref.shape[0]
        gu = jnp.dot(xs_ref[...].astype(BF16), wgu_bf[...], preferred_element_type=F32) + bgu_ref[...]
        glu = jnp.minimum(gu[:, :d_ff], SWIGLU_LIMIT)
        lin = jnp.clip(gu[:, d_ff:], -SWIGLU_LIMIT, SWIGLU_LIMIT)
        act = glu * (1.0 / (1.0 + jnp.exp(-SWIGLU_ALPHA * glu))) * (lin + 1.0)
        y_ref[...] = jnp.dot(act.astype(BF16), wd_bf[...], preferred_element_type=F32) + bd_ref[...]

    @pl.when(jnp.logical_not(used))
    def _():
        y_ref[...] = jnp.zeros_like(y_ref)


def _experts(xs, blk_expert, n_used, layer, w_gu, b_gu, w_down, b_down):
    n_rows, d = xs.shape
    _, n_e, _, f2 = w_gu.shape
    n_blocks = n_rows // MOE_ROWS

    def blk(i, be, nu):
        return jnp.minimum(i, nu[0] - 1)

    def wmap(i, be, nu):
        return (layer, be[blk(i, be, nu)], 0, 0)

    grid_spec = pltpu.PrefetchScalarGridSpec(
        num_scalar_prefetch=2,
        grid=(n_blocks,),
        in_specs=[pl.BlockSpec((MOE_ROWS, d), lambda i, be, nu: (blk(i, be, nu), 0)),
                  pl.BlockSpec((None, None, d, f2), wmap),
                  pl.BlockSpec((None, None, 1, f2), wmap),
                  pl.BlockSpec((None, None, f2 // 2, d), wmap),
                  pl.BlockSpec((None, None, 1, d), wmap)],
        out_specs=pl.BlockSpec((MOE_ROWS, d), lambda i, be, nu: (i, 0)),
        scratch_shapes=[pltpu.VMEM((d, f2), BF16), pltpu.VMEM((f2 // 2, d), BF16)],
    )
    depth = w_gu.shape[0]
    return pl.pallas_call(
        _expert_kernel,
        grid_spec=grid_spec,
        out_shape=jax.ShapeDtypeStruct((n_rows, d), F32),
        compiler_params=pltpu.CompilerParams(dimension_semantics=("arbitrary",),
                                             vmem_limit_bytes=EXPERT_VMEM_LIMIT),
        name="moe_experts",
    )(blk_expert, n_used, xs, w_gu, b_gu.reshape(depth, n_e, 1, f2), w_down, b_down.reshape(depth, n_e, 1, d))


def _combine_kernel(y0_ref, y1_ref, y2_ref, y3_ref, gate_ref, x_ref, g2_ref, o_ref):
    gate = gate_ref[...]
    y = gate[:, 0:1] * y0_ref[...]
    for k, y_ref in enumerate((y1_ref, y2_ref, y3_ref), start=1):
        y = y + gate[:, k:k + 1] * y_ref[...]
    o_ref[...] = x_ref[...] + g2_ref[...] * y


def _combine(y4, gate, x, g2, *, seq, tm=512):
    t, d = x.shape
    bpb = seq // tm
    nblk = t // tm
    y_specs = [pl.BlockSpec((tm, d), functools.partial(lambda i, k: (k * nblk + i, 0), k=k)) for k in range(TOP_K)]
    return pl.pallas_call(
        _combine_kernel,
        grid=(nblk,),
        in_specs=y_specs + [pl.BlockSpec((tm, TOP_K), lambda i: (i, 0)),
                            pl.BlockSpec((tm, d), lambda i: (i, 0)),
                            pl.BlockSpec((None, 1, d), lambda i: (i // bpb, 0, 0))],
        out_specs=pl.BlockSpec((tm, d), lambda i: (i, 0)),
        out_shape=jax.ShapeDtypeStruct((t, d), F32),
        compiler_params=_cparams(("parallel",)),
        name="moe_combine",
    )(y4, y4, y4, y4, gate, x, g2)


def _moe(x, norm, g2, w_router, b_router, layer, w_gu, b_gu, w_down, b_down, *, seq):
    t, d = x.shape
    n_e = w_router.shape[1]
    idx_t, gate_t, rank_t, counts, h = _router(x, norm, w_router, b_router, seq=seq)
    e_ids = jnp.arange(n_e, dtype=jnp.int32)
    counts = counts[:, 0].astype(jnp.int32)
    padded = (counts + MOE_ROWS - 1) // MOE_ROWS * MOE_ROWS
    pend = jnp.cumsum(padded)
    pstart = pend - padded
    n_pad = n_e * MOE_ROWS
    n_rows = t * TOP_K + n_pad
    n_blocks = n_rows // MOE_ROWS
    dest_t = jnp.sum(jnp.where(idx_t[:, :, None] == e_ids, pstart, 0), axis=-1) + rank_t
    dest_flat = dest_t.reshape(TOP_K * t)
    blk_start = jnp.arange(n_blocks, dtype=jnp.int32) * MOE_ROWS
    blk_expert = jnp.minimum(jnp.sum(blk_start[:, None] >= pend[None, :], axis=1), n_e - 1).astype(jnp.int32)
    n_used = (pend[-1:] // MOE_ROWS).astype(jnp.int32)
    pad_len = padded - counts
    pad_end = jnp.cumsum(pad_len)
    j = jnp.arange(n_pad, dtype=jnp.int32)
    pe = jnp.sum(j[:, None] >= pad_end[None, :], axis=1)
    sel = jnp.minimum(pe, n_e - 1)[:, None] == e_ids
    in_expert = jnp.sum(jnp.where(sel, pstart + counts - (pad_end - pad_len), 0), axis=-1) + j
    pad_rows = jnp.where(pe < n_e, in_expert, pend[-1] + j - pad_end[-1]).astype(jnp.int32)

    xs = _sc_dispatch(h, dest_flat, pad_rows)
    yb = _experts(xs, blk_expert, n_used, layer, w_gu, b_gu, w_down, b_down)
    y4 = _sc_gather(yb, dest_flat)
    return _combine(y4, gate_t.T, x, g2, seq=seq)


def _final_norm_kernel(x_ref, g_ref, o_ref):
    o_ref[...] = _rms(x_ref[...]) * g_ref[...]


def _final_norm(x, g, tm=1024):
    t, d = x.shape
    return pl.pallas_call(
        _final_norm_kernel,
        grid=(t // tm,),
        in_specs=[pl.BlockSpec((tm, d), lambda i: (i, 0)), pl.BlockSpec((1, d), lambda i: (0, 0))],
        out_specs=pl.BlockSpec((tm, d), lambda i: (i, 0)),
        out_shape=jax.ShapeDtypeStruct((t, d), F32),
        compiler_params=_cparams(("parallel",)),
        name="final_norm",
    )(x, g.reshape(1, d))


def _dup_heads(w, n_heads):
    d = w.shape[0]
    return jnp.repeat(w.reshape(d, n_heads, 1, HEAD_DIM), 2, axis=2).reshape(d, n_heads * LANES)


def _mixer_mla_swa(x, norm, g1, cos, sin, w_in, g_q, w_qb, g_kv, w_kvb, sink, w_out, *, bsz, seq):
    d = x.shape[1]
    o = np.cumsum((0, MLA_Q_LORA, MLA_KV_LORA, MLA_ROPE, SWA_Q_HEADS * HEAD_DIM,
                   SWA_KV_HEADS * HEAD_DIM, SWA_KV_HEADS * HEAD_DIM))
    w_cq, w_ckv, w_kr, w_qs, w_ks, w_vs = (w_in[:, o[i]:o[i + 1]] for i in range(6))
    w_lat = jnp.concatenate([w_cq, jnp.zeros((d, LANES), F32), w_ckv], axis=1).astype(BF16)
    lat = _proj([x], [w_lat], prologue="ada", norm=norm, seq=seq, tm=512, tn=w_lat.shape[1],
                out_dtype=F32, name="proj_latent")
    w_b = jnp.concatenate([w_qs, _dup_heads(w_ks, SWA_KV_HEADS), _dup_heads(w_vs, SWA_KV_HEADS),
                           w_kr, jnp.zeros((d, LANES - MLA_ROPE), F32)], axis=1).astype(BF16)
    qkv_b = _proj([x], [w_b], prologue="ada", norm=norm, seq=seq, tm=512, tn=w_b.shape[1], out_dtype=BF16,
                  rope=(cos, sin), rope_pattern=(1,) * 10 + (0, 0) + (1,), name="proj_swa")
    kr_col = (w_b.shape[1] - LANES) // LANES
    qd = MLA_NOPE + MLA_ROPE
    w_q = w_qb.reshape(MLA_Q_LORA, MLA_HEADS, qd)
    w_q = jnp.concatenate([w_q, jnp.zeros((MLA_Q_LORA, MLA_HEADS, 2 * LANES - qd), F32)], axis=2)
    w_q = w_q.reshape(MLA_Q_LORA, MLA_HEADS * 2 * LANES).astype(BF16)
    q_cat = _proj([lat], [w_q], prologue="rms", norm=(g_q.reshape(1, -1),), lhs_cols=(MLA_Q_LORA, 0),
                  seq=seq, tm=512, tn=w_q.shape[1], out_dtype=BF16, rope=(cos, sin),
                  rope_pattern=(0, 1) * MLA_HEADS, out_scale=qd ** -0.5, name="proj_mla_q")
    w_kv = w_kvb.reshape(MLA_KV_LORA, MLA_HEADS, MLA_NOPE + MLA_V)
    w_kv = jnp.concatenate([w_kv[:, :, :MLA_NOPE].reshape(MLA_KV_LORA, -1),
                            w_kv[:, :, MLA_NOPE:].reshape(MLA_KV_LORA, -1)], axis=1).astype(BF16)
    kv = _proj([lat], [w_kv], prologue="rms", norm=(g_kv.reshape(1, -1),), lhs_cols=(MLA_KV_LORA, 2),
               seq=seq, tm=512, tn=w_kv.shape[1], out_dtype=BF16, name="proj_mla_kv")
    o_a = _mla_attention(q_cat, kv, qkv_b, kr_col, bsz=bsz, seq=seq, tq=256)
    qkv_b4 = qkv_b.reshape(bsz, 1, seq, qkv_b.shape[1])
    (o_b,) = _banded_attention(qkv_b4, qkv_b4, qkv_b4, bsz=bsz, seq=seq, dil=1, radius=SWA_RADIUS,
                               q_col=0, k_col=4, v_col=5, kv_width=2 * LANES,
                               pairs_per_kv=SWA_Q_HEADS // SWA_KV_HEADS // 2, sink=sink, want_lse=False,
                               name="swa_attention")
    o_b = o_b.reshape(bsz * seq, SWA_Q_HEADS * HEAD_DIM)
    na = MLA_HEADS * MLA_V
    return _proj([o_a, o_b], [w_out[:na].astype(BF16), w_out[na:].astype(BF16)], prologue="plain",
                 seq=seq, tm=512, tn=d, out_dtype=F32, residual=(x, g1), name="proj_out_ab")


def _mixer_dilated(x, norm, g1, cos, sin, w_in, w_out, *, bsz, seq):
    d = x.shape[1]
    qw = DIL_HEADS * HEAD_DIM
    ng = len(DIL_PATTERN)
    w_bf = w_in.astype(BF16)
    outs, lses = [], []
    for g, (window, dil) in enumerate(DIL_PATTERN):
        qkv = _proj([x], [w_bf[:, g * 3 * qw:(g + 1) * 3 * qw]], prologue="ada", norm=norm, seq=seq, tm=512,
                    tn=qw, out_dtype=BF16, rope=(cos, sin), rope_pattern=(1,) * (qw // LANES), rope_tiles=2,
                    out_dil=dil, name=f"proj_dil_{dil}")
        qkv = qkv.reshape(bsz, dil, seq // dil, 3 * qw)
        o, lse = _banded_attention(qkv, qkv, qkv, bsz=bsz, seq=seq, dil=dil, radius=window // (2 * dil),
                                   q_col=0, k_col=1, v_col=2, kv_width=qw, pairs_per_kv=1,
                                   want_lse=True, name=f"dilated_attention_{dil}")
        outs.append(o)
        lses.append(lse)
    return _proj(outs + lses, [w_out.astype(BF16)], prologue="dil", seq=seq, tm=512, tn=d, out_dtype=F32,
                 residual=(x, g1), in_dils=tuple(dil for _, dil in DIL_PATTERN), name="proj_out_c")


def kernel(x, c, positions, w_mod, b_mod, g_norm_mix, g_norm_ffn, w_in_ab, mla_g_q, mla_w_qb, mla_g_kv,
           mla_w_kvb, swa_sink, w_out_ab, w_in_c, w_out_c, w_router, b_router, w_gu, b_gu, w_down, b_down,
           g_final):
    bsz, seq, d = x.shape
    depth = w_mod.shape[0]
    cos, sin = _rope_tables(positions)
    mod = _modulation(c, w_mod, b_mod)
    xt = x.reshape(bsz * seq, d)
    for layer in range(depth):
        sh1, sc1, g1, sh2, sc2, g2 = (mod[layer, :, i * d:(i + 1) * d].reshape(bsz, 1, d) for i in range(6))
        li = layer // 2
        norm = (g_norm_mix[layer].reshape(1, d), sc1, sh1)
        if layer % 2 == 0:
            xt = _mixer_mla_swa(xt, norm, g1, cos, sin, w_in_ab[li], mla_g_q[li], mla_w_qb[li], mla_g_kv[li],
                                mla_w_kvb[li], swa_sink[li], w_out_ab[li], bsz=bsz, seq=seq)
        else:
            xt = _mixer_dilated(xt, norm, g1, cos, sin, w_in_c[li], w_out_c[li], bsz=bsz, seq=seq)
        norm = (g_norm_ffn[layer].reshape(1, d), sc2, sh2)
        xt = _moe(xt, norm, g2, w_router[layer], b_router[layer], layer, w_gu, b_gu, w_down, b_down, seq=seq)
    return _final_norm(xt, g_final).reshape(bsz, seq, d)
```

```python
import functools

import jax
import jax.numpy as jnp
import numpy as np
from jax import lax
from jax.experimental import pallas as pl
from jax.experimental.pallas import tpu as pltpu
from jax.experimental.pallas import tpu_sc as plsc

F32 = jnp.float32
BF16 = jnp.bfloat16

EPS = 1e-6
ROPE_THETA = 10000.0
LANES = 128
HEAD_DIM = 64
ROPE_HALF = HEAD_DIM // 2

MLA_HEADS = 8
MLA_Q_LORA = 384
MLA_KV_LORA = 256
MLA_NOPE = 128
MLA_ROPE = 64
MLA_V = 128
SWA_Q_HEADS = 16
SWA_KV_HEADS = 2
SWA_RADIUS = 128
DIL_PATTERN = ((128, 1), (512, 4), (2048, 16))
DIL_HEADS = 16
N_EXPERTS = 32
TOP_K = 4
SWIGLU_LIMIT = 7.0
SWIGLU_ALPHA = 1.702
MOE_ROWS = 512
BAND_SUB_ROWS = 128

VMEM_LIMIT = 48 * 1024 * 1024
EXPERT_VMEM_LIMIT = 56 * 1024 * 1024


def _cparams(sem):
    return pltpu.CompilerParams(dimension_semantics=sem, vmem_limit_bytes=VMEM_LIMIT)


def _mod_kernel(c_ref, w_ref, b_ref, o_ref):
    c = c_ref[...]
    ca = c * (1.0 / (1.0 + jnp.exp(-c)))
    o_ref[...] = jnp.dot(ca, w_ref[...], preferred_element_type=F32,
                         precision=lax.Precision.HIGHEST) + b_ref[...]


def _modulation(c, w_mod, b_mod):
    depth, d, n = w_mod.shape
    bsz = c.shape[0]
    tn = 1536
    return pl.pallas_call(
        _mod_kernel,
        grid=(depth, n // tn),
        in_specs=[pl.BlockSpec((bsz, d), lambda l, j: (0, 0)),
                  pl.BlockSpec((None, d, tn), lambda l, j: (l, 0, j)),
                  pl.BlockSpec((None, 1, tn), lambda l, j: (l, 0, j))],
        out_specs=pl.BlockSpec((None, bsz, tn), lambda l, j: (l, 0, j)),
        out_shape=jax.ShapeDtypeStruct((depth, bsz, n), F32),
        compiler_params=_cparams(("parallel", "parallel")),
        name="modulation",
    )(c, w_mod, b_mod.reshape(depth, 1, n))


def _rope_table_kernel(pos_ref, inv_ref, sign_ref, cos_ref, sin_ref):
    ang = inv_ref[...] * pos_ref[...].astype(F32)
    cos_ref[...] = jnp.transpose(jnp.cos(ang))
    sin_ref[...] = jnp.transpose(jnp.sin(ang) * sign_ref[...])


def _rope_tables(positions):
    bsz, s = positions.shape
    inv = ROPE_THETA ** (-jnp.arange(0, HEAD_DIM, 2, dtype=F32) / HEAD_DIM)
    inv128 = jnp.tile(inv, LANES // ROPE_HALF).reshape(LANES, 1)
    sign = jnp.where((jnp.arange(LANES) % HEAD_DIM) < ROPE_HALF, -1.0, 1.0).astype(F32).reshape(LANES, 1)
    out = jax.ShapeDtypeStruct((bsz * s, LANES), F32)
    return pl.pallas_call(
        _rope_table_kernel,
        grid=(bsz,),
        in_specs=[pl.BlockSpec((None, 1, s), lambda b: (b, 0, 0)),
                  pl.BlockSpec((LANES, 1), lambda b: (0, 0)),
                  pl.BlockSpec((LANES, 1), lambda b: (0, 0))],
        out_specs=[pl.BlockSpec((s, LANES), lambda b: (b, 0)),
                   pl.BlockSpec((s, LANES), lambda b: (b, 0))],
        out_shape=[out, out],
        compiler_params=_cparams(("parallel",)),
        name="rope_tables",
    )(positions.reshape(bsz, 1, s), inv128, sign)


def _rms(x):
    return x * lax.rsqrt(jnp.mean(x * x, axis=-1, keepdims=True) + EPS)


def _ada_norm(x, g, sc, sh):
    return (_rms(x) * g) * (1.0 + sc) + sh


ROPE_ROTATE, ROPE_PAIR = 1, 2


def _pair_layout(w):
    k, n = w.shape
    w = w.reshape(k, n // LANES, 2, 2, ROPE_HALF)
    return jnp.swapaxes(w, 2, 3).reshape(k, n)


def _rope_group(a, cos, sin, first_half):
    rot = jnp.where(first_half, pltpu.roll(a, LANES - ROPE_HALF, 1), pltpu.roll(a, ROPE_HALF, 1))
    return a * cos + rot * sin


def _proj_kernel(*refs, prologue, n_lhs, rope_pattern, sub_tn, out_scale, residual, in_dils, out_dil):
    pos = 0
    if prologue == "ada":
        x_ref, g_ref, sc_ref, sh_ref = refs[:4]
        pos = 4
    elif prologue == "rms":
        x_ref, g_ref = refs[:2]
        pos = 2
    elif prologue == "dil":
        o_refs = refs[0:3]
        l_refs = refs[3:6]
        pos = 6
    else:
        x_refs = refs[:n_lhs]
        pos = n_lhs
    w_refs = refs[pos:pos + n_lhs]
    pos += n_lhs
    if rope_pattern is not None:
        cos_ref, sin_ref = refs[pos:pos + 2]
        pos += 2
    if residual:
        res_ref, gate_ref = refs[pos:pos + 2]
        pos += 2
    o_ref = refs[pos]
    pos += 1
    h_ref = refs[pos] if prologue != "plain" else None
    slab_ref = refs[pos + 1] if (prologue == "dil" or out_dil > 1) else None

    if prologue != "plain":
        @pl.when(pl.program_id(1) == 0)
        def _():
            if prologue == "ada":
                h_ref[...] = _ada_norm(x_ref[...], g_ref[...], sc_ref[...], sh_ref[...]).astype(BF16)
            elif prologue == "rms":
                h_ref[...] = (_rms(x_ref[...]) * g_ref[...]).astype(BF16)
            else:
                tm = h_ref.shape[0]
                for c in range(h_ref.shape[1] // LANES):
                    cols = slice(c * LANES, (c + 1) * LANES)
                    vals = []
                    for i, (ref, dil) in enumerate(zip(o_refs + l_refs, in_dils + in_dils)):
                        if dil == 1:
                            vals.append(ref[0, :, cols].astype(F32))
                        else:
                            for r in range(dil):
                                slab_ref[i, pl.ds(r, tm // dil, stride=dil), :] = ref[r, :, cols].astype(F32)
                            vals.append(slab_ref[i])
                    o0, o1, o2, l0, l1, l2 = vals
                    m = jnp.maximum(jnp.maximum(l0, l1), l2)
                    e0, e1, e2 = jnp.exp(l0 - m), jnp.exp(l1 - m), jnp.exp(l2 - m)
                    h_ref[:, cols] = ((e0 * o0 + e1 * o1 + e2 * o2) / (e0 + e1 + e2)).astype(BF16)

    n_total = w_refs[0].shape[1]
    tn = n_total if sub_tn is None else sub_tn

    def matmul(t):
        cols = slice(t * tn, (t + 1) * tn)
        if prologue != "plain":
            acc = jnp.dot(h_ref[...], w_refs[0][:, cols], preferred_element_type=F32)
        else:
            acc = jnp.dot(x_refs[0][...], w_refs[0][:, cols], preferred_element_type=F32)
            for xr, wr in zip(x_refs[1:], w_refs[1:]):
                acc = acc + jnp.dot(xr[...], wr[:, cols], preferred_element_type=F32)
        return acc * out_scale if out_scale != 1.0 else acc

    if residual:
        o_ref[...] = (res_ref[...] + gate_ref[...] * matmul(0)).astype(o_ref.dtype)
        return

    pattern = rope_pattern if rope_pattern is not None else (0,) * (n_total // LANES)
    if any(pattern):
        cos = cos_ref[...]
        sin = sin_ref[...]
        lane = lax.broadcasted_iota(jnp.int32, cos.shape, 1)
        first_half = (lane % HEAD_DIM) < ROPE_HALF
        if ROPE_PAIR in pattern:
            sin_pair = jnp.where(lane < HEAD_DIM, -1.0, 1.0) * jnp.where(first_half, -sin, sin)

    def epilogue(acc, t):
        tm = acc.shape[0]
        for c in range(tn // LANES):
            g = t * (tn // LANES) + c
            cols = slice(g * LANES, (g + 1) * LANES)
            a = acc[:, c * LANES:(c + 1) * LANES]
            if pattern[g] == ROPE_ROTATE:
                a = _rope_group(a, cos, sin, first_half)
            elif pattern[g] == ROPE_PAIR:
                a = a * cos + pltpu.roll(a, HEAD_DIM, 1) * sin_pair
            if out_dil == 1:
                o_ref[:, cols] = a.astype(o_ref.dtype)
            else:
                slab_ref[0] = a
                for r in range(out_dil):
                    o_ref[r, :, cols] = slab_ref[0, pl.ds(r, tm // out_dil, stride=out_dil), :].astype(o_ref.dtype)

    acc_next = matmul(0)
    for t in range(n_total // tn):
        acc = acc_next
        if t + 1 < n_total // tn:
            acc_next = matmul(t + 1)
        epilogue(acc, t)


def _proj(lhs, ws, *, prologue, seq, tm, tn, out_dtype, norm=None, lhs_cols=None, rope=None,
          rope_pattern=None, sub_tn=None, out_scale=1.0, residual=None, in_dils=None, out_dil=1, name):
    k, n = ws[0].shape
    bsz_seq = lhs[0].shape[0] if prologue != "dil" else lhs[0].shape[0] * seq
    t = bsz_seq
    assert t % tm == 0 and n % tn == 0 and seq % tm == 0
    bpb = seq // tm
    n_lhs = len(ws)
    args, specs = [], []
    row = lambda i, j: (i, 0)
    if prologue in ("ada", "rms"):
        kw, kc = lhs_cols if lhs_cols is not None else (k, 0)
        args.append(lhs[0])
        specs.append(pl.BlockSpec((tm, kw), lambda i, j: (i, kc)))
        args.append(norm[0])
        specs.append(pl.BlockSpec((1, k), lambda i, j: (0, 0)))
        if prologue == "ada":
            for v in norm[1:3]:
                args.append(v)
                specs.append(pl.BlockSpec((None, 1, k), lambda i, j: (i // bpb, 0, 0)))
    elif prologue == "dil":
        for a, dil in zip(lhs, in_dils + in_dils):
            args.append(a)
            specs.append(pl.BlockSpec((None, dil, tm // dil, k), lambda i, j: (i // bpb, 0, i % bpb, 0)))
    else:
        for a, w in zip(lhs, ws):
            args.append(a)
            specs.append(pl.BlockSpec((tm, w.shape[0]), row))
    for w in ws:
        args.append(w)
        specs.append(pl.BlockSpec((w.shape[0], tn), lambda i, j: (0, j)))
    if rope_pattern is not None:
        assert len(rope_pattern) == tn // LANES
        for tab in rope:
            args.append(tab)
            specs.append(pl.BlockSpec((tm, LANES), row))
    if residual is not None:
        res, gate = residual
        args.append(res)
        specs.append(pl.BlockSpec((tm, tn), lambda i, j: (i, j)))
        args.append(gate)
        specs.append(pl.BlockSpec((None, 1, tn), lambda i, j: (i // bpb, 0, j)))
    scratch = [] if prologue == "plain" else [pltpu.VMEM((tm, k), BF16)]
    if prologue == "dil":
        scratch.append(pltpu.VMEM((2 * len(in_dils), tm, LANES), F32))
    elif out_dil > 1:
        scratch.append(pltpu.VMEM((1, tm, LANES), F32))
    if out_dil == 1:
        out_spec = pl.BlockSpec((tm, tn), lambda i, j: (i, j))
        out_shape = jax.ShapeDtypeStruct((t, n), out_dtype)
    else:
        out_spec = pl.BlockSpec((None, out_dil, tm // out_dil, tn), lambda i, j: (i // bpb, 0, i % bpb, j))
        out_shape = jax.ShapeDtypeStruct((t // seq, out_dil, seq // out_dil, n), out_dtype)
    kern = functools.partial(_proj_kernel, prologue=prologue, n_lhs=n_lhs, rope_pattern=rope_pattern,
                             sub_tn=sub_tn, out_scale=out_scale, residual=residual is not None,
                             in_dils=in_dils, out_dil=out_dil)
    return pl.pallas_call(
        kern,
        grid=(t // tm, n // tn),
        in_specs=specs,
        out_specs=out_spec,
        out_shape=out_shape,
        scratch_shapes=scratch,
        compiler_params=_cparams(("parallel", "arbitrary")),
        name=name,
    )(*args)


MLA_HEADS_PER_STEP = 4


def _mla_scores(q_ref, kn_ref, kr, h):
    k = jnp.concatenate([kn_ref[:, h * LANES:(h + 1) * LANES], kr], axis=1)
    q = q_ref[:, h * 2 * LANES:(h + 1) * 2 * LANES]
    return lax.dot_general(q, k, (((1,), (1,)), ((), ())), preferred_element_type=F32)


def _mla_kernel(q_ref, kn_ref, kr_ref, v_ref, o_ref):
    kr = kr_ref[...]
    s_next = _mla_scores(q_ref, kn_ref, kr, 0)
    for h in range(MLA_HEADS_PER_STEP):
        s = s_next
        if h + 1 < MLA_HEADS_PER_STEP:
            s_next = _mla_scores(q_ref, kn_ref, kr, h + 1)
        m = jnp.max(s, axis=1, keepdims=True)
        p = jnp.exp(s - m)
        den = jnp.sum(p, axis=1, keepdims=True)
        o = jnp.dot(p.astype(BF16), v_ref[:, h * LANES:(h + 1) * LANES], preferred_element_type=F32)
        o_ref[:, h * LANES:(h + 1) * LANES] = (o / den).astype(o_ref.dtype)


def _mla_attention(q_cat, kv, qkv_b, kr_col, *, bsz, seq, tq):
    t = bsz * seq
    nq = seq // tq
    hps = MLA_HEADS_PER_STEP
    groups = MLA_HEADS // hps
    return pl.pallas_call(
        _mla_kernel,
        grid=(bsz, groups, nq),
        in_specs=[pl.BlockSpec((tq, hps * 2 * LANES), lambda b, h, i: (b * nq + i, h)),
                  pl.BlockSpec((seq, hps * LANES), lambda b, h, i: (b, h)),
                  pl.BlockSpec((seq, LANES), lambda b, h, i: (b, kr_col)),
                  pl.BlockSpec((seq, hps * LANES), lambda b, h, i: (b, groups + h))],
        out_specs=pl.BlockSpec((tq, hps * LANES), lambda b, h, i: (b * nq + i, h)),
        out_shape=jax.ShapeDtypeStruct((t, MLA_HEADS * MLA_V), BF16),
        compiler_params=_cparams(("parallel", "parallel", "parallel")),
        name="mla_attention",
    )(q_cat, kv, qkv_b, kv)


def _banded_kernel(*refs, radius, tq, hb, length, n_pairs, pairs_per_kv, has_sink, want_lse):
    pos = 0
    if has_sink:
        sink_ref = refs[0]
        pos = 1
    q_ref, kp_ref, kc_ref, kn_ref, vp_ref, vc_ref, vn_ref = refs[pos:pos + 7]
    pos += 7
    o_ref = refs[pos]
    lse_ref = refs[pos + 1] if want_lse else None

    j = pl.program_id(2)
    kw = jnp.concatenate([kp_ref[...], kc_ref[...], kn_ref[...]], axis=0)
    vw = jnp.concatenate([vp_ref[...], vc_ref[...], vn_ref[...]], axis=0)
    scale = HEAD_DIM ** -0.5
    qn = min(tq, BAND_SUB_ROWS)
    kn = qn + 2 * hb
    lane = lax.broadcasted_iota(jnp.int32, (qn, LANES), 1)
    low = lane < HEAD_DIM
    first = (lane % HEAD_DIM) < ROPE_HALF

    for q0 in range(0, tq, qn):
        qpos = j * tq + q0 + lax.broadcasted_iota(jnp.int32, (qn, kn), 0)
        kpos = j * tq - hb + q0 + lax.broadcasted_iota(jnp.int32, (qn, kn), 1)
        valid = (jnp.abs(qpos - kpos) <= radius) & (kpos >= 0) & (kpos < length)
        for p in range(n_pairs):
            c = p // pairs_per_kv
            qp = q_ref[q0:q0 + qn, p * LANES:(p + 1) * LANES] * scale
            kp = kw[q0:q0 + kn, c * LANES:(c + 1) * LANES]
            vp = vw[q0:q0 + kn, c * LANES:(c + 1) * LANES]
            vp = jnp.concatenate([vp, jnp.ones_like(vp)], axis=1)
            halves = []
            for half in range(2):
                qh = jnp.where(first if half == 0 else ~first, qp, jnp.zeros_like(qp))
                s = lax.dot_general(qh, kp, (((1,), (1,)), ((), ())), preferred_element_type=F32)
                s = jnp.where(valid, s, -jnp.inf)
                m = jnp.max(s, axis=1, keepdims=True)
                if has_sink:
                    sk = sink_ref[2 * p + half]
                    m = jnp.maximum(m, sk)
                oe = jnp.dot(jnp.exp(s - m).astype(BF16), vp, preferred_element_type=F32)
                den = oe[:, LANES:]
                if has_sink:
                    den = den + jnp.exp(sk - m)
                halves.append((oe[:, :LANES] / den, m + jnp.log(den) if want_lse else None))
            o_ref[q0:q0 + qn, p * LANES:(p + 1) * LANES] = jnp.where(
                low, halves[0][0], halves[1][0]).astype(o_ref.dtype)
            if want_lse:
                lse_ref[q0:q0 + qn, p * LANES:(p + 1) * LANES] = jnp.where(low, halves[0][1], halves[1][1])


def _banded_attention(q_arr, k_arr, v_arr, *, bsz, seq, dil, radius, q_col, k_col, v_col, kv_width,
                      pairs_per_kv, sink=None, want_lse, name):
    length = seq // dil
    hb = radius
    tq = min(256, length)
    assert tq % hb == 0 and length % tq == 0
    qw = DIL_HEADS * HEAD_DIM
    n_pairs = qw // LANES
    rep = tq // hb
    last_hb = length // hb - 1

    def prev_spec(col):
        return pl.BlockSpec((None, None, hb, kv_width),
                            lambda b, r, j: (b, r, jnp.maximum(j * rep - 1, 0), col))

    def cur_spec(col):
        return pl.BlockSpec((None, None, tq, kv_width), lambda b, r, j: (b, r, j, col))

    def next_spec(col):
        return pl.BlockSpec((None, None, hb, kv_width),
                            lambda b, r, j: (b, r, jnp.minimum((j + 1) * rep, last_hb), col))

    in_specs = [pl.BlockSpec((None, None, tq, qw), lambda b, r, j: (b, r, j, q_col)),
                prev_spec(k_col), cur_spec(k_col), next_spec(k_col),
                prev_spec(v_col), cur_spec(v_col), next_spec(v_col)]
    args = [q_arr, k_arr, k_arr, k_arr, v_arr, v_arr, v_arr]
    if sink is not None:
        in_specs = [pl.BlockSpec(memory_space=pltpu.SMEM)] + in_specs
        args = [sink] + args
    out_spec = pl.BlockSpec((None, None, tq, qw), lambda b, r, j: (b, r, j, 0))
    out_specs = [out_spec]
    out_shape = [jax.ShapeDtypeStruct((bsz, dil, length, qw), BF16)]
    if want_lse:
        out_specs.append(out_spec)
        out_shape.append(jax.ShapeDtypeStruct((bsz, dil, length, qw), F32))
    kern = functools.partial(_banded_kernel, radius=radius, tq=tq, hb=hb, length=length, n_pairs=n_pairs,
                             pairs_per_kv=pairs_per_kv, has_sink=sink is not None, want_lse=want_lse)
    return pl.pallas_call(
        kern,
        grid=(bsz, dil, length // tq),
        in_specs=in_specs,
        out_specs=out_specs,
        out_shape=out_shape,
        compiler_params=_cparams(("parallel", "parallel", "parallel")),
        name=name,
    )(*args)


def _router_kernel(x_ref, g_ref, sc_ref, sh_ref, wt_ref, b_ref, tri_ref,
                   idx_ref, gate_ref, rank_ref, cnt_ref, h_ref, carry_ref):
    @pl.when(pl.program_id(0) == 0)
    def _():
        carry_ref[...] = jnp.zeros_like(carry_ref)

    h = _ada_norm(x_ref[...], g_ref[...], sc_ref[...], sh_ref[...])
    h_ref[...] = h
    lt = lax.dot_general(wt_ref[...], h, (((1,), (1,)), ((), ())), preferred_element_type=F32,
                         precision=lax.Precision.HIGHEST) + b_ref[...]
    n_e, tm = lt.shape
    e_iota = lax.broadcasted_iota(jnp.int32, (n_e, tm), 0)
    vals, sels = [], []
    for k in range(TOP_K):
        m = jnp.max(lt, axis=0, keepdims=True)
        idx = jnp.min(jnp.where(lt == m, e_iota, n_e), axis=0, keepdims=True)
        sel = e_iota == idx
        idx_ref[k:k + 1, :] = idx
        vals.append(m)
        sels.append(sel)
        lt = jnp.where(sel, -jnp.inf, lt)
    exps = [jnp.exp(v - vals[0]) for v in vals]
    den = exps[0] + exps[1] + exps[2] + exps[3]
    for k in range(TOP_K):
        gate_ref[k:k + 1, :] = exps[k] / den

    onehot = jnp.where(sels[0] | sels[1] | sels[2] | sels[3], 1.0, 0.0)
    carry = carry_ref[:, 0:1]
    before = jnp.dot(onehot.astype(BF16), tri_ref[...], preferred_element_type=F32) + carry
    for k in range(TOP_K):
        rank_ref[k:k + 1, :] = jnp.sum(jnp.where(sels[k], before, 0.0), axis=0, keepdims=True).astype(jnp.int32)
    total = carry + jnp.sum(onehot, axis=1, keepdims=True)
    carry_ref[...] = jnp.broadcast_to(total, carry_ref.shape)
    cnt_ref[...] = jnp.broadcast_to(total, cnt_ref.shape)


def _router(x, norm, w_router, b_router, *, seq, tm=512):
    t, d = x.shape
    n_e = w_router.shape[1]
    bpb = seq // tm
    tri = (jnp.arange(tm)[:, None] < jnp.arange(tm)[None, :]).astype(BF16)
    g, sc, sh = norm
    out_i = jax.ShapeDtypeStruct((TOP_K, t), jnp.int32)
    tok = pl.BlockSpec((TOP_K, tm), lambda i: (0, i))
    return pl.pallas_call(
        _router_kernel,
        grid=(t // tm,),
        in_specs=[pl.BlockSpec((tm, d), lambda i: (i, 0)),
                  pl.BlockSpec((1, d), lambda i: (0, 0)),
                  pl.BlockSpec((None, 1, d), lambda i: (i // bpb, 0, 0)),
                  pl.BlockSpec((None, 1, d), lambda i: (i // bpb, 0, 0)),
                  pl.BlockSpec((n_e, d), lambda i: (0, 0)),
                  pl.BlockSpec((n_e, 1), lambda i: (0, 0)),
                  pl.BlockSpec((tm, tm), lambda i: (0, 0))],
        out_specs=[tok, tok, tok, pl.BlockSpec((n_e, LANES), lambda i: (0, 0)),
                   pl.BlockSpec((tm, d), lambda i: (i, 0))],
        out_shape=[out_i, jax.ShapeDtypeStruct((TOP_K, t), F32), out_i,
                   jax.ShapeDtypeStruct((n_e, LANES), F32), jax.ShapeDtypeStruct((t, d), F32)],
        scratch_shapes=[pltpu.VMEM((n_e, LANES), F32)],
        compiler_params=_cparams(("arbitrary",)),
        name="router",
    )(x, g, sc, sh, w_router.T, b_router.reshape(n_e, 1), tri)


SC_CORES = 2
SC_SUBCORES = 16
SC_WORKERS = SC_CORES * SC_SUBCORES
SC_ROWS = 32
SC_INDEX_WINDOW = 128


def _sc_worker_id():
    return lax.axis_index("subcore") * SC_CORES + lax.axis_index("core")


def _sc_mesh():
    return plsc.VectorSubcoreMesh(core_axis_name="core", subcore_axis_name="subcore")


def _sc_dispatch(h, dest_flat, pad_rows):
    t, d = h.shape
    n_pad = pad_rows.shape[0]
    n_rows = TOP_K * t + n_pad
    tok_w = t // SC_WORKERS
    pad_w = n_pad // SC_WORKERS
    assert tok_w % SC_ROWS == 0 and pad_w % SC_ROWS == 0

    @functools.partial(
        pl.kernel, out_type=jax.ShapeDtypeStruct((n_rows, d), h.dtype), mesh=_sc_mesh(),
        scratch_types=[pltpu.VMEM((SC_ROWS,), jnp.int32), pltpu.VMEM((SC_ROWS, d), h.dtype)])
    def scatter_rows(h_hbm, dest_hbm, pad_hbm, zero_hbm, xs_hbm, idx_v, rows_v):
        wid = _sc_worker_id()

        @pl.loop(0, tok_w // SC_ROWS)
        def _(b):
            t0 = wid * tok_w + b * SC_ROWS
            pltpu.sync_copy(h_hbm.at[pl.ds(t0, SC_ROWS)], rows_v)
            for k in range(TOP_K):
                pltpu.sync_copy(dest_hbm.at[pl.ds(k * t + t0, SC_ROWS)], idx_v)
                pltpu.sync_copy(rows_v, xs_hbm.at[idx_v])

        pltpu.sync_copy(zero_hbm, rows_v)

        @pl.loop(0, pad_w // SC_ROWS)
        def _(b):
            pltpu.sync_copy(pad_hbm.at[pl.ds(wid * pad_w + b * SC_ROWS, SC_ROWS)], idx_v)
            pltpu.sync_copy(rows_v, xs_hbm.at[idx_v])

    return scatter_rows(h, dest_flat, pad_rows, jnp.zeros((SC_ROWS, d), h.dtype))


def _sc_gather(table, idx):
    n = idx.shape[0]
    d = table.shape[1]
    per_w = n // SC_WORKERS
    assert per_w % SC_INDEX_WINDOW == 0

    @functools.partial(
        pl.kernel, out_type=jax.ShapeDtypeStruct((n, d), table.dtype), mesh=_sc_mesh(),
        scratch_types=[pltpu.VMEM((SC_INDEX_WINDOW,), jnp.int32), pltpu.VMEM((SC_ROWS, d), table.dtype)])
    def gather_rows(table_hbm, idx_hbm, out_hbm, idx_v, rows_v):
        base = _sc_worker_id() * per_w

        @pl.loop(0, per_w // SC_INDEX_WINDOW)
        def _(b):
            off = base + b * SC_INDEX_WINDOW
            pltpu.sync_copy(idx_hbm.at[pl.ds(off, SC_INDEX_WINDOW)], idx_v)
            for c in range(SC_INDEX_WINDOW // SC_ROWS):
                pltpu.sync_copy(table_hbm.at[idx_v.at[pl.ds(c * SC_ROWS, SC_ROWS)]], rows_v)
                pltpu.sync_copy(rows_v, out_hbm.at[pl.ds(off + c * SC_ROWS, SC_ROWS)])

    return gather_rows(table, idx)


def _expert_kernel(be_ref, nu_ref, xs_ref, wgu_ref, bgu_ref, wd_ref, bd_ref, y_ref, wgu_bf, wd_bf):
    i = pl.program_id(0)
    used = i < nu_ref[0]

    @pl.when(used & ((i == 0) | (be_ref[i] != be_ref[jnp.maximum(i - 1, 0)])))
    def _():
        wgu_bf[...] = wgu_ref[...].astype(BF16)
        wd_bf[...] = wd_ref[...].astype(BF16)

    @pl.when(used)
    def _():
        d_ff = wd_ref.shape[0]
        gu = jnp.dot(xs_ref[...].astype(BF16), wgu_bf[...], preferred_element_type=F32) + bgu_ref[...]
        glu = jnp.minimum(gu[:, :d_ff], SWIGLU_LIMIT)
        lin = jnp.clip(gu[:, d_ff:], -SWIGLU_LIMIT, SWIGLU_LIMIT)
        act = glu * (1.0 / (1.0 + jnp.exp(-SWIGLU_ALPHA * glu))) * (lin + 1.0)
        y_ref[...] = jnp.dot(act.astype(BF16), wd_bf[...], preferred_element_type=F32) + bd_ref[...]

    @pl.when(jnp.logical_not(used))
    def _():
        y_ref[...] = jnp.zeros_like(y_ref)


def _experts(xs, blk_expert, n_used, layer, w_gu, b_gu, w_down, b_down):
    n_rows, d = xs.shape
    _, n_e, _, f2 = w_gu.shape
    n_blocks = n_rows // MOE_ROWS

    def blk(i, be, nu):
        return jnp.minimum(i, nu[0] - 1)

    def wmap(i, be, nu):
        return (layer, be[blk(i, be, nu)], 0, 0)

    grid_spec = pltpu.PrefetchScalarGridSpec(
        num_scalar_prefetch=2,
        grid=(n_blocks,),
        in_specs=[pl.BlockSpec((MOE_ROWS, d), lambda i, be, nu: (blk(i, be, nu), 0)),
                  pl.BlockSpec((None, None, d, f2), wmap),
                  pl.BlockSpec((None, None, 1, f2), wmap),
                  pl.BlockSpec((None, None, f2 // 2, d), wmap),
                  pl.BlockSpec((None, None, 1, d), wmap)],
        out_specs=pl.BlockSpec((MOE_ROWS, d), lambda i, be, nu: (i, 0)),
        scratch_shapes=[pltpu.VMEM((d, f2), BF16), pltpu.VMEM((f2 // 2, d), BF16)],
    )
    depth = w_gu.shape[0]
    return pl.pallas_call(
        _expert_kernel,
        grid_spec=grid_spec,
        out_shape=jax.ShapeDtypeStruct((n_rows, d), F32),
        compiler_params=pltpu.CompilerParams(dimension_semantics=("arbitrary",),
                                             vmem_limit_bytes=EXPERT_VMEM_LIMIT),
        name="moe_experts",
    )(blk_expert, n_used, xs, w_gu, b_gu.reshape(depth, n_e, 1, f2), w_down, b_down.reshape(depth, n_e, 1, d))


def _combine_kernel(y0_ref, y1_ref, y2_ref, y3_ref, gate_ref, x_ref, g2_ref, o_ref):
    gate = gate_ref[...]
    y = gate[:, 0:1] * y0_ref[...]
    for k, y_ref in enumerate((y1_ref, y2_ref, y3_ref), start=1):
        y = y + gate[:, k:k + 1] * y_ref[...]
    o_ref[...] = x_ref[...] + g2_ref[...] * y


def _combine(y4, gate, x, g2, *, seq, tm=512):
    t, d = x.shape
    bpb = seq // tm
    nblk = t // tm
    y_specs = [pl.BlockSpec((tm, d), functools.partial(lambda i, k: (k * nblk + i, 0), k=k)) for k in range(TOP_K)]
    return pl.pallas_call(
        _combine_kernel,
        grid=(nblk,),
        in_specs=y_specs + [pl.BlockSpec((tm, TOP_K), lambda i: (i, 0)),
                            pl.BlockSpec((tm, d), lambda i: (i, 0)),
                            pl.BlockSpec((None, 1, d), lambda i: (i // bpb, 0, 0))],
        out_specs=pl.BlockSpec((tm, d), lambda i: (i, 0)),
        out_shape=jax.ShapeDtypeStruct((t, d), F32),
        compiler_params=_cparams(("parallel",)),
        name="moe_combine",
    )(y4, y4, y4, y4, gate, x, g2)


def _moe(x, norm, g2, w_router, b_router, layer, w_gu, b_gu, w_down, b_down, *, seq):
    t, d = x.shape
    n_e = w_router.shape[1]
    idx_t, gate_t, rank_t, counts, h = _router(x, norm, w_router, b_router, seq=seq)
    e_ids = jnp.arange(n_e, dtype=jnp.int32)
    counts = counts[:, 0].astype(jnp.int32)
    padded = (counts + MOE_ROWS - 1) // MOE_ROWS * MOE_ROWS
    pend = jnp.cumsum(padded)
    pstart = pend - padded
    n_pad = n_e * MOE_ROWS
    n_rows = t * TOP_K + n_pad
    n_blocks = n_rows // MOE_ROWS
    dest_t = jnp.sum(jnp.where(idx_t[:, :, None] == e_ids, pstart, 0), axis=-1) + rank_t
    dest_flat = dest_t.reshape(TOP_K * t)
    blk_start = jnp.arange(n_blocks, dtype=jnp.int32) * MOE_ROWS
    blk_expert = jnp.minimum(jnp.sum(blk_start[:, None] >= pend[None, :], axis=1), n_e - 1).astype(jnp.int32)
    n_used = (pend[-1:] // MOE_ROWS).astype(jnp.int32)
    pad_len = padded - counts
    pad_end = jnp.cumsum(pad_len)
    j = jnp.arange(n_pad, dtype=jnp.int32)
    pe = jnp.sum(j[:, None] >= pad_end[None, :], axis=1)
    sel = jnp.minimum(pe, n_e - 1)[:, None] == e_ids
    in_expert = jnp.sum(jnp.where(sel, pstart + counts - (pad_end - pad_len), 0), axis=-1) + j
    pad_rows = jnp.where(pe < n_e, in_expert, pend[-1] + j - pad_end[-1]).astype(jnp.int32)

    xs = _sc_dispatch(h, dest_flat, pad_rows)
    yb = _experts(xs, blk_expert, n_used, layer, w_gu, b_gu, w_down, b_down)
    y4 = _sc_gather(yb, dest_flat)
    return _combine(y4, gate_t.T, x, g2, seq=seq)


def _final_norm_kernel(x_ref, g_ref, o_ref):
    o_ref[...] = _rms(x_ref[...]) * g_ref[...]


def _final_norm(x, g, tm=1024):
    t, d = x.shape
    return pl.pallas_call(
        _final_norm_kernel,
        grid=(t // tm,),
        in_specs=[pl.BlockSpec((tm, d), lambda i: (i, 0)), pl.BlockSpec((1, d), lambda i: (0, 0))],
        out_specs=pl.BlockSpec((tm, d), lambda i: (i, 0)),
        out_shape=jax.ShapeDtypeStruct((t, d), F32),
        compiler_params=_cparams(("parallel",)),
        name="final_norm",
    )(x, g.reshape(1, d))


def _dup_heads(w, n_heads):
    d = w.shape[0]
    return jnp.repeat(w.reshape(d, n_heads, 1, HEAD_DIM), 2, axis=2).reshape(d, n_heads * LANES)


def _mixer_mla_swa(x, norm, g1, cos, sin, w_in, g_q, w_qb, g_kv, w_kvb, sink, w_out, *, bsz, seq):
    d = x.shape[1]
    o = np.cumsum((0, MLA_Q_LORA, MLA_KV_LORA, MLA_ROPE, SWA_Q_HEADS * HEAD_DIM,
                   SWA_KV_HEADS * HEAD_DIM, SWA_KV_HEADS * HEAD_DIM))
    w_cq, w_ckv, w_kr, w_qs, w_ks, w_vs = (w_in[:, o[i]:o[i + 1]] for i in range(6))
    w_lat = jnp.concatenate([w_cq, jnp.zeros((d, LANES), F32), w_ckv], axis=1).astype(BF16)
    lat = _proj([x], [w_lat], prologue="ada", norm=norm, seq=seq, tm=512, tn=w_lat.shape[1],
                out_dtype=F32, name="proj_latent")
    w_b = jnp.concatenate([_pair_layout(w_qs), _pair_layout(_dup_heads(w_ks, SWA_KV_HEADS)),
                           _dup_heads(w_vs, SWA_KV_HEADS),
                           w_kr, jnp.zeros((d, LANES - MLA_ROPE), F32)], axis=1).astype(BF16)
    qkv_b = _proj([x], [w_b], prologue="ada", norm=norm, seq=seq, tm=512, tn=w_b.shape[1], out_dtype=BF16,
                  rope=(cos, sin), rope_pattern=(ROPE_PAIR,) * 10 + (0, 0) + (ROPE_ROTATE,), name="proj_swa")
    kr_col = (w_b.shape[1] - LANES) // LANES
    qd = MLA_NOPE + MLA_ROPE
    w_q = w_qb.reshape(MLA_Q_LORA, MLA_HEADS, qd)
    w_q = jnp.concatenate([w_q, jnp.zeros((MLA_Q_LORA, MLA_HEADS, 2 * LANES - qd), F32)], axis=2)
    w_q = w_q.reshape(MLA_Q_LORA, MLA_HEADS * 2 * LANES).astype(BF16)
    q_cat = _proj([lat], [w_q], prologue="rms", norm=(g_q.reshape(1, -1),), lhs_cols=(MLA_Q_LORA, 0),
                  seq=seq, tm=512, tn=w_q.shape[1], out_dtype=BF16, rope=(cos, sin),
                  rope_pattern=(0, 1) * MLA_HEADS, out_scale=qd ** -0.5, name="proj_mla_q")
    w_kv = w_kvb.reshape(MLA_KV_LORA, MLA_HEADS, MLA_NOPE + MLA_V)
    w_kv = jnp.concatenate([w_kv[:, :, :MLA_NOPE].reshape(MLA_KV_LORA, -1),
                            w_kv[:, :, MLA_NOPE:].reshape(MLA_KV_LORA, -1)], axis=1).astype(BF16)
    kv = _proj([lat], [w_kv], prologue="rms", norm=(g_kv.reshape(1, -1),), lhs_cols=(MLA_KV_LORA, 2),
               seq=seq, tm=512, tn=w_kv.shape[1], out_dtype=BF16, name="proj_mla_kv")
    o_a = _mla_attention(q_cat, kv, qkv_b, kr_col, bsz=bsz, seq=seq, tq=256)
    qkv_b4 = qkv_b.reshape(bsz, 1, seq, qkv_b.shape[1])
    (o_b,) = _banded_attention(qkv_b4, qkv_b4, qkv_b4, bsz=bsz, seq=seq, dil=1, radius=SWA_RADIUS,
                               q_col=0, k_col=4, v_col=5, kv_width=2 * LANES,
                               pairs_per_kv=SWA_Q_HEADS // SWA_KV_HEADS // 2, sink=sink, want_lse=False,
                               name="swa_attention")
    o_b = o_b.reshape(bsz * seq, SWA_Q_HEADS * HEAD_DIM)
    na = MLA_HEADS * MLA_V
    return _proj([o_a, o_b], [w_out[:na].astype(BF16), w_out[na:].astype(BF16)], prologue="plain",
                 seq=seq, tm=512, tn=d, out_dtype=F32, residual=(x, g1), name="proj_out_ab")


def _mixer_dilated(x, norm, g1, cos, sin, w_in, w_out, *, bsz, seq):
    d = x.shape[1]
    qw = DIL_HEADS * HEAD_DIM
    w_bf = w_in.astype(BF16)
    outs, lses = [], []
    for g, (window, dil) in enumerate(DIL_PATTERN):
        w_g = w_bf[:, g * 3 * qw:(g + 1) * 3 * qw]
        w_g = jnp.concatenate([_pair_layout(w_g[:, :2 * qw]), w_g[:, 2 * qw:]], axis=1)
        qkv = _proj([x], [w_g], prologue="ada", norm=norm, seq=seq, tm=512, tn=3 * qw, sub_tn=qw, out_dtype=BF16,
                    rope=(cos, sin), rope_pattern=(ROPE_PAIR,) * (2 * qw // LANES) + (0,) * (qw // LANES),
                    out_dil=dil, name=f"proj_dil_{dil}")
        qkv = qkv.reshape(bsz, dil, seq // dil, 3 * qw)
        o, lse = _banded_attention(qkv, qkv, qkv, bsz=bsz, seq=seq, dil=dil, radius=window // (2 * dil),
                                   q_col=0, k_col=1, v_col=2, kv_width=qw, pairs_per_kv=1,
                                   want_lse=True, name=f"dilated_attention_{dil}")
        outs.append(o)
        lses.append(lse)
    return _proj(outs + lses, [w_out.astype(BF16)], prologue="dil", seq=seq, tm=512, tn=d, out_dtype=F32,
                 residual=(x, g1), in_dils=tuple(dil for _, dil in DIL_PATTERN), name="proj_out_c")


def kernel(x, c, positions, w_mod, b_mod, g_norm_mix, g_norm_ffn, w_in_ab, mla_g_q, mla_w_qb, mla_g_kv,
           mla_w_kvb, swa_sink, w_out_ab, w_in_c, w_out_c, w_router, b_router, w_gu, b_gu, w_down, b_down,
           g_final):
    bsz, seq, d = x.shape
    depth = w_mod.shape[0]
    cos, sin = _rope_tables(positions)
    mod = _modulation(c, w_mod, b_mod)
    xt = x.reshape(bsz * seq, d)
    for layer in range(depth):
        sh1, sc1, g1, sh2, sc2, g2 = (mod[layer, :, i * d:(i + 1) * d].reshape(bsz, 1, d) for i in range(6))
        li = layer // 2
        norm = (g_norm_mix[layer].reshape(1, d), sc1, sh1)
        if layer % 2 == 0:
            xt = _mixer_mla_swa(xt, norm, g1, cos, sin, w_in_ab[li], mla_g_q[li], mla_w_qb[li], mla_g_kv[li],
                                mla_w_kvb[li], swa_sink[li], w_out_ab[li], bsz=bsz, seq=seq)
        else:
            xt = _mixer_dilated(xt, norm, g1, cos, sin, w_in_c[li], w_out_c[li], bsz=bsz, seq=seq)
        norm = (g_norm_ffn[layer].reshape(1, d), sc2, sh2)
        xt = _moe(xt, norm, g2, w_router[layer], b_router[layer], layer, w_gu, b_gu, w_down, b_down, seq=seq)
    return _final_norm(xt, g_final).reshape(bsz, seq, d)
```

```python
import functools

import jax
import jax.numpy as jnp
import numpy as np
from jax import lax
from jax.experimental import pallas as pl
from jax.experimental.pallas import tpu as pltpu
from jax.experimental.pallas import tpu_sc as plsc

F32 = jnp.float32
BF16 = jnp.bfloat16

EPS = 1e-6
ROPE_THETA = 10000.0
LANES = 128
HEAD_DIM = 64
ROPE_HALF = HEAD_DIM // 2

MLA_HEADS = 8
MLA_Q_LORA = 384
MLA_KV_LORA = 256
MLA_NOPE = 128
MLA_ROPE = 64
MLA_V = 128
SWA_Q_HEADS = 16
SWA_KV_HEADS = 2
SWA_RADIUS = 128
DIL_PATTERN = ((128, 1), (512, 4), (2048, 16))
DIL_HEADS = 16
N_EXPERTS = 32
TOP_K = 4
SWIGLU_LIMIT = 7.0
SWIGLU_ALPHA = 1.702
MOE_ROWS = 512
BAND_SUB_ROWS = 128

VMEM_LIMIT = 48 * 1024 * 1024
EXPERT_VMEM_LIMIT = 56 * 1024 * 1024


def _cparams(sem):
    return pltpu.CompilerParams(dimension_semantics=sem, vmem_limit_bytes=VMEM_LIMIT)


def _mod_kernel(c_ref, w_ref, b_ref, o_ref):
    c = c_ref[...]
    ca = c * (1.0 / (1.0 + jnp.exp(-c)))
    o_ref[...] = jnp.dot(ca, w_ref[...], preferred_element_type=F32,
                         precision=lax.Precision.HIGHEST) + b_ref[...]


def _modulation(c, w_mod, b_mod):
    depth, d, n = w_mod.shape
    bsz = c.shape[0]
    tn = 1536
    return pl.pallas_call(
        _mod_kernel,
        grid=(depth, n // tn),
        in_specs=[pl.BlockSpec((bsz, d), lambda l, j: (0, 0)),
                  pl.BlockSpec((None, d, tn), lambda l, j: (l, 0, j)),
                  pl.BlockSpec((None, 1, tn), lambda l, j: (l, 0, j))],
        out_specs=pl.BlockSpec((None, bsz, tn), lambda l, j: (l, 0, j)),
        out_shape=jax.ShapeDtypeStruct((depth, bsz, n), F32),
        compiler_params=_cparams(("parallel", "parallel")),
        name="modulation",
    )(c, w_mod, b_mod.reshape(depth, 1, n))


def _rope_table_kernel(pos_ref, inv_ref, sign_ref, cos_ref, sin_ref):
    ang = inv_ref[...] * pos_ref[...].astype(F32)
    cos_ref[...] = jnp.transpose(jnp.cos(ang))
    sin_ref[...] = jnp.transpose(jnp.sin(ang) * sign_ref[...])


def _rope_tables(positions):
    bsz, s = positions.shape
    inv = ROPE_THETA ** (-jnp.arange(0, HEAD_DIM, 2, dtype=F32) / HEAD_DIM)
    inv128 = jnp.tile(inv, LANES // ROPE_HALF).reshape(LANES, 1)
    sign = jnp.where((jnp.arange(LANES) % HEAD_DIM) < ROPE_HALF, -1.0, 1.0).astype(F32).reshape(LANES, 1)
    out = jax.ShapeDtypeStruct((bsz * s, LANES), F32)
    return pl.pallas_call(
        _rope_table_kernel,
        grid=(bsz,),
        in_specs=[pl.BlockSpec((None, 1, s), lambda b: (b, 0, 0)),
                  pl.BlockSpec((LANES, 1), lambda b: (0, 0)),
                  pl.BlockSpec((LANES, 1), lambda b: (0, 0))],
        out_specs=[pl.BlockSpec((s, LANES), lambda b: (b, 0)),
                   pl.BlockSpec((s, LANES), lambda b: (b, 0))],
        out_shape=[out, out],
        compiler_params=_cparams(("parallel",)),
        name="rope_tables",
    )(positions.reshape(bsz, 1, s), inv128, sign)


def _rms(x):
    return x * lax.rsqrt(jnp.mean(x * x, axis=-1, keepdims=True) + EPS)


def _ada_norm(x, g, sc, sh):
    return (_rms(x) * g) * (1.0 + sc) + sh


def _pack_bf16_pairs(x):
    w = x.shape[1] // 2
    hi = pltpu.bitcast(x[:, :w].astype(BF16).astype(F32), jnp.int32)
    lo = pltpu.bitcast(x[:, w:].astype(BF16).astype(F32), jnp.int32)
    return hi | lax.shift_right_logical(lo, 16)


def _unpack_bf16_pairs(p):
    hi = pltpu.bitcast(p & jnp.int32(-65536), F32)
    lo = pltpu.bitcast(lax.shift_left(p, 16), F32)
    return hi, lo


ROPE_ROTATE, ROPE_PAIR = 1, 2


def _pair_layout(w):
    k, n = w.shape
    w = w.reshape(k, n // LANES, 2, 2, ROPE_HALF)
    return jnp.swapaxes(w, 2, 3).reshape(k, n)


def _rope_group(a, cos, sin, first_half):
    rot = jnp.where(first_half, pltpu.roll(a, LANES - ROPE_HALF, 1), pltpu.roll(a, ROPE_HALF, 1))
    return a * cos + rot * sin


def _proj_kernel(*refs, prologue, n_lhs, rope_pattern, sub_tn, out_scale, residual, in_dils, out_dil):
    pos = 0
    if prologue == "ada":
        x_ref, g_ref, sc_ref, sh_ref = refs[:4]
        pos = 4
    elif prologue == "rms":
        x_ref, g_ref = refs[:2]
        pos = 2
    elif prologue == "dil":
        o_refs = refs[0:3]
        l_refs = refs[3:6]
        pos = 6
    else:
        x_refs = refs[:n_lhs]
        pos = n_lhs
    w_refs = refs[pos:pos + n_lhs]
    pos += n_lhs
    if rope_pattern is not None:
        cos_ref, sin_ref = refs[pos:pos + 2]
        pos += 2
    if residual:
        res_ref, gate_ref = refs[pos:pos + 2]
        pos += 2
    o_ref = refs[pos]
    pos += 1
    h_ref = refs[pos] if prologue != "plain" else None
    slab_ref = refs[pos + 1] if (prologue == "dil" or out_dil > 1) else None

    if prologue != "plain":
        @pl.when(pl.program_id(1) == 0)
        def _():
            if prologue == "ada":
                h_ref[...] = _ada_norm(x_ref[...], g_ref[...], sc_ref[...], sh_ref[...]).astype(BF16)
            elif prologue == "rms":
                h_ref[...] = (_rms(x_ref[...]) * g_ref[...]).astype(BF16)
            else:
                tm = h_ref.shape[0]
                for c in range(h_ref.shape[1] // LANES):
                    cols = slice(c * LANES, (c + 1) * LANES)
                    vals = []
                    for i, (ref, dil) in enumerate(zip(o_refs + l_refs, in_dils + in_dils)):
                        if dil == 1:
                            vals.append(ref[0, :, cols].astype(F32))
                        else:
                            for r in range(dil):
                                slab_ref[i, pl.ds(r, tm // dil, stride=dil), :] = ref[r, :, cols].astype(F32)
                            vals.append(slab_ref[i])
                    o0, o1, o2, l0, l1, l2 = vals
                    m = jnp.maximum(jnp.maximum(l0, l1), l2)
                    e0, e1, e2 = jnp.exp(l0 - m), jnp.exp(l1 - m), jnp.exp(l2 - m)
                    h_ref[:, cols] = ((e0 * o0 + e1 * o1 + e2 * o2) / (e0 + e1 + e2)).astype(BF16)

    n_total = w_refs[0].shape[1]
    tn = n_total if sub_tn is None else sub_tn

    def matmul(t):
        cols = slice(t * tn, (t + 1) * tn)
        if prologue != "plain":
            acc = jnp.dot(h_ref[...], w_refs[0][:, cols], preferred_element_type=F32)
        else:
            acc = jnp.dot(x_refs[0][...], w_refs[0][:, cols], preferred_element_type=F32)
            for xr, wr in zip(x_refs[1:], w_refs[1:]):
                acc = acc + jnp.dot(xr[...], wr[:, cols], preferred_element_type=F32)
        return acc * out_scale if out_scale != 1.0 else acc

    if residual:
        o_ref[...] = (res_ref[...] + gate_ref[...] * matmul(0)).astype(o_ref.dtype)
        return

    pattern = rope_pattern if rope_pattern is not None else (0,) * (n_total // LANES)
    if any(pattern):
        cos = cos_ref[...]
        sin = sin_ref[...]
        lane = lax.broadcasted_iota(jnp.int32, cos.shape, 1)
        first_half = (lane % HEAD_DIM) < ROPE_HALF
        if ROPE_PAIR in pattern:
            sin_pair = jnp.where(lane < HEAD_DIM, -1.0, 1.0) * jnp.where(first_half, -sin, sin)

    def epilogue(acc, t):
        tm = acc.shape[0]
        for c in range(tn // LANES):
            g = t * (tn // LANES) + c
            cols = slice(g * LANES, (g + 1) * LANES)
            a = acc[:, c * LANES:(c + 1) * LANES]
            if pattern[g] == ROPE_ROTATE:
                a = _rope_group(a, cos, sin, first_half)
            elif pattern[g] == ROPE_PAIR:
                a = a * cos + pltpu.roll(a, HEAD_DIM, 1) * sin_pair
            if out_dil == 1:
                o_ref[:, cols] = a.astype(o_ref.dtype)
            else:
                slab_ref[0] = a
                for r in range(out_dil):
                    o_ref[r, :, cols] = slab_ref[0, pl.ds(r, tm // out_dil, stride=out_dil), :].astype(o_ref.dtype)

    acc_next = matmul(0)
    for t in range(n_total // tn):
        acc = acc_next
        if t + 1 < n_total // tn:
            acc_next = matmul(t + 1)
        epilogue(acc, t)


def _proj(lhs, ws, *, prologue, seq, tm, tn, out_dtype, norm=None, lhs_cols=None, rope=None,
          rope_pattern=None, sub_tn=None, out_scale=1.0, residual=None, in_dils=None, out_dil=1, name):
    k, n = ws[0].shape
    bsz_seq = lhs[0].shape[0] if prologue != "dil" else lhs[0].shape[0] * seq
    t = bsz_seq
    assert t % tm == 0 and n % tn == 0 and seq % tm == 0
    bpb = seq // tm
    n_lhs = len(ws)
    args, specs = [], []
    row = lambda i, j: (i, 0)
    if prologue in ("ada", "rms"):
        kw, kc = lhs_cols if lhs_cols is not None else (k, 0)
        args.append(lhs[0])
        specs.append(pl.BlockSpec((tm, kw), lambda i, j: (i, kc)))
        args.append(norm[0])
        specs.append(pl.BlockSpec((1, k), lambda i, j: (0, 0)))
        if prologue == "ada":
            for v in norm[1:3]:
                args.append(v)
                specs.append(pl.BlockSpec((None, 1, k), lambda i, j: (i // bpb, 0, 0)))
    elif prologue == "dil":
        for a, dil in zip(lhs, in_dils + in_dils):
            args.append(a)
            specs.append(pl.BlockSpec((None, dil, tm // dil, k), lambda i, j: (i // bpb, 0, i % bpb, 0)))
    else:
        for a, w in zip(lhs, ws):
            args.append(a)
            specs.append(pl.BlockSpec((tm, w.shape[0]), row))
    for w in ws:
        args.append(w)
        specs.append(pl.BlockSpec((w.shape[0], tn), lambda i, j: (0, j)))
    if rope_pattern is not None:
        assert len(rope_pattern) == tn // LANES
        for tab in rope:
            args.append(tab)
            specs.append(pl.BlockSpec((tm, LANES), row))
    if residual is not None:
        res, gate = residual
        args.append(res)
        specs.append(pl.BlockSpec((tm, tn), lambda i, j: (i, j)))
        args.append(gate)
        specs.append(pl.BlockSpec((None, 1, tn), lambda i, j: (i // bpb, 0, j)))
    scratch = [] if prologue == "plain" else [pltpu.VMEM((tm, k), BF16)]
    if prologue == "dil":
        scratch.append(pltpu.VMEM((2 * len(in_dils), tm, LANES), F32))
    elif out_dil > 1:
        scratch.append(pltpu.VMEM((1, tm, LANES), F32))
    if out_dil == 1:
        out_spec = pl.BlockSpec((tm, tn), lambda i, j: (i, j))
        out_shape = jax.ShapeDtypeStruct((t, n), out_dtype)
    else:
        out_spec = pl.BlockSpec((None, out_dil, tm // out_dil, tn), lambda i, j: (i // bpb, 0, i % bpb, j))
        out_shape = jax.ShapeDtypeStruct((t // seq, out_dil, seq // out_dil, n), out_dtype)
    kern = functools.partial(_proj_kernel, prologue=prologue, n_lhs=n_lhs, rope_pattern=rope_pattern,
                             sub_tn=sub_tn, out_scale=out_scale, residual=residual is not None,
                             in_dils=in_dils, out_dil=out_dil)
    return pl.pallas_call(
        kern,
        grid=(t // tm, n // tn),
        in_specs=specs,
        out_specs=out_spec,
        out_shape=out_shape,
        scratch_shapes=scratch,
        compiler_params=_cparams(("parallel", "arbitrary")),
        name=name,
    )(*args)


MLA_HEADS_PER_STEP = 4


def _mla_scores(q_ref, kn_ref, kr, h):
    k = jnp.concatenate([kn_ref[:, h * LANES:(h + 1) * LANES], kr], axis=1)
    q = q_ref[:, h * 2 * LANES:(h + 1) * 2 * LANES]
    return lax.dot_general(q, k, (((1,), (1,)), ((), ())), preferred_element_type=F32)


def _mla_kernel(q_ref, kn_ref, kr_ref, v_ref, o_ref):
    kr = kr_ref[...]
    s_next = _mla_scores(q_ref, kn_ref, kr, 0)
    for h in range(MLA_HEADS_PER_STEP):
        s = s_next
        if h + 1 < MLA_HEADS_PER_STEP:
            s_next = _mla_scores(q_ref, kn_ref, kr, h + 1)
        m = jnp.max(s, axis=1, keepdims=True)
        p = jnp.exp(s - m).astype(BF16)
        v = v_ref[:, h * LANES:(h + 1) * LANES]
        oe = jnp.dot(p, jnp.concatenate([v, jnp.ones_like(v)], axis=1), preferred_element_type=F32)
        o_ref[:, h * LANES:(h + 1) * LANES] = (oe[:, :LANES] / oe[:, LANES:]).astype(o_ref.dtype)


def _mla_attention(q_cat, kv, qkv_b, kr_col, *, bsz, seq, tq):
    t = bsz * seq
    nq = seq // tq
    hps = MLA_HEADS_PER_STEP
    groups = MLA_HEADS // hps
    return pl.pallas_call(
        _mla_kernel,
        grid=(bsz, groups, nq),
        in_specs=[pl.BlockSpec((tq, hps * 2 * LANES), lambda b, h, i: (b * nq + i, h)),
                  pl.BlockSpec((seq, hps * LANES), lambda b, h, i: (b, h)),
                  pl.BlockSpec((seq, LANES), lambda b, h, i: (b, kr_col)),
                  pl.BlockSpec((seq, hps * LANES), lambda b, h, i: (b, groups + h))],
        out_specs=pl.BlockSpec((tq, hps * LANES), lambda b, h, i: (b * nq + i, h)),
        out_shape=jax.ShapeDtypeStruct((t, MLA_HEADS * MLA_V), BF16),
        compiler_params=_cparams(("parallel", "parallel", "parallel")),
        name="mla_attention",
    )(q_cat, kv, qkv_b, kv)


def _banded_kernel(*refs, radius, tq, hb, length, n_pairs, pairs_per_kv, has_sink, want_lse):
    pos = 0
    if has_sink:
        sink_ref = refs[0]
        pos = 1
    q_ref, kp_ref, kc_ref, kn_ref, vp_ref, vc_ref, vn_ref = refs[pos:pos + 7]
    pos += 7
    o_ref = refs[pos]
    lse_ref = refs[pos + 1] if want_lse else None

    j = pl.program_id(2)
    kw = jnp.concatenate([kp_ref[...], kc_ref[...], kn_ref[...]], axis=0)
    vw = jnp.concatenate([vp_ref[...], vc_ref[...], vn_ref[...]], axis=0)
    scale = HEAD_DIM ** -0.5
    qn = min(tq, BAND_SUB_ROWS)
    kn = qn + 2 * hb
    lane = lax.broadcasted_iota(jnp.int32, (qn, LANES), 1)
    low = lane < HEAD_DIM
    first = (lane % HEAD_DIM) < ROPE_HALF

    for q0 in range(0, tq, qn):
        qpos = j * tq + q0 + lax.broadcasted_iota(jnp.int32, (qn, kn), 0)
        kpos = j * tq - hb + q0 + lax.broadcasted_iota(jnp.int32, (qn, kn), 1)
        valid = (jnp.abs(qpos - kpos) <= radius) & (kpos >= 0) & (kpos < length)
        for p in range(n_pairs):
            c = p // pairs_per_kv
            qp = q_ref[q0:q0 + qn, p * LANES:(p + 1) * LANES] * scale
            kp = kw[q0:q0 + kn, c * LANES:(c + 1) * LANES]
            vp = vw[q0:q0 + kn, c * LANES:(c + 1) * LANES]
            vp = jnp.concatenate([vp, jnp.ones_like(vp)], axis=1)
            halves = []
            for half in range(2):
                qh = jnp.where(first if half == 0 else ~first, qp, jnp.zeros_like(qp))
                s = lax.dot_general(qh, kp, (((1,), (1,)), ((), ())), preferred_element_type=F32)
                s = jnp.where(valid, s, -jnp.inf)
                m = jnp.max(s, axis=1, keepdims=True)
                if has_sink:
                    sk = sink_ref[2 * p + half]
                    m = jnp.maximum(m, sk)
                oe = jnp.dot(jnp.exp(s - m).astype(BF16), vp, preferred_element_type=F32)
                den = oe[:, LANES:]
                if has_sink:
                    den = den + jnp.exp(sk - m)
                halves.append((oe[:, :LANES] / den, m + jnp.log(den) if want_lse else None))
            o_ref[q0:q0 + qn, p * LANES:(p + 1) * LANES] = jnp.where(
                low, halves[0][0], halves[1][0]).astype(o_ref.dtype)
            if want_lse:
                lse_ref[q0:q0 + qn, p * LANES:(p + 1) * LANES] = jnp.where(low, halves[0][1], halves[1][1])


def _banded_attention(q_arr, k_arr, v_arr, *, bsz, seq, dil, radius, q_col, k_col, v_col, kv_width,
                      pairs_per_kv, sink=None, want_lse, name):
    length = seq // dil
    hb = radius
    tq = min(256, length)
    assert tq % hb == 0 and length % tq == 0
    qw = DIL_HEADS * HEAD_DIM
    n_pairs = qw // LANES
    rep = tq // hb
    last_hb = length // hb - 1

    def prev_spec(col):
        return pl.BlockSpec((None, None, hb, kv_width),
                            lambda b, r, j: (b, r, jnp.maximum(j * rep - 1, 0), col))

    def cur_spec(col):
        return pl.BlockSpec((None, None, tq, kv_width), lambda b, r, j: (b, r, j, col))

    def next_spec(col):
        return pl.BlockSpec((None, None, hb, kv_width),
                            lambda b, r, j: (b, r, jnp.minimum((j + 1) * rep, last_hb), col))

    in_specs = [pl.BlockSpec((None, None, tq, qw), lambda b, r, j: (b, r, j, q_col)),
                prev_spec(k_col), cur_spec(k_col), next_spec(k_col),
                prev_spec(v_col), cur_spec(v_col), next_spec(v_col)]
    args = [q_arr, k_arr, k_arr, k_arr, v_arr, v_arr, v_arr]
    if sink is not None:
        in_specs = [pl.BlockSpec(memory_space=pltpu.SMEM)] + in_specs
        args = [sink] + args
    out_spec = pl.BlockSpec((None, None, tq, qw), lambda b, r, j: (b, r, j, 0))
    out_specs = [out_spec]
    out_shape = [jax.ShapeDtypeStruct((bsz, dil, length, qw), BF16)]
    if want_lse:
        out_specs.append(out_spec)
        out_shape.append(jax.ShapeDtypeStruct((bsz, dil, length, qw), F32))
    kern = functools.partial(_banded_kernel, radius=radius, tq=tq, hb=hb, length=length, n_pairs=n_pairs,
                             pairs_per_kv=pairs_per_kv, has_sink=sink is not None, want_lse=want_lse)
    return pl.pallas_call(
        kern,
        grid=(bsz, dil, length // tq),
        in_specs=in_specs,
        out_specs=out_specs,
        out_shape=out_shape,
        compiler_params=_cparams(("parallel", "parallel", "parallel")),
        name=name,
    )(*args)


def _router_kernel(x_ref, g_ref, sc_ref, sh_ref, wt_ref, b_ref, tri_ref,
                   idx_ref, gate_ref, rank_ref, cnt_ref, h_ref, carry_ref):
    @pl.when(pl.program_id(0) == 0)
    def _():
        carry_ref[...] = jnp.zeros_like(carry_ref)

    h = _ada_norm(x_ref[...], g_ref[...], sc_ref[...], sh_ref[...])
    h_ref[...] = _pack_bf16_pairs(h)
    lt = lax.dot_general(wt_ref[...], h, (((1,), (1,)), ((), ())), preferred_element_type=F32,
                         precision=lax.Precision.HIGHEST) + b_ref[...]
    n_e, tm = lt.shape
    e_iota = lax.broadcasted_iota(jnp.int32, (n_e, tm), 0)
    vals, sels = [], []
    for k in range(TOP_K):
        m = jnp.max(lt, axis=0, keepdims=True)
        idx = jnp.min(jnp.where(lt == m, e_iota, n_e), axis=0, keepdims=True)
        sel = e_iota == idx
        idx_ref[k:k + 1, :] = idx
        vals.append(m)
        sels.append(sel)
        lt = jnp.where(sel, -jnp.inf, lt)
    exps = [jnp.exp(v - vals[0]) for v in vals]
    den = exps[0] + exps[1] + exps[2] + exps[3]
    for k in range(TOP_K):
        gate_ref[k:k + 1, :] = exps[k] / den

    onehot = jnp.where(sels[0] | sels[1] | sels[2] | sels[3], 1.0, 0.0)
    carry = carry_ref[:, 0:1]
    before = jnp.dot(onehot.astype(BF16), tri_ref[...], preferred_element_type=F32) + carry
    for k in range(TOP_K):
        rank_ref[k:k + 1, :] = jnp.sum(jnp.where(sels[k], before, 0.0), axis=0, keepdims=True).astype(jnp.int32)
    total = carry + jnp.sum(onehot, axis=1, keepdims=True)
    carry_ref[...] = jnp.broadcast_to(total, carry_ref.shape)
    cnt_ref[...] = jnp.broadcast_to(total, cnt_ref.shape)


def _router(x, norm, w_router, b_router, *, seq, tm=512):
    t, d = x.shape
    n_e = w_router.shape[1]
    bpb = seq // tm
    tri = (jnp.arange(tm)[:, None] < jnp.arange(tm)[None, :]).astype(BF16)
    g, sc, sh = norm
    out_i = jax.ShapeDtypeStruct((TOP_K, t), jnp.int32)
    tok = pl.BlockSpec((TOP_K, tm), lambda i: (0, i))
    return pl.pallas_call(
        _router_kernel,
        grid=(t // tm,),
        in_specs=[pl.BlockSpec((tm, d), lambda i: (i, 0)),
                  pl.BlockSpec((1, d), lambda i: (0, 0)),
                  pl.BlockSpec((None, 1, d), lambda i: (i // bpb, 0, 0)),
                  pl.BlockSpec((None, 1, d), lambda i: (i // bpb, 0, 0)),
                  pl.BlockSpec((n_e, d), lambda i: (0, 0)),
                  pl.BlockSpec((n_e, 1), lambda i: (0, 0)),
                  pl.BlockSpec((tm, tm), lambda i: (0, 0))],
        out_specs=[tok, tok, tok, pl.BlockSpec((n_e, LANES), lambda i: (0, 0)),
                   pl.BlockSpec((tm, d // 2), lambda i: (i, 0))],
        out_shape=[out_i, jax.ShapeDtypeStruct((TOP_K, t), F32), out_i,
                   jax.ShapeDtypeStruct((n_e, LANES), F32), jax.ShapeDtypeStruct((t, d // 2), jnp.int32)],
        scratch_shapes=[pltpu.VMEM((n_e, LANES), F32)],
        compiler_params=_cparams(("arbitrary",)),
        name="router",
    )(x, g, sc, sh, w_router.T, b_router.reshape(n_e, 1), tri)


SC_CORES = 2
SC_SUBCORES = 16
SC_WORKERS = SC_CORES * SC_SUBCORES
SC_ROWS = 128


def _sc_worker_id():
    return lax.axis_index("subcore") * SC_CORES + lax.axis_index("core")


def _sc_mesh():
    return plsc.VectorSubcoreMesh(core_axis_name="core", subcore_axis_name="subcore")


def _sc_dispatch(h, dest_flat, pad_rows):
    t, d = h.shape
    n_pad = pad_rows.shape[0]
    n_rows = TOP_K * t + n_pad
    tok_w = t // SC_WORKERS
    pad_w = n_pad // SC_WORKERS
    assert tok_w % SC_ROWS == 0 and pad_w % SC_ROWS == 0

    @functools.partial(
        pl.kernel, out_type=jax.ShapeDtypeStruct((n_rows, d), h.dtype), mesh=_sc_mesh(),
        scratch_types=[pltpu.VMEM((SC_ROWS,), jnp.int32), pltpu.VMEM((SC_ROWS, d), h.dtype)])
    def scatter_rows(h_hbm, dest_hbm, pad_hbm, zero_hbm, xs_hbm, idx_v, rows_v):
        wid = _sc_worker_id()

        @pl.loop(0, tok_w // SC_ROWS)
        def _(b):
            t0 = wid * tok_w + b * SC_ROWS
            pltpu.sync_copy(h_hbm.at[pl.ds(t0, SC_ROWS)], rows_v)
            for k in range(TOP_K):
                pltpu.sync_copy(dest_hbm.at[pl.ds(k * t + t0, SC_ROWS)], idx_v)
                pltpu.sync_copy(rows_v, xs_hbm.at[idx_v])

        pltpu.sync_copy(zero_hbm, rows_v)

        @pl.loop(0, pad_w // SC_ROWS)
        def _(b):
            pltpu.sync_copy(pad_hbm.at[pl.ds(wid * pad_w + b * SC_ROWS, SC_ROWS)], idx_v)
            pltpu.sync_copy(rows_v, xs_hbm.at[idx_v])

    return scatter_rows(h, dest_flat, pad_rows, jnp.zeros((SC_ROWS, d), h.dtype))


def _sc_gather(table, idx):
    n = idx.shape[0]
    d = table.shape[1]
    per_w = n // SC_WORKERS
    assert per_w % SC_ROWS == 0

    @functools.partial(
        pl.kernel, out_type=jax.ShapeDtypeStruct((n, d), table.dtype), mesh=_sc_mesh(),
        scratch_types=[pltpu.VMEM((SC_ROWS,), jnp.int32), pltpu.VMEM((SC_ROWS, d), table.dtype)])
    def gather_rows(table_hbm, idx_hbm, out_hbm, idx_v, rows_v):
        base = _sc_worker_id() * per_w

        @pl.loop(0, per_w // SC_ROWS)
        def _(b):
            off = base + b * SC_ROWS
            pltpu.sync_copy(idx_hbm.at[pl.ds(off, SC_ROWS)], idx_v)
            pltpu.sync_copy(table_hbm.at[idx_v], rows_v)
            pltpu.sync_copy(rows_v, out_hbm.at[pl.ds(off, SC_ROWS)])

    return gather_rows(table, idx)


def _expert_kernel(be_ref, nu_ref, xs_ref, wgu_ref, bgu_ref, wd_ref, bd_ref, y_ref, wgu_bf, wd_bf):
    i = pl.program_id(0)
    used = i < nu_ref[0]

    @pl.when(used & ((i == 0) | (be_ref[i] != be_ref[jnp.maximum(i - 1, 0)])))
    def _():
        wgu_bf[...] = wgu_ref[...].astype(BF16)
        wd_bf[...] = wd_ref[...].astype(BF16)

    @pl.when(used)
    def _():
        d_ff = wd_ref.shape[0]
        x_hi, x_lo = _unpack_bf16_pairs(xs_ref[...])
        x = jnp.concatenate([x_hi.astype(BF16), x_lo.astype(BF16)], axis=1)
        gu = jnp.dot(x, wgu_bf[...], preferred_element_type=F32) + bgu_ref[...]
        glu = jnp.minimum(gu[:, :d_ff], SWIGLU_LIMIT)
        lin = jnp.clip(gu[:, d_ff:], -SWIGLU_LIMIT, SWIGLU_LIMIT)
        act = glu * (1.0 / (1.0 + jnp.exp(-SWIGLU_ALPHA * glu))) * (lin + 1.0)
        y = jnp.dot(act.astype(BF16), wd_bf[...], preferred_element_type=F32) + bd_ref[...]
        y_ref[...] = _pack_bf16_pairs(y)

    @pl.when(jnp.logical_not(used))
    def _():
        y_ref[...] = jnp.zeros_like(y_ref)


def _experts(xs, blk_expert, n_used, layer, w_gu, b_gu, w_down, b_down):
    n_rows, dw = xs.shape
    _, n_e, d, f2 = w_gu.shape
    n_blocks = n_rows // MOE_ROWS

    def blk(i, be, nu):
        return jnp.minimum(i, nu[0] - 1)

    def wmap(i, be, nu):
        return (layer, be[blk(i, be, nu)], 0, 0)

    grid_spec = pltpu.PrefetchScalarGridSpec(
        num_scalar_prefetch=2,
        grid=(n_blocks,),
        in_specs=[pl.BlockSpec((MOE_ROWS, dw), lambda i, be, nu: (blk(i, be, nu), 0)),
                  pl.BlockSpec((None, None, d, f2), wmap),
                  pl.BlockSpec((None, None, 1, f2), wmap),
                  pl.BlockSpec((None, None, f2 // 2, d), wmap),
                  pl.BlockSpec((None, None, 1, d), wmap)],
        out_specs=pl.BlockSpec((MOE_ROWS, dw), lambda i, be, nu: (i, 0)),
        scratch_shapes=[pltpu.VMEM((d, f2), BF16), pltpu.VMEM((f2 // 2, d), BF16)],
    )
    depth = w_gu.shape[0]
    return pl.pallas_call(
        _expert_kernel,
        grid_spec=grid_spec,
        out_shape=jax.ShapeDtypeStruct((n_rows, dw), jnp.int32),
        compiler_params=pltpu.CompilerParams(dimension_semantics=("arbitrary",),
                                             vmem_limit_bytes=EXPERT_VMEM_LIMIT),
        name="moe_experts",
    )(blk_expert, n_used, xs, w_gu, b_gu.reshape(depth, n_e, 1, f2), w_down, b_down.reshape(depth, n_e, 1, d))


def _combine_kernel(y0_ref, y1_ref, y2_ref, y3_ref, gate_ref, x_ref, g2_ref, o_ref):
    gate = gate_ref[...]
    w = y0_ref.shape[1]
    y_hi, y_lo = None, None
    for k, y_ref in enumerate((y0_ref, y1_ref, y2_ref, y3_ref)):
        hi, lo = _unpack_bf16_pairs(y_ref[...])
        gk = gate[:, k:k + 1]
        y_hi = gk * hi if y_hi is None else y_hi + gk * hi
        y_lo = gk * lo if y_lo is None else y_lo + gk * lo
    o_ref[:, :w] = x_ref[:, :w] + g2_ref[:, :w] * y_hi
    o_ref[:, w:] = x_ref[:, w:] + g2_ref[:, w:] * y_lo


def _combine(y4, gate, x, g2, *, seq, tm=512):
    t, d = x.shape
    bpb = seq // tm
    nblk = t // tm
    y_specs = [pl.BlockSpec((tm, d // 2), functools.partial(lambda i, k: (k * nblk + i, 0), k=k))
               for k in range(TOP_K)]
    return pl.pallas_call(
        _combine_kernel,
        grid=(nblk,),
        in_specs=y_specs + [pl.BlockSpec((tm, TOP_K), lambda i: (i, 0)),
                            pl.BlockSpec((tm, d), lambda i: (i, 0)),
                            pl.BlockSpec((None, 1, d), lambda i: (i // bpb, 0, 0))],
        out_specs=pl.BlockSpec((tm, d), lambda i: (i, 0)),
        out_shape=jax.ShapeDtypeStruct((t, d), F32),
        compiler_params=_cparams(("parallel",)),
        name="moe_combine",
    )(y4, y4, y4, y4, gate, x, g2)


def _moe(x, norm, g2, w_router, b_router, layer, w_gu, b_gu, w_down, b_down, *, seq):
    t, d = x.shape
    n_e = w_router.shape[1]
    idx_t, gate_t, rank_t, counts, h = _router(x, norm, w_router, b_router, seq=seq)
    e_ids = jnp.arange(n_e, dtype=jnp.int32)
    counts = counts[:, 0].astype(jnp.int32)
    padded = (counts + MOE_ROWS - 1) // MOE_ROWS * MOE_ROWS
    pend = jnp.cumsum(padded)
    pstart = pend - padded
    n_pad = n_e * MOE_ROWS
    n_rows = t * TOP_K + n_pad
    n_blocks = n_rows // MOE_ROWS
    dest_t = jnp.sum(jnp.where(idx_t[:, :, None] == e_ids, pstart, 0), axis=-1) + rank_t
    dest_flat = dest_t.reshape(TOP_K * t)
    blk_start = jnp.arange(n_blocks, dtype=jnp.int32) * MOE_ROWS
    blk_expert = jnp.minimum(jnp.sum(blk_start[:, None] >= pend[None, :], axis=1), n_e - 1).astype(jnp.int32)
    n_used = (pend[-1:] // MOE_ROWS).astype(jnp.int32)
    pad_len = padded - counts
    pad_end = jnp.cumsum(pad_len)
    j = jnp.arange(n_pad, dtype=jnp.int32)
    pe = jnp.sum(j[:, None] >= pad_end[None, :], axis=1)
    sel = jnp.minimum(pe, n_e - 1)[:, None] == e_ids
    in_expert = jnp.sum(jnp.where(sel, pstart + counts - (pad_end - pad_len), 0), axis=-1) + j
    pad_rows = jnp.where(pe < n_e, in_expert, pend[-1] + j - pad_end[-1]).astype(jnp.int32)

    xs = _sc_dispatch(h, dest_flat, pad_rows)
    yb = _experts(xs, blk_expert, n_used, layer, w_gu, b_gu, w_down, b_down)
    y4 = _sc_gather(yb, dest_flat)
    return _combine(y4, gate_t.T, x, g2, seq=seq)


def _final_norm_kernel(x_ref, g_ref, o_ref):
    o_ref[...] = _rms(x_ref[...]) * g_ref[...]


def _final_norm(x, g, tm=1024):
    t, d = x.shape
    return pl.pallas_call(
        _final_norm_kernel,
        grid=(t // tm,),
        in_specs=[pl.BlockSpec((tm, d), lambda i: (i, 0)), pl.BlockSpec((1, d), lambda i: (0, 0))],
        out_specs=pl.BlockSpec((tm, d), lambda i: (i, 0)),
        out_shape=jax.ShapeDtypeStruct((t, d), F32),
        compiler_params=_cparams(("parallel",)),
        name="final_norm",
    )(x, g.reshape(1, d))


def _dup_heads(w, n_heads):
    d = w.shape[0]
    return jnp.repeat(w.reshape(d, n_heads, 1, HEAD_DIM), 2, axis=2).reshape(d, n_heads * LANES)


def _mixer_mla_swa(x, norm, g1, cos, sin, w_in, g_q, w_qb, g_kv, w_kvb, sink, w_out, *, bsz, seq):
    d = x.shape[1]
    o = np.cumsum((0, MLA_Q_LORA, MLA_KV_LORA, MLA_ROPE, SWA_Q_HEADS * HEAD_DIM,
                   SWA_KV_HEADS * HEAD_DIM, SWA_KV_HEADS * HEAD_DIM))
    w_cq, w_ckv, w_kr, w_qs, w_ks, w_vs = (w_in[:, o[i]:o[i + 1]] for i in range(6))
    w_lat = jnp.concatenate([w_cq, jnp.zeros((d, LANES), F32), w_ckv], axis=1).astype(BF16)
    lat = _proj([x], [w_lat], prologue="ada", norm=norm, seq=seq, tm=512, tn=w_lat.shape[1],
                out_dtype=F32, name="proj_latent")
    w_b = jnp.concatenate([_pair_layout(w_qs), _pair_layout(_dup_heads(w_ks, SWA_KV_HEADS)),
                           _dup_heads(w_vs, SWA_KV_HEADS),
                           w_kr, jnp.zeros((d, LANES - MLA_ROPE), F32)], axis=1).astype(BF16)
    qkv_b = _proj([x], [w_b], prologue="ada", norm=norm, seq=seq, tm=512, tn=w_b.shape[1], out_dtype=BF16,
                  rope=(cos, sin), rope_pattern=(ROPE_PAIR,) * 10 + (0, 0) + (ROPE_ROTATE,), name="proj_swa")
    kr_col = (w_b.shape[1] - LANES) // LANES
    qd = MLA_NOPE + MLA_ROPE
    w_q = w_qb.reshape(MLA_Q_LORA, MLA_HEADS, qd)
    w_q = jnp.concatenate([w_q, jnp.zeros((MLA_Q_LORA, MLA_HEADS, 2 * LANES - qd), F32)], axis=2)
    w_q = w_q.reshape(MLA_Q_LORA, MLA_HEADS * 2 * LANES).astype(BF16)
    q_cat = _proj([lat], [w_q], prologue="rms", norm=(g_q.reshape(1, -1),), lhs_cols=(MLA_Q_LORA, 0),
                  seq=seq, tm=512, tn=w_q.shape[1], out_dtype=BF16, rope=(cos, sin),
                  rope_pattern=(0, 1) * MLA_HEADS, out_scale=qd ** -0.5, name="proj_mla_q")
    w_kv = w_kvb.reshape(MLA_KV_LORA, MLA_HEADS, MLA_NOPE + MLA_V)
    w_kv = jnp.concatenate([w_kv[:, :, :MLA_NOPE].reshape(MLA_KV_LORA, -1),
                            w_kv[:, :, MLA_NOPE:].reshape(MLA_KV_LORA, -1)], axis=1).astype(BF16)
    kv = _proj([lat], [w_kv], prologue="rms", norm=(g_kv.reshape(1, -1),), lhs_cols=(MLA_KV_LORA, 2),
               seq=seq, tm=512, tn=w_kv.shape[1], out_dtype=BF16, name="proj_mla_kv")
    o_a = _mla_attention(q_cat, kv, qkv_b, kr_col, bsz=bsz, seq=seq, tq=256)
    qkv_b4 = qkv_b.reshape(bsz, 1, seq, qkv_b.shape[1])
    (o_b,) = _banded_attention(qkv_b4, qkv_b4, qkv_b4, bsz=bsz, seq=seq, dil=1, radius=SWA_RADIUS,
                               q_col=0, k_col=4, v_col=5, kv_width=2 * LANES,
                               pairs_per_kv=SWA_Q_HEADS // SWA_KV_HEADS // 2, sink=sink, want_lse=False,
                               name="swa_attention")
    o_b = o_b.reshape(bsz * seq, SWA_Q_HEADS * HEAD_DIM)
    na = MLA_HEADS * MLA_V
    return _proj([o_a, o_b], [w_out[:na].astype(BF16), w_out[na:].astype(BF16)], prologue="plain",
                 seq=seq, tm=512, tn=d, out_dtype=F32, residual=(x, g1), name="proj_out_ab")


def _mixer_dilated(x, norm, g1, cos, sin, w_in, w_out, *, bsz, seq):
    d = x.shape[1]
    qw = DIL_HEADS * HEAD_DIM
    w_bf = w_in.astype(BF16)
    outs, lses = [], []
    for g, (window, dil) in enumerate(DIL_PATTERN):
        w_g = w_bf[:, g * 3 * qw:(g + 1) * 3 * qw]
        w_g = jnp.concatenate([_pair_layout(w_g[:, :2 * qw]), w_g[:, 2 * qw:]], axis=1)
        qkv = _proj([x], [w_g], prologue="ada", norm=norm, seq=seq, tm=512, tn=3 * qw, sub_tn=qw, out_dtype=BF16,
                    rope=(cos, sin), rope_pattern=(ROPE_PAIR,) * (2 * qw // LANES) + (0,) * (qw // LANES),
                    out_dil=dil, name=f"proj_dil_{dil}")
        qkv = qkv.reshape(bsz, dil, seq // dil, 3 * qw)
        o, lse = _banded_attention(qkv, qkv, qkv, bsz=bsz, seq=seq, dil=dil, radius=window // (2 * dil),
                                   q_col=0, k_col=1, v_col=2, kv_width=qw, pairs_per_kv=1,
                                   want_lse=True, name=f"dilated_attention_{dil}")
        outs.append(o)
        lses.append(lse)
    return _proj(outs + lses, [w_out.astype(BF16)], prologue="dil", seq=seq, tm=512, tn=d, out_dtype=F32,
                 residual=(x, g1), in_dils=tuple(dil for _, dil in DIL_PATTERN), name="proj_out_c")


def kernel(x, c, positions, w_mod, b_mod, g_norm_mix, g_norm_ffn, w_in_ab, mla_g_q, mla_w_qb, mla_g_kv,
           mla_w_kvb, swa_sink, w_out_ab, w_in_c, w_out_c, w_router, b_router, w_gu, b_gu, w_down, b_down,
           g_final):
    bsz, seq, d = x.shape
    depth = w_mod.shape[0]
    cos, sin = _rope_tables(positions)
    mod = _modulation(c, w_mod, b_mod)
    xt = x.reshape(bsz * seq, d)
    for layer in range(depth):
        sh1, sc1, g1, sh2, sc2, g2 = (mod[layer, :, i * d:(i + 1) * d].reshape(bsz, 1, d) for i in range(6))
        li = layer // 2
        norm = (g_norm_mix[layer].reshape(1, d), sc1, sh1)
        if layer % 2 == 0:
            xt = _mixer_mla_swa(xt, norm, g1, cos, sin, w_in_ab[li], mla_g_q[li], mla_w_qb[li], mla_g_kv[li],
                                mla_w_kvb[li], swa_sink[li], w_out_ab[li], bsz=bsz, seq=seq)
        else:
            xt = _mixer_dilated(xt, norm, g1, cos, sin, w_in_c[li], w_out_c[li], bsz=bsz, seq=seq)
        norm = (g_norm_ffn[layer].reshape(1, d), sc2, sh2)
        xt = _moe(xt, norm, g2, w_router[layer], b_router[layer], layer, w_gu, b_gu, w_down, b_down, seq=seq)
    return _final_norm(xt, g_final).reshape(bsz, seq, d)
```

```python
import functools

import jax
import jax.numpy as jnp
import numpy as np
from jax import lax
from jax.experimental import pallas as pl
from jax.experimental.pallas import tpu as pltpu
from jax.experimental.pallas import tpu_sc as plsc

F32 = jnp.float32
BF16 = jnp.bfloat16

EPS = 1e-6
ROPE_THETA = 10000.0
LANES = 128
HEAD_DIM = 64
ROPE_HALF = HEAD_DIM // 2

MLA_HEADS = 8
MLA_Q_LORA = 384
MLA_KV_LORA = 256
MLA_NOPE = 128
MLA_ROPE = 64
MLA_V = 128
SWA_Q_HEADS = 16
SWA_KV_HEADS = 2
SWA_RADIUS = 128
DIL_PATTERN = ((128, 1), (512, 4), (2048, 16))
DIL_HEADS = 16
N_EXPERTS = 32
TOP_K = 4
SWIGLU_LIMIT = 7.0
SWIGLU_ALPHA = 1.702
MOE_ROWS = 512
MOE_SUB_ROWS = 256
BAND_SUB_ROWS = 128

VMEM_LIMIT = 48 * 1024 * 1024
EXPERT_VMEM_LIMIT = 56 * 1024 * 1024


def _cparams(sem):
    return pltpu.CompilerParams(dimension_semantics=sem, vmem_limit_bytes=VMEM_LIMIT)


def _mod_kernel(c_ref, w_ref, b_ref, o_ref):
    c = c_ref[...]
    ca = c * (1.0 / (1.0 + jnp.exp(-c)))
    o_ref[...] = jnp.dot(ca, w_ref[...], preferred_element_type=F32,
                         precision=lax.Precision.HIGHEST) + b_ref[...]


def _modulation(c, w_mod, b_mod):
    depth, d, n = w_mod.shape
    bsz = c.shape[0]
    tn = 1536
    return pl.pallas_call(
        _mod_kernel,
        grid=(depth, n // tn),
        in_specs=[pl.BlockSpec((bsz, d), lambda l, j: (0, 0)),
                  pl.BlockSpec((None, d, tn), lambda l, j: (l, 0, j)),
                  pl.BlockSpec((None, 1, tn), lambda l, j: (l, 0, j))],
        out_specs=pl.BlockSpec((None, bsz, tn), lambda l, j: (l, 0, j)),
        out_shape=jax.ShapeDtypeStruct((depth, bsz, n), F32),
        compiler_params=_cparams(("parallel", "parallel")),
        name="modulation",
    )(c, w_mod, b_mod.reshape(depth, 1, n))


def _rope_table_kernel(pos_ref, inv_ref, sign_ref, cos_ref, sin_ref):
    ang = inv_ref[...] * pos_ref[...].astype(F32)
    cos_ref[...] = jnp.transpose(jnp.cos(ang))
    sin_ref[...] = jnp.transpose(jnp.sin(ang) * sign_ref[...])


def _rope_tables(positions):
    bsz, s = positions.shape
    inv = ROPE_THETA ** (-jnp.arange(0, HEAD_DIM, 2, dtype=F32) / HEAD_DIM)
    inv128 = jnp.tile(inv, LANES // ROPE_HALF).reshape(LANES, 1)
    sign = jnp.where((jnp.arange(LANES) % HEAD_DIM) < ROPE_HALF, -1.0, 1.0).astype(F32).reshape(LANES, 1)
    out = jax.ShapeDtypeStruct((bsz * s, LANES), F32)
    return pl.pallas_call(
        _rope_table_kernel,
        grid=(bsz,),
        in_specs=[pl.BlockSpec((None, 1, s), lambda b: (b, 0, 0)),
                  pl.BlockSpec((LANES, 1), lambda b: (0, 0)),
                  pl.BlockSpec((LANES, 1), lambda b: (0, 0))],
        out_specs=[pl.BlockSpec((s, LANES), lambda b: (b, 0)),
                   pl.BlockSpec((s, LANES), lambda b: (b, 0))],
        out_shape=[out, out],
        compiler_params=_cparams(("parallel",)),
        name="rope_tables",
    )(positions.reshape(bsz, 1, s), inv128, sign)


def _rms(x):
    return x * lax.rsqrt(jnp.mean(x * x, axis=-1, keepdims=True) + EPS)


def _ada_norm(x, g, sc, sh):
    return (_rms(x) * g) * (1.0 + sc) + sh


def _pack_bf16_pairs(x):
    w = x.shape[1] // 2
    hi = pltpu.bitcast(x[:, :w].astype(BF16).astype(F32), jnp.int32)
    lo = pltpu.bitcast(x[:, w:].astype(BF16).astype(F32), jnp.int32)
    return hi | lax.shift_right_logical(lo, 16)


def _unpack_bf16_pairs(p):
    hi = pltpu.bitcast(p & jnp.int32(-65536), F32)
    lo = pltpu.bitcast(lax.shift_left(p, 16), F32)
    return hi, lo


ROPE_ROTATE, ROPE_PAIR = 1, 2


def _pair_layout(w):
    k, n = w.shape
    w = w.reshape(k, n // LANES, 2, 2, ROPE_HALF)
    return jnp.swapaxes(w, 2, 3).reshape(k, n)


def _rope_group(a, cos, sin, first_half):
    rot = jnp.where(first_half, pltpu.roll(a, LANES - ROPE_HALF, 1), pltpu.roll(a, ROPE_HALF, 1))
    return a * cos + rot * sin


def _proj_kernel(*refs, prologue, n_lhs, rope_pattern, sub_tn, out_scale, residual, in_dils, out_dil):
    pos = 0
    if prologue == "ada":
        x_ref, g_ref, sc_ref, sh_ref = refs[:4]
        pos = 4
    elif prologue == "rms":
        x_ref, g_ref = refs[:2]
        pos = 2
    elif prologue == "dil":
        o_refs = refs[0:3]
        l_refs = refs[3:6]
        pos = 6
    else:
        x_refs = refs[:n_lhs]
        pos = n_lhs
    w_refs = refs[pos:pos + n_lhs]
    pos += n_lhs
    if rope_pattern is not None:
        cos_ref, sin_ref = refs[pos:pos + 2]
        pos += 2
    if residual:
        res_ref, gate_ref = refs[pos:pos + 2]
        pos += 2
    o_ref = refs[pos]
    pos += 1
    h_ref = refs[pos] if prologue != "plain" else None
    slab_ref = refs[pos + 1] if (prologue == "dil" or out_dil > 1) else None

    if prologue != "plain":
        @pl.when(pl.program_id(1) == 0)
        def _():
            if prologue == "ada":
                h_ref[...] = _ada_norm(x_ref[...], g_ref[...], sc_ref[...], sh_ref[...]).astype(BF16)
            elif prologue == "rms":
                h_ref[...] = (_rms(x_ref[...]) * g_ref[...]).astype(BF16)
            else:
                tm = h_ref.shape[0]
                for c in range(h_ref.shape[1] // LANES):
                    cols = slice(c * LANES, (c + 1) * LANES)
                    vals = []
                    for i, (ref, dil) in enumerate(zip(o_refs + l_refs, in_dils + in_dils)):
                        if dil == 1:
                            vals.append(ref[0, :, cols].astype(F32))
                        else:
                            for r in range(dil):
                                slab_ref[i, pl.ds(r, tm // dil, stride=dil), :] = ref[r, :, cols].astype(F32)
                            vals.append(slab_ref[i])
                    o0, o1, o2, l0, l1, l2 = vals
                    m = jnp.maximum(jnp.maximum(l0, l1), l2)
                    e0, e1, e2 = jnp.exp(l0 - m), jnp.exp(l1 - m), jnp.exp(l2 - m)
                    h_ref[:, cols] = ((e0 * o0 + e1 * o1 + e2 * o2) / (e0 + e1 + e2)).astype(BF16)

    n_total = w_refs[0].shape[1]
    tn = n_total if sub_tn is None else sub_tn

    def matmul(t):
        cols = slice(t * tn, (t + 1) * tn)
        if prologue != "plain":
            acc = jnp.dot(h_ref[...], w_refs[0][:, cols], preferred_element_type=F32)
        else:
            acc = jnp.dot(x_refs[0][...], w_refs[0][:, cols], preferred_element_type=F32)
            for xr, wr in zip(x_refs[1:], w_refs[1:]):
                acc = acc + jnp.dot(xr[...], wr[:, cols], preferred_element_type=F32)
        return acc * out_scale if out_scale != 1.0 else acc

    if residual:
        o_ref[...] = (res_ref[...] + gate_ref[...] * matmul(0)).astype(o_ref.dtype)
        return

    pattern = rope_pattern if rope_pattern is not None else (0,) * (n_total // LANES)
    if any(pattern):
        cos = cos_ref[...]
        sin = sin_ref[...]
        lane = lax.broadcasted_iota(jnp.int32, cos.shape, 1)
        first_half = (lane % HEAD_DIM) < ROPE_HALF
        if ROPE_PAIR in pattern:
            sin_pair = jnp.where(lane < HEAD_DIM, -1.0, 1.0) * jnp.where(first_half, -sin, sin)

    def epilogue(acc, t):
        tm = acc.shape[0]
        for c in range(tn // LANES):
            g = t * (tn // LANES) + c
            cols = slice(g * LANES, (g + 1) * LANES)
            a = acc[:, c * LANES:(c + 1) * LANES]
            if pattern[g] == ROPE_ROTATE:
                a = _rope_group(a, cos, sin, first_half)
            elif pattern[g] == ROPE_PAIR:
                a = a * cos + pltpu.roll(a, HEAD_DIM, 1) * sin_pair
            if out_dil == 1:
                o_ref[:, cols] = a.astype(o_ref.dtype)
            else:
                slab_ref[0] = a
                for r in range(out_dil):
                    o_ref[r, :, cols] = slab_ref[0, pl.ds(r, tm // out_dil, stride=out_dil), :].astype(o_ref.dtype)

    acc_next = matmul(0)
    for t in range(n_total // tn):
        acc = acc_next
        if t + 1 < n_total // tn:
            acc_next = matmul(t + 1)
        epilogue(acc, t)


def _proj(lhs, ws, *, prologue, seq, tm, tn, out_dtype, norm=None, lhs_cols=None, rope=None,
          rope_pattern=None, sub_tn=None, out_scale=1.0, residual=None, in_dils=None, out_dil=1, name):
    k, n = ws[0].shape
    bsz_seq = lhs[0].shape[0] if prologue != "dil" else lhs[0].shape[0] * seq
    t = bsz_seq
    assert t % tm == 0 and n % tn == 0 and seq % tm == 0
    bpb = seq // tm
    n_lhs = len(ws)
    args, specs = [], []
    row = lambda i, j: (i, 0)
    if prologue in ("ada", "rms"):
        kw, kc = lhs_cols if lhs_cols is not None else (k, 0)
        args.append(lhs[0])
        specs.append(pl.BlockSpec((tm, kw), lambda i, j: (i, kc)))
        args.append(norm[0])
        specs.append(pl.BlockSpec((1, k), lambda i, j: (0, 0)))
        if prologue == "ada":
            for v in norm[1:3]:
                args.append(v)
                specs.append(pl.BlockSpec((None, 1, k), lambda i, j: (i // bpb, 0, 0)))
    elif prologue == "dil":
        for a, dil in zip(lhs, in_dils + in_dils):
            args.append(a)
            specs.append(pl.BlockSpec((None, dil, tm // dil, k), lambda i, j: (i // bpb, 0, i % bpb, 0)))
    else:
        for a, w in zip(lhs, ws):
            args.append(a)
            specs.append(pl.BlockSpec((tm, w.shape[0]), row))
    for w in ws:
        args.append(w)
        specs.append(pl.BlockSpec((w.shape[0], tn), lambda i, j: (0, j)))
    if rope_pattern is not None:
        assert len(rope_pattern) == tn // LANES
        for tab in rope:
            args.append(tab)
            specs.append(pl.BlockSpec((tm, LANES), row))
    if residual is not None:
        res, gate = residual
        args.append(res)
        specs.append(pl.BlockSpec((tm, tn), lambda i, j: (i, j)))
        args.append(gate)
        specs.append(pl.BlockSpec((None, 1, tn), lambda i, j: (i // bpb, 0, j)))
    scratch = [] if prologue == "plain" else [pltpu.VMEM((tm, k), BF16)]
    if prologue == "dil":
        scratch.append(pltpu.VMEM((2 * len(in_dils), tm, LANES), F32))
    elif out_dil > 1:
        scratch.append(pltpu.VMEM((1, tm, LANES), F32))
    if out_dil == 1:
        out_spec = pl.BlockSpec((tm, tn), lambda i, j: (i, j))
        out_shape = jax.ShapeDtypeStruct((t, n), out_dtype)
    else:
        out_spec = pl.BlockSpec((None, out_dil, tm // out_dil, tn), lambda i, j: (i // bpb, 0, i % bpb, j))
        out_shape = jax.ShapeDtypeStruct((t // seq, out_dil, seq // out_dil, n), out_dtype)
    kern = functools.partial(_proj_kernel, prologue=prologue, n_lhs=n_lhs, rope_pattern=rope_pattern,
                             sub_tn=sub_tn, out_scale=out_scale, residual=residual is not None,
                             in_dils=in_dils, out_dil=out_dil)
    return pl.pallas_call(
        kern,
        grid=(t // tm, n // tn),
        in_specs=specs,
        out_specs=out_spec,
        out_shape=out_shape,
        scratch_shapes=scratch,
        compiler_params=_cparams(("parallel", "arbitrary")),
        name=name,
    )(*args)


MLA_HEADS_PER_STEP = 4


def _mla_scores(q_ref, kn_ref, kr, h):
    k = jnp.concatenate([kn_ref[:, h * LANES:(h + 1) * LANES], kr], axis=1)
    q = q_ref[:, h * 2 * LANES:(h + 1) * 2 * LANES]
    return lax.dot_general(q, k, (((1,), (1,)), ((), ())), preferred_element_type=F32)


def _mla_kernel(q_ref, kn_ref, kr_ref, v_ref, o_ref):
    kr = kr_ref[...]
    s_next = _mla_scores(q_ref, kn_ref, kr, 0)
    for h in range(MLA_HEADS_PER_STEP):
        s = s_next
        if h + 1 < MLA_HEADS_PER_STEP:
            s_next = _mla_scores(q_ref, kn_ref, kr, h + 1)
        m = jnp.max(s, axis=1, keepdims=True)
        p = jnp.exp(s - m).astype(BF16)
        v = v_ref[:, h * LANES:(h + 1) * LANES]
        oe = jnp.dot(p, jnp.concatenate([v, jnp.ones_like(v)], axis=1), preferred_element_type=F32)
        o_ref[:, h * LANES:(h + 1) * LANES] = (oe[:, :LANES] / oe[:, LANES:]).astype(o_ref.dtype)


def _mla_attention(q_cat, kv, qkv_b, kr_col, *, bsz, seq, tq):
    t = bsz * seq
    nq = seq // tq
    hps = MLA_HEADS_PER_STEP
    groups = MLA_HEADS // hps
    return pl.pallas_call(
        _mla_kernel,
        grid=(bsz, groups, nq),
        in_specs=[pl.BlockSpec((tq, hps * 2 * LANES), lambda b, h, i: (b * nq + i, h)),
                  pl.BlockSpec((seq, hps * LANES), lambda b, h, i: (b, h)),
                  pl.BlockSpec((seq, LANES), lambda b, h, i: (b, kr_col)),
                  pl.BlockSpec((seq, hps * LANES), lambda b, h, i: (b, groups + h))],
        out_specs=pl.BlockSpec((tq, hps * LANES), lambda b, h, i: (b * nq + i, h)),
        out_shape=jax.ShapeDtypeStruct((t, MLA_HEADS * MLA_V), BF16),
        compiler_params=_cparams(("parallel", "parallel", "parallel")),
        name="mla_attention",
    )(q_cat, kv, qkv_b, kv)


def _banded_kernel(*refs, radius, tq, hb, length, n_pairs, pairs_per_kv, has_sink, want_lse):
    pos = 0
    if has_sink:
        sink_ref = refs[0]
        pos = 1
    q_ref, kp_ref, kc_ref, kn_ref, vp_ref, vc_ref, vn_ref = refs[pos:pos + 7]
    pos += 7
    o_ref = refs[pos]
    lse_ref = refs[pos + 1] if want_lse else None

    j = pl.program_id(2)
    kw = jnp.concatenate([kp_ref[...], kc_ref[...], kn_ref[...]], axis=0)
    vw = jnp.concatenate([vp_ref[...], vc_ref[...], vn_ref[...]], axis=0)
    scale = HEAD_DIM ** -0.5
    qn = min(tq, BAND_SUB_ROWS)
    kn = qn + 2 * hb
    lane = lax.broadcasted_iota(jnp.int32, (qn, LANES), 1)
    low = lane < HEAD_DIM
    first = (lane % HEAD_DIM) < ROPE_HALF

    for q0 in range(0, tq, qn):
        qpos = j * tq + q0 + lax.broadcasted_iota(jnp.int32, (qn, kn), 0)
        kpos = j * tq - hb + q0 + lax.broadcasted_iota(jnp.int32, (qn, kn), 1)
        valid = (jnp.abs(qpos - kpos) <= radius) & (kpos >= 0) & (kpos < length)
        for p in range(n_pairs):
            c = p // pairs_per_kv
            qp = q_ref[q0:q0 + qn, p * LANES:(p + 1) * LANES] * scale
            kp = kw[q0:q0 + kn, c * LANES:(c + 1) * LANES]
            vp = vw[q0:q0 + kn, c * LANES:(c + 1) * LANES]
            vp = jnp.concatenate([vp, jnp.ones_like(vp)], axis=1)
            halves = []
            for half in range(2):
                qh = jnp.where(first if half == 0 else ~first, qp, jnp.zeros_like(qp))
                s = lax.dot_general(qh, kp, (((1,), (1,)), ((), ())), preferred_element_type=F32)
                s = jnp.where(valid, s, -jnp.inf)
                m = jnp.max(s, axis=1, keepdims=True)
                if has_sink:
                    sk = sink_ref[2 * p + half]
                    m = jnp.maximum(m, sk)
                oe = jnp.dot(jnp.exp(s - m).astype(BF16), vp, preferred_element_type=F32)
                den = oe[:, LANES:]
                if has_sink:
                    den = den + jnp.exp(sk - m)
                halves.append((oe[:, :LANES] / den, m + jnp.log(den) if want_lse else None))
            o_ref[q0:q0 + qn, p * LANES:(p + 1) * LANES] = jnp.where(
                low, halves[0][0], halves[1][0]).astype(o_ref.dtype)
            if want_lse:
                lse_ref[q0:q0 + qn, p * LANES:(p + 1) * LANES] = jnp.where(low, halves[0][1], halves[1][1])


def _banded_attention(q_arr, k_arr, v_arr, *, bsz, seq, dil, radius, q_col, k_col, v_col, kv_width,
                      pairs_per_kv, sink=None, want_lse, name):
    length = seq // dil
    hb = radius
    tq = min(256, length)
    assert tq % hb == 0 and length % tq == 0
    qw = DIL_HEADS * HEAD_DIM
    n_pairs = qw // LANES
    rep = tq // hb
    last_hb = length // hb - 1

    def prev_spec(col):
        return pl.BlockSpec((None, None, hb, kv_width),
                            lambda b, r, j: (b, r, jnp.maximum(j * rep - 1, 0), col))

    def cur_spec(col):
        return pl.BlockSpec((None, None, tq, kv_width), lambda b, r, j: (b, r, j, col))

    def next_spec(col):
        return pl.BlockSpec((None, None, hb, kv_width),
                            lambda b, r, j: (b, r, jnp.minimum((j + 1) * rep, last_hb), col))

    in_specs = [pl.BlockSpec((None, None, tq, qw), lambda b, r, j: (b, r, j, q_col)),
                prev_spec(k_col), cur_spec(k_col), next_spec(k_col),
                prev_spec(v_col), cur_spec(v_col), next_spec(v_col)]
    args = [q_arr, k_arr, k_arr, k_arr, v_arr, v_arr, v_arr]
    if sink is not None:
        in_specs = [pl.BlockSpec(memory_space=pltpu.SMEM)] + in_specs
        args = [sink] + args
    out_spec = pl.BlockSpec((None, None, tq, qw), lambda b, r, j: (b, r, j, 0))
    out_specs = [out_spec]
    out_shape = [jax.ShapeDtypeStruct((bsz, dil, length, qw), BF16)]
    if want_lse:
        out_specs.append(out_spec)
        out_shape.append(jax.ShapeDtypeStruct((bsz, dil, length, qw), F32))
    kern = functools.partial(_banded_kernel, radius=radius, tq=tq, hb=hb, length=length, n_pairs=n_pairs,
                             pairs_per_kv=pairs_per_kv, has_sink=sink is not None, want_lse=want_lse)
    return pl.pallas_call(
        kern,
        grid=(bsz, dil, length // tq),
        in_specs=in_specs,
        out_specs=out_specs,
        out_shape=out_shape,
        compiler_params=_cparams(("parallel", "parallel", "parallel")),
        name=name,
    )(*args)


def _router_kernel(x_ref, g_ref, sc_ref, sh_ref, wt_ref, b_ref, tri_ref,
                   idx_ref, gate_ref, rank_ref, cnt_ref, h_ref, carry_ref):
    @pl.when(pl.program_id(0) == 0)
    def _():
        carry_ref[...] = jnp.zeros_like(carry_ref)

    h = _ada_norm(x_ref[...], g_ref[...], sc_ref[...], sh_ref[...])
    h_ref[...] = _pack_bf16_pairs(h)
    lt = lax.dot_general(wt_ref[...], h, (((1,), (1,)), ((), ())), preferred_element_type=F32,
                         precision=lax.Precision.HIGHEST) + b_ref[...]
    n_e, tm = lt.shape
    e_iota = lax.broadcasted_iota(jnp.int32, (n_e, tm), 0)
    vals, sels = [], []
    for k in range(TOP_K):
        m = jnp.max(lt, axis=0, keepdims=True)
        idx = jnp.min(jnp.where(lt == m, e_iota, n_e), axis=0, keepdims=True)
        sel = e_iota == idx
        idx_ref[k:k + 1, :] = idx
        vals.append(m)
        sels.append(sel)
        lt = jnp.where(sel, -jnp.inf, lt)
    exps = [jnp.exp(v - vals[0]) for v in vals]
    den = exps[0] + exps[1] + exps[2] + exps[3]
    for k in range(TOP_K):
        gate_ref[k:k + 1, :] = exps[k] / den

    onehot = jnp.where(sels[0] | sels[1] | sels[2] | sels[3], 1.0, 0.0)
    carry = carry_ref[:, 0:1]
    before = jnp.dot(onehot.astype(BF16), tri_ref[...], preferred_element_type=F32) + carry
    for k in range(TOP_K):
        rank_ref[k:k + 1, :] = jnp.sum(jnp.where(sels[k], before, 0.0), axis=0, keepdims=True).astype(jnp.int32)
    total = carry + jnp.sum(onehot, axis=1, keepdims=True)
    carry_ref[...] = jnp.broadcast_to(total, carry_ref.shape)
    cnt_ref[...] = jnp.broadcast_to(total, cnt_ref.shape)


def _router(x, norm, w_router, b_router, *, seq, tm=512):
    t, d = x.shape
    n_e = w_router.shape[1]
    bpb = seq // tm
    tri = (jnp.arange(tm)[:, None] < jnp.arange(tm)[None, :]).astype(BF16)
    g, sc, sh = norm
    out_i = jax.ShapeDtypeStruct((TOP_K, t), jnp.int32)
    tok = pl.BlockSpec((TOP_K, tm), lambda i: (0, i))
    return pl.pallas_call(
        _router_kernel,
        grid=(t // tm,),
        in_specs=[pl.BlockSpec((tm, d), lambda i: (i, 0)),
                  pl.BlockSpec((1, d), lambda i: (0, 0)),
                  pl.BlockSpec((None, 1, d), lambda i: (i // bpb, 0, 0)),
                  pl.BlockSpec((None, 1, d), lambda i: (i // bpb, 0, 0)),
                  pl.BlockSpec((n_e, d), lambda i: (0, 0)),
                  pl.BlockSpec((n_e, 1), lambda i: (0, 0)),
                  pl.BlockSpec((tm, tm), lambda i: (0, 0))],
        out_specs=[tok, tok, tok, pl.BlockSpec((n_e, LANES), lambda i: (0, 0)),
                   pl.BlockSpec((tm, d // 2), lambda i: (i, 0))],
        out_shape=[out_i, jax.ShapeDtypeStruct((TOP_K, t), F32), out_i,
                   jax.ShapeDtypeStruct((n_e, LANES), F32), jax.ShapeDtypeStruct((t, d // 2), jnp.int32)],
        scratch_shapes=[pltpu.VMEM((n_e, LANES), F32)],
        compiler_params=_cparams(("arbitrary",)),
        name="router",
    )(x, g, sc, sh, w_router.T, b_router.reshape(n_e, 1), tri)


SC_CORES = 2
SC_SUBCORES = 16
SC_WORKERS = SC_CORES * SC_SUBCORES
SC_ROWS = 64


def _sc_worker_id():
    return lax.axis_index("subcore") * SC_CORES + lax.axis_index("core")


def _sc_mesh():
    return plsc.VectorSubcoreMesh(core_axis_name="core", subcore_axis_name="subcore")


def _sc_dispatch(h, dest_flat, pad_rows):
    t, d = h.shape
    n_pad = pad_rows.shape[0]
    n_rows = TOP_K * t + n_pad
    tok_w = t // SC_WORKERS
    pad_w = n_pad // SC_WORKERS
    nb = tok_w // SC_ROWS
    assert tok_w % (2 * SC_ROWS) == 0 and pad_w % SC_ROWS == 0

    @functools.partial(
        pl.kernel, out_type=jax.ShapeDtypeStruct((n_rows, d), h.dtype), mesh=_sc_mesh(),
        scratch_types=[pltpu.VMEM((SC_ROWS,), jnp.int32), pltpu.VMEM((2, SC_ROWS, d), h.dtype),
                       pltpu.SemaphoreType.DMA((2,))])
    def scatter_rows(h_hbm, dest_hbm, pad_hbm, zero_hbm, xs_hbm, idx_v, rows_v, sem):
        wid = _sc_worker_id()
        base = wid * tok_w

        def load(b, slot):
            return pltpu.make_async_copy(h_hbm.at[pl.ds(base + b * SC_ROWS, SC_ROWS)], rows_v.at[slot],
                                         sem.at[slot])

        load(0, 0).start()

        @pl.loop(0, nb, step=2)
        def _(i):
            for slot in range(2):
                b = i + slot
                load(b, slot).wait()

                @pl.when(b + 1 < nb)
                def _():
                    load(b + 1, 1 - slot).start()

                for k in range(TOP_K):
                    pltpu.sync_copy(dest_hbm.at[pl.ds(k * t + base + b * SC_ROWS, SC_ROWS)], idx_v)
                    pltpu.sync_copy(rows_v.at[slot], xs_hbm.at[idx_v])

        pltpu.sync_copy(zero_hbm, rows_v.at[0])

        @pl.loop(0, pad_w // SC_ROWS)
        def _(b):
            pltpu.sync_copy(pad_hbm.at[pl.ds(wid * pad_w + b * SC_ROWS, SC_ROWS)], idx_v)
            pltpu.sync_copy(rows_v.at[0], xs_hbm.at[idx_v])

    return scatter_rows(h, dest_flat, pad_rows, jnp.zeros((SC_ROWS, d), h.dtype))


def _sc_gather(table, idx):
    n = idx.shape[0]
    d = table.shape[1]
    per_w = n // SC_WORKERS
    nb = per_w // SC_ROWS
    assert per_w % (2 * SC_ROWS) == 0

    @functools.partial(
        pl.kernel, out_type=jax.ShapeDtypeStruct((n, d), table.dtype), mesh=_sc_mesh(),
        scratch_types=[pltpu.VMEM((2, SC_ROWS), jnp.int32), pltpu.VMEM((2, SC_ROWS, d), table.dtype),
                       pltpu.SemaphoreType.DMA((2,))])
    def gather_rows(table_hbm, idx_hbm, out_hbm, idx_v, rows_v, sem):
        base = _sc_worker_id() * per_w

        def gather(slot):
            return pltpu.make_async_copy(table_hbm.at[idx_v.at[slot]], rows_v.at[slot], sem.at[slot])

        def start(b, slot):
            pltpu.sync_copy(idx_hbm.at[pl.ds(base + b * SC_ROWS, SC_ROWS)], idx_v.at[slot])
            gather(slot).start()

        start(0, 0)
        start(1, 1)

        @pl.loop(0, nb, step=2)
        def _(i):
            for slot in range(2):
                b = i + slot
                gather(slot).wait()
                pltpu.sync_copy(rows_v.at[slot], out_hbm.at[pl.ds(base + b * SC_ROWS, SC_ROWS)])

                @pl.when(b + 2 < nb)
                def _():
                    start(b + 2, slot)

    return gather_rows(table, idx)


def _expert_kernel(be_ref, nv_ref, sb_ref, xs_ref, wgu_ref, bgu_ref, wd_ref, bd_ref, y_ref, wgu_bf, wd_bf):
    i = pl.program_id(0)
    n_valid = nv_ref[i]

    @pl.when((n_valid > 0) & ((i == 0) | (be_ref[i] != be_ref[jnp.maximum(i - 1, 0)])))
    def _():
        wgu_bf[...] = wgu_ref[...].astype(BF16)
        wd_bf[...] = wd_ref[...].astype(BF16)

    d_ff = wd_ref.shape[0]
    for r0 in range(0, MOE_ROWS, MOE_SUB_ROWS):
        rows = slice(r0, r0 + MOE_SUB_ROWS)

        @pl.when(n_valid > r0)
        def _():
            x_hi, x_lo = _unpack_bf16_pairs(xs_ref[rows, :])
            x = jnp.concatenate([x_hi.astype(BF16), x_lo.astype(BF16)], axis=1)
            gu = jnp.dot(x, wgu_bf[...], preferred_element_type=F32) + bgu_ref[...]
            glu = jnp.minimum(gu[:, :d_ff], SWIGLU_LIMIT)
            lin = jnp.clip(gu[:, d_ff:], -SWIGLU_LIMIT, SWIGLU_LIMIT)
            act = glu * (1.0 / (1.0 + jnp.exp(-SWIGLU_ALPHA * glu))) * (lin + 1.0)
            y = jnp.dot(act.astype(BF16), wd_bf[...], preferred_element_type=F32) + bd_ref[...]
            y_ref[rows, :] = _pack_bf16_pairs(y)

        @pl.when(n_valid <= r0)
        def _():
            y_ref[rows, :] = jnp.zeros((MOE_SUB_ROWS, y_ref.shape[1]), y_ref.dtype)


def _experts(xs, blk_expert, n_valid, src_blk, layer, w_gu, b_gu, w_down, b_down):
    n_rows, dw = xs.shape
    _, n_e, d, f2 = w_gu.shape
    n_blocks = n_rows // MOE_ROWS

    def wmap(i, be, nv, sb):
        return (layer, be[i], 0, 0)

    grid_spec = pltpu.PrefetchScalarGridSpec(
        num_scalar_prefetch=3,
        grid=(n_blocks,),
        in_specs=[pl.BlockSpec((MOE_ROWS, dw), lambda i, be, nv, sb: (sb[i], 0)),
                  pl.BlockSpec((None, None, d, f2), wmap),
                  pl.BlockSpec((None, None, 1, f2), wmap),
                  pl.BlockSpec((None, None, f2 // 2, d), wmap),
                  pl.BlockSpec((None, None, 1, d), wmap)],
        out_specs=pl.BlockSpec((MOE_ROWS, dw), lambda i, be, nv, sb: (i, 0)),
        scratch_shapes=[pltpu.VMEM((d, f2), BF16), pltpu.VMEM((f2 // 2, d), BF16)],
    )
    depth = w_gu.shape[0]
    return pl.pallas_call(
        _expert_kernel,
        grid_spec=grid_spec,
        out_shape=jax.ShapeDtypeStruct((n_rows, dw), jnp.int32),
        compiler_params=pltpu.CompilerParams(dimension_semantics=("arbitrary",),
                                             vmem_limit_bytes=EXPERT_VMEM_LIMIT),
        name="moe_experts",
    )(blk_expert, n_valid, src_blk, xs, w_gu, b_gu.reshape(depth, n_e, 1, f2), w_down, b_down.reshape(depth, n_e, 1, d))


def _combine_kernel(y0_ref, y1_ref, y2_ref, y3_ref, gate_ref, x_ref, g2_ref, o_ref):
    gate = gate_ref[...]
    w = y0_ref.shape[1]
    y_hi, y_lo = None, None
    for k, y_ref in enumerate((y0_ref, y1_ref, y2_ref, y3_ref)):
        hi, lo = _unpack_bf16_pairs(y_ref[...])
        gk = gate[:, k:k + 1]
        y_hi = gk * hi if y_hi is None else y_hi + gk * hi
        y_lo = gk * lo if y_lo is None else y_lo + gk * lo
    o_ref[:, :w] = x_ref[:, :w] + g2_ref[:, :w] * y_hi
    o_ref[:, w:] = x_ref[:, w:] + g2_ref[:, w:] * y_lo


def _combine(y4, gate, x, g2, *, seq, tm=512):
    t, d = x.shape
    bpb = seq // tm
    nblk = t // tm
    y_specs = [pl.BlockSpec((tm, d // 2), functools.partial(lambda i, k: (k * nblk + i, 0), k=k))
               for k in range(TOP_K)]
    return pl.pallas_call(
        _combine_kernel,
        grid=(nblk,),
        in_specs=y_specs + [pl.BlockSpec((tm, TOP_K), lambda i: (i, 0)),
                            pl.BlockSpec((tm, d), lambda i: (i, 0)),
                            pl.BlockSpec((None, 1, d), lambda i: (i // bpb, 0, 0))],
        out_specs=pl.BlockSpec((tm, d), lambda i: (i, 0)),
        out_shape=jax.ShapeDtypeStruct((t, d), F32),
        compiler_params=_cparams(("parallel",)),
        name="moe_combine",
    )(y4, y4, y4, y4, gate, x, g2)


def _moe(x, norm, g2, w_router, b_router, layer, w_gu, b_gu, w_down, b_down, *, seq):
    t, d = x.shape
    n_e = w_router.shape[1]
    idx_t, gate_t, rank_t, counts, h = _router(x, norm, w_router, b_router, seq=seq)
    e_ids = jnp.arange(n_e, dtype=jnp.int32)
    counts = counts[:, 0].astype(jnp.int32)
    padded = (counts + MOE_ROWS - 1) // MOE_ROWS * MOE_ROWS
    pend = jnp.cumsum(padded)
    pstart = pend - padded
    n_pad = n_e * MOE_ROWS
    n_rows = t * TOP_K + n_pad
    n_blocks = n_rows // MOE_ROWS
    dest_t = jnp.sum(jnp.where(idx_t[:, :, None] == e_ids, pstart, 0), axis=-1) + rank_t
    dest_flat = dest_t.reshape(TOP_K * t)
    blk_start = jnp.arange(n_blocks, dtype=jnp.int32) * MOE_ROWS
    n_used = pend[-1] // MOE_ROWS
    src_blk = jnp.minimum(jnp.arange(n_blocks, dtype=jnp.int32), n_used - 1)
    blk_expert = jnp.minimum(jnp.sum((src_blk * MOE_ROWS)[:, None] >= pend[None, :], axis=1), n_e - 1)
    blk_expert = blk_expert.astype(jnp.int32)
    own = blk_expert[:, None] == e_ids
    rows_left = jnp.sum(jnp.where(own, pstart + counts, 0), axis=-1) - blk_start
    n_valid = jnp.where(blk_start < pend[-1], jnp.clip(rows_left, 0, MOE_ROWS), 0).astype(jnp.int32)
    pad_len = padded - counts
    pad_end = jnp.cumsum(pad_len)
    j = jnp.arange(n_pad, dtype=jnp.int32)
    pe = jnp.sum(j[:, None] >= pad_end[None, :], axis=1)
    sel = jnp.minimum(pe, n_e - 1)[:, None] == e_ids
    in_expert = jnp.sum(jnp.where(sel, pstart + counts - (pad_end - pad_len), 0), axis=-1) + j
    pad_rows = jnp.where(pe < n_e, in_expert, pend[-1] + j - pad_end[-1]).astype(jnp.int32)

    xs = _sc_dispatch(h, dest_flat, pad_rows)
    yb = _experts(xs, blk_expert, n_valid, src_blk, layer, w_gu, b_gu, w_down, b_down)
    y4 = _sc_gather(yb, dest_flat)
    return _combine(y4, gate_t.T, x, g2, seq=seq)


def _final_norm_kernel(x_ref, g_ref, o_ref):
    o_ref[...] = _rms(x_ref[...]) * g_ref[...]


def _final_norm(x, g, tm=1024):
    t, d = x.shape
    return pl.pallas_call(
        _final_norm_kernel,
        grid=(t // tm,),
        in_specs=[pl.BlockSpec((tm, d), lambda i: (i, 0)), pl.BlockSpec((1, d), lambda i: (0, 0))],
        out_specs=pl.BlockSpec((tm, d), lambda i: (i, 0)),
        out_shape=jax.ShapeDtypeStruct((t, d), F32),
        compiler_params=_cparams(("parallel",)),
        name="final_norm",
    )(x, g.reshape(1, d))


def _dup_heads(w, n_heads):
    d = w.shape[0]
    return jnp.repeat(w.reshape(d, n_heads, 1, HEAD_DIM), 2, axis=2).reshape(d, n_heads * LANES)


def _mixer_mla_swa(x, norm, g1, cos, sin, w_in, g_q, w_qb, g_kv, w_kvb, sink, w_out, *, bsz, seq):
    d = x.shape[1]
    o = np.cumsum((0, MLA_Q_LORA, MLA_KV_LORA, MLA_ROPE, SWA_Q_HEADS * HEAD_DIM,
                   SWA_KV_HEADS * HEAD_DIM, SWA_KV_HEADS * HEAD_DIM))
    w_cq, w_ckv, w_kr, w_qs, w_ks, w_vs = (w_in[:, o[i]:o[i + 1]] for i in range(6))
    w_lat = jnp.concatenate([w_cq, jnp.zeros((d, LANES), F32), w_ckv], axis=1).astype(BF16)
    lat = _proj([x], [w_lat], prologue="ada", norm=norm, seq=seq, tm=512, tn=w_lat.shape[1],
                out_dtype=F32, name="proj_latent")
    w_b = jnp.concatenate([_pair_layout(w_qs), _pair_layout(_dup_heads(w_ks, SWA_KV_HEADS)),
                           _dup_heads(w_vs, SWA_KV_HEADS),
                           w_kr, jnp.zeros((d, LANES - MLA_ROPE), F32)], axis=1).astype(BF16)
    qkv_b = _proj([x], [w_b], prologue="ada", norm=norm, seq=seq, tm=512, tn=w_b.shape[1], out_dtype=BF16,
                  rope=(cos, sin), rope_pattern=(ROPE_PAIR,) * 10 + (0, 0) + (ROPE_ROTATE,), name="proj_swa")
    kr_col = (w_b.shape[1] - LANES) // LANES
    qd = MLA_NOPE + MLA_ROPE
    w_q = w_qb.reshape(MLA_Q_LORA, MLA_HEADS, qd)
    w_q = jnp.concatenate([w_q, jnp.zeros((MLA_Q_LORA, MLA_HEADS, 2 * LANES - qd), F32)], axis=2)
    w_q = w_q.reshape(MLA_Q_LORA, MLA_HEADS * 2 * LANES).astype(BF16)
    q_cat = _proj([lat], [w_q], prologue="rms", norm=(g_q.reshape(1, -1),), lhs_cols=(MLA_Q_LORA, 0),
                  seq=seq, tm=512, tn=w_q.shape[1], out_dtype=BF16, rope=(cos, sin),
                  rope_pattern=(0, 1) * MLA_HEADS, out_scale=qd ** -0.5, name="proj_mla_q")
    w_kv = w_kvb.reshape(MLA_KV_LORA, MLA_HEADS, MLA_NOPE + MLA_V)
    w_kv = jnp.concatenate([w_kv[:, :, :MLA_NOPE].reshape(MLA_KV_LORA, -1),
                            w_kv[:, :, MLA_NOPE:].reshape(MLA_KV_LORA, -1)], axis=1).astype(BF16)
    kv = _proj([lat], [w_kv], prologue="rms", norm=(g_kv.reshape(1, -1),), lhs_cols=(MLA_KV_LORA, 2),
               seq=seq, tm=512, tn=w_kv.shape[1], out_dtype=BF16, name="proj_mla_kv")
    o_a = _mla_attention(q_cat, kv, qkv_b, kr_col, bsz=bsz, seq=seq, tq=256)
    qkv_b4 = qkv_b.reshape(bsz, 1, seq, qkv_b.shape[1])
    (o_b,) = _banded_attention(qkv_b4, qkv_b4, qkv_b4, bsz=bsz, seq=seq, dil=1, radius=SWA_RADIUS,
                               q_col=0, k_col=4, v_col=5, kv_width=2 * LANES,
                               pairs_per_kv=SWA_Q_HEADS // SWA_KV_HEADS // 2, sink=sink, want_lse=False,
                               name="swa_attention")
    o_b = o_b.reshape(bsz * seq, SWA_Q_HEADS * HEAD_DIM)
    na = MLA_HEADS * MLA_V
    return _proj([o_a, o_b], [w_out[:na].astype(BF16), w_out[na:].astype(BF16)], prologue="plain",
                 seq=seq, tm=512, tn=d, out_dtype=F32, residual=(x, g1), name="proj_out_ab")


def _mixer_dilated(x, norm, g1, cos, sin, w_in, w_out, *, bsz, seq):
    d = x.shape[1]
    qw = DIL_HEADS * HEAD_DIM
    w_bf = w_in.astype(BF16)
    outs, lses = [], []
    for g, (window, dil) in enumerate(DIL_PATTERN):
        w_g = w_bf[:, g * 3 * qw:(g + 1) * 3 * qw]
        w_g = jnp.concatenate([_pair_layout(w_g[:, :2 * qw]), w_g[:, 2 * qw:]], axis=1)
        qkv = _proj([x], [w_g], prologue="ada", norm=norm, seq=seq, tm=512, tn=3 * qw, sub_tn=qw, out_dtype=BF16,
                    rope=(cos, sin), rope_pattern=(ROPE_PAIR,) * (2 * qw // LANES) + (0,) * (qw // LANES),
                    out_dil=dil, name=f"proj_dil_{dil}")
        qkv = qkv.reshape(bsz, dil, seq // dil, 3 * qw)
        o, lse = _banded_attention(qkv, qkv, qkv, bsz=bsz, seq=seq, dil=dil, radius=window // (2 * dil),
                                   q_col=0, k_col=1, v_col=2, kv_width=qw, pairs_per_kv=1,
                                   want_lse=True, name=f"dilated_attention_{dil}")
        outs.append(o)
        lses.append(lse)
    return _proj(outs + lses, [w_out.astype(BF16)], prologue="dil", seq=seq, tm=512, tn=d, out_dtype=F32,
                 residual=(x, g1), in_dils=tuple(dil for _, dil in DIL_PATTERN), name="proj_out_c")


def kernel(x, c, positions, w_mod, b_mod, g_norm_mix, g_norm_ffn, w_in_ab, mla_g_q, mla_w_qb, mla_g_kv,
           mla_w_kvb, swa_sink, w_out_ab, w_in_c, w_out_c, w_router, b_router, w_gu, b_gu, w_down, b_down,
           g_final):
    bsz, seq, d = x.shape
    depth = w_mod.shape[0]
    cos, sin = _rope_tables(positions)
    mod = _modulation(c, w_mod, b_mod)
    xt = x.reshape(bsz * seq, d)
    for layer in range(depth):
        sh1, sc1, g1, sh2, sc2, g2 = (mod[layer, :, i * d:(i + 1) * d].reshape(bsz, 1, d) for i in range(6))
        li = layer // 2
        norm = (g_norm_mix[layer].reshape(1, d), sc1, sh1)
        if layer % 2 == 0:
            xt = _mixer_mla_swa(xt, norm, g1, cos, sin, w_in_ab[li], mla_g_q[li], mla_w_qb[li], mla_g_kv[li],
                                mla_w_kvb[li], swa_sink[li], w_out_ab[li], bsz=bsz, seq=seq)
        else:
            xt = _mixer_dilated(xt, norm, g1, cos, sin, w_in_c[li], w_out_c[li], bsz=bsz, seq=seq)
        norm = (g_norm_ffn[layer].reshape(1, d), sc2, sh2)
        xt = _moe(xt, norm, g2, w_router[layer], b_router[layer], layer, w_gu, b_gu, w_down, b_down, seq=seq)
    return _final_norm(xt, g_final).reshape(bsz, seq, d)
```

```python
import functools

import jax
import jax.numpy as jnp
import numpy as np
from jax import lax
from jax.experimental import pallas as pl
from jax.experimental.pallas import tpu as pltpu
from jax.experimental.pallas import tpu_sc as plsc

F32 = jnp.float32
BF16 = jnp.bfloat16

EPS = 1e-6
ROPE_THETA = 10000.0
LANES = 128
HEAD_DIM = 64
ROPE_HALF = HEAD_DIM // 2

MLA_HEADS = 8
MLA_Q_LORA = 384
MLA_KV_LORA = 256
MLA_NOPE = 128
MLA_ROPE = 64
MLA_V = 128
SWA_Q_HEADS = 16
SWA_KV_HEADS = 2
SWA_RADIUS = 128
DIL_PATTERN = ((128, 1), (512, 4), (2048, 16))
DIL_HEADS = 16
N_EXPERTS = 32
TOP_K = 4
SWIGLU_LIMIT = 7.0
SWIGLU_ALPHA = 1.702
MOE_ROWS = 512
BAND_SUB_ROWS = 128

VMEM_LIMIT = 48 * 1024 * 1024
EXPERT_VMEM_LIMIT = 56 * 1024 * 1024


def _cparams(sem):
    return pltpu.CompilerParams(dimension_semantics=sem, vmem_limit_bytes=VMEM_LIMIT)


def _mod_kernel(c_ref, w_ref, b_ref, o_ref):
    c = c_ref[...]
    ca = c * (1.0 / (1.0 + jnp.exp(-c)))
    o_ref[...] = jnp.dot(ca, w_ref[...], preferred_element_type=F32,
                         precision=lax.Precision.HIGHEST) + b_ref[...]


def _modulation(c, w_mod, b_mod):
    depth, d, n = w_mod.shape
    bsz = c.shape[0]
    tn = 1536
    return pl.pallas_call(
        _mod_kernel,
        grid=(depth, n // tn),
        in_specs=[pl.BlockSpec((bsz, d), lambda l, j: (0, 0)),
                  pl.BlockSpec((None, d, tn), lambda l, j: (l, 0, j)),
                  pl.BlockSpec((None, 1, tn), lambda l, j: (l, 0, j))],
        out_specs=pl.BlockSpec((None, bsz, tn), lambda l, j: (l, 0, j)),
        out_shape=jax.ShapeDtypeStruct((depth, bsz, n), F32),
        compiler_params=_cparams(("parallel", "parallel")),
        name="modulation",
    )(c, w_mod, b_mod.reshape(depth, 1, n))


def _rope_table_kernel(pos_ref, inv_ref, sign_ref, cos_ref, sin_ref):
    ang = inv_ref[...] * pos_ref[...].astype(F32)
    cos_ref[...] = jnp.transpose(jnp.cos(ang))
    sin_ref[...] = jnp.transpose(jnp.sin(ang) * sign_ref[...])


def _rope_tables(positions):
    bsz, s = positions.shape
    inv = ROPE_THETA ** (-jnp.arange(0, HEAD_DIM, 2, dtype=F32) / HEAD_DIM)
    inv128 = jnp.tile(inv, LANES // ROPE_HALF).reshape(LANES, 1)
    sign = jnp.where((jnp.arange(LANES) % HEAD_DIM) < ROPE_HALF, -1.0, 1.0).astype(F32).reshape(LANES, 1)
    out = jax.ShapeDtypeStruct((bsz * s, LANES), F32)
    return pl.pallas_call(
        _rope_table_kernel,
        grid=(bsz,),
        in_specs=[pl.BlockSpec((None, 1, s), lambda b: (b, 0, 0)),
                  pl.BlockSpec((LANES, 1), lambda b: (0, 0)),
                  pl.BlockSpec((LANES, 1), lambda b: (0, 0))],
        out_specs=[pl.BlockSpec((s, LANES), lambda b: (b, 0)),
                   pl.BlockSpec((s, LANES), lambda b: (b, 0))],
        out_shape=[out, out],
        compiler_params=_cparams(("parallel",)),
        name="rope_tables",
    )(positions.reshape(bsz, 1, s), inv128, sign)


def _rms(x):
    return x * lax.rsqrt(jnp.mean(x * x, axis=-1, keepdims=True) + EPS)


def _ada_norm(x, g, sc, sh):
    return (_rms(x) * g) * (1.0 + sc) + sh


def _pack_bf16_pairs(x):
    w = x.shape[1] // 2
    hi = pltpu.bitcast(x[:, :w].astype(BF16).astype(F32), jnp.int32)
    lo = pltpu.bitcast(x[:, w:].astype(BF16).astype(F32), jnp.int32)
    return hi | lax.shift_right_logical(lo, 16)


def _unpack_bf16_pairs(p):
    hi = pltpu.bitcast(p & jnp.int32(-65536), F32)
    lo = pltpu.bitcast(lax.shift_left(p, 16), F32)
    return hi, lo


ROPE_ROTATE, ROPE_PAIR = 1, 2


def _pair_layout(w):
    k, n = w.shape
    w = w.reshape(k, n // LANES, 2, 2, ROPE_HALF)
    return jnp.swapaxes(w, 2, 3).reshape(k, n)


def _rope_group(a, cos, sin, first_half):
    rot = jnp.where(first_half, pltpu.roll(a, LANES - ROPE_HALF, 1), pltpu.roll(a, ROPE_HALF, 1))
    return a * cos + rot * sin


def _proj_kernel(*refs, prologue, n_lhs, rope_pattern, sub_tn, out_scale, residual, in_dils, out_dil):
    pos = 0
    if prologue == "ada":
        x_ref, g_ref, sc_ref, sh_ref = refs[:4]
        pos = 4
    elif prologue == "rms":
        x_ref, g_ref = refs[:2]
        pos = 2
    elif prologue == "dil":
        o_refs = refs[0:3]
        l_refs = refs[3:6]
        pos = 6
    else:
        x_refs = refs[:n_lhs]
        pos = n_lhs
    w_refs = refs[pos:pos + n_lhs]
    pos += n_lhs
    if rope_pattern is not None:
        cos_ref, sin_ref = refs[pos:pos + 2]
        pos += 2
    if residual:
        res_ref, gate_ref = refs[pos:pos + 2]
        pos += 2
    o_ref = refs[pos]
    pos += 1
    h_ref = refs[pos] if prologue != "plain" else None
    slab_ref = refs[pos + 1] if (prologue == "dil" or out_dil > 1) else None

    if prologue != "plain":
        @pl.when(pl.program_id(1) == 0)
        def _():
            if prologue == "ada":
                h_ref[...] = _ada_norm(x_ref[...], g_ref[...], sc_ref[...], sh_ref[...]).astype(BF16)
            elif prologue == "rms":
                h_ref[...] = (_rms(x_ref[...]) * g_ref[...]).astype(BF16)
            else:
                tm = h_ref.shape[0]
                for c in range(h_ref.shape[1] // LANES):
                    cols = slice(c * LANES, (c + 1) * LANES)
                    vals = []
                    for i, (ref, dil) in enumerate(zip(o_refs + l_refs, in_dils + in_dils)):
                        if dil == 1:
                            vals.append(ref[0, :, cols].astype(F32))
                        else:
                            for r in range(dil):
                                slab_ref[i, pl.ds(r, tm // dil, stride=dil), :] = ref[r, :, cols].astype(F32)
                            vals.append(slab_ref[i])
                    o0, o1, o2, l0, l1, l2 = vals
                    m = jnp.maximum(jnp.maximum(l0, l1), l2)
                    e0, e1, e2 = jnp.exp(l0 - m), jnp.exp(l1 - m), jnp.exp(l2 - m)
                    h_ref[:, cols] = ((e0 * o0 + e1 * o1 + e2 * o2) / (e0 + e1 + e2)).astype(BF16)

    n_total = w_refs[0].shape[1]
    tn = n_total if sub_tn is None else sub_tn

    def matmul(t):
        cols = slice(t * tn, (t + 1) * tn)
        if prologue != "plain":
            acc = jnp.dot(h_ref[...], w_refs[0][:, cols], preferred_element_type=F32)
        else:
            acc = jnp.dot(x_refs[0][...], w_refs[0][:, cols], preferred_element_type=F32)
            for xr, wr in zip(x_refs[1:], w_refs[1:]):
                acc = acc + jnp.dot(xr[...], wr[:, cols], preferred_element_type=F32)
        return acc * out_scale if out_scale != 1.0 else acc

    if residual:
        o_ref[...] = (res_ref[...] + gate_ref[...] * matmul(0)).astype(o_ref.dtype)
        return

    pattern = rope_pattern if rope_pattern is not None else (0,) * (n_total // LANES)
    if any(pattern):
        cos = cos_ref[...]
        sin = sin_ref[...]
        lane = lax.broadcasted_iota(jnp.int32, cos.shape, 1)
        first_half = (lane % HEAD_DIM) < ROPE_HALF
        if ROPE_PAIR in pattern:
            sin_pair = jnp.where(lane < HEAD_DIM, -1.0, 1.0) * jnp.where(first_half, -sin, sin)

    def epilogue(acc, t):
        tm = acc.shape[0]
        for c in range(tn // LANES):
            g = t * (tn // LANES) + c
            cols = slice(g * LANES, (g + 1) * LANES)
            a = acc[:, c * LANES:(c + 1) * LANES]
            if pattern[g] == ROPE_ROTATE:
                a = _rope_group(a, cos, sin, first_half)
            elif pattern[g] == ROPE_PAIR:
                a = a * cos + pltpu.roll(a, HEAD_DIM, 1) * sin_pair
            if out_dil == 1:
                o_ref[:, cols] = a.astype(o_ref.dtype)
            else:
                slab_ref[0] = a
                for r in range(out_dil):
                    o_ref[r, :, cols] = slab_ref[0, pl.ds(r, tm // out_dil, stride=out_dil), :].astype(o_ref.dtype)

    acc_next = matmul(0)
    for t in range(n_total // tn):
        acc = acc_next
        if t + 1 < n_total // tn:
            acc_next = matmul(t + 1)
        epilogue(acc, t)


def _proj(lhs, ws, *, prologue, seq, tm, tn, out_dtype, norm=None, lhs_cols=None, rope=None,
          rope_pattern=None, sub_tn=None, out_scale=1.0, residual=None, in_dils=None, out_dil=1, name):
    k, n = ws[0].shape
    bsz_seq = lhs[0].shape[0] if prologue != "dil" else lhs[0].shape[0] * seq
    t = bsz_seq
    assert t % tm == 0 and n % tn == 0 and seq % tm == 0
    bpb = seq // tm
    n_lhs = len(ws)
    args, specs = [], []
    row = lambda i, j: (i, 0)
    if prologue in ("ada", "rms"):
        kw, kc = lhs_cols if lhs_cols is not None else (k, 0)
        args.append(lhs[0])
        specs.append(pl.BlockSpec((tm, kw), lambda i, j: (i, kc)))
        args.append(norm[0])
        specs.append(pl.BlockSpec((1, k), lambda i, j: (0, 0)))
        if prologue == "ada":
            for v in norm[1:3]:
                args.append(v)
                specs.append(pl.BlockSpec((None, 1, k), lambda i, j: (i // bpb, 0, 0)))
    elif prologue == "dil":
        for a, dil in zip(lhs, in_dils + in_dils):
            args.append(a)
            specs.append(pl.BlockSpec((None, dil, tm // dil, k), lambda i, j: (i // bpb, 0, i % bpb, 0)))
    else:
        for a, w in zip(lhs, ws):
            args.append(a)
            specs.append(pl.BlockSpec((tm, w.shape[0]), row))
    for w in ws:
        args.append(w)
        specs.append(pl.BlockSpec((w.shape[0], tn), lambda i, j: (0, j)))
    if rope_pattern is not None:
        assert len(rope_pattern) == tn // LANES
        for tab in rope:
            args.append(tab)
            specs.append(pl.BlockSpec((tm, LANES), row))
    if residual is not None:
        res, gate = residual
        args.append(res)
        specs.append(pl.BlockSpec((tm, tn), lambda i, j: (i, j)))
        args.append(gate)
        specs.append(pl.BlockSpec((None, 1, tn), lambda i, j: (i // bpb, 0, j)))
    scratch = [] if prologue == "plain" else [pltpu.VMEM((tm, k), BF16)]
    if prologue == "dil":
        scratch.append(pltpu.VMEM((2 * len(in_dils), tm, LANES), F32))
    elif out_dil > 1:
        scratch.append(pltpu.VMEM((1, tm, LANES), F32))
    if out_dil == 1:
        out_spec = pl.BlockSpec((tm, tn), lambda i, j: (i, j))
        out_shape = jax.ShapeDtypeStruct((t, n), out_dtype)
    else:
        out_spec = pl.BlockSpec((None, out_dil, tm // out_dil, tn), lambda i, j: (i // bpb, 0, i % bpb, j))
        out_shape = jax.ShapeDtypeStruct((t // seq, out_dil, seq // out_dil, n), out_dtype)
    kern = functools.partial(_proj_kernel, prologue=prologue, n_lhs=n_lhs, rope_pattern=rope_pattern,
                             sub_tn=sub_tn, out_scale=out_scale, residual=residual is not None,
                             in_dils=in_dils, out_dil=out_dil)
    return pl.pallas_call(
        kern,
        grid=(t // tm, n // tn),
        in_specs=specs,
        out_specs=out_spec,
        out_shape=out_shape,
        scratch_shapes=scratch,
        compiler_params=_cparams(("parallel", "arbitrary")),
        name=name,
    )(*args)


MLA_HEADS_PER_STEP = 4


def _mla_scores(q_ref, kn_ref, kr, h):
    k = jnp.concatenate([kn_ref[:, h * LANES:(h + 1) * LANES], kr], axis=1)
    q = q_ref[:, h * 2 * LANES:(h + 1) * 2 * LANES]
    return lax.dot_general(q, k, (((1,), (1,)), ((), ())), preferred_element_type=F32)


def _mla_kernel(q_ref, kn_ref, kr_ref, v_ref, o_ref):
    kr = kr_ref[...]
    s_next = _mla_scores(q_ref, kn_ref, kr, 0)
    for h in range(MLA_HEADS_PER_STEP):
        s = s_next
        if h + 1 < MLA_HEADS_PER_STEP:
            s_next = _mla_scores(q_ref, kn_ref, kr, h + 1)
        m = jnp.max(s, axis=1, keepdims=True)
        p = jnp.exp(s - m).astype(BF16)
        v = v_ref[:, h * LANES:(h + 1) * LANES]
        oe = jnp.dot(p, jnp.concatenate([v, jnp.ones_like(v)], axis=1), preferred_element_type=F32)
        o_ref[:, h * LANES:(h + 1) * LANES] = (oe[:, :LANES] / oe[:, LANES:]).astype(o_ref.dtype)


def _mla_attention(q_cat, kv, qkv_b, kr_col, *, bsz, seq, tq):
    t = bsz * seq
    nq = seq // tq
    hps = MLA_HEADS_PER_STEP
    groups = MLA_HEADS // hps
    return pl.pallas_call(
        _mla_kernel,
        grid=(bsz, groups, nq),
        in_specs=[pl.BlockSpec((tq, hps * 2 * LANES), lambda b, h, i: (b * nq + i, h)),
                  pl.BlockSpec((seq, hps * LANES), lambda b, h, i: (b, h)),
                  pl.BlockSpec((seq, LANES), lambda b, h, i: (b, kr_col)),
                  pl.BlockSpec((seq, hps * LANES), lambda b, h, i: (b, groups + h))],
        out_specs=pl.BlockSpec((tq, hps * LANES), lambda b, h, i: (b * nq + i, h)),
        out_shape=jax.ShapeDtypeStruct((t, MLA_HEADS * MLA_V), BF16),
        compiler_params=_cparams(("parallel", "parallel", "parallel")),
        name="mla_attention",
    )(q_cat, kv, qkv_b, kv)


def _banded_kernel(*refs, radius, tq, hb, length, n_pairs, pairs_per_kv, has_sink, want_lse):
    pos = 0
    if has_sink:
        sink_ref = refs[0]
        pos = 1
    q_ref, kp_ref, kc_ref, kn_ref, vp_ref, vc_ref, vn_ref = refs[pos:pos + 7]
    pos += 7
    o_ref = refs[pos]
    lse_ref = refs[pos + 1] if want_lse else None

    j = pl.program_id(2)
    kw = jnp.concatenate([kp_ref[...], kc_ref[...], kn_ref[...]], axis=0)
    vw = jnp.concatenate([vp_ref[...], vc_ref[...], vn_ref[...]], axis=0)
    scale = HEAD_DIM ** -0.5
    qn = min(tq, BAND_SUB_ROWS)
    kn = qn + 2 * hb
    lane = lax.broadcasted_iota(jnp.int32, (qn, LANES), 1)
    low = lane < HEAD_DIM
    first = (lane % HEAD_DIM) < ROPE_HALF

    for q0 in range(0, tq, qn):
        qpos = j * tq + q0 + lax.broadcasted_iota(jnp.int32, (qn, kn), 0)
        kpos = j * tq - hb + q0 + lax.broadcasted_iota(jnp.int32, (qn, kn), 1)
        valid = (jnp.abs(qpos - kpos) <= radius) & (kpos >= 0) & (kpos < length)
        for p in range(n_pairs):
            c = p // pairs_per_kv
            qp = q_ref[q0:q0 + qn, p * LANES:(p + 1) * LANES] * scale
            kp = kw[q0:q0 + kn, c * LANES:(c + 1) * LANES]
            vp = vw[q0:q0 + kn, c * LANES:(c + 1) * LANES]
            vp = jnp.concatenate([vp, jnp.ones_like(vp)], axis=1)
            halves = []
            for half in range(2):
                qh = jnp.where(first if half == 0 else ~first, qp, jnp.zeros_like(qp))
                s = lax.dot_general(qh, kp, (((1,), (1,)), ((), ())), preferred_element_type=F32)
                s = jnp.where(valid, s, -jnp.inf)
                m = jnp.max(s, axis=1, keepdims=True)
                if has_sink:
                    sk = sink_ref[2 * p + half]
                    m = jnp.maximum(m, sk)
                oe = jnp.dot(jnp.exp(s - m).astype(BF16), vp, preferred_element_type=F32)
                den = oe[:, LANES:]
                if has_sink:
                    den = den + jnp.exp(sk - m)
                halves.append((oe[:, :LANES] / den, m + jnp.log(den) if want_lse else None))
            o_ref[q0:q0 + qn, p * LANES:(p + 1) * LANES] = jnp.where(
                low, halves[0][0], halves[1][0]).astype(o_ref.dtype)
            if want_lse:
                lse_ref[q0:q0 + qn, p * LANES:(p + 1) * LANES] = jnp.where(low, halves[0][1], halves[1][1])


def _banded_attention(q_arr, k_arr, v_arr, *, bsz, seq, dil, radius, q_col, k_col, v_col, kv_width,
                      pairs_per_kv, sink=None, want_lse, name):
    length = seq // dil
    hb = radius
    tq = min(256, length)
    assert tq % hb == 0 and length % tq == 0
    qw = DIL_HEADS * HEAD_DIM
    n_pairs = qw // LANES
    rep = tq // hb
    last_hb = length // hb - 1

    def prev_spec(col):
        return pl.BlockSpec((None, None, hb, kv_width),
                            lambda b, r, j: (b, r, jnp.maximum(j * rep - 1, 0), col))

    def cur_spec(col):
        return pl.BlockSpec((None, None, tq, kv_width), lambda b, r, j: (b, r, j, col))

    def next_spec(col):
        return pl.BlockSpec((None, None, hb, kv_width),
                            lambda b, r, j: (b, r, jnp.minimum((j + 1) * rep, last_hb), col))

    in_specs = [pl.BlockSpec((None, None, tq, qw), lambda b, r, j: (b, r, j, q_col)),
                prev_spec(k_col), cur_spec(k_col), next_spec(k_col),
                prev_spec(v_col), cur_spec(v_col), next_spec(v_col)]
    args = [q_arr, k_arr, k_arr, k_arr, v_arr, v_arr, v_arr]
    if sink is not None:
        in_specs = [pl.BlockSpec(memory_space=pltpu.SMEM)] + in_specs
        args = [sink] + args
    out_spec = pl.BlockSpec((None, None, tq, qw), lambda b, r, j: (b, r, j, 0))
    out_specs = [out_spec]
    out_shape = [jax.ShapeDtypeStruct((bsz, dil, length, qw), BF16)]
    if want_lse:
        out_specs.append(out_spec)
        out_shape.append(jax.ShapeDtypeStruct((bsz, dil, length, qw), F32))
    kern = functools.partial(_banded_kernel, radius=radius, tq=tq, hb=hb, length=length, n_pairs=n_pairs,
                             pairs_per_kv=pairs_per_kv, has_sink=sink is not None, want_lse=want_lse)
    return pl.pallas_call(
        kern,
        grid=(bsz, dil, length // tq),
        in_specs=in_specs,
        out_specs=out_specs,
        out_shape=out_shape,
        compiler_params=_cparams(("parallel", "parallel", "parallel")),
        name=name,
    )(*args)


def _router_kernel(x_ref, g_ref, sc_ref, sh_ref, wt_ref, b_ref, tri_ref,
                   idx_ref, gate_ref, rank_ref, cnt_ref, h_ref, carry_ref):
    @pl.when(pl.program_id(0) == 0)
    def _():
        carry_ref[...] = jnp.zeros_like(carry_ref)

    h = _ada_norm(x_ref[...], g_ref[...], sc_ref[...], sh_ref[...])
    h_ref[...] = _pack_bf16_pairs(h)
    lt = lax.dot_general(wt_ref[...], h, (((1,), (1,)), ((), ())), preferred_element_type=F32,
                         precision=lax.Precision.HIGHEST) + b_ref[...]
    n_e, tm = lt.shape
    e_iota = lax.broadcasted_iota(jnp.int32, (n_e, tm), 0)
    vals, sels = [], []
    for k in range(TOP_K):
        m = jnp.max(lt, axis=0, keepdims=True)
        idx = jnp.min(jnp.where(lt == m, e_iota, n_e), axis=0, keepdims=True)
        sel = e_iota == idx
        idx_ref[k:k + 1, :] = idx
        vals.append(m)
        sels.append(sel)
        lt = jnp.where(sel, -jnp.inf, lt)
    exps = [jnp.exp(v - vals[0]) for v in vals]
    den = exps[0] + exps[1] + exps[2] + exps[3]
    for k in range(TOP_K):
        gate_ref[k:k + 1, :] = exps[k] / den

    onehot = jnp.where(sels[0] | sels[1] | sels[2] | sels[3], 1.0, 0.0)
    carry = carry_ref[:, 0:1]
    before = jnp.dot(onehot.astype(BF16), tri_ref[...], preferred_element_type=F32) + carry
    for k in range(TOP_K):
        rank_ref[k:k + 1, :] = jnp.sum(jnp.where(sels[k], before, 0.0), axis=0, keepdims=True).astype(jnp.int32)
    total = carry + jnp.sum(onehot, axis=1, keepdims=True)
    carry_ref[...] = jnp.broadcast_to(total, carry_ref.shape)
    cnt_ref[...] = jnp.broadcast_to(total, cnt_ref.shape)


def _router(x, norm, w_router, b_router, *, seq, tm=512):
    t, d = x.shape
    n_e = w_router.shape[1]
    bpb = seq // tm
    tri = (jnp.arange(tm)[:, None] < jnp.arange(tm)[None, :]).astype(BF16)
    g, sc, sh = norm
    out_i = jax.ShapeDtypeStruct((TOP_K, t), jnp.int32)
    tok = pl.BlockSpec((TOP_K, tm), lambda i: (0, i))
    return pl.pallas_call(
        _router_kernel,
        grid=(t // tm,),
        in_specs=[pl.BlockSpec((tm, d), lambda i: (i, 0)),
                  pl.BlockSpec((1, d), lambda i: (0, 0)),
                  pl.BlockSpec((None, 1, d), lambda i: (i // bpb, 0, 0)),
                  pl.BlockSpec((None, 1, d), lambda i: (i // bpb, 0, 0)),
                  pl.BlockSpec((n_e, d), lambda i: (0, 0)),
                  pl.BlockSpec((n_e, 1), lambda i: (0, 0)),
                  pl.BlockSpec((tm, tm), lambda i: (0, 0))],
        out_specs=[tok, tok, tok, pl.BlockSpec((n_e, LANES), lambda i: (0, 0)),
                   pl.BlockSpec((tm, d // 2), lambda i: (i, 0))],
        out_shape=[out_i, jax.ShapeDtypeStruct((TOP_K, t), F32), out_i,
                   jax.ShapeDtypeStruct((n_e, LANES), F32), jax.ShapeDtypeStruct((t, d // 2), jnp.int32)],
        scratch_shapes=[pltpu.VMEM((n_e, LANES), F32)],
        compiler_params=_cparams(("arbitrary",)),
        name="router",
    )(x, g, sc, sh, w_router.T, b_router.reshape(n_e, 1), tri)


SC_CORES = 2
SC_SUBCORES = 16
SC_WORKERS = SC_CORES * SC_SUBCORES
SC_ROWS = 64


def _sc_worker_id():
    return lax.axis_index("subcore") * SC_CORES + lax.axis_index("core")


def _sc_mesh():
    return plsc.VectorSubcoreMesh(core_axis_name="core", subcore_axis_name="subcore")


def _sc_dispatch(h, dest_flat, pad_rows):
    t, d = h.shape
    n_pad = pad_rows.shape[0]
    n_rows = TOP_K * t + n_pad
    tok_w = t // SC_WORKERS
    pad_w = n_pad // SC_WORKERS
    nb = tok_w // SC_ROWS
    assert tok_w % (2 * SC_ROWS) == 0 and pad_w % SC_ROWS == 0

    @functools.partial(
        pl.kernel, out_type=jax.ShapeDtypeStruct((n_rows, d), h.dtype), mesh=_sc_mesh(),
        scratch_types=[pltpu.VMEM((SC_ROWS,), jnp.int32), pltpu.VMEM((2, SC_ROWS, d), h.dtype),
                       pltpu.SemaphoreType.DMA((2,))])
    def scatter_rows(h_hbm, dest_hbm, pad_hbm, zero_hbm, xs_hbm, idx_v, rows_v, sem):
        wid = _sc_worker_id()
        base = wid * tok_w

        def load(b, slot):
            return pltpu.make_async_copy(h_hbm.at[pl.ds(base + b * SC_ROWS, SC_ROWS)], rows_v.at[slot],
                                         sem.at[slot])

        load(0, 0).start()

        @pl.loop(0, nb, step=2)
        def _(i):
            for slot in range(2):
                b = i + slot
                load(b, slot).wait()

                @pl.when(b + 1 < nb)
                def _():
                    load(b + 1, 1 - slot).start()

                for k in range(TOP_K):
                    pltpu.sync_copy(dest_hbm.at[pl.ds(k * t + base + b * SC_ROWS, SC_ROWS)], idx_v)
                    pltpu.sync_copy(rows_v.at[slot], xs_hbm.at[idx_v])

        pltpu.sync_copy(zero_hbm, rows_v.at[0])

        @pl.loop(0, pad_w // SC_ROWS)
        def _(b):
            pltpu.sync_copy(pad_hbm.at[pl.ds(wid * pad_w + b * SC_ROWS, SC_ROWS)], idx_v)
            pltpu.sync_copy(rows_v.at[0], xs_hbm.at[idx_v])

    return scatter_rows(h, dest_flat, pad_rows, jnp.zeros((SC_ROWS, d), h.dtype))


def _sc_gather(table, idx):
    n = idx.shape[0]
    d = table.shape[1]
    per_w = n // SC_WORKERS
    nb = per_w // SC_ROWS
    assert per_w % (2 * SC_ROWS) == 0

    @functools.partial(
        pl.kernel, out_type=jax.ShapeDtypeStruct((n, d), table.dtype), mesh=_sc_mesh(),
        scratch_types=[pltpu.VMEM((2, SC_ROWS), jnp.int32), pltpu.VMEM((2, SC_ROWS, d), table.dtype),
                       pltpu.SemaphoreType.DMA((2,))])
    def gather_rows(table_hbm, idx_hbm, out_hbm, idx_v, rows_v, sem):
        base = _sc_worker_id() * per_w

        def gather(slot):
            return pltpu.make_async_copy(table_hbm.at[idx_v.at[slot]], rows_v.at[slot], sem.at[slot])

        def start(b, slot):
            pltpu.sync_copy(idx_hbm.at[pl.ds(base + b * SC_ROWS, SC_ROWS)], idx_v.at[slot])
            gather(slot).start()

        start(0, 0)
        start(1, 1)

        @pl.loop(0, nb, step=2)
        def _(i):
            for slot in range(2):
                b = i + slot
                gather(slot).wait()
                pltpu.sync_copy(rows_v.at[slot], out_hbm.at[pl.ds(base + b * SC_ROWS, SC_ROWS)])

                @pl.when(b + 2 < nb)
                def _():
                    start(b + 2, slot)

    return gather_rows(table, idx)


def _expert_kernel(be_ref, nv_ref, sb_ref, xs_ref, wgu_ref, bgu_ref, wd_ref, bd_ref, y_ref, wgu_bf, wd_bf):
    i = pl.program_id(0)
    used = nv_ref[i] > 0

    @pl.when(used & ((i == 0) | (be_ref[i] != be_ref[jnp.maximum(i - 1, 0)])))
    def _():
        wgu_bf[...] = wgu_ref[...].astype(BF16)
        wd_bf[...] = wd_ref[...].astype(BF16)

    @pl.when(used)
    def _():
        d_ff = wd_ref.shape[0]
        x_hi, x_lo = _unpack_bf16_pairs(xs_ref[...])
        x = jnp.concatenate([x_hi.astype(BF16), x_lo.astype(BF16)], axis=1)
        gu = jnp.dot(x, wgu_bf[...], preferred_element_type=F32) + bgu_ref[...]
        glu = jnp.minimum(gu[:, :d_ff], SWIGLU_LIMIT)
        lin = jnp.clip(gu[:, d_ff:], -SWIGLU_LIMIT, SWIGLU_LIMIT)
        act = glu * (1.0 / (1.0 + jnp.exp(-SWIGLU_ALPHA * glu))) * (lin + 1.0)
        y = jnp.dot(act.astype(BF16), wd_bf[...], preferred_element_type=F32) + bd_ref[...]
        y_ref[...] = _pack_bf16_pairs(y)

    @pl.when(jnp.logical_not(used))
    def _():
        y_ref[...] = jnp.zeros_like(y_ref)


def _experts(xs, blk_expert, n_valid, src_blk, layer, w_gu, b_gu, w_down, b_down):
    n_rows, dw = xs.shape
    _, n_e, d, f2 = w_gu.shape
    n_blocks = n_rows // MOE_ROWS

    def wmap(i, be, nv, sb):
        return (layer, be[i], 0, 0)

    grid_spec = pltpu.PrefetchScalarGridSpec(
        num_scalar_prefetch=3,
        grid=(n_blocks,),
        in_specs=[pl.BlockSpec((MOE_ROWS, dw), lambda i, be, nv, sb: (sb[i], 0)),
                  pl.BlockSpec((None, None, d, f2), wmap),
                  pl.BlockSpec((None, None, 1, f2), wmap),
                  pl.BlockSpec((None, None, f2 // 2, d), wmap),
                  pl.BlockSpec((None, None, 1, d), wmap)],
        out_specs=pl.BlockSpec((MOE_ROWS, dw), lambda i, be, nv, sb: (i, 0)),
        scratch_shapes=[pltpu.VMEM((d, f2), BF16), pltpu.VMEM((f2 // 2, d), BF16)],
    )
    depth = w_gu.shape[0]
    return pl.pallas_call(
        _expert_kernel,
        grid_spec=grid_spec,
        out_shape=jax.ShapeDtypeStruct((n_rows, dw), jnp.int32),
        compiler_params=pltpu.CompilerParams(dimension_semantics=("arbitrary",),
                                             vmem_limit_bytes=EXPERT_VMEM_LIMIT),
        name="moe_experts",
    )(blk_expert, n_valid, src_blk, xs, w_gu, b_gu.reshape(depth, n_e, 1, f2), w_down, b_down.reshape(depth, n_e, 1, d))


def _combine_kernel(y0_ref, y1_ref, y2_ref, y3_ref, gate_ref, x_ref, g2_ref, o_ref):
    gate = gate_ref[...]
    w = y0_ref.shape[1]
    y_hi, y_lo = None, None
    for k, y_ref in enumerate((y0_ref, y1_ref, y2_ref, y3_ref)):
        hi, lo = _unpack_bf16_pairs(y_ref[...])
        gk = gate[:, k:k + 1]
        y_hi = gk * hi if y_hi is None else y_hi + gk * hi
        y_lo = gk * lo if y_lo is None else y_lo + gk * lo
    o_ref[:, :w] = x_ref[:, :w] + g2_ref[:, :w] * y_hi
    o_ref[:, w:] = x_ref[:, w:] + g2_ref[:, w:] * y_lo


def _combine(y4, gate, x, g2, *, seq, tm=512):
    t, d = x.shape
    bpb = seq // tm
    nblk = t // tm
    y_specs = [pl.BlockSpec((tm, d // 2), functools.partial(lambda i, k: (k * nblk + i, 0), k=k))
               for k in range(TOP_K)]
    return pl.pallas_call(
        _combine_kernel,
        grid=(nblk,),
        in_specs=y_specs + [pl.BlockSpec((tm, TOP_K), lambda i: (i, 0)),
                            pl.BlockSpec((tm, d), lambda i: (i, 0)),
                            pl.BlockSpec((None, 1, d), lambda i: (i // bpb, 0, 0))],
        out_specs=pl.BlockSpec((tm, d), lambda i: (i, 0)),
        out_shape=jax.ShapeDtypeStruct((t, d), F32),
        compiler_params=_cparams(("parallel",)),
        name="moe_combine",
    )(y4, y4, y4, y4, gate, x, g2)


def _moe(x, norm, g2, w_router, b_router, layer, w_gu, b_gu, w_down, b_down, *, seq):
    t, d = x.shape
    n_e = w_router.shape[1]
    idx_t, gate_t, rank_t, counts, h = _router(x, norm, w_router, b_router, seq=seq)
    e_ids = jnp.arange(n_e, dtype=jnp.int32)
    counts = counts[:, 0].astype(jnp.int32)
    padded = (counts + MOE_ROWS - 1) // MOE_ROWS * MOE_ROWS
    pend = jnp.cumsum(padded)
    pstart = pend - padded
    n_pad = n_e * MOE_ROWS
    n_rows = t * TOP_K + n_pad
    n_blocks = n_rows // MOE_ROWS
    dest_t = jnp.sum(jnp.where(idx_t[:, :, None] == e_ids, pstart, 0), axis=-1) + rank_t
    dest_flat = dest_t.reshape(TOP_K * t)
    blk_start = jnp.arange(n_blocks, dtype=jnp.int32) * MOE_ROWS
    n_used = pend[-1] // MOE_ROWS
    src_blk = jnp.minimum(jnp.arange(n_blocks, dtype=jnp.int32), n_used - 1)
    blk_expert = jnp.minimum(jnp.sum((src_blk * MOE_ROWS)[:, None] >= pend[None, :], axis=1), n_e - 1)
    blk_expert = blk_expert.astype(jnp.int32)
    own = blk_expert[:, None] == e_ids
    rows_left = jnp.sum(jnp.where(own, pstart + counts, 0), axis=-1) - blk_start
    n_valid = jnp.where(blk_start < pend[-1], jnp.clip(rows_left, 0, MOE_ROWS), 0).astype(jnp.int32)
    pad_len = padded - counts
    pad_end = jnp.cumsum(pad_len)
    j = jnp.arange(n_pad, dtype=jnp.int32)
    pe = jnp.sum(j[:, None] >= pad_end[None, :], axis=1)
    sel = jnp.minimum(pe, n_e - 1)[:, None] == e_ids
    in_expert = jnp.sum(jnp.where(sel, pstart + counts - (pad_end - pad_len), 0), axis=-1) + j
    pad_rows = jnp.where(pe < n_e, in_expert, pend[-1] + j - pad_end[-1]).astype(jnp.int32)

    xs = _sc_dispatch(h, dest_flat, pad_rows)
    yb = _experts(xs, blk_expert, n_valid, src_blk, layer, w_gu, b_gu, w_down, b_down)
    y4 = _sc_gather(yb, dest_flat)
    return _combine(y4, gate_t.T, x, g2, seq=seq)


def _final_norm_kernel(x_ref, g_ref, o_ref):
    o_ref[...] = _rms(x_ref[...]) * g_ref[...]


def _final_norm(x, g, tm=1024):
    t, d = x.shape
    return pl.pallas_call(
        _final_norm_kernel,
        grid=(t // tm,),
        in_specs=[pl.BlockSpec((tm, d), lambda i: (i, 0)), pl.BlockSpec((1, d), lambda i: (0, 0))],
        out_specs=pl.BlockSpec((tm, d), lambda i: (i, 0)),
        out_shape=jax.ShapeDtypeStruct((t, d), F32),
        compiler_params=_cparams(("parallel",)),
        name="final_norm",
    )(x, g.reshape(1, d))


def _dup_heads(w, n_heads):
    d = w.shape[0]
    return jnp.repeat(w.reshape(d, n_heads, 1, HEAD_DIM), 2, axis=2).reshape(d, n_heads * LANES)


def _mixer_mla_swa(x, norm, g1, cos, sin, w_in, g_q, w_qb, g_kv, w_kvb, sink, w_out, *, bsz, seq):
    d = x.shape[1]
    o = np.cumsum((0, MLA_Q_LORA, MLA_KV_LORA, MLA_ROPE, SWA_Q_HEADS * HEAD_DIM,
                   SWA_KV_HEADS * HEAD_DIM, SWA_KV_HEADS * HEAD_DIM))
    w_cq, w_ckv, w_kr, w_qs, w_ks, w_vs = (w_in[:, o[i]:o[i + 1]] for i in range(6))
    w_lat = jnp.concatenate([w_cq, jnp.zeros((d, LANES), F32), w_ckv], axis=1).astype(BF16)
    lat = _proj([x], [w_lat], prologue="ada", norm=norm, seq=seq, tm=512, tn=w_lat.shape[1],
                out_dtype=F32, name="proj_latent")
    w_b = jnp.concatenate([_pair_layout(w_qs), _pair_layout(_dup_heads(w_ks, SWA_KV_HEADS)),
                           _dup_heads(w_vs, SWA_KV_HEADS),
                           w_kr, jnp.zeros((d, LANES - MLA_ROPE), F32)], axis=1).astype(BF16)
    qkv_b = _proj([x], [w_b], prologue="ada", norm=norm, seq=seq, tm=512, tn=w_b.shape[1], out_dtype=BF16,
                  rope=(cos, sin), rope_pattern=(ROPE_PAIR,) * 10 + (0, 0) + (ROPE_ROTATE,), name="proj_swa")
    kr_col = (w_b.shape[1] - LANES) // LANES
    qd = MLA_NOPE + MLA_ROPE
    w_q = w_qb.reshape(MLA_Q_LORA, MLA_HEADS, qd)
    w_q = jnp.concatenate([w_q, jnp.zeros((MLA_Q_LORA, MLA_HEADS, 2 * LANES - qd), F32)], axis=2)
    w_q = w_q.reshape(MLA_Q_LORA, MLA_HEADS * 2 * LANES).astype(BF16)
    q_cat = _proj([lat], [w_q], prologue="rms", norm=(g_q.reshape(1, -1),), lhs_cols=(MLA_Q_LORA, 0),
                  seq=seq, tm=512, tn=w_q.shape[1], out_dtype=BF16, rope=(cos, sin),
                  rope_pattern=(0, 1) * MLA_HEADS, out_scale=qd ** -0.5, name="proj_mla_q")
    w_kv = w_kvb.reshape(MLA_KV_LORA, MLA_HEADS, MLA_NOPE + MLA_V)
    w_kv = jnp.concatenate([w_kv[:, :, :MLA_NOPE].reshape(MLA_KV_LORA, -1),
                            w_kv[:, :, MLA_NOPE:].reshape(MLA_KV_LORA, -1)], axis=1).astype(BF16)
    kv = _proj([lat], [w_kv], prologue="rms", norm=(g_kv.reshape(1, -1),), lhs_cols=(MLA_KV_LORA, 2),
               seq=seq, tm=512, tn=w_kv.shape[1], out_dtype=BF16, name="proj_mla_kv")
    o_a = _mla_attention(q_cat, kv, qkv_b, kr_col, bsz=bsz, seq=seq, tq=256)
    qkv_b4 = qkv_b.reshape(bsz, 1, seq, qkv_b.shape[1])
    (o_b,) = _banded_attention(qkv_b4, qkv_b4, qkv_b4, bsz=bsz, seq=seq, dil=1, radius=SWA_RADIUS,
                               q_col=0, k_col=4, v_col=5, kv_width=2 * LANES,
                               pairs_per_kv=SWA_Q_HEADS // SWA_KV_HEADS // 2, sink=sink, want_lse=False,
                               name="swa_attention")
    o_b = o_b.reshape(bsz * seq, SWA_Q_HEADS * HEAD_DIM)
    na = MLA_HEADS * MLA_V
    return _proj([o_a, o_b], [w_out[:na].astype(BF16), w_out[na:].astype(BF16)], prologue="plain",
                 seq=seq, tm=512, tn=d, out_dtype=F32, residual=(x, g1), name="proj_out_ab")


def _mixer_dilated(x, norm, g1, cos, sin, w_in, w_out, *, bsz, seq):
    d = x.shape[1]
    qw = DIL_HEADS * HEAD_DIM
    w_bf = w_in.astype(BF16)
    outs, lses = [], []
    for g, (window, dil) in enumerate(DIL_PATTERN):
        w_g = w_bf[:, g * 3 * qw:(g + 1) * 3 * qw]
        w_g = jnp.concatenate([_pair_layout(w_g[:, :2 * qw]), w_g[:, 2 * qw:]], axis=1)
        qkv = _proj([x], [w_g], prologue="ada", norm=norm, seq=seq, tm=512, tn=3 * qw, sub_tn=qw, out_dtype=BF16,
                    rope=(cos, sin), rope_pattern=(ROPE_PAIR,) * (2 * qw // LANES) + (0,) * (qw // LANES),
                    out_dil=dil, name=f"proj_dil_{dil}")
        qkv = qkv.reshape(bsz, dil, seq // dil, 3 * qw)
        o, lse = _banded_attention(qkv, qkv, qkv, bsz=bsz, seq=seq, dil=dil, radius=window // (2 * dil),
                                   q_col=0, k_col=1, v_col=2, kv_width=qw, pairs_per_kv=1,
                                   want_lse=True, name=f"dilated_attention_{dil}")
        outs.append(o)
        lses.append(lse)
    return _proj(outs + lses, [w_out.astype(BF16)], prologue="dil", seq=seq, tm=512, tn=d, out_dtype=F32,
                 residual=(x, g1), in_dils=tuple(dil for _, dil in DIL_PATTERN), name="proj_out_c")


def kernel(x, c, positions, w_mod, b_mod, g_norm_mix, g_norm_ffn, w_in_ab, mla_g_q, mla_w_qb, mla_g_kv,
           mla_w_kvb, swa_sink, w_out_ab, w_in_c, w_out_c, w_router, b_router, w_gu, b_gu, w_down, b_down,
           g_final):
    bsz, seq, d = x.shape
    depth = w_mod.shape[0]
    cos, sin = _rope_tables(positions)
    mod = _modulation(c, w_mod, b_mod)
    xt = x.reshape(bsz * seq, d)
    for layer in range(depth):
        sh1, sc1, g1, sh2, sc2, g2 = (mod[layer, :, i * d:(i + 1) * d].reshape(bsz, 1, d) for i in range(6))
        li = layer // 2
        norm = (g_norm_mix[layer].reshape(1, d), sc1, sh1)
        if layer % 2 == 0:
            xt = _mixer_mla_swa(xt, norm, g1, cos, sin, w_in_ab[li], mla_g_q[li], mla_w_qb[li], mla_g_kv[li],
                                mla_w_kvb[li], swa_sink[li], w_out_ab[li], bsz=bsz, seq=seq)
        else:
            xt = _mixer_dilated(xt, norm, g1, cos, sin, w_in_c[li], w_out_c[li], bsz=bsz, seq=seq)
        norm = (g_norm_ffn[layer].reshape(1, d), sc2, sh2)
        xt = _moe(xt, norm, g2, w_router[layer], b_router[layer], layer, w_gu, b_gu, w_down, b_down, seq=seq)
    return _final_norm(xt, g_final).reshape(bsz, seq, d)
```

```python
import functools

import jax
import jax.numpy as jnp
import numpy as np
from jax import lax
from jax.experimental import pallas as pl
from jax.experimental.pallas import tpu as pltpu
from jax.experimental.pallas import tpu_sc as plsc

F32 = jnp.float32
BF16 = jnp.bfloat16

EPS = 1e-6
ROPE_THETA = 10000.0
LANES = 128
HEAD_DIM = 64
ROPE_HALF = HEAD_DIM // 2

MLA_HEADS = 8
MLA_Q_LORA = 384
MLA_KV_LORA = 256
MLA_NOPE = 128
MLA_ROPE = 64
MLA_V = 128
SWA_Q_HEADS = 16
SWA_KV_HEADS = 2
SWA_RADIUS = 128
DIL_PATTERN = ((128, 1), (512, 4), (2048, 16))
DIL_HEADS = 16
N_EXPERTS = 32
TOP_K = 4
SWIGLU_LIMIT = 7.0
SWIGLU_ALPHA = 1.702
MOE_ROWS = 512
BAND_SUB_ROWS = 128

VMEM_LIMIT = 48 * 1024 * 1024
EXPERT_VMEM_LIMIT = 56 * 1024 * 1024


def _cparams(sem):
    return pltpu.CompilerParams(dimension_semantics=sem, vmem_limit_bytes=VMEM_LIMIT)


def _mod_kernel(c_ref, w_ref, b_ref, o_ref):
    c = c_ref[...]
    ca = c * (1.0 / (1.0 + jnp.exp(-c)))
    o_ref[...] = jnp.dot(ca, w_ref[...], preferred_element_type=F32,
                         precision=lax.Precision.HIGHEST) + b_ref[...]


def _modulation(c, w_mod, b_mod):
    depth, d, n = w_mod.shape
    bsz = c.shape[0]
    tn = 1536
    return pl.pallas_call(
        _mod_kernel,
        grid=(depth, n // tn),
        in_specs=[pl.BlockSpec((bsz, d), lambda l, j: (0, 0)),
                  pl.BlockSpec((None, d, tn), lambda l, j: (l, 0, j)),
                  pl.BlockSpec((None, 1, tn), lambda l, j: (l, 0, j))],
        out_specs=pl.BlockSpec((None, bsz, tn), lambda l, j: (l, 0, j)),
        out_shape=jax.ShapeDtypeStruct((depth, bsz, n), F32),
        compiler_params=_cparams(("parallel", "parallel")),
        name="modulation",
    )(c, w_mod, b_mod.reshape(depth, 1, n))


def _rope_table_kernel(pos_ref, inv_ref, sign_ref, cos_ref, sin_ref):
    ang = inv_ref[...] * pos_ref[...].astype(F32)
    cos_ref[...] = jnp.transpose(jnp.cos(ang))
    sin_ref[...] = jnp.transpose(jnp.sin(ang) * sign_ref[...])


def _rope_tables(positions):
    bsz, s = positions.shape
    inv = ROPE_THETA ** (-jnp.arange(0, HEAD_DIM, 2, dtype=F32) / HEAD_DIM)
    inv128 = jnp.tile(inv, LANES // ROPE_HALF).reshape(LANES, 1)
    sign = jnp.where((jnp.arange(LANES) % HEAD_DIM) < ROPE_HALF, -1.0, 1.0).astype(F32).reshape(LANES, 1)
    out = jax.ShapeDtypeStruct((bsz * s, LANES), F32)
    return pl.pallas_call(
        _rope_table_kernel,
        grid=(bsz,),
        in_specs=[pl.BlockSpec((None, 1, s), lambda b: (b, 0, 0)),
                  pl.BlockSpec((LANES, 1), lambda b: (0, 0)),
                  pl.BlockSpec((LANES, 1), lambda b: (0, 0))],
        out_specs=[pl.BlockSpec((s, LANES), lambda b: (b, 0)),
                   pl.BlockSpec((s, LANES), lambda b: (b, 0))],
        out_shape=[out, out],
        compiler_params=_cparams(("parallel",)),
        name="rope_tables",
    )(positions.reshape(bsz, 1, s), inv128, sign)


def _rms(x):
    return x * lax.rsqrt(jnp.mean(x * x, axis=-1, keepdims=True) + EPS)


def _ada_norm(x, g, sc, sh):
    return (_rms(x) * g) * (1.0 + sc) + sh


def _pack_bf16_pairs(x):
    w = x.shape[1] // 2
    hi = pltpu.bitcast(x[:, :w].astype(BF16).astype(F32), jnp.int32)
    lo = pltpu.bitcast(x[:, w:].astype(BF16).astype(F32), jnp.int32)
    return hi | lax.shift_right_logical(lo, 16)


def _unpack_bf16_pairs(p):
    hi = pltpu.bitcast(p & jnp.int32(-65536), F32)
    lo = pltpu.bitcast(lax.shift_left(p, 16), F32)
    return hi, lo


ROPE_ROTATE, ROPE_PAIR = 1, 2


def _pair_layout(w):
    k, n = w.shape
    w = w.reshape(k, n // LANES, 2, 2, ROPE_HALF)
    return jnp.swapaxes(w, 2, 3).reshape(k, n)


def _rope_group(a, cos, sin, first_half):
    rot = jnp.where(first_half, pltpu.roll(a, LANES - ROPE_HALF, 1), pltpu.roll(a, ROPE_HALF, 1))
    return a * cos + rot * sin


def _proj_kernel(*refs, prologue, n_lhs, rope_pattern, sub_tn, out_scale, residual, in_dils, out_dil):
    pos = 0
    if prologue == "ada":
        x_ref, g_ref, sc_ref, sh_ref = refs[:4]
        pos = 4
    elif prologue == "rms":
        x_ref, g_ref = refs[:2]
        pos = 2
    elif prologue == "dil":
        o_refs = refs[0:3]
        l_refs = refs[3:6]
        pos = 6
    else:
        x_refs = refs[:n_lhs]
        pos = n_lhs
    w_refs = refs[pos:pos + n_lhs]
    pos += n_lhs
    if rope_pattern is not None:
        cos_ref, sin_ref = refs[pos:pos + 2]
        pos += 2
    if residual:
        res_ref, gate_ref = refs[pos:pos + 2]
        pos += 2
    o_ref = refs[pos]
    pos += 1
    h_ref = refs[pos] if prologue != "plain" else None
    slab_ref = refs[pos + 1] if (prologue == "dil" or out_dil > 1) else None

    if prologue != "plain":
        @pl.when(pl.program_id(1) == 0)
        def _():
            if prologue == "ada":
                h_ref[...] = _ada_norm(x_ref[...], g_ref[...], sc_ref[...], sh_ref[...]).astype(BF16)
            elif prologue == "rms":
                h_ref[...] = (_rms(x_ref[...]) * g_ref[...]).astype(BF16)
            else:
                tm = h_ref.shape[0]
                for c in range(h_ref.shape[1] // LANES):
                    cols = slice(c * LANES, (c + 1) * LANES)
                    vals = []
                    for i, (ref, dil) in enumerate(zip(o_refs + l_refs, in_dils + in_dils)):
                        if dil == 1:
                            vals.append(ref[0, :, cols].astype(F32))
                        else:
                            for r in range(dil):
                                slab_ref[i, pl.ds(r, tm // dil, stride=dil), :] = ref[r, :, cols].astype(F32)
                            vals.append(slab_ref[i])
                    o0, o1, o2, l0, l1, l2 = vals
                    m = jnp.maximum(jnp.maximum(l0, l1), l2)
                    e0, e1, e2 = jnp.exp(l0 - m), jnp.exp(l1 - m), jnp.exp(l2 - m)
                    h_ref[:, cols] = ((e0 * o0 + e1 * o1 + e2 * o2) / (e0 + e1 + e2)).astype(BF16)

    n_total = w_refs[0].shape[1]
    tn = n_total if sub_tn is None else sub_tn

    def matmul(t):
        cols = slice(t * tn, (t + 1) * tn)
        if prologue != "plain":
            acc = jnp.dot(h_ref[...], w_refs[0][:, cols], preferred_element_type=F32)
        else:
            acc = jnp.dot(x_refs[0][...], w_refs[0][:, cols], preferred_element_type=F32)
            for xr, wr in zip(x_refs[1:], w_refs[1:]):
                acc = acc + jnp.dot(xr[...], wr[:, cols], preferred_element_type=F32)
        return acc * out_scale if out_scale != 1.0 else acc

    if residual:
        o_ref[...] = (res_ref[...] + gate_ref[...] * matmul(0)).astype(o_ref.dtype)
        return

    pattern = rope_pattern if rope_pattern is not None else (0,) * (n_total // LANES)
    if any(pattern):
        cos = cos_ref[...]
        sin = sin_ref[...]
        lane = lax.broadcasted_iota(jnp.int32, cos.shape, 1)
        first_half = (lane % HEAD_DIM) < ROPE_HALF
        if ROPE_PAIR in pattern:
            sin_pair = jnp.where(lane < HEAD_DIM, -1.0, 1.0) * jnp.where(first_half, -sin, sin)

    def epilogue(acc, t):
        tm = acc.shape[0]
        for c in range(tn // LANES):
            g = t * (tn // LANES) + c
            cols = slice(g * LANES, (g + 1) * LANES)
            a = acc[:, c * LANES:(c + 1) * LANES]
            if pattern[g] == ROPE_ROTATE:
                a = _rope_group(a, cos, sin, first_half)
            elif pattern[g] == ROPE_PAIR:
                a = a * cos + pltpu.roll(a, HEAD_DIM, 1) * sin_pair
            if out_dil == 1:
                o_ref[:, cols] = a.astype(o_ref.dtype)
            else:
                slab_ref[0] = a
                for r in range(out_dil):
                    o_ref[r, :, cols] = slab_ref[0, pl.ds(r, tm // out_dil, stride=out_dil), :].astype(o_ref.dtype)

    acc_next = matmul(0)
    for t in range(n_total // tn):
        acc = acc_next
        if t + 1 < n_total // tn:
            acc_next = matmul(t + 1)
        epilogue(acc, t)


def _proj(lhs, ws, *, prologue, seq, tm, tn, out_dtype, norm=None, lhs_cols=None, rope=None,
          rope_pattern=None, sub_tn=None, out_scale=1.0, residual=None, in_dils=None, out_dil=1, name):
    k, n = ws[0].shape
    bsz_seq = lhs[0].shape[0] if prologue != "dil" else lhs[0].shape[0] * seq
    t = bsz_seq
    assert t % tm == 0 and n % tn == 0 and seq % tm == 0
    bpb = seq // tm
    n_lhs = len(ws)
    args, specs = [], []
    row = lambda i, j: (i, 0)
    if prologue in ("ada", "rms"):
        kw, kc = lhs_cols if lhs_cols is not None else (k, 0)
        args.append(lhs[0])
        specs.append(pl.BlockSpec((tm, kw), lambda i, j: (i, kc)))
        args.append(norm[0])
        specs.append(pl.BlockSpec((1, k), lambda i, j: (0, 0)))
        if prologue == "ada":
            for v in norm[1:3]:
                args.append(v)
                specs.append(pl.BlockSpec((None, 1, k), lambda i, j: (i // bpb, 0, 0)))
    elif prologue == "dil":
        for a, dil in zip(lhs, in_dils + in_dils):
            args.append(a)
            specs.append(pl.BlockSpec((None, dil, tm // dil, k), lambda i, j: (i // bpb, 0, i % bpb, 0)))
    else:
        for a, w in zip(lhs, ws):
            args.append(a)
            specs.append(pl.BlockSpec((tm, w.shape[0]), row))
    for w in ws:
        args.append(w)
        specs.append(pl.BlockSpec((w.shape[0], tn), lambda i, j: (0, j)))
    if rope_pattern is not None:
        assert len(rope_pattern) == tn // LANES
        for tab in rope:
            args.append(tab)
            specs.append(pl.BlockSpec((tm, LANES), row))
    if residual is not None:
        res, gate = residual
        args.append(res)
        specs.append(pl.BlockSpec((tm, tn), lambda i, j: (i, j)))
        args.append(gate)
        specs.append(pl.BlockSpec((None, 1, tn), lambda i, j: (i // bpb, 0, j)))
    scratch = [] if prologue == "plain" else [pltpu.VMEM((tm, k), BF16)]
    if prologue == "dil":
        scratch.append(pltpu.VMEM((2 * len(in_dils), tm, LANES), F32))
    elif out_dil > 1:
        scratch.append(pltpu.VMEM((1, tm, LANES), F32))
    if out_dil == 1:
        out_spec = pl.BlockSpec((tm, tn), lambda i, j: (i, j))
        out_shape = jax.ShapeDtypeStruct((t, n), out_dtype)
    else:
        out_spec = pl.BlockSpec((None, out_dil, tm // out_dil, tn), lambda i, j: (i // bpb, 0, i % bpb, j))
        out_shape = jax.ShapeDtypeStruct((t // seq, out_dil, seq // out_dil, n), out_dtype)
    kern = functools.partial(_proj_kernel, prologue=prologue, n_lhs=n_lhs, rope_pattern=rope_pattern,
                             sub_tn=sub_tn, out_scale=out_scale, residual=residual is not None,
                             in_dils=in_dils, out_dil=out_dil)
    return pl.pallas_call(
        kern,
        grid=(t // tm, n // tn),
        in_specs=specs,
        out_specs=out_spec,
        out_shape=out_shape,
        scratch_shapes=scratch,
        compiler_params=_cparams(("parallel", "arbitrary")),
        name=name,
    )(*args)


MLA_HEADS_PER_STEP = 4


def _mla_scores(q_ref, kn_ref, kr, h):
    k = jnp.concatenate([kn_ref[:, h * LANES:(h + 1) * LANES], kr], axis=1)
    q = q_ref[:, h * 2 * LANES:(h + 1) * 2 * LANES]
    return lax.dot_general(q, k, (((1,), (1,)), ((), ())), preferred_element_type=F32)


def _mla_kernel(q_ref, kn_ref, kr_ref, v_ref, o_ref):
    kr = kr_ref[...]
    s_next = _mla_scores(q_ref, kn_ref, kr, 0)
    for h in range(MLA_HEADS_PER_STEP):
        s = s_next
        if h + 1 < MLA_HEADS_PER_STEP:
            s_next = _mla_scores(q_ref, kn_ref, kr, h + 1)
        m = jnp.max(s, axis=1, keepdims=True)
        p = jnp.exp(s - m).astype(BF16)
        v = v_ref[:, h * LANES:(h + 1) * LANES]
        oe = jnp.dot(p, jnp.concatenate([v, jnp.ones_like(v)], axis=1), preferred_element_type=F32)
        o_ref[:, h * LANES:(h + 1) * LANES] = (oe[:, :LANES] / oe[:, LANES:]).astype(o_ref.dtype)


def _mla_attention(q_cat, kv, qkv_b, kr_col, *, bsz, seq, tq):
    t = bsz * seq
    nq = seq // tq
    hps = MLA_HEADS_PER_STEP
    groups = MLA_HEADS // hps
    return pl.pallas_call(
        _mla_kernel,
        grid=(bsz, groups, nq),
        in_specs=[pl.BlockSpec((tq, hps * 2 * LANES), lambda b, h, i: (b * nq + i, h)),
                  pl.BlockSpec((seq, hps * LANES), lambda b, h, i: (b, h)),
                  pl.BlockSpec((seq, LANES), lambda b, h, i: (b, kr_col)),
                  pl.BlockSpec((seq, hps * LANES), lambda b, h, i: (b, groups + h))],
        out_specs=pl.BlockSpec((tq, hps * LANES), lambda b, h, i: (b * nq + i, h)),
        out_shape=jax.ShapeDtypeStruct((t, MLA_HEADS * MLA_V), BF16),
        compiler_params=_cparams(("parallel", "parallel", "parallel")),
        name="mla_attention",
    )(q_cat, kv, qkv_b, kv)


def _banded_kernel(*refs, radius, tq, hb, length, n_pairs, pairs_per_kv, has_sink, want_lse):
    pos = 0
    if has_sink:
        sink_ref = refs[0]
        pos = 1
    q_ref, kp_ref, kc_ref, kn_ref, vp_ref, vc_ref, vn_ref = refs[pos:pos + 7]
    pos += 7
    o_ref = refs[pos]
    lse_ref = refs[pos + 1] if want_lse else None

    j = pl.program_id(2)
    kw = jnp.concatenate([kp_ref[...], kc_ref[...], kn_ref[...]], axis=0)
    vw = jnp.concatenate([vp_ref[...], vc_ref[...], vn_ref[...]], axis=0)
    scale = HEAD_DIM ** -0.5
    qn = min(tq, BAND_SUB_ROWS)
    kn = qn + 2 * hb
    lane = lax.broadcasted_iota(jnp.int32, (qn, LANES), 1)
    low = lane < HEAD_DIM
    first = (lane % HEAD_DIM) < ROPE_HALF

    for q0 in range(0, tq, qn):
        qpos = j * tq + q0 + lax.broadcasted_iota(jnp.int32, (qn, kn), 0)
        kpos = j * tq - hb + q0 + lax.broadcasted_iota(jnp.int32, (qn, kn), 1)
        valid = (jnp.abs(qpos - kpos) <= radius) & (kpos >= 0) & (kpos < length)
        for p in range(n_pairs):
            c = p // pairs_per_kv
            qp = q_ref[q0:q0 + qn, p * LANES:(p + 1) * LANES] * scale
            kp = kw[q0:q0 + kn, c * LANES:(c + 1) * LANES]
            vp = vw[q0:q0 + kn, c * LANES:(c + 1) * LANES]
            vp = jnp.concatenate([vp, jnp.ones_like(vp)], axis=1)
            halves = []
            for half in range(2):
                qh = jnp.where(first if half == 0 else ~first, qp, jnp.zeros_like(qp))
                s = lax.dot_general(qh, kp, (((1,), (1,)), ((), ())), preferred_element_type=F32)
                s = jnp.where(valid, s, -jnp.inf)
                m = jnp.max(s, axis=1, keepdims=True)
                if has_sink:
                    sk = sink_ref[2 * p + half]
                    m = jnp.maximum(m, sk)
                oe = jnp.dot(jnp.exp(s - m).astype(BF16), vp, preferred_element_type=F32)
                den = oe[:, LANES:]
                if has_sink:
                    den = den + jnp.exp(sk - m)
                halves.append((oe[:, :LANES] / den, m + jnp.log(den) if want_lse else None))
            o_ref[q0:q0 + qn, p * LANES:(p + 1) * LANES] = jnp.where(
                low, halves[0][0], halves[1][0]).astype(o_ref.dtype)
            if want_lse:
                lse_ref[q0:q0 + qn, p * LANES:(p + 1) * LANES] = jnp.where(low, halves[0][1], halves[1][1])


def _banded_attention(q_arr, k_arr, v_arr, *, bsz, seq, dil, radius, q_col, k_col, v_col, kv_width,
                      pairs_per_kv, sink=None, want_lse, name):
    length = seq // dil
    hb = radius
    tq = min(256, length)
    assert tq % hb == 0 and length % tq == 0
    qw = DIL_HEADS * HEAD_DIM
    n_pairs = qw // LANES
    rep = tq // hb
    last_hb = length // hb - 1

    def prev_spec(col):
        return pl.BlockSpec((None, None, hb, kv_width),
                            lambda b, r, j: (b, r, jnp.maximum(j * rep - 1, 0), col))

    def cur_spec(col):
        return pl.BlockSpec((None, None, tq, kv_width), lambda b, r, j: (b, r, j, col))

    def next_spec(col):
        return pl.BlockSpec((None, None, hb, kv_width),
                            lambda b, r, j: (b, r, jnp.minimum((j + 1) * rep, last_hb), col))

    in_specs = [pl.BlockSpec((None, None, tq, qw), lambda b, r, j: (b, r, j, q_col)),
                prev_spec(k_col), cur_spec(k_col), next_spec(k_col),
                prev_spec(v_col), cur_spec(v_col), next_spec(v_col)]
    args = [q_arr, k_arr, k_arr, k_arr, v_arr, v_arr, v_arr]
    if sink is not None:
        in_specs = [pl.BlockSpec(memory_space=pltpu.SMEM)] + in_specs
        args = [sink] + args
    out_spec = pl.BlockSpec((None, None, tq, qw), lambda b, r, j: (b, r, j, 0))
    out_specs = [out_spec]
    out_shape = [jax.ShapeDtypeStruct((bsz, dil, length, qw), BF16)]
    if want_lse:
        out_specs.append(out_spec)
        out_shape.append(jax.ShapeDtypeStruct((bsz, dil, length, qw), F32))
    kern = functools.partial(_banded_kernel, radius=radius, tq=tq, hb=hb, length=length, n_pairs=n_pairs,
                             pairs_per_kv=pairs_per_kv, has_sink=sink is not None, want_lse=want_lse)
    return pl.pallas_call(
        kern,
        grid=(bsz, dil, length // tq),
        in_specs=in_specs,
        out_specs=out_specs,
        out_shape=out_shape,
        compiler_params=_cparams(("parallel", "parallel", "parallel")),
        name=name,
    )(*args)


def _router_kernel(x_ref, g_ref, sc_ref, sh_ref, wt_ref, b_ref, tri_ref,
                   idx_ref, gate_ref, rank_ref, cnt_ref, h_ref, carry_ref):
    @pl.when(pl.program_id(0) == 0)
    def _():
        carry_ref[...] = jnp.zeros_like(carry_ref)

    h = _ada_norm(x_ref[...], g_ref[...], sc_ref[...], sh_ref[...])
    h_ref[...] = _pack_bf16_pairs(h)
    lt = lax.dot_general(wt_ref[...], h, (((1,), (1,)), ((), ())), preferred_element_type=F32,
                         precision=lax.Precision.HIGHEST) + b_ref[...]
    n_e, tm = lt.shape
    e_iota = lax.broadcasted_iota(jnp.int32, (n_e, tm), 0)
    vals, sels = [], []
    for k in range(TOP_K):
        m = jnp.max(lt, axis=0, keepdims=True)
        idx = jnp.min(jnp.where(lt == m, e_iota, n_e), axis=0, keepdims=True)
        sel = e_iota == idx
        idx_ref[k:k + 1, :] = idx
        vals.append(m)
        sels.append(sel)
        lt = jnp.where(sel, -jnp.inf, lt)
    exps = [jnp.exp(v - vals[0]) for v in vals]
    den = exps[0] + exps[1] + exps[2] + exps[3]
    for k in range(TOP_K):
        gate_ref[k:k + 1, :] = exps[k] / den

    onehot = jnp.where(sels[0] | sels[1] | sels[2] | sels[3], 1.0, 0.0)
    carry = carry_ref[:, 0:1]
    before = jnp.dot(onehot.astype(BF16), tri_ref[...], preferred_element_type=F32) + carry
    for k in range(TOP_K):
        rank_ref[k:k + 1, :] = jnp.sum(jnp.where(sels[k], before, 0.0), axis=0, keepdims=True).astype(jnp.int32)
    total = carry + jnp.sum(onehot, axis=1, keepdims=True)
    carry_ref[...] = jnp.broadcast_to(total, carry_ref.shape)
    cnt_ref[...] = jnp.broadcast_to(total, cnt_ref.shape)


def _router(x, norm, w_router, b_router, *, seq, tok0, t, tm=512):
    d = x.shape[1]
    n_e = w_router.shape[1]
    bpb = seq // tm
    off = tok0 // tm
    tri = (jnp.arange(tm)[:, None] < jnp.arange(tm)[None, :]).astype(BF16)
    g, sc, sh = norm
    out_i = jax.ShapeDtypeStruct((TOP_K, t), jnp.int32)
    tok = pl.BlockSpec((TOP_K, tm), lambda i: (0, i))
    return pl.pallas_call(
        _router_kernel,
        grid=(t // tm,),
        in_specs=[pl.BlockSpec((tm, d), lambda i: (i + off, 0)),
                  pl.BlockSpec((1, d), lambda i: (0, 0)),
                  pl.BlockSpec((None, 1, d), lambda i: ((i + off) // bpb, 0, 0)),
                  pl.BlockSpec((None, 1, d), lambda i: ((i + off) // bpb, 0, 0)),
                  pl.BlockSpec((n_e, d), lambda i: (0, 0)),
                  pl.BlockSpec((n_e, 1), lambda i: (0, 0)),
                  pl.BlockSpec((tm, tm), lambda i: (0, 0))],
        out_specs=[tok, tok, tok, pl.BlockSpec((n_e, LANES), lambda i: (0, 0)),
                   pl.BlockSpec((tm, d // 2), lambda i: (i, 0))],
        out_shape=[out_i, jax.ShapeDtypeStruct((TOP_K, t), F32), out_i,
                   jax.ShapeDtypeStruct((n_e, LANES), F32), jax.ShapeDtypeStruct((t, d // 2), jnp.int32)],
        scratch_shapes=[pltpu.VMEM((n_e, LANES), F32)],
        compiler_params=_cparams(("arbitrary",)),
        name="router",
    )(x, g, sc, sh, w_router.T, b_router.reshape(n_e, 1), tri)


SC_CORES = 2
SC_SUBCORES = 16
SC_WORKERS = SC_CORES * SC_SUBCORES
SC_ROWS = 64


def _sc_worker_id():
    return lax.axis_index("subcore") * SC_CORES + lax.axis_index("core")


def _sc_mesh():
    return plsc.VectorSubcoreMesh(core_axis_name="core", subcore_axis_name="subcore")


def _sc_dispatch(h, dest_flat, pad_rows):
    t, d = h.shape
    n_pad = pad_rows.shape[0]
    n_rows = TOP_K * t + n_pad
    tok_w = t // SC_WORKERS
    pad_w = n_pad // SC_WORKERS
    nb = tok_w // SC_ROWS
    assert tok_w % (2 * SC_ROWS) == 0 and pad_w % SC_ROWS == 0

    @functools.partial(
        pl.kernel, out_type=jax.ShapeDtypeStruct((n_rows, d), h.dtype), mesh=_sc_mesh(),
        scratch_types=[pltpu.VMEM((SC_ROWS,), jnp.int32), pltpu.VMEM((2, SC_ROWS, d), h.dtype),
                       pltpu.SemaphoreType.DMA((2,))])
    def scatter_rows(h_hbm, dest_hbm, pad_hbm, zero_hbm, xs_hbm, idx_v, rows_v, sem):
        wid = _sc_worker_id()
        base = wid * tok_w

        def load(b, slot):
            return pltpu.make_async_copy(h_hbm.at[pl.ds(base + b * SC_ROWS, SC_ROWS)], rows_v.at[slot],
                                         sem.at[slot])

        load(0, 0).start()

        @pl.loop(0, nb, step=2)
        def _(i):
            for slot in range(2):
                b = i + slot
                load(b, slot).wait()

                @pl.when(b + 1 < nb)
                def _():
                    load(b + 1, 1 - slot).start()

                for k in range(TOP_K):
                    pltpu.sync_copy(dest_hbm.at[pl.ds(k * t + base + b * SC_ROWS, SC_ROWS)], idx_v)
                    pltpu.sync_copy(rows_v.at[slot], xs_hbm.at[idx_v])

        pltpu.sync_copy(zero_hbm, rows_v.at[0])

        @pl.loop(0, pad_w // SC_ROWS)
        def _(b):
            pltpu.sync_copy(pad_hbm.at[pl.ds(wid * pad_w + b * SC_ROWS, SC_ROWS)], idx_v)
            pltpu.sync_copy(rows_v.at[0], xs_hbm.at[idx_v])

    return scatter_rows(h, dest_flat, pad_rows, jnp.zeros((SC_ROWS, d), h.dtype))


def _sc_gather(table, idx):
    n = idx.shape[0]
    d = table.shape[1]
    per_w = n // SC_WORKERS
    nb = per_w // SC_ROWS
    assert per_w % (2 * SC_ROWS) == 0

    @functools.partial(
        pl.kernel, out_type=jax.ShapeDtypeStruct((n, d), table.dtype), mesh=_sc_mesh(),
        scratch_types=[pltpu.VMEM((2, SC_ROWS), jnp.int32), pltpu.VMEM((2, SC_ROWS, d), table.dtype),
                       pltpu.SemaphoreType.DMA((2,))])
    def gather_rows(table_hbm, idx_hbm, out_hbm, idx_v, rows_v, sem):
        base = _sc_worker_id() * per_w

        def gather(slot):
            return pltpu.make_async_copy(table_hbm.at[idx_v.at[slot]], rows_v.at[slot], sem.at[slot])

        def start(b, slot):
            pltpu.sync_copy(idx_hbm.at[pl.ds(base + b * SC_ROWS, SC_ROWS)], idx_v.at[slot])
            gather(slot).start()

        start(0, 0)
        start(1, 1)

        @pl.loop(0, nb, step=2)
        def _(i):
            for slot in range(2):
                b = i + slot
                gather(slot).wait()
                pltpu.sync_copy(rows_v.at[slot], out_hbm.at[pl.ds(base + b * SC_ROWS, SC_ROWS)])

                @pl.when(b + 2 < nb)
                def _():
                    start(b + 2, slot)

    return gather_rows(table, idx)


def _expert_kernel(be_ref, nv_ref, sb_ref, xs_ref, wgu_ref, bgu_ref, wd_ref, bd_ref, y_ref, wgu_bf, wd_bf):
    i = pl.program_id(0)
    used = nv_ref[i] > 0

    @pl.when(used & ((i == 0) | (be_ref[i] != be_ref[jnp.maximum(i - 1, 0)])))
    def _():
        wgu_bf[...] = wgu_ref[...].astype(BF16)
        wd_bf[...] = wd_ref[...].astype(BF16)

    @pl.when(used)
    def _():
        d_ff = wd_ref.shape[0]
        x_hi, x_lo = _unpack_bf16_pairs(xs_ref[...])
        x = jnp.concatenate([x_hi.astype(BF16), x_lo.astype(BF16)], axis=1)
        gu = jnp.dot(x, wgu_bf[...], preferred_element_type=F32) + bgu_ref[...]
        glu = jnp.minimum(gu[:, :d_ff], SWIGLU_LIMIT)
        lin = jnp.clip(gu[:, d_ff:], -SWIGLU_LIMIT, SWIGLU_LIMIT)
        act = glu * (1.0 / (1.0 + jnp.exp(-SWIGLU_ALPHA * glu))) * (lin + 1.0)
        y = jnp.dot(act.astype(BF16), wd_bf[...], preferred_element_type=F32) + bd_ref[...]
        y_ref[...] = _pack_bf16_pairs(y)

    @pl.when(jnp.logical_not(used))
    def _():
        y_ref[...] = jnp.zeros_like(y_ref)


def _experts(xs, blk_expert, n_valid, src_blk, layer, w_gu, b_gu, w_down, b_down):
    n_rows, dw = xs.shape
    _, n_e, d, f2 = w_gu.shape
    n_blocks = n_rows // MOE_ROWS

    def wmap(i, be, nv, sb):
        return (layer, be[i], 0, 0)

    grid_spec = pltpu.PrefetchScalarGridSpec(
        num_scalar_prefetch=3,
        grid=(n_blocks,),
        in_specs=[pl.BlockSpec((MOE_ROWS, dw), lambda i, be, nv, sb: (sb[i], 0)),
                  pl.BlockSpec((None, None, d, f2), wmap),
                  pl.BlockSpec((None, None, 1, f2), wmap),
                  pl.BlockSpec((None, None, f2 // 2, d), wmap),
                  pl.BlockSpec((None, None, 1, d), wmap)],
        out_specs=pl.BlockSpec((MOE_ROWS, dw), lambda i, be, nv, sb: (i, 0)),
        scratch_shapes=[pltpu.VMEM((d, f2), BF16), pltpu.VMEM((f2 // 2, d), BF16)],
    )
    depth = w_gu.shape[0]
    return pl.pallas_call(
        _expert_kernel,
        grid_spec=grid_spec,
        out_shape=jax.ShapeDtypeStruct((n_rows, dw), jnp.int32),
        compiler_params=pltpu.CompilerParams(dimension_semantics=("arbitrary",),
                                             vmem_limit_bytes=EXPERT_VMEM_LIMIT),
        name="moe_experts",
    )(blk_expert, n_valid, src_blk, xs, w_gu, b_gu.reshape(depth, n_e, 1, f2), w_down, b_down.reshape(depth, n_e, 1, d))


def _combine_kernel(y0_ref, y1_ref, y2_ref, y3_ref, gate_ref, x_ref, g2_ref, *rest):
    o_ref = rest[-1]
    gate = gate_ref[...]
    w = y0_ref.shape[1]
    y_hi, y_lo = None, None
    for k, y_ref in enumerate((y0_ref, y1_ref, y2_ref, y3_ref)):
        hi, lo = _unpack_bf16_pairs(y_ref[...])
        gk = gate[:, k:k + 1]
        y_hi = gk * hi if y_hi is None else y_hi + gk * hi
        y_lo = gk * lo if y_lo is None else y_lo + gk * lo
    o_ref[:, :w] = x_ref[:, :w] + g2_ref[:, :w] * y_hi
    o_ref[:, w:] = x_ref[:, w:] + g2_ref[:, w:] * y_lo


def _combine(y4, gate, x, g2, prev, *, seq, tok0, tm=512):
    t_all, d = x.shape
    t = y4.shape[0] // TOP_K
    bpb = seq // tm
    nblk = t // tm
    off = tok0 // tm
    y_specs = [pl.BlockSpec((tm, d // 2), functools.partial(lambda i, k: (k * nblk + i, 0), k=k))
               for k in range(TOP_K)]
    in_specs = y_specs + [pl.BlockSpec((tm, TOP_K), lambda i: (i, 0)),
                          pl.BlockSpec((tm, d), lambda i: (i + off, 0)),
                          pl.BlockSpec((None, 1, d), lambda i: ((i + off) // bpb, 0, 0))]
    args = [y4, y4, y4, y4, gate, x, g2]
    aliases = {}
    if prev is not None:
        in_specs.append(pl.BlockSpec(memory_space=pl.ANY))
        args.append(prev)
        aliases = {len(args) - 1: 0}
    return pl.pallas_call(
        _combine_kernel,
        grid=(nblk,),
        in_specs=in_specs,
        out_specs=pl.BlockSpec((tm, d), lambda i: (i + off, 0)),
        out_shape=jax.ShapeDtypeStruct((t_all, d), F32),
        input_output_aliases=aliases,
        compiler_params=_cparams(("parallel",)),
        name="moe_combine",
    )(*args)


MOE_TOKEN_GROUPS = 2


def _moe(x, norm, g2, w_router, b_router, layer, w_gu, b_gu, w_down, b_down, *, seq):
    t_all = x.shape[0]
    out = None
    for grp in range(MOE_TOKEN_GROUPS):
        out = _moe_group(x, norm, g2, w_router, b_router, layer, w_gu, b_gu, w_down, b_down, out, seq=seq,
                         tok0=grp * (t_all // MOE_TOKEN_GROUPS), t=t_all // MOE_TOKEN_GROUPS)
    return out


def _moe_group(x, norm, g2, w_router, b_router, layer, w_gu, b_gu, w_down, b_down, prev, *, seq, tok0, t):
    n_e = w_router.shape[1]
    idx_t, gate_t, rank_t, counts, h = _router(x, norm, w_router, b_router, seq=seq, tok0=tok0, t=t)
    e_ids = jnp.arange(n_e, dtype=jnp.int32)
    counts = counts[:, 0].astype(jnp.int32)
    padded = (counts + MOE_ROWS - 1) // MOE_ROWS * MOE_ROWS
    pend = jnp.cumsum(padded)
    pstart = pend - padded
    n_pad = n_e * MOE_ROWS
    n_rows = t * TOP_K + n_pad
    n_blocks = n_rows // MOE_ROWS
    dest_t = jnp.sum(jnp.where(idx_t[:, :, None] == e_ids, pstart, 0), axis=-1) + rank_t
    dest_flat = dest_t.reshape(TOP_K * t)
    blk_start = jnp.arange(n_blocks, dtype=jnp.int32) * MOE_ROWS
    n_used = pend[-1] // MOE_ROWS
    src_blk = jnp.minimum(jnp.arange(n_blocks, dtype=jnp.int32), n_used - 1)
    blk_expert = jnp.minimum(jnp.sum((src_blk * MOE_ROWS)[:, None] >= pend[None, :], axis=1), n_e - 1)
    blk_expert = blk_expert.astype(jnp.int32)
    own = blk_expert[:, None] == e_ids
    rows_left = jnp.sum(jnp.where(own, pstart + counts, 0), axis=-1) - blk_start
    n_valid = jnp.where(blk_start < pend[-1], jnp.clip(rows_left, 0, MOE_ROWS), 0).astype(jnp.int32)
    pad_len = padded - counts
    pad_end = jnp.cumsum(pad_len)
    j = jnp.arange(n_pad, dtype=jnp.int32)
    pe = jnp.sum(j[:, None] >= pad_end[None, :], axis=1)
    sel = jnp.minimum(pe, n_e - 1)[:, None] == e_ids
    in_expert = jnp.sum(jnp.where(sel, pstart + counts - (pad_end - pad_len), 0), axis=-1) + j
    pad_rows = jnp.where(pe < n_e, in_expert, pend[-1] + j - pad_end[-1]).astype(jnp.int32)

    xs = _sc_dispatch(h, dest_flat, pad_rows)
    yb = _experts(xs, blk_expert, n_valid, src_blk, layer, w_gu, b_gu, w_down, b_down)
    y4 = _sc_gather(yb, dest_flat)
    return _combine(y4, gate_t.T, x, g2, prev, seq=seq, tok0=tok0)


def _final_norm_kernel(x_ref, g_ref, o_ref):
    o_ref[...] = _rms(x_ref[...]) * g_ref[...]


def _final_norm(x, g, tm=1024):
    t, d = x.shape
    return pl.pallas_call(
        _final_norm_kernel,
        grid=(t // tm,),
        in_specs=[pl.BlockSpec((tm, d), lambda i: (i, 0)), pl.BlockSpec((1, d), lambda i: (0, 0))],
        out_specs=pl.BlockSpec((tm, d), lambda i: (i, 0)),
        out_shape=jax.ShapeDtypeStruct((t, d), F32),
        compiler_params=_cparams(("parallel",)),
        name="final_norm",
    )(x, g.reshape(1, d))


def _dup_heads(w, n_heads):
    d = w.shape[0]
    return jnp.repeat(w.reshape(d, n_heads, 1, HEAD_DIM), 2, axis=2).reshape(d, n_heads * LANES)


def _mixer_mla_swa(x, norm, g1, cos, sin, w_in, g_q, w_qb, g_kv, w_kvb, sink, w_out, *, bsz, seq):
    d = x.shape[1]
    o = np.cumsum((0, MLA_Q_LORA, MLA_KV_LORA, MLA_ROPE, SWA_Q_HEADS * HEAD_DIM,
                   SWA_KV_HEADS * HEAD_DIM, SWA_KV_HEADS * HEAD_DIM))
    w_cq, w_ckv, w_kr, w_qs, w_ks, w_vs = (w_in[:, o[i]:o[i + 1]] for i in range(6))
    w_lat = jnp.concatenate([w_cq, jnp.zeros((d, LANES), F32), w_ckv], axis=1).astype(BF16)
    lat = _proj([x], [w_lat], prologue="ada", norm=norm, seq=seq, tm=512, tn=w_lat.shape[1],
                out_dtype=F32, name="proj_latent")
    w_b = jnp.concatenate([_pair_layout(w_qs), _pair_layout(_dup_heads(w_ks, SWA_KV_HEADS)),
                           _dup_heads(w_vs, SWA_KV_HEADS),
                           w_kr, jnp.zeros((d, LANES - MLA_ROPE), F32)], axis=1).astype(BF16)
    qkv_b = _proj([x], [w_b], prologue="ada", norm=norm, seq=seq, tm=512, tn=w_b.shape[1], out_dtype=BF16,
                  rope=(cos, sin), rope_pattern=(ROPE_PAIR,) * 10 + (0, 0) + (ROPE_ROTATE,), name="proj_swa")
    kr_col = (w_b.shape[1] - LANES) // LANES
    qd = MLA_NOPE + MLA_ROPE
    w_q = w_qb.reshape(MLA_Q_LORA, MLA_HEADS, qd)
    w_q = jnp.concatenate([w_q, jnp.zeros((MLA_Q_LORA, MLA_HEADS, 2 * LANES - qd), F32)], axis=2)
    w_q = w_q.reshape(MLA_Q_LORA, MLA_HEADS * 2 * LANES).astype(BF16)
    q_cat = _proj([lat], [w_q], prologue="rms", norm=(g_q.reshape(1, -1),), lhs_cols=(MLA_Q_LORA, 0),
                  seq=seq, tm=512, tn=w_q.shape[1], out_dtype=BF16, rope=(cos, sin),
                  rope_pattern=(0, 1) * MLA_HEADS, out_scale=qd ** -0.5, name="proj_mla_q")
    w_kv = w_kvb.reshape(MLA_KV_LORA, MLA_HEADS, MLA_NOPE + MLA_V)
    w_kv = jnp.concatenate([w_kv[:, :, :MLA_NOPE].reshape(MLA_KV_LORA, -1),
                            w_kv[:, :, MLA_NOPE:].reshape(MLA_KV_LORA, -1)], axis=1).astype(BF16)
    kv = _proj([lat], [w_kv], prologue="rms", norm=(g_kv.reshape(1, -1),), lhs_cols=(MLA_KV_LORA, 2),
               seq=seq, tm=512, tn=w_kv.shape[1], out_dtype=BF16, name="proj_mla_kv")
    o_a = _mla_attention(q_cat, kv, qkv_b, kr_col, bsz=bsz, seq=seq, tq=256)
    qkv_b4 = qkv_b.reshape(bsz, 1, seq, qkv_b.shape[1])
    (o_b,) = _banded_attention(qkv_b4, qkv_b4, qkv_b4, bsz=bsz, seq=seq, dil=1, radius=SWA_RADIUS,
                               q_col=0, k_col=4, v_col=5, kv_width=2 * LANES,
                               pairs_per_kv=SWA_Q_HEADS // SWA_KV_HEADS // 2, sink=sink, want_lse=False,
                               name="swa_attention")
    o_b = o_b.reshape(bsz * seq, SWA_Q_HEADS * HEAD_DIM)
    na = MLA_HEADS * MLA_V
    return _proj([o_a, o_b], [w_out[:na].astype(BF16), w_out[na:].astype(BF16)], prologue="plain",
                 seq=seq, tm=512, tn=d, out_dtype=F32, residual=(x, g1), name="proj_out_ab")


def _mixer_dilated(x, norm, g1, cos, sin, w_in, w_out, *, bsz, seq):
    d = x.shape[1]
    qw = DIL_HEADS * HEAD_DIM
    w_bf = w_in.astype(BF16)
    outs, lses = [], []
    for g, (window, dil) in enumerate(DIL_PATTERN):
        w_g = w_bf[:, g * 3 * qw:(g + 1) * 3 * qw]
        w_g = jnp.concatenate([_pair_layout(w_g[:, :2 * qw]), w_g[:, 2 * qw:]], axis=1)
        qkv = _proj([x], [w_g], prologue="ada", norm=norm, seq=seq, tm=512, tn=3 * qw, sub_tn=qw, out_dtype=BF16,
                    rope=(cos, sin), rope_pattern=(ROPE_PAIR,) * (2 * qw // LANES) + (0,) * (qw // LANES),
                    out_dil=dil, name=f"proj_dil_{dil}")
        qkv = qkv.reshape(bsz, dil, seq // dil, 3 * qw)
        o, lse = _banded_attention(qkv, qkv, qkv, bsz=bsz, seq=seq, dil=dil, radius=window // (2 * dil),
                                   q_col=0, k_col=1, v_col=2, kv_width=qw, pairs_per_kv=1,
                                   want_lse=True, name=f"dilated_attention_{dil}")
        outs.append(o)
        lses.append(lse)
    return _proj(outs + lses, [w_out.astype(BF16)], prologue="dil", seq=seq, tm=512, tn=d, out_dtype=F32,
                 residual=(x, g1), in_dils=tuple(dil for _, dil in DIL_PATTERN), name="proj_out_c")


def kernel(x, c, positions, w_mod, b_mod, g_norm_mix, g_norm_ffn, w_in_ab, mla_g_q, mla_w_qb, mla_g_kv,
           mla_w_kvb, swa_sink, w_out_ab, w_in_c, w_out_c, w_router, b_router, w_gu, b_gu, w_down, b_down,
           g_final):
    bsz, seq, d = x.shape
    depth = w_mod.shape[0]
    cos, sin = _rope_tables(positions)
    mod = _modulation(c, w_mod, b_mod)
    xt = x.reshape(bsz * seq, d)
    for layer in range(depth):
        sh1, sc1, g1, sh2, sc2, g2 = (mod[layer, :, i * d:(i + 1) * d].reshape(bsz, 1, d) for i in range(6))
        li = layer // 2
        norm = (g_norm_mix[layer].reshape(1, d), sc1, sh1)
        if layer % 2 == 0:
            xt = _mixer_mla_swa(xt, norm, g1, cos, sin, w_in_ab[li], mla_g_q[li], mla_w_qb[li], mla_g_kv[li],
                                mla_w_kvb[li], swa_sink[li], w_out_ab[li], bsz=bsz, seq=seq)
        else:
            xt = _mixer_dilated(xt, norm, g1, cos, sin, w_in_c[li], w_out_c[li], bsz=bsz, seq=seq)
        norm = (g_norm_ffn[layer].reshape(1, d), sc2, sh2)
        xt = _moe(xt, norm, g2, w_router[layer], b_router[layer], layer, w_gu, b_gu, w_down, b_down, seq=seq)
    return _final_norm(xt, g_final).reshape(bsz, seq, d)
```

```python
import functools

import jax
import jax.numpy as jnp
import numpy as np
from jax import lax
from jax.experimental import pallas as pl
from jax.experimental.pallas import tpu as pltpu
from jax.experimental.pallas import tpu_sc as plsc

F32 = jnp.float32
BF16 = jnp.bfloat16

EPS = 1e-6
ROPE_THETA = 10000.0
LANES = 128
HEAD_DIM = 64
ROPE_HALF = HEAD_DIM // 2

MLA_HEADS = 8
MLA_Q_LORA = 384
MLA_KV_LORA = 256
MLA_NOPE = 128
MLA_ROPE = 64
MLA_V = 128
SWA_Q_HEADS = 16
SWA_KV_HEADS = 2
SWA_RADIUS = 128
DIL_PATTERN = ((128, 1), (512, 4), (2048, 16))
DIL_HEADS = 16
N_EXPERTS = 32
TOP_K = 4
SWIGLU_LIMIT = 7.0
SWIGLU_ALPHA = 1.702
MOE_ROWS = 256
BAND_SUB_ROWS = 128

VMEM_LIMIT = 48 * 1024 * 1024
EXPERT_VMEM_LIMIT = 56 * 1024 * 1024


def _cparams(sem):
    return pltpu.CompilerParams(dimension_semantics=sem, vmem_limit_bytes=VMEM_LIMIT)


def _mod_kernel(c_ref, w_ref, b_ref, o_ref):
    c = c_ref[...]
    ca = c * (1.0 / (1.0 + jnp.exp(-c)))
    o_ref[...] = jnp.dot(ca, w_ref[...], preferred_element_type=F32,
                         precision=lax.Precision.HIGHEST) + b_ref[...]


def _modulation(c, w_mod, b_mod):
    depth, d, n = w_mod.shape
    bsz = c.shape[0]
    tn = 1536
    return pl.pallas_call(
        _mod_kernel,
        grid=(depth, n // tn),
        in_specs=[pl.BlockSpec((bsz, d), lambda l, j: (0, 0)),
                  pl.BlockSpec((None, d, tn), lambda l, j: (l, 0, j)),
                  pl.BlockSpec((None, 1, tn), lambda l, j: (l, 0, j))],
        out_specs=pl.BlockSpec((None, bsz, tn), lambda l, j: (l, 0, j)),
        out_shape=jax.ShapeDtypeStruct((depth, bsz, n), F32),
        compiler_params=_cparams(("parallel", "parallel")),
        name="modulation",
    )(c, w_mod, b_mod.reshape(depth, 1, n))


def _rope_table_kernel(pos_ref, inv_ref, sign_ref, cos_ref, sin_ref):
    ang = inv_ref[...] * pos_ref[...].astype(F32)
    cos_ref[...] = jnp.transpose(jnp.cos(ang))
    sin_ref[...] = jnp.transpose(jnp.sin(ang) * sign_ref[...])


def _rope_tables(positions):
    bsz, s = positions.shape
    inv = ROPE_THETA ** (-jnp.arange(0, HEAD_DIM, 2, dtype=F32) / HEAD_DIM)
    inv128 = jnp.tile(inv, LANES // ROPE_HALF).reshape(LANES, 1)
    sign = jnp.where((jnp.arange(LANES) % HEAD_DIM) < ROPE_HALF, -1.0, 1.0).astype(F32).reshape(LANES, 1)
    out = jax.ShapeDtypeStruct((bsz * s, LANES), F32)
    return pl.pallas_call(
        _rope_table_kernel,
        grid=(bsz,),
        in_specs=[pl.BlockSpec((None, 1, s), lambda b: (b, 0, 0)),
                  pl.BlockSpec((LANES, 1), lambda b: (0, 0)),
                  pl.BlockSpec((LANES, 1), lambda b: (0, 0))],
        out_specs=[pl.BlockSpec((s, LANES), lambda b: (b, 0)),
                   pl.BlockSpec((s, LANES), lambda b: (b, 0))],
        out_shape=[out, out],
        compiler_params=_cparams(("parallel",)),
        name="rope_tables",
    )(positions.reshape(bsz, 1, s), inv128, sign)


def _rms(x):
    return x * lax.rsqrt(jnp.mean(x * x, axis=-1, keepdims=True) + EPS)


def _ada_norm(x, g, sc, sh):
    return (_rms(x) * g) * (1.0 + sc) + sh


def _pack_bf16_pairs(x):
    w = x.shape[1] // 2
    hi = pltpu.bitcast(x[:, :w].astype(BF16).astype(F32), jnp.int32)
    lo = pltpu.bitcast(x[:, w:].astype(BF16).astype(F32), jnp.int32)
    return hi | lax.shift_right_logical(lo, 16)


def _unpack_bf16_pairs(p):
    hi = pltpu.bitcast(p & jnp.int32(-65536), F32)
    lo = pltpu.bitcast(lax.shift_left(p, 16), F32)
    return hi, lo


ROPE_ROTATE, ROPE_PAIR = 1, 2


def _pair_layout(w):
    k, n = w.shape
    w = w.reshape(k, n // LANES, 2, 2, ROPE_HALF)
    return jnp.swapaxes(w, 2, 3).reshape(k, n)


def _rope_group(a, cos, sin, first_half):
    rot = jnp.where(first_half, pltpu.roll(a, LANES - ROPE_HALF, 1), pltpu.roll(a, ROPE_HALF, 1))
    return a * cos + rot * sin


def _proj_kernel(*refs, prologue, n_lhs, rope_pattern, sub_tn, out_scale, residual, in_dils, out_dil):
    pos = 0
    if prologue == "ada":
        x_ref, g_ref, sc_ref, sh_ref = refs[:4]
        pos = 4
    elif prologue == "rms":
        x_ref, g_ref = refs[:2]
        pos = 2
    elif prologue == "dil":
        o_refs = refs[0:3]
        l_refs = refs[3:6]
        pos = 6
    else:
        x_refs = refs[:n_lhs]
        pos = n_lhs
    w_refs = refs[pos:pos + n_lhs]
    pos += n_lhs
    if rope_pattern is not None:
        cos_ref, sin_ref = refs[pos:pos + 2]
        pos += 2
    if residual:
        res_ref, gate_ref = refs[pos:pos + 2]
        pos += 2
    o_ref = refs[pos]
    pos += 1
    h_ref = refs[pos] if prologue != "plain" else None
    slab_ref = refs[pos + 1] if (prologue == "dil" or out_dil > 1) else None

    if prologue != "plain":
        @pl.when(pl.program_id(1) == 0)
        def _():
            if prologue == "ada":
                h_ref[...] = _ada_norm(x_ref[...], g_ref[...], sc_ref[...], sh_ref[...]).astype(BF16)
            elif prologue == "rms":
                h_ref[...] = (_rms(x_ref[...]) * g_ref[...]).astype(BF16)
            else:
                tm = h_ref.shape[0]
                for c in range(h_ref.shape[1] // LANES):
                    cols = slice(c * LANES, (c + 1) * LANES)
                    vals = []
                    for i, (ref, dil) in enumerate(zip(o_refs + l_refs, in_dils + in_dils)):
                        if dil == 1:
                            vals.append(ref[0, :, cols].astype(F32))
                        else:
                            for r in range(dil):
                                slab_ref[i, pl.ds(r, tm // dil, stride=dil), :] = ref[r, :, cols].astype(F32)
                            vals.append(slab_ref[i])
                    o0, o1, o2, l0, l1, l2 = vals
                    m = jnp.maximum(jnp.maximum(l0, l1), l2)
                    e0, e1, e2 = jnp.exp(l0 - m), jnp.exp(l1 - m), jnp.exp(l2 - m)
                    h_ref[:, cols] = ((e0 * o0 + e1 * o1 + e2 * o2) / (e0 + e1 + e2)).astype(BF16)

    n_total = w_refs[0].shape[1]
    tn = n_total if sub_tn is None else sub_tn

    def matmul(t):
        cols = slice(t * tn, (t + 1) * tn)
        if prologue != "plain":
            acc = jnp.dot(h_ref[...], w_refs[0][:, cols], preferred_element_type=F32)
        else:
            acc = jnp.dot(x_refs[0][...], w_refs[0][:, cols], preferred_element_type=F32)
            for xr, wr in zip(x_refs[1:], w_refs[1:]):
                acc = acc + jnp.dot(xr[...], wr[:, cols], preferred_element_type=F32)
        return acc * out_scale if out_scale != 1.0 else acc

    if residual:
        o_ref[...] = (res_ref[...] + gate_ref[...] * matmul(0)).astype(o_ref.dtype)
        return

    pattern = rope_pattern if rope_pattern is not None else (0,) * (n_total // LANES)
    if any(pattern):
        cos = cos_ref[...]
        sin = sin_ref[...]
        lane = lax.broadcasted_iota(jnp.int32, cos.shape, 1)
        first_half = (lane % HEAD_DIM) < ROPE_HALF
        if ROPE_PAIR in pattern:
            sin_pair = jnp.where(lane < HEAD_DIM, -1.0, 1.0) * jnp.where(first_half, -sin, sin)

    def epilogue(acc, t):
        tm = acc.shape[0]
        for c in range(tn // LANES):
            g = t * (tn // LANES) + c
            cols = slice(g * LANES, (g + 1) * LANES)
            a = acc[:, c * LANES:(c + 1) * LANES]
            if pattern[g] == ROPE_ROTATE:
                a = _rope_group(a, cos, sin, first_half)
            elif pattern[g] == ROPE_PAIR:
                a = a * cos + pltpu.roll(a, HEAD_DIM, 1) * sin_pair
            if out_dil == 1:
                o_ref[:, cols] = a.astype(o_ref.dtype)
            else:
                slab_ref[0] = a
                for r in range(out_dil):
                    o_ref[r, :, cols] = slab_ref[0, pl.ds(r, tm // out_dil, stride=out_dil), :].astype(o_ref.dtype)

    acc_next = matmul(0)
    for t in range(n_total // tn):
        acc = acc_next
        if t + 1 < n_total // tn:
            acc_next = matmul(t + 1)
        epilogue(acc, t)


def _proj(lhs, ws, *, prologue, seq, tm, tn, out_dtype, norm=None, lhs_cols=None, rope=None,
          rope_pattern=None, sub_tn=None, out_scale=1.0, residual=None, in_dils=None, out_dil=1, name):
    k, n = ws[0].shape
    bsz_seq = lhs[0].shape[0] if prologue != "dil" else lhs[0].shape[0] * seq
    t = bsz_seq
    assert t % tm == 0 and n % tn == 0 and seq % tm == 0
    bpb = seq // tm
    n_lhs = len(ws)
    args, specs = [], []
    row = lambda i, j: (i, 0)
    if prologue in ("ada", "rms"):
        kw, kc = lhs_cols if lhs_cols is not None else (k, 0)
        args.append(lhs[0])
        specs.append(pl.BlockSpec((tm, kw), lambda i, j: (i, kc)))
        args.append(norm[0])
        specs.append(pl.BlockSpec((1, k), lambda i, j: (0, 0)))
        if prologue == "ada":
            for v in norm[1:3]:
                args.append(v)
                specs.append(pl.BlockSpec((None, 1, k), lambda i, j: (i // bpb, 0, 0)))
    elif prologue == "dil":
        for a, dil in zip(lhs, in_dils + in_dils):
            args.append(a)
            specs.append(pl.BlockSpec((None, dil, tm // dil, k), lambda i, j: (i // bpb, 0, i % bpb, 0)))
    else:
        for a, w in zip(lhs, ws):
            args.append(a)
            specs.append(pl.BlockSpec((tm, w.shape[0]), row))
    for w in ws:
        args.append(w)
        specs.append(pl.BlockSpec((w.shape[0], tn), lambda i, j: (0, j)))
    if rope_pattern is not None:
        assert len(rope_pattern) == tn // LANES
        for tab in rope:
            args.append(tab)
            specs.append(pl.BlockSpec((tm, LANES), row))
    if residual is not None:
        res, gate = residual
        args.append(res)
        specs.append(pl.BlockSpec((tm, tn), lambda i, j: (i, j)))
        args.append(gate)
        specs.append(pl.BlockSpec((None, 1, tn), lambda i, j: (i // bpb, 0, j)))
    scratch = [] if prologue == "plain" else [pltpu.VMEM((tm, k), BF16)]
    if prologue == "dil":
        scratch.append(pltpu.VMEM((2 * len(in_dils), tm, LANES), F32))
    elif out_dil > 1:
        scratch.append(pltpu.VMEM((1, tm, LANES), F32))
    if out_dil == 1:
        out_spec = pl.BlockSpec((tm, tn), lambda i, j: (i, j))
        out_shape = jax.ShapeDtypeStruct((t, n), out_dtype)
    else:
        out_spec = pl.BlockSpec((None, out_dil, tm // out_dil, tn), lambda i, j: (i // bpb, 0, i % bpb, j))
        out_shape = jax.ShapeDtypeStruct((t // seq, out_dil, seq // out_dil, n), out_dtype)
    kern = functools.partial(_proj_kernel, prologue=prologue, n_lhs=n_lhs, rope_pattern=rope_pattern,
                             sub_tn=sub_tn, out_scale=out_scale, residual=residual is not None,
                             in_dils=in_dils, out_dil=out_dil)
    return pl.pallas_call(
        kern,
        grid=(t // tm, n // tn),
        in_specs=specs,
        out_specs=out_spec,
        out_shape=out_shape,
        scratch_shapes=scratch,
        compiler_params=_cparams(("parallel", "arbitrary")),
        name=name,
    )(*args)


MLA_HEADS_PER_STEP = 4


def _mla_scores(q_ref, kn_ref, kr, h):
    k = jnp.concatenate([kn_ref[:, h * LANES:(h + 1) * LANES], kr], axis=1)
    q = q_ref[:, h * 2 * LANES:(h + 1) * 2 * LANES]
    return lax.dot_general(q, k, (((1,), (1,)), ((), ())), preferred_element_type=F32)


def _mla_kernel(q_ref, kn_ref, kr_ref, v_ref, o_ref):
    kr = kr_ref[...]
    s_next = _mla_scores(q_ref, kn_ref, kr, 0)
    for h in range(MLA_HEADS_PER_STEP):
        s = s_next
        if h + 1 < MLA_HEADS_PER_STEP:
            s_next = _mla_scores(q_ref, kn_ref, kr, h + 1)
        m = jnp.max(s, axis=1, keepdims=True)
        p = jnp.exp(s - m).astype(BF16)
        v = v_ref[:, h * LANES:(h + 1) * LANES]
        oe = jnp.dot(p, jnp.concatenate([v, jnp.ones_like(v)], axis=1), preferred_element_type=F32)
        o_ref[:, h * LANES:(h + 1) * LANES] = (oe[:, :LANES] / oe[:, LANES:]).astype(o_ref.dtype)


def _mla_attention(q_cat, kv, qkv_b, kr_col, *, bsz, seq, tq):
    t = bsz * seq
    nq = seq // tq
    hps = MLA_HEADS_PER_STEP
    groups = MLA_HEADS // hps
    return pl.pallas_call(
        _mla_kernel,
        grid=(bsz, groups, nq),
        in_specs=[pl.BlockSpec((tq, hps * 2 * LANES), lambda b, h, i: (b * nq + i, h)),
                  pl.BlockSpec((seq, hps * LANES), lambda b, h, i: (b, h)),
                  pl.BlockSpec((seq, LANES), lambda b, h, i: (b, kr_col)),
                  pl.BlockSpec((seq, hps * LANES), lambda b, h, i: (b, groups + h))],
        out_specs=pl.BlockSpec((tq, hps * LANES), lambda b, h, i: (b * nq + i, h)),
        out_shape=jax.ShapeDtypeStruct((t, MLA_HEADS * MLA_V), BF16),
        compiler_params=_cparams(("parallel", "parallel", "parallel")),
        name="mla_attention",
    )(q_cat, kv, qkv_b, kv)


def _banded_kernel(*refs, radius, tq, hb, length, n_pairs, pairs_per_kv, has_sink, want_lse):
    pos = 0
    if has_sink:
        sink_ref = refs[0]
        pos = 1
    q_ref, kp_ref, kc_ref, kn_ref, vp_ref, vc_ref, vn_ref = refs[pos:pos + 7]
    pos += 7
    o_ref = refs[pos]
    lse_ref = refs[pos + 1] if want_lse else None

    j = pl.program_id(2)
    kw = jnp.concatenate([kp_ref[...], kc_ref[...], kn_ref[...]], axis=0)
    vw = jnp.concatenate([vp_ref[...], vc_ref[...], vn_ref[...]], axis=0)
    scale = HEAD_DIM ** -0.5
    qn = min(tq, BAND_SUB_ROWS)
    kn = qn + 2 * hb
    lane = lax.broadcasted_iota(jnp.int32, (qn, LANES), 1)
    low = lane < HEAD_DIM
    first = (lane % HEAD_DIM) < ROPE_HALF

    for q0 in range(0, tq, qn):
        qpos = j * tq + q0 + lax.broadcasted_iota(jnp.int32, (qn, kn), 0)
        kpos = j * tq - hb + q0 + lax.broadcasted_iota(jnp.int32, (qn, kn), 1)
        valid = (jnp.abs(qpos - kpos) <= radius) & (kpos >= 0) & (kpos < length)
        for p in range(n_pairs):
            c = p // pairs_per_kv
            qp = q_ref[q0:q0 + qn, p * LANES:(p + 1) * LANES] * scale
            kp = kw[q0:q0 + kn, c * LANES:(c + 1) * LANES]
            vp = vw[q0:q0 + kn, c * LANES:(c + 1) * LANES]
            vp = jnp.concatenate([vp, jnp.ones_like(vp)], axis=1)
            halves = []
            for half in range(2):
                qh = jnp.where(first if half == 0 else ~first, qp, jnp.zeros_like(qp))
                s = lax.dot_general(qh, kp, (((1,), (1,)), ((), ())), preferred_element_type=F32)
                s = jnp.where(valid, s, -jnp.inf)
                m = jnp.max(s, axis=1, keepdims=True)
                if has_sink:
                    sk = sink_ref[2 * p + half]
                    m = jnp.maximum(m, sk)
                oe = jnp.dot(jnp.exp(s - m).astype(BF16), vp, preferred_element_type=F32)
                den = oe[:, LANES:]
                if has_sink:
                    den = den + jnp.exp(sk - m)
                halves.append((oe[:, :LANES] / den, m + jnp.log(den) if want_lse else None))
            o_ref[q0:q0 + qn, p * LANES:(p + 1) * LANES] = jnp.where(
                low, halves[0][0], halves[1][0]).astype(o_ref.dtype)
            if want_lse:
                lse_ref[q0:q0 + qn, p * LANES:(p + 1) * LANES] = jnp.where(low, halves[0][1], halves[1][1])


def _banded_attention(q_arr, k_arr, v_arr, *, bsz, seq, dil, radius, q_col, k_col, v_col, kv_width,
                      pairs_per_kv, sink=None, want_lse, name):
    length = seq // dil
    hb = radius
    tq = min(256, length)
    assert tq % hb == 0 and length % tq == 0
    qw = DIL_HEADS * HEAD_DIM
    n_pairs = qw // LANES
    rep = tq // hb
    last_hb = length // hb - 1

    def prev_spec(col):
        return pl.BlockSpec((None, None, hb, kv_width),
                            lambda b, r, j: (b, r, jnp.maximum(j * rep - 1, 0), col))

    def cur_spec(col):
        return pl.BlockSpec((None, None, tq, kv_width), lambda b, r, j: (b, r, j, col))

    def next_spec(col):
        return pl.BlockSpec((None, None, hb, kv_width),
                            lambda b, r, j: (b, r, jnp.minimum((j + 1) * rep, last_hb), col))

    in_specs = [pl.BlockSpec((None, None, tq, qw), lambda b, r, j: (b, r, j, q_col)),
                prev_spec(k_col), cur_spec(k_col), next_spec(k_col),
                prev_spec(v_col), cur_spec(v_col), next_spec(v_col)]
    args = [q_arr, k_arr, k_arr, k_arr, v_arr, v_arr, v_arr]
    if sink is not None:
        in_specs = [pl.BlockSpec(memory_space=pltpu.SMEM)] + in_specs
        args = [sink] + args
    out_spec = pl.BlockSpec((None, None, tq, qw), lambda b, r, j: (b, r, j, 0))
    out_specs = [out_spec]
    out_shape = [jax.ShapeDtypeStruct((bsz, dil, length, qw), BF16)]
    if want_lse:
        out_specs.append(out_spec)
        out_shape.append(jax.ShapeDtypeStruct((bsz, dil, length, qw), F32))
    kern = functools.partial(_banded_kernel, radius=radius, tq=tq, hb=hb, length=length, n_pairs=n_pairs,
                             pairs_per_kv=pairs_per_kv, has_sink=sink is not None, want_lse=want_lse)
    return pl.pallas_call(
        kern,
        grid=(bsz, dil, length // tq),
        in_specs=in_specs,
        out_specs=out_specs,
        out_shape=out_shape,
        compiler_params=_cparams(("parallel", "parallel", "parallel")),
        name=name,
    )(*args)


def _router_kernel(x_ref, g_ref, sc_ref, sh_ref, wt_ref, b_ref, tri_ref,
                   idx_ref, gate_ref, rank_ref, cnt_ref, h_ref, carry_ref):
    @pl.when(pl.program_id(0) == 0)
    def _():
        carry_ref[...] = jnp.zeros_like(carry_ref)

    h = _ada_norm(x_ref[...], g_ref[...], sc_ref[...], sh_ref[...])
    h_ref[...] = _pack_bf16_pairs(h)
    lt = lax.dot_general(wt_ref[...], h, (((1,), (1,)), ((), ())), preferred_element_type=F32,
                         precision=lax.Precision.HIGHEST) + b_ref[...]
    n_e, tm = lt.shape
    e_iota = lax.broadcasted_iota(jnp.int32, (n_e, tm), 0)
    vals, sels = [], []
    for k in range(TOP_K):
        m = jnp.max(lt, axis=0, keepdims=True)
        idx = jnp.min(jnp.where(lt == m, e_iota, n_e), axis=0, keepdims=True)
        sel = e_iota == idx
        idx_ref[k:k + 1, :] = idx
        vals.append(m)
        sels.append(sel)
        lt = jnp.where(sel, -jnp.inf, lt)
    exps = [jnp.exp(v - vals[0]) for v in vals]
    den = exps[0] + exps[1] + exps[2] + exps[3]
    for k in range(TOP_K):
        gate_ref[k:k + 1, :] = exps[k] / den

    onehot = jnp.where(sels[0] | sels[1] | sels[2] | sels[3], 1.0, 0.0)
    carry = carry_ref[:, 0:1]
    before = jnp.dot(onehot.astype(BF16), tri_ref[...], preferred_element_type=F32) + carry
    for k in range(TOP_K):
        rank_ref[k:k + 1, :] = jnp.sum(jnp.where(sels[k], before, 0.0), axis=0, keepdims=True).astype(jnp.int32)
    total = carry + jnp.sum(onehot, axis=1, keepdims=True)
    carry_ref[...] = jnp.broadcast_to(total, carry_ref.shape)
    cnt_ref[...] = jnp.broadcast_to(total, cnt_ref.shape)


def _router(x, norm, w_router, b_router, *, seq, tok0, t, tm=512):
    d = x.shape[1]
    n_e = w_router.shape[1]
    bpb = seq // tm
    off = tok0 // tm
    tri = (jnp.arange(tm)[:, None] < jnp.arange(tm)[None, :]).astype(BF16)
    g, sc, sh = norm
    out_i = jax.ShapeDtypeStruct((TOP_K, t), jnp.int32)
    tok = pl.BlockSpec((TOP_K, tm), lambda i: (0, i))
    return pl.pallas_call(
        _router_kernel,
        grid=(t // tm,),
        in_specs=[pl.BlockSpec((tm, d), lambda i: (i + off, 0)),
                  pl.BlockSpec((1, d), lambda i: (0, 0)),
                  pl.BlockSpec((None, 1, d), lambda i: ((i + off) // bpb, 0, 0)),
                  pl.BlockSpec((None, 1, d), lambda i: ((i + off) // bpb, 0, 0)),
                  pl.BlockSpec((n_e, d), lambda i: (0, 0)),
                  pl.BlockSpec((n_e, 1), lambda i: (0, 0)),
                  pl.BlockSpec((tm, tm), lambda i: (0, 0))],
        out_specs=[tok, tok, tok, pl.BlockSpec((n_e, LANES), lambda i: (0, 0)),
                   pl.BlockSpec((tm, d // 2), lambda i: (i, 0))],
        out_shape=[out_i, jax.ShapeDtypeStruct((TOP_K, t), F32), out_i,
                   jax.ShapeDtypeStruct((n_e, LANES), F32), jax.ShapeDtypeStruct((t, d // 2), jnp.int32)],
        scratch_shapes=[pltpu.VMEM((n_e, LANES), F32)],
        compiler_params=_cparams(("arbitrary",)),
        name="router",
    )(x, g, sc, sh, w_router.T, b_router.reshape(n_e, 1), tri)


SC_CORES = 2
SC_SUBCORES = 16
SC_WORKERS = SC_CORES * SC_SUBCORES
SC_ROWS = 64


def _sc_worker_id():
    return lax.axis_index("subcore") * SC_CORES + lax.axis_index("core")


def _sc_mesh():
    return plsc.VectorSubcoreMesh(core_axis_name="core", subcore_axis_name="subcore")


def _sc_dispatch(h, dest_flat, pad_rows):
    t, d = h.shape
    n_pad = pad_rows.shape[0]
    n_rows = TOP_K * t + n_pad
    tok_w = t // SC_WORKERS
    pad_w = n_pad // SC_WORKERS
    nb = tok_w // SC_ROWS
    assert tok_w % (2 * SC_ROWS) == 0 and pad_w % SC_ROWS == 0

    @functools.partial(
        pl.kernel, out_type=jax.ShapeDtypeStruct((n_rows, d), h.dtype), mesh=_sc_mesh(),
        scratch_types=[pltpu.VMEM((SC_ROWS,), jnp.int32), pltpu.VMEM((2, SC_ROWS, d), h.dtype),
                       pltpu.SemaphoreType.DMA((2,))])
    def scatter_rows(h_hbm, dest_hbm, pad_hbm, zero_hbm, xs_hbm, idx_v, rows_v, sem):
        wid = _sc_worker_id()
        base = wid * tok_w

        def load(b, slot):
            return pltpu.make_async_copy(h_hbm.at[pl.ds(base + b * SC_ROWS, SC_ROWS)], rows_v.at[slot],
                                         sem.at[slot])

        load(0, 0).start()

        @pl.loop(0, nb, step=2)
        def _(i):
            for slot in range(2):
                b = i + slot
                load(b, slot).wait()

                @pl.when(b + 1 < nb)
                def _():
                    load(b + 1, 1 - slot).start()

                for k in range(TOP_K):
                    pltpu.sync_copy(dest_hbm.at[pl.ds(k * t + base + b * SC_ROWS, SC_ROWS)], idx_v)
                    pltpu.sync_copy(rows_v.at[slot], xs_hbm.at[idx_v])

        pltpu.sync_copy(zero_hbm, rows_v.at[0])

        @pl.loop(0, pad_w // SC_ROWS)
        def _(b):
            pltpu.sync_copy(pad_hbm.at[pl.ds(wid * pad_w + b * SC_ROWS, SC_ROWS)], idx_v)
            pltpu.sync_copy(rows_v.at[0], xs_hbm.at[idx_v])

    return scatter_rows(h, dest_flat, pad_rows, jnp.zeros((SC_ROWS, d), h.dtype))


def _sc_gather(table, idx):
    n = idx.shape[0]
    d = table.shape[1]
    per_w = n // SC_WORKERS
    nb = per_w // SC_ROWS
    assert per_w % (2 * SC_ROWS) == 0

    @functools.partial(
        pl.kernel, out_type=jax.ShapeDtypeStruct((n, d), table.dtype), mesh=_sc_mesh(),
        scratch_types=[pltpu.VMEM((2, SC_ROWS), jnp.int32), pltpu.VMEM((2, SC_ROWS, d), table.dtype),
                       pltpu.SemaphoreType.DMA((2,))])
    def gather_rows(table_hbm, idx_hbm, out_hbm, idx_v, rows_v, sem):
        base = _sc_worker_id() * per_w

        def gather(slot):
            return pltpu.make_async_copy(table_hbm.at[idx_v.at[slot]], rows_v.at[slot], sem.at[slot])

        def start(b, slot):
            pltpu.sync_copy(idx_hbm.at[pl.ds(base + b * SC_ROWS, SC_ROWS)], idx_v.at[slot])
            gather(slot).start()

        start(0, 0)
        start(1, 1)

        @pl.loop(0, nb, step=2)
        def _(i):
            for slot in range(2):
                b = i + slot
                gather(slot).wait()
                pltpu.sync_copy(rows_v.at[slot], out_hbm.at[pl.ds(base + b * SC_ROWS, SC_ROWS)])

                @pl.when(b + 2 < nb)
                def _():
                    start(b + 2, slot)

    return gather_rows(table, idx)


def _expert_kernel(be_ref, nv_ref, sb_ref, xs_ref, wgu_ref, bgu_ref, wd_ref, bd_ref, y_ref, wgu_bf, wd_bf):
    i = pl.program_id(0)
    used = nv_ref[i] > 0

    @pl.when(used & ((i == 0) | (be_ref[i] != be_ref[jnp.maximum(i - 1, 0)])))
    def _():
        wgu_bf[...] = wgu_ref[...].astype(BF16)
        wd_bf[...] = wd_ref[...].astype(BF16)

    @pl.when(used)
    def _():
        d_ff = wd_ref.shape[0]
        x_hi, x_lo = _unpack_bf16_pairs(xs_ref[...])
        x = jnp.concatenate([x_hi.astype(BF16), x_lo.astype(BF16)], axis=1)
        gu = jnp.dot(x, wgu_bf[...], preferred_element_type=F32) + bgu_ref[...]
        glu = jnp.minimum(gu[:, :d_ff], SWIGLU_LIMIT)
        lin = jnp.clip(gu[:, d_ff:], -SWIGLU_LIMIT, SWIGLU_LIMIT)
        act = glu * (1.0 / (1.0 + jnp.exp(-SWIGLU_ALPHA * glu))) * (lin + 1.0)
        y = jnp.dot(act.astype(BF16), wd_bf[...], preferred_element_type=F32) + bd_ref[...]
        y_ref[...] = _pack_bf16_pairs(y)

    @pl.when(jnp.logical_not(used))
    def _():
        y_ref[...] = jnp.zeros_like(y_ref)


def _experts(xs, blk_expert, n_valid, src_blk, layer, w_gu, b_gu, w_down, b_down):
    n_rows, dw = xs.shape
    _, n_e, d, f2 = w_gu.shape
    n_blocks = n_rows // MOE_ROWS

    def wmap(i, be, nv, sb):
        return (layer, be[i], 0, 0)

    grid_spec = pltpu.PrefetchScalarGridSpec(
        num_scalar_prefetch=3,
        grid=(n_blocks,),
        in_specs=[pl.BlockSpec((MOE_ROWS, dw), lambda i, be, nv, sb: (sb[i], 0)),
                  pl.BlockSpec((None, None, d, f2), wmap),
                  pl.BlockSpec((None, None, 1, f2), wmap),
                  pl.BlockSpec((None, None, f2 // 2, d), wmap),
                  pl.BlockSpec((None, None, 1, d), wmap)],
        out_specs=pl.BlockSpec((MOE_ROWS, dw), lambda i, be, nv, sb: (i, 0)),
        scratch_shapes=[pltpu.VMEM((d, f2), BF16), pltpu.VMEM((f2 // 2, d), BF16)],
    )
    depth = w_gu.shape[0]
    return pl.pallas_call(
        _expert_kernel,
        grid_spec=grid_spec,
        out_shape=jax.ShapeDtypeStruct((n_rows, dw), jnp.int32),
        compiler_params=pltpu.CompilerParams(dimension_semantics=("arbitrary",),
                                             vmem_limit_bytes=EXPERT_VMEM_LIMIT),
        name="moe_experts",
    )(blk_expert, n_valid, src_blk, xs, w_gu, b_gu.reshape(depth, n_e, 1, f2), w_down, b_down.reshape(depth, n_e, 1, d))


def _combine_kernel(y0_ref, y1_ref, y2_ref, y3_ref, gate_ref, x_ref, g2_ref, *rest):
    o_ref = rest[-1]
    gate = gate_ref[...]
    w = y0_ref.shape[1]
    y_hi, y_lo = None, None
    for k, y_ref in enumerate((y0_ref, y1_ref, y2_ref, y3_ref)):
        hi, lo = _unpack_bf16_pairs(y_ref[...])
        gk = gate[:, k:k + 1]
        y_hi = gk * hi if y_hi is None else y_hi + gk * hi
        y_lo = gk * lo if y_lo is None else y_lo + gk * lo
    o_ref[:, :w] = x_ref[:, :w] + g2_ref[:, :w] * y_hi
    o_ref[:, w:] = x_ref[:, w:] + g2_ref[:, w:] * y_lo


def _combine(y4, gate, x, g2, prev, *, seq, tok0, tm=512):
    t_all, d = x.shape
    t = y4.shape[0] // TOP_K
    bpb = seq // tm
    nblk = t // tm
    off = tok0 // tm
    y_specs = [pl.BlockSpec((tm, d // 2), functools.partial(lambda i, k: (k * nblk + i, 0), k=k))
               for k in range(TOP_K)]
    in_specs = y_specs + [pl.BlockSpec((tm, TOP_K), lambda i: (i, 0)),
                          pl.BlockSpec((tm, d), lambda i: (i + off, 0)),
                          pl.BlockSpec((None, 1, d), lambda i: ((i + off) // bpb, 0, 0))]
    args = [y4, y4, y4, y4, gate, x, g2]
    aliases = {}
    if prev is not None:
        in_specs.append(pl.BlockSpec(memory_space=pl.ANY))
        args.append(prev)
        aliases = {len(args) - 1: 0}
    return pl.pallas_call(
        _combine_kernel,
        grid=(nblk,),
        in_specs=in_specs,
        out_specs=pl.BlockSpec((tm, d), lambda i: (i + off, 0)),
        out_shape=jax.ShapeDtypeStruct((t_all, d), F32),
        input_output_aliases=aliases,
        compiler_params=_cparams(("parallel",)),
        name="moe_combine",
    )(*args)


MOE_TOKEN_GROUPS = 2


def _moe(x, norm, g2, w_router, b_router, layer, w_gu, b_gu, w_down, b_down, *, seq):
    t_all = x.shape[0]
    out = None
    for grp in range(MOE_TOKEN_GROUPS):
        out = _moe_group(x, norm, g2, w_router, b_router, layer, w_gu, b_gu, w_down, b_down, out, seq=seq,
                         tok0=grp * (t_all // MOE_TOKEN_GROUPS), t=t_all // MOE_TOKEN_GROUPS)
    return out


def _moe_group(x, norm, g2, w_router, b_router, layer, w_gu, b_gu, w_down, b_down, prev, *, seq, tok0, t):
    n_e = w_router.shape[1]
    idx_t, gate_t, rank_t, counts, h = _router(x, norm, w_router, b_router, seq=seq, tok0=tok0, t=t)
    e_ids = jnp.arange(n_e, dtype=jnp.int32)
    counts = counts[:, 0].astype(jnp.int32)
    padded = (counts + MOE_ROWS - 1) // MOE_ROWS * MOE_ROWS
    pend = jnp.cumsum(padded)
    pstart = pend - padded
    n_pad = n_e * MOE_ROWS
    n_rows = t * TOP_K + n_pad
    n_blocks = n_rows // MOE_ROWS
    dest_t = jnp.sum(jnp.where(idx_t[:, :, None] == e_ids, pstart, 0), axis=-1) + rank_t
    dest_flat = dest_t.reshape(TOP_K * t)
    blk_start = jnp.arange(n_blocks, dtype=jnp.int32) * MOE_ROWS
    n_used = pend[-1] // MOE_ROWS
    src_blk = jnp.minimum(jnp.arange(n_blocks, dtype=jnp.int32), n_used - 1)
    blk_expert = jnp.minimum(jnp.sum((src_blk * MOE_ROWS)[:, None] >= pend[None, :], axis=1), n_e - 1)
    blk_expert = blk_expert.astype(jnp.int32)
    own = blk_expert[:, None] == e_ids
    rows_left = jnp.sum(jnp.where(own, pstart + counts, 0), axis=-1) - blk_start
    n_valid = jnp.where(blk_start < pend[-1], jnp.clip(rows_left, 0, MOE_ROWS), 0).astype(jnp.int32)
    pad_len = padded - counts
    pad_end = jnp.cumsum(pad_len)
    j = jnp.arange(n_pad, dtype=jnp.int32)
    pe = jnp.sum(j[:, None] >= pad_end[None, :], axis=1)
    sel = jnp.minimum(pe, n_e - 1)[:, None] == e_ids
    in_expert = jnp.sum(jnp.where(sel, pstart + counts - (pad_end - pad_len), 0), axis=-1) + j
    pad_rows = jnp.where(pe < n_e, in_expert, pend[-1] + j - pad_end[-1]).astype(jnp.int32)

    xs = _sc_dispatch(h, dest_flat, pad_rows)
    yb = _experts(xs, blk_expert, n_valid, src_blk, layer, w_gu, b_gu, w_down, b_down)
    y4 = _sc_gather(yb, dest_flat)
    return _combine(y4, gate_t.T, x, g2, prev, seq=seq, tok0=tok0)


def _final_norm_kernel(x_ref, g_ref, o_ref):
    o_ref[...] = _rms(x_ref[...]) * g_ref[...]


def _final_norm(x, g, tm=1024):
    t, d = x.shape
    return pl.pallas_call(
        _final_norm_kernel,
        grid=(t // tm,),
        in_specs=[pl.BlockSpec((tm, d), lambda i: (i, 0)), pl.BlockSpec((1, d), lambda i: (0, 0))],
        out_specs=pl.BlockSpec((tm, d), lambda i: (i, 0)),
        out_shape=jax.ShapeDtypeStruct((t, d), F32),
        compiler_params=_cparams(("parallel",)),
        name="final_norm",
    )(x, g.reshape(1, d))


def _dup_heads(w, n_heads):
    d = w.shape[0]
    return jnp.repeat(w.reshape(d, n_heads, 1, HEAD_DIM), 2, axis=2).reshape(d, n_heads * LANES)


def _mixer_mla_swa(x, norm, g1, cos, sin, w_in, g_q, w_qb, g_kv, w_kvb, sink, w_out, *, bsz, seq):
    d = x.shape[1]
    o = np.cumsum((0, MLA_Q_LORA, MLA_KV_LORA, MLA_ROPE, SWA_Q_HEADS * HEAD_DIM,
                   SWA_KV_HEADS * HEAD_DIM, SWA_KV_HEADS * HEAD_DIM))
    w_cq, w_ckv, w_kr, w_qs, w_ks, w_vs = (w_in[:, o[i]:o[i + 1]] for i in range(6))
    w_lat = jnp.concatenate([w_cq, jnp.zeros((d, LANES), F32), w_ckv], axis=1).astype(BF16)
    lat = _proj([x], [w_lat], prologue="ada", norm=norm, seq=seq, tm=512, tn=w_lat.shape[1],
                out_dtype=F32, name="proj_latent")
    w_b = jnp.concatenate([_pair_layout(w_qs), _pair_layout(_dup_heads(w_ks, SWA_KV_HEADS)),
                           _dup_heads(w_vs, SWA_KV_HEADS),
                           w_kr, jnp.zeros((d, LANES - MLA_ROPE), F32)], axis=1).astype(BF16)
    qkv_b = _proj([x], [w_b], prologue="ada", norm=norm, seq=seq, tm=512, tn=w_b.shape[1], out_dtype=BF16,
                  rope=(cos, sin), rope_pattern=(ROPE_PAIR,) * 10 + (0, 0) + (ROPE_ROTATE,), name="proj_swa")
    kr_col = (w_b.shape[1] - LANES) // LANES
    qd = MLA_NOPE + MLA_ROPE
    w_q = w_qb.reshape(MLA_Q_LORA, MLA_HEADS, qd)
    w_q = jnp.concatenate([w_q, jnp.zeros((MLA_Q_LORA, MLA_HEADS, 2 * LANES - qd), F32)], axis=2)
    w_q = w_q.reshape(MLA_Q_LORA, MLA_HEADS * 2 * LANES).astype(BF16)
    q_cat = _proj([lat], [w_q], prologue="rms", norm=(g_q.reshape(1, -1),), lhs_cols=(MLA_Q_LORA, 0),
                  seq=seq, tm=512, tn=w_q.shape[1], out_dtype=BF16, rope=(cos, sin),
                  rope_pattern=(0, 1) * MLA_HEADS, out_scale=qd ** -0.5, name="proj_mla_q")
    w_kv = w_kvb.reshape(MLA_KV_LORA, MLA_HEADS, MLA_NOPE + MLA_V)
    w_kv = jnp.concatenate([w_kv[:, :, :MLA_NOPE].reshape(MLA_KV_LORA, -1),
                            w_kv[:, :, MLA_NOPE:].reshape(MLA_KV_LORA, -1)], axis=1).astype(BF16)
    kv = _proj([lat], [w_kv], prologue="rms", norm=(g_kv.reshape(1, -1),), lhs_cols=(MLA_KV_LORA, 2),
               seq=seq, tm=512, tn=w_kv.shape[1], out_dtype=BF16, name="proj_mla_kv")
    o_a = _mla_attention(q_cat, kv, qkv_b, kr_col, bsz=bsz, seq=seq, tq=256)
    qkv_b4 = qkv_b.reshape(bsz, 1, seq, qkv_b.shape[1])
    (o_b,) = _banded_attention(qkv_b4, qkv_b4, qkv_b4, bsz=bsz, seq=seq, dil=1, radius=SWA_RADIUS,
                               q_col=0, k_col=4, v_col=5, kv_width=2 * LANES,
                               pairs_per_kv=SWA_Q_HEADS // SWA_KV_HEADS // 2, sink=sink, want_lse=False,
                               name="swa_attention")
    o_b = o_b.reshape(bsz * seq, SWA_Q_HEADS * HEAD_DIM)
    na = MLA_HEADS * MLA_V
    return _proj([o_a, o_b], [w_out[:na].astype(BF16), w_out[na:].astype(BF16)], prologue="plain",
                 seq=seq, tm=512, tn=d, out_dtype=F32, residual=(x, g1), name="proj_out_ab")


def _mixer_dilated(x, norm, g1, cos, sin, w_in, w_out, *, bsz, seq):
    d = x.shape[1]
    qw = DIL_HEADS * HEAD_DIM
    w_bf = w_in.astype(BF16)
    outs, lses = [], []
    for g, (window, dil) in enumerate(DIL_PATTERN):
        w_g = w_bf[:, g * 3 * qw:(g + 1) * 3 * qw]
        w_g = jnp.concatenate([_pair_layout(w_g[:, :2 * qw]), w_g[:, 2 * qw:]], axis=1)
        qkv = _proj([x], [w_g], prologue="ada", norm=norm, seq=seq, tm=512, tn=3 * qw, sub_tn=qw, out_dtype=BF16,
                    rope=(cos, sin), rope_pattern=(ROPE_PAIR,) * (2 * qw // LANES) + (0,) * (qw // LANES),
                    out_dil=dil, name=f"proj_dil_{dil}")
        qkv = qkv.reshape(bsz, dil, seq // dil, 3 * qw)
        o, lse = _banded_attention(qkv, qkv, qkv, bsz=bsz, seq=seq, dil=dil, radius=window // (2 * dil),
                                   q_col=0, k_col=1, v_col=2, kv_width=qw, pairs_per_kv=1,
                                   want_lse=True, name=f"dilated_attention_{dil}")
        outs.append(o)
        lses.append(lse)
    return _proj(outs + lses, [w_out.astype(BF16)], prologue="dil", seq=seq, tm=512, tn=d, out_dtype=F32,
                 residual=(x, g1), in_dils=tuple(dil for _, dil in DIL_PATTERN), name="proj_out_c")


def kernel(x, c, positions, w_mod, b_mod, g_norm_mix, g_norm_ffn, w_in_ab, mla_g_q, mla_w_qb, mla_g_kv,
           mla_w_kvb, swa_sink, w_out_ab, w_in_c, w_out_c, w_router, b_router, w_gu, b_gu, w_down, b_down,
           g_final):
    bsz, seq, d = x.shape
    depth = w_mod.shape[0]
    cos, sin = _rope_tables(positions)
    mod = _modulation(c, w_mod, b_mod)
    xt = x.reshape(bsz * seq, d)
    for layer in range(depth):
        sh1, sc1, g1, sh2, sc2, g2 = (mod[layer, :, i * d:(i + 1) * d].reshape(bsz, 1, d) for i in range(6))
        li = layer // 2
        norm = (g_norm_mix[layer].reshape(1, d), sc1, sh1)
        if layer % 2 == 0:
            xt = _mixer_mla_swa(xt, norm, g1, cos, sin, w_in_ab[li], mla_g_q[li], mla_w_qb[li], mla_g_kv[li],
                                mla_w_kvb[li], swa_sink[li], w_out_ab[li], bsz=bsz, seq=seq)
        else:
            xt = _mixer_dilated(xt, norm, g1, cos, sin, w_in_c[li], w_out_c[li], bsz=bsz, seq=seq)
        norm = (g_norm_ffn[layer].reshape(1, d), sc2, sh2)
        xt = _moe(xt, norm, g2, w_router[layer], b_router[layer], layer, w_gu, b_gu, w_down, b_down, seq=seq)
    return _final_norm(xt, g_final).reshape(bsz, seq, d)
```

```python
import functools

import jax
import jax.numpy as jnp
import numpy as np
from jax import lax
from jax.experimental import pallas as pl
from jax.experimental.pallas import tpu as pltpu
from jax.experimental.pallas import tpu_sc as plsc

F32 = jnp.float32
BF16 = jnp.bfloat16

EPS = 1e-6
ROPE_THETA = 10000.0
LANES = 128
HEAD_DIM = 64
ROPE_HALF = HEAD_DIM // 2

MLA_HEADS = 8
MLA_Q_LORA = 384
MLA_KV_LORA = 256
MLA_NOPE = 128
MLA_ROPE = 64
MLA_V = 128
SWA_Q_HEADS = 16
SWA_KV_HEADS = 2
SWA_RADIUS = 128
DIL_PATTERN = ((128, 1), (512, 4), (2048, 16))
DIL_HEADS = 16
N_EXPERTS = 32
TOP_K = 4
SWIGLU_LIMIT = 7.0
SWIGLU_ALPHA = 1.702
MOE_ROWS = 512
BAND_SUB_ROWS = 128

VMEM_LIMIT = 48 * 1024 * 1024
EXPERT_VMEM_LIMIT = 56 * 1024 * 1024


def _cparams(sem):
    return pltpu.CompilerParams(dimension_semantics=sem, vmem_limit_bytes=VMEM_LIMIT)


def _mod_kernel(c_ref, w_ref, b_ref, o_ref):
    c = c_ref[...]
    ca = c * (1.0 / (1.0 + jnp.exp(-c)))
    o_ref[...] = jnp.dot(ca, w_ref[...], preferred_element_type=F32,
                         precision=lax.Precision.HIGHEST) + b_ref[...]


def _modulation(c, w_mod, b_mod):
    depth, d, n = w_mod.shape
    bsz = c.shape[0]
    tn = 1536
    return pl.pallas_call(
        _mod_kernel,
        grid=(depth, n // tn),
        in_specs=[pl.BlockSpec((bsz, d), lambda l, j: (0, 0)),
                  pl.BlockSpec((None, d, tn), lambda l, j: (l, 0, j)),
                  pl.BlockSpec((None, 1, tn), lambda l, j: (l, 0, j))],
        out_specs=pl.BlockSpec((None, bsz, tn), lambda l, j: (l, 0, j)),
        out_shape=jax.ShapeDtypeStruct((depth, bsz, n), F32),
        compiler_params=_cparams(("parallel", "parallel")),
        name="modulation",
    )(c, w_mod, b_mod.reshape(depth, 1, n))


def _rope_table_kernel(pos_ref, inv_ref, sign_ref, cos_ref, sin_ref):
    ang = inv_ref[...] * pos_ref[...].astype(F32)
    cos_ref[...] = jnp.transpose(jnp.cos(ang))
    sin_ref[...] = jnp.transpose(jnp.sin(ang) * sign_ref[...])


def _rope_tables(positions):
    bsz, s = positions.shape
    inv = ROPE_THETA ** (-jnp.arange(0, HEAD_DIM, 2, dtype=F32) / HEAD_DIM)
    inv128 = jnp.tile(inv, LANES // ROPE_HALF).reshape(LANES, 1)
    sign = jnp.where((jnp.arange(LANES) % HEAD_DIM) < ROPE_HALF, -1.0, 1.0).astype(F32).reshape(LANES, 1)
    out = jax.ShapeDtypeStruct((bsz * s, LANES), F32)
    return pl.pallas_call(
        _rope_table_kernel,
        grid=(bsz,),
        in_specs=[pl.BlockSpec((None, 1, s), lambda b: (b, 0, 0)),
                  pl.BlockSpec((LANES, 1), lambda b: (0, 0)),
                  pl.BlockSpec((LANES, 1), lambda b: (0, 0))],
        out_specs=[pl.BlockSpec((s, LANES), lambda b: (b, 0)),
                   pl.BlockSpec((s, LANES), lambda b: (b, 0))],
        out_shape=[out, out],
        compiler_params=_cparams(("parallel",)),
        name="rope_tables",
    )(positions.reshape(bsz, 1, s), inv128, sign)


def _rms(x):
    return x * lax.rsqrt(jnp.mean(x * x, axis=-1, keepdims=True) + EPS)


def _ada_norm(x, g, sc, sh):
    return (_rms(x) * g) * (1.0 + sc) + sh


def _pack_bf16_pairs(x):
    w = x.shape[1] // 2
    hi = pltpu.bitcast(x[:, :w].astype(BF16).astype(F32), jnp.int32)
    lo = pltpu.bitcast(x[:, w:].astype(BF16).astype(F32), jnp.int32)
    return hi | lax.shift_right_logical(lo, 16)


def _unpack_bf16_pairs(p):
    hi = pltpu.bitcast(p & jnp.int32(-65536), F32)
    lo = pltpu.bitcast(lax.shift_left(p, 16), F32)
    return hi, lo


ROPE_ROTATE, ROPE_PAIR = 1, 2


def _pair_layout(w):
    k, n = w.shape
    w = w.reshape(k, n // LANES, 2, 2, ROPE_HALF)
    return jnp.swapaxes(w, 2, 3).reshape(k, n)


def _rope_group(a, cos, sin, first_half):
    rot = jnp.where(first_half, pltpu.roll(a, LANES - ROPE_HALF, 1), pltpu.roll(a, ROPE_HALF, 1))
    return a * cos + rot * sin


def _proj_kernel(*refs, prologue, n_lhs, rope_pattern, sub_tn, out_scale, residual, in_dils, out_dil):
    pos = 0
    if prologue == "ada":
        x_ref, g_ref, sc_ref, sh_ref = refs[:4]
        pos = 4
    elif prologue == "rms":
        x_ref, g_ref = refs[:2]
        pos = 2
    elif prologue == "dil":
        o_refs = refs[0:3]
        l_refs = refs[3:6]
        pos = 6
    else:
        x_refs = refs[:n_lhs]
        pos = n_lhs
    w_refs = refs[pos:pos + n_lhs]
    pos += n_lhs
    if rope_pattern is not None:
        cos_ref, sin_ref = refs[pos:pos + 2]
        pos += 2
    if residual:
        res_ref, gate_ref = refs[pos:pos + 2]
        pos += 2
    o_ref = refs[pos]
    pos += 1
    h_ref = refs[pos] if prologue != "plain" else None
    slab_ref = refs[pos + 1] if (prologue == "dil" or out_dil > 1) else None

    if prologue != "plain":
        @pl.when(pl.program_id(1) == 0)
        def _():
            if prologue == "ada":
                h_ref[...] = _ada_norm(x_ref[...], g_ref[...], sc_ref[...], sh_ref[...]).astype(BF16)
            elif prologue == "rms":
                h_ref[...] = (_rms(x_ref[...]) * g_ref[...]).astype(BF16)
            else:
                tm = h_ref.shape[0]
                for c in range(h_ref.shape[1] // LANES):
                    cols = slice(c * LANES, (c + 1) * LANES)
                    vals = []
                    for i, (ref, dil) in enumerate(zip(o_refs + l_refs, in_dils + in_dils)):
                        if dil == 1:
                            vals.append(ref[0, :, cols].astype(F32))
                        else:
                            for r in range(dil):
                                slab_ref[i, pl.ds(r, tm // dil, stride=dil), :] = ref[r, :, cols].astype(F32)
                            vals.append(slab_ref[i])
                    o0, o1, o2, l0, l1, l2 = vals
                    m = jnp.maximum(jnp.maximum(l0, l1), l2)
                    e0, e1, e2 = jnp.exp(l0 - m), jnp.exp(l1 - m), jnp.exp(l2 - m)
                    h_ref[:, cols] = ((e0 * o0 + e1 * o1 + e2 * o2) / (e0 + e1 + e2)).astype(BF16)

    n_total = w_refs[0].shape[1]
    tn = n_total if sub_tn is None else sub_tn

    def matmul(t):
        cols = slice(t * tn, (t + 1) * tn)
        if prologue != "plain":
            acc = jnp.dot(h_ref[...], w_refs[0][:, cols], preferred_element_type=F32)
        else:
            acc = jnp.dot(x_refs[0][...], w_refs[0][:, cols], preferred_element_type=F32)
            for xr, wr in zip(x_refs[1:], w_refs[1:]):
                acc = acc + jnp.dot(xr[...], wr[:, cols], preferred_element_type=F32)
        return acc * out_scale if out_scale != 1.0 else acc

    if residual:
        o_ref[...] = (res_ref[...] + gate_ref[...] * matmul(0)).astype(o_ref.dtype)
        return

    pattern = rope_pattern if rope_pattern is not None else (0,) * (n_total // LANES)
    if any(pattern):
        cos = cos_ref[...]
        sin = sin_ref[...]
        lane = lax.broadcasted_iota(jnp.int32, cos.shape, 1)
        first_half = (lane % HEAD_DIM) < ROPE_HALF
        if ROPE_PAIR in pattern:
            sin_pair = jnp.where(lane < HEAD_DIM, -1.0, 1.0) * jnp.where(first_half, -sin, sin)

    def epilogue(acc, t):
        tm = acc.shape[0]
        for c in range(tn // LANES):
            g = t * (tn // LANES) + c
            cols = slice(g * LANES, (g + 1) * LANES)
            a = acc[:, c * LANES:(c + 1) * LANES]
            if pattern[g] == ROPE_ROTATE:
                a = _rope_group(a, cos, sin, first_half)
            elif pattern[g] == ROPE_PAIR:
                a = a * cos + pltpu.roll(a, HEAD_DIM, 1) * sin_pair
            if out_dil == 1:
                o_ref[:, cols] = a.astype(o_ref.dtype)
            else:
                slab_ref[0] = a
                for r in range(out_dil):
                    o_ref[r, :, cols] = slab_ref[0, pl.ds(r, tm // out_dil, stride=out_dil), :].astype(o_ref.dtype)

    acc_next = matmul(0)
    for t in range(n_total // tn):
        acc = acc_next
        if t + 1 < n_total // tn:
            acc_next = matmul(t + 1)
        epilogue(acc, t)


def _proj(lhs, ws, *, prologue, seq, tm, tn, out_dtype, norm=None, lhs_cols=None, rope=None,
          rope_pattern=None, sub_tn=None, out_scale=1.0, residual=None, in_dils=None, out_dil=1, name):
    k, n = ws[0].shape
    bsz_seq = lhs[0].shape[0] if prologue != "dil" else lhs[0].shape[0] * seq
    t = bsz_seq
    assert t % tm == 0 and n % tn == 0 and seq % tm == 0
    bpb = seq // tm
    n_lhs = len(ws)
    args, specs = [], []
    row = lambda i, j: (i, 0)
    if prologue in ("ada", "rms"):
        kw, kc = lhs_cols if lhs_cols is not None else (k, 0)
        args.append(lhs[0])
        specs.append(pl.BlockSpec((tm, kw), lambda i, j: (i, kc)))
        args.append(norm[0])
        specs.append(pl.BlockSpec((1, k), lambda i, j: (0, 0)))
        if prologue == "ada":
            for v in norm[1:3]:
                args.append(v)
                specs.append(pl.BlockSpec((None, 1, k), lambda i, j: (i // bpb, 0, 0)))
    elif prologue == "dil":
        for a, dil in zip(lhs, in_dils + in_dils):
            args.append(a)
            specs.append(pl.BlockSpec((None, dil, tm // dil, k), lambda i, j: (i // bpb, 0, i % bpb, 0)))
    else:
        for a, w in zip(lhs, ws):
            args.append(a)
            specs.append(pl.BlockSpec((tm, w.shape[0]), row))
    for w in ws:
        args.append(w)
        specs.append(pl.BlockSpec((w.shape[0], tn), lambda i, j: (0, j)))
    if rope_pattern is not None:
        assert len(rope_pattern) == tn // LANES
        for tab in rope:
            args.append(tab)
            specs.append(pl.BlockSpec((tm, LANES), row))
    if residual is not None:
        res, gate = residual
        args.append(res)
        specs.append(pl.BlockSpec((tm, tn), lambda i, j: (i, j)))
        args.append(gate)
        specs.append(pl.BlockSpec((None, 1, tn), lambda i, j: (i // bpb, 0, j)))
    scratch = [] if prologue == "plain" else [pltpu.VMEM((tm, k), BF16)]
    if prologue == "dil":
        scratch.append(pltpu.VMEM((2 * len(in_dils), tm, LANES), F32))
    elif out_dil > 1:
        scratch.append(pltpu.VMEM((1, tm, LANES), F32))
    if out_dil == 1:
        out_spec = pl.BlockSpec((tm, tn), lambda i, j: (i, j))
        out_shape = jax.ShapeDtypeStruct((t, n), out_dtype)
    else:
        out_spec = pl.BlockSpec((None, out_dil, tm // out_dil, tn), lambda i, j: (i // bpb, 0, i % bpb, j))
        out_shape = jax.ShapeDtypeStruct((t // seq, out_dil, seq // out_dil, n), out_dtype)
    kern = functools.partial(_proj_kernel, prologue=prologue, n_lhs=n_lhs, rope_pattern=rope_pattern,
                             sub_tn=sub_tn, out_scale=out_scale, residual=residual is not None,
                             in_dils=in_dils, out_dil=out_dil)
    return pl.pallas_call(
        kern,
        grid=(t // tm, n // tn),
        in_specs=specs,
        out_specs=out_spec,
        out_shape=out_shape,
        scratch_shapes=scratch,
        compiler_params=_cparams(("parallel", "arbitrary")),
        name=name,
    )(*args)


MLA_HEADS_PER_STEP = 4


def _mla_scores(q_ref, kn_ref, kr, h):
    k = jnp.concatenate([kn_ref[:, h * LANES:(h + 1) * LANES], kr], axis=1)
    q = q_ref[:, h * 2 * LANES:(h + 1) * 2 * LANES]
    return lax.dot_general(q, k, (((1,), (1,)), ((), ())), preferred_element_type=F32)


def _mla_kernel(q_ref, kn_ref, kr_ref, v_ref, o_ref):
    kr = kr_ref[...]
    s_next = _mla_scores(q_ref, kn_ref, kr, 0)
    for h in range(MLA_HEADS_PER_STEP):
        s = s_next
        if h + 1 < MLA_HEADS_PER_STEP:
            s_next = _mla_scores(q_ref, kn_ref, kr, h + 1)
        m = jnp.max(s, axis=1, keepdims=True)
        p = jnp.exp(s - m).astype(BF16)
        v = v_ref[:, h * LANES:(h + 1) * LANES]
        oe = jnp.dot(p, jnp.concatenate([v, jnp.ones_like(v)], axis=1), preferred_element_type=F32)
        o_ref[:, h * LANES:(h + 1) * LANES] = (oe[:, :LANES] / oe[:, LANES:]).astype(o_ref.dtype)


def _mla_attention(q_cat, kv, qkv_b, kr_col, *, bsz, seq, tq):
    t = bsz * seq
    nq = seq // tq
    hps = MLA_HEADS_PER_STEP
    groups = MLA_HEADS // hps
    return pl.pallas_call(
        _mla_kernel,
        grid=(bsz, groups, nq),
        in_specs=[pl.BlockSpec((tq, hps * 2 * LANES), lambda b, h, i: (b * nq + i, h)),
                  pl.BlockSpec((seq, hps * LANES), lambda b, h, i: (b, h)),
                  pl.BlockSpec((seq, LANES), lambda b, h, i: (b, kr_col)),
                  pl.BlockSpec((seq, hps * LANES), lambda b, h, i: (b, groups + h))],
        out_specs=pl.BlockSpec((tq, hps * LANES), lambda b, h, i: (b * nq + i, h)),
        out_shape=jax.ShapeDtypeStruct((t, MLA_HEADS * MLA_V), BF16),
        compiler_params=_cparams(("parallel", "parallel", "parallel")),
        name="mla_attention",
    )(q_cat, kv, qkv_b, kv)


def _banded_kernel(*refs, radius, tq, hb, length, n_pairs, pairs_per_kv, has_sink, want_lse):
    pos = 0
    if has_sink:
        sink_ref = refs[0]
        pos = 1
    q_ref, kp_ref, kc_ref, kn_ref, vp_ref, vc_ref, vn_ref = refs[pos:pos + 7]
    pos += 7
    o_ref = refs[pos]
    lse_ref = refs[pos + 1] if want_lse else None

    j = pl.program_id(2)
    kw = jnp.concatenate([kp_ref[...], kc_ref[...], kn_ref[...]], axis=0)
    vw = jnp.concatenate([vp_ref[...], vc_ref[...], vn_ref[...]], axis=0)
    scale = HEAD_DIM ** -0.5
    qn = min(tq, BAND_SUB_ROWS)
    kn = qn + 2 * hb
    lane = lax.broadcasted_iota(jnp.int32, (qn, LANES), 1)
    low = lane < HEAD_DIM
    first = (lane % HEAD_DIM) < ROPE_HALF

    for q0 in range(0, tq, qn):
        qpos = j * tq + q0 + lax.broadcasted_iota(jnp.int32, (qn, kn), 0)
        kpos = j * tq - hb + q0 + lax.broadcasted_iota(jnp.int32, (qn, kn), 1)
        valid = (jnp.abs(qpos - kpos) <= radius) & (kpos >= 0) & (kpos < length)
        for p in range(n_pairs):
            c = p // pairs_per_kv
            qp = q_ref[q0:q0 + qn, p * LANES:(p + 1) * LANES] * scale
            kp = kw[q0:q0 + kn, c * LANES:(c + 1) * LANES]
            vp = vw[q0:q0 + kn, c * LANES:(c + 1) * LANES]
            vp = jnp.concatenate([vp, jnp.ones_like(vp)], axis=1)
            halves = []
            for half in range(2):
                qh = jnp.where(first if half == 0 else ~first, qp, jnp.zeros_like(qp))
                s = lax.dot_general(qh, kp, (((1,), (1,)), ((), ())), preferred_element_type=F32)
                s = jnp.where(valid, s, -jnp.inf)
                m = jnp.max(s, axis=1, keepdims=True)
                if has_sink:
                    sk = sink_ref[2 * p + half]
                    m = jnp.maximum(m, sk)
                oe = jnp.dot(jnp.exp(s - m).astype(BF16), vp, preferred_element_type=F32)
                den = oe[:, LANES:]
                if has_sink:
                    den = den + jnp.exp(sk - m)
                halves.append((oe[:, :LANES] / den, m + jnp.log(den) if want_lse else None))
            o_ref[q0:q0 + qn, p * LANES:(p + 1) * LANES] = jnp.where(
                low, halves[0][0], halves[1][0]).astype(o_ref.dtype)
            if want_lse:
                lse_ref[q0:q0 + qn, p * LANES:(p + 1) * LANES] = jnp.where(low, halves[0][1], halves[1][1])


def _banded_attention(q_arr, k_arr, v_arr, *, bsz, seq, dil, radius, q_col, k_col, v_col, kv_width,
                      pairs_per_kv, sink=None, want_lse, name):
    length = seq // dil
    hb = radius
    tq = min(256, length)
    assert tq % hb == 0 and length % tq == 0
    qw = DIL_HEADS * HEAD_DIM
    n_pairs = qw // LANES
    rep = tq // hb
    last_hb = length // hb - 1

    def prev_spec(col):
        return pl.BlockSpec((None, None, hb, kv_width),
                            lambda b, r, j: (b, r, jnp.maximum(j * rep - 1, 0), col))

    def cur_spec(col):
        return pl.BlockSpec((None, None, tq, kv_width), lambda b, r, j: (b, r, j, col))

    def next_spec(col):
        return pl.BlockSpec((None, None, hb, kv_width),
                            lambda b, r, j: (b, r, jnp.minimum((j + 1) * rep, last_hb), col))

    in_specs = [pl.BlockSpec((None, None, tq, qw), lambda b, r, j: (b, r, j, q_col)),
                prev_spec(k_col), cur_spec(k_col), next_spec(k_col),
                prev_spec(v_col), cur_spec(v_col), next_spec(v_col)]
    args = [q_arr, k_arr, k_arr, k_arr, v_arr, v_arr, v_arr]
    if sink is not None:
        in_specs = [pl.BlockSpec(memory_space=pltpu.SMEM)] + in_specs
        args = [sink] + args
    out_spec = pl.BlockSpec((None, None, tq, qw), lambda b, r, j: (b, r, j, 0))
    out_specs = [out_spec]
    out_shape = [jax.ShapeDtypeStruct((bsz, dil, length, qw), BF16)]
    if want_lse:
        out_specs.append(out_spec)
        out_shape.append(jax.ShapeDtypeStruct((bsz, dil, length, qw), F32))
    kern = functools.partial(_banded_kernel, radius=radius, tq=tq, hb=hb, length=length, n_pairs=n_pairs,
                             pairs_per_kv=pairs_per_kv, has_sink=sink is not None, want_lse=want_lse)
    return pl.pallas_call(
        kern,
        grid=(bsz, dil, length // tq),
        in_specs=in_specs,
        out_specs=out_specs,
        out_shape=out_shape,
        compiler_params=_cparams(("parallel", "parallel", "parallel")),
        name=name,
    )(*args)


def _router_kernel(x_ref, g_ref, sc_ref, sh_ref, wt_ref, b_ref, tri_ref,
                   idx_ref, gate_ref, rank_ref, cnt_ref, h_ref, carry_ref):
    @pl.when(pl.program_id(0) == 0)
    def _():
        carry_ref[...] = jnp.zeros_like(carry_ref)

    h = _ada_norm(x_ref[...], g_ref[...], sc_ref[...], sh_ref[...])
    h_ref[...] = _pack_bf16_pairs(h)
    lt = lax.dot_general(wt_ref[...], h, (((1,), (1,)), ((), ())), preferred_element_type=F32,
                         precision=lax.Precision.HIGHEST) + b_ref[...]
    n_e, tm = lt.shape
    e_iota = lax.broadcasted_iota(jnp.int32, (n_e, tm), 0)
    vals, sels = [], []
    for k in range(TOP_K):
        m = jnp.max(lt, axis=0, keepdims=True)
        idx = jnp.min(jnp.where(lt == m, e_iota, n_e), axis=0, keepdims=True)
        sel = e_iota == idx
        idx_ref[k:k + 1, :] = idx
        vals.append(m)
        sels.append(sel)
        lt = jnp.where(sel, -jnp.inf, lt)
    exps = [jnp.exp(v - vals[0]) for v in vals]
    den = exps[0] + exps[1] + exps[2] + exps[3]
    for k in range(TOP_K):
        gate_ref[k:k + 1, :] = exps[k] / den

    onehot = jnp.where(sels[0] | sels[1] | sels[2] | sels[3], 1.0, 0.0)
    carry = carry_ref[:, 0:1]
    before = jnp.dot(onehot.astype(BF16), tri_ref[...], preferred_element_type=F32) + carry
    for k in range(TOP_K):
        rank_ref[k:k + 1, :] = jnp.sum(jnp.where(sels[k], before, 0.0), axis=0, keepdims=True).astype(jnp.int32)
    total = carry + jnp.sum(onehot, axis=1, keepdims=True)
    carry_ref[...] = jnp.broadcast_to(total, carry_ref.shape)
    cnt_ref[...] = jnp.broadcast_to(total, cnt_ref.shape)


def _router(x, norm, w_router, b_router, *, seq, tm=512):
    t, d = x.shape
    n_e = w_router.shape[1]
    bpb = seq // tm
    tri = (jnp.arange(tm)[:, None] < jnp.arange(tm)[None, :]).astype(BF16)
    g, sc, sh = norm
    out_i = jax.ShapeDtypeStruct((TOP_K, t), jnp.int32)
    tok = pl.BlockSpec((TOP_K, tm), lambda i: (0, i))
    return pl.pallas_call(
        _router_kernel,
        grid=(t // tm,),
        in_specs=[pl.BlockSpec((tm, d), lambda i: (i, 0)),
                  pl.BlockSpec((1, d), lambda i: (0, 0)),
                  pl.BlockSpec((None, 1, d), lambda i: (i // bpb, 0, 0)),
                  pl.BlockSpec((None, 1, d), lambda i: (i // bpb, 0, 0)),
                  pl.BlockSpec((n_e, d), lambda i: (0, 0)),
                  pl.BlockSpec((n_e, 1), lambda i: (0, 0)),
                  pl.BlockSpec((tm, tm), lambda i: (0, 0))],
        out_specs=[tok, tok, tok, pl.BlockSpec((n_e, LANES), lambda i: (0, 0)),
                   pl.BlockSpec((tm, d // 2), lambda i: (i, 0))],
        out_shape=[out_i, jax.ShapeDtypeStruct((TOP_K, t), F32), out_i,
                   jax.ShapeDtypeStruct((n_e, LANES), F32), jax.ShapeDtypeStruct((t, d // 2), jnp.int32)],
        scratch_shapes=[pltpu.VMEM((n_e, LANES), F32)],
        compiler_params=_cparams(("arbitrary",)),
        name="router",
    )(x, g, sc, sh, w_router.T, b_router.reshape(n_e, 1), tri)


SC_CORES = 2
SC_SUBCORES = 16
SC_WORKERS = SC_CORES * SC_SUBCORES
SC_ROWS = 64


def _sc_worker_id():
    return lax.axis_index("subcore") * SC_CORES + lax.axis_index("core")


def _sc_mesh():
    return plsc.VectorSubcoreMesh(core_axis_name="core", subcore_axis_name="subcore")


def _sc_dispatch(h, dest_flat, pad_rows):
    t, d = h.shape
    n_pad = pad_rows.shape[0]
    n_rows = TOP_K * t + n_pad
    tok_w = t // SC_WORKERS
    pad_w = n_pad // SC_WORKERS
    nb = tok_w // SC_ROWS
    assert tok_w % (2 * SC_ROWS) == 0 and pad_w % SC_ROWS == 0

    @functools.partial(
        pl.kernel, out_type=jax.ShapeDtypeStruct((n_rows, d), h.dtype), mesh=_sc_mesh(),
        scratch_types=[pltpu.VMEM((SC_ROWS,), jnp.int32), pltpu.VMEM((2, SC_ROWS, d), h.dtype),
                       pltpu.SemaphoreType.DMA((2,))])
    def scatter_rows(h_hbm, dest_hbm, pad_hbm, zero_hbm, xs_hbm, idx_v, rows_v, sem):
        wid = _sc_worker_id()
        base = wid * tok_w

        def load(b, slot):
            return pltpu.make_async_copy(h_hbm.at[pl.ds(base + b * SC_ROWS, SC_ROWS)], rows_v.at[slot],
                                         sem.at[slot])

        load(0, 0).start()

        @pl.loop(0, nb, step=2)
        def _(i):
            for slot in range(2):
                b = i + slot
                load(b, slot).wait()

                @pl.when(b + 1 < nb)
                def _():
                    load(b + 1, 1 - slot).start()

                for k in range(TOP_K):
                    pltpu.sync_copy(dest_hbm.at[pl.ds(k * t + base + b * SC_ROWS, SC_ROWS)], idx_v)
                    pltpu.sync_copy(rows_v.at[slot], xs_hbm.at[idx_v])

        pltpu.sync_copy(zero_hbm, rows_v.at[0])

        @pl.loop(0, pad_w // SC_ROWS)
        def _(b):
            pltpu.sync_copy(pad_hbm.at[pl.ds(wid * pad_w + b * SC_ROWS, SC_ROWS)], idx_v)
            pltpu.sync_copy(rows_v.at[0], xs_hbm.at[idx_v])

    return scatter_rows(h, dest_flat, pad_rows, jnp.zeros((SC_ROWS, d), h.dtype))


def _sc_gather(table, idx):
    n = idx.shape[0]
    d = table.shape[1]
    per_w = n // SC_WORKERS
    nb = per_w // SC_ROWS
    assert per_w % (2 * SC_ROWS) == 0

    @functools.partial(
        pl.kernel, out_type=jax.ShapeDtypeStruct((n, d), table.dtype), mesh=_sc_mesh(),
        scratch_types=[pltpu.VMEM((2, SC_ROWS), jnp.int32), pltpu.VMEM((2, SC_ROWS, d), table.dtype),
                       pltpu.SemaphoreType.DMA((2,))])
    def gather_rows(table_hbm, idx_hbm, out_hbm, idx_v, rows_v, sem):
        base = _sc_worker_id() * per_w

        def gather(slot):
            return pltpu.make_async_copy(table_hbm.at[idx_v.at[slot]], rows_v.at[slot], sem.at[slot])

        def start(b, slot):
            pltpu.sync_copy(idx_hbm.at[pl.ds(base + b * SC_ROWS, SC_ROWS)], idx_v.at[slot])
            gather(slot).start()

        start(0, 0)
        start(1, 1)

        @pl.loop(0, nb, step=2)
        def _(i):
            for slot in range(2):
                b = i + slot
                gather(slot).wait()
                pltpu.sync_copy(rows_v.at[slot], out_hbm.at[pl.ds(base + b * SC_ROWS, SC_ROWS)])

                @pl.when(b + 2 < nb)
                def _():
                    start(b + 2, slot)

    return gather_rows(table, idx)


def _expert_kernel(be_ref, nv_ref, sb_ref, xs_ref, wgu_ref, bgu_ref, wd_ref, bd_ref, y_ref, wgu_bf, wd_bf):
    i = pl.program_id(0)
    used = nv_ref[i] > 0

    @pl.when(used & ((i == 0) | (be_ref[i] != be_ref[jnp.maximum(i - 1, 0)])))
    def _():
        wgu_bf[...] = wgu_ref[...].astype(BF16)
        wd_bf[...] = wd_ref[...].astype(BF16)

    @pl.when(used)
    def _():
        d_ff = wd_ref.shape[0]
        x_hi, x_lo = _unpack_bf16_pairs(xs_ref[...])
        x = jnp.concatenate([x_hi.astype(BF16), x_lo.astype(BF16)], axis=1)
        gu = jnp.dot(x, wgu_bf[...], preferred_element_type=F32) + bgu_ref[...]
        glu = jnp.minimum(gu[:, :d_ff], SWIGLU_LIMIT)
        lin = jnp.clip(gu[:, d_ff:], -SWIGLU_LIMIT, SWIGLU_LIMIT)
        act = glu * (1.0 / (1.0 + jnp.exp(-SWIGLU_ALPHA * glu))) * (lin + 1.0)
        y = jnp.dot(act.astype(BF16), wd_bf[...], preferred_element_type=F32) + bd_ref[...]
        y_ref[...] = _pack_bf16_pairs(y)

    @pl.when(jnp.logical_not(used))
    def _():
        y_ref[...] = jnp.zeros_like(y_ref)


def _experts(xs, blk_expert, n_valid, src_blk, layer, w_gu, b_gu, w_down, b_down):
    n_rows, dw = xs.shape
    _, n_e, d, f2 = w_gu.shape
    n_blocks = n_rows // MOE_ROWS

    def wmap(i, be, nv, sb):
        return (layer, be[i], 0, 0)

    grid_spec = pltpu.PrefetchScalarGridSpec(
        num_scalar_prefetch=3,
        grid=(n_blocks,),
        in_specs=[pl.BlockSpec((MOE_ROWS, dw), lambda i, be, nv, sb: (sb[i], 0)),
                  pl.BlockSpec((None, None, d, f2), wmap),
                  pl.BlockSpec((None, None, 1, f2), wmap),
                  pl.BlockSpec((None, None, f2 // 2, d), wmap),
                  pl.BlockSpec((None, None, 1, d), wmap)],
        out_specs=pl.BlockSpec((MOE_ROWS, dw), lambda i, be, nv, sb: (i, 0)),
        scratch_shapes=[pltpu.VMEM((d, f2), BF16), pltpu.VMEM((f2 // 2, d), BF16)],
    )
    depth = w_gu.shape[0]
    return pl.pallas_call(
        _expert_kernel,
        grid_spec=grid_spec,
        out_shape=jax.ShapeDtypeStruct((n_rows, dw), jnp.int32),
        compiler_params=pltpu.CompilerParams(dimension_semantics=("arbitrary",),
                                             vmem_limit_bytes=EXPERT_VMEM_LIMIT),
        name="moe_experts",
    )(blk_expert, n_valid, src_blk, xs, w_gu, b_gu.reshape(depth, n_e, 1, f2), w_down, b_down.reshape(depth, n_e, 1, d))


def _combine_kernel(y0_ref, y1_ref, y2_ref, y3_ref, gate_ref, x_ref, g2_ref, *rest):
    o_ref = rest[-1]
    gate = gate_ref[...]
    w = y0_ref.shape[1]
    y_hi, y_lo = None, None
    for k, y_ref in enumerate((y0_ref, y1_ref, y2_ref, y3_ref)):
        hi, lo = _unpack_bf16_pairs(y_ref[...])
        gk = gate[:, k:k + 1]
        y_hi = gk * hi if y_hi is None else y_hi + gk * hi
        y_lo = gk * lo if y_lo is None else y_lo + gk * lo
    o_ref[:, :w] = x_ref[:, :w] + g2_ref[:, :w] * y_hi
    o_ref[:, w:] = x_ref[:, w:] + g2_ref[:, w:] * y_lo


def _combine(y4, gate, x, g2, prev, *, seq, tok0, tm=512):
    t_all, d = x.shape
    t = y4.shape[0] // TOP_K
    bpb = seq // tm
    nblk = t // tm
    off = tok0 // tm
    y_specs = [pl.BlockSpec((tm, d // 2), functools.partial(lambda i, k: (k * nblk + i, 0), k=k))
               for k in range(TOP_K)]
    in_specs = y_specs + [pl.BlockSpec((tm, TOP_K), lambda i: (i + off, 0)),
                          pl.BlockSpec((tm, d), lambda i: (i + off, 0)),
                          pl.BlockSpec((None, 1, d), lambda i: ((i + off) // bpb, 0, 0))]
    args = [y4, y4, y4, y4, gate, x, g2]
    aliases = {}
    if prev is not None:
        in_specs.append(pl.BlockSpec(memory_space=pl.ANY))
        args.append(prev)
        aliases = {len(args) - 1: 0}
    return pl.pallas_call(
        _combine_kernel,
        grid=(nblk,),
        in_specs=in_specs,
        out_specs=pl.BlockSpec((tm, d), lambda i: (i + off, 0)),
        out_shape=jax.ShapeDtypeStruct((t_all, d), F32),
        input_output_aliases=aliases,
        compiler_params=_cparams(("parallel",)),
        name="moe_combine",
    )(*args)


COMBINE_CHUNKS = 2


def _moe(x, norm, g2, w_router, b_router, layer, w_gu, b_gu, w_down, b_down, *, seq):
    t = x.shape[0]
    n_e = w_router.shape[1]
    idx_t, gate_t, rank_t, counts, h = _router(x, norm, w_router, b_router, seq=seq)
    e_ids = jnp.arange(n_e, dtype=jnp.int32)
    counts = counts[:, 0].astype(jnp.int32)
    padded = (counts + MOE_ROWS - 1) // MOE_ROWS * MOE_ROWS
    pend = jnp.cumsum(padded)
    pstart = pend - padded
    n_pad = n_e * MOE_ROWS
    n_rows = t * TOP_K + n_pad
    n_blocks = n_rows // MOE_ROWS
    dest_t = jnp.sum(jnp.where(idx_t[:, :, None] == e_ids, pstart, 0), axis=-1) + rank_t
    dest_flat = dest_t.reshape(TOP_K * t)
    blk_start = jnp.arange(n_blocks, dtype=jnp.int32) * MOE_ROWS
    n_used = pend[-1] // MOE_ROWS
    src_blk = jnp.minimum(jnp.arange(n_blocks, dtype=jnp.int32), n_used - 1)
    blk_expert = jnp.minimum(jnp.sum((src_blk * MOE_ROWS)[:, None] >= pend[None, :], axis=1), n_e - 1)
    blk_expert = blk_expert.astype(jnp.int32)
    own = blk_expert[:, None] == e_ids
    rows_left = jnp.sum(jnp.where(own, pstart + counts, 0), axis=-1) - blk_start
    n_valid = jnp.where(blk_start < pend[-1], jnp.clip(rows_left, 0, MOE_ROWS), 0).astype(jnp.int32)
    pad_len = padded - counts
    pad_end = jnp.cumsum(pad_len)
    j = jnp.arange(n_pad, dtype=jnp.int32)
    pe = jnp.sum(j[:, None] >= pad_end[None, :], axis=1)
    sel = jnp.minimum(pe, n_e - 1)[:, None] == e_ids
    in_expert = jnp.sum(jnp.where(sel, pstart + counts - (pad_end - pad_len), 0), axis=-1) + j
    pad_rows = jnp.where(pe < n_e, in_expert, pend[-1] + j - pad_end[-1]).astype(jnp.int32)

    xs = _sc_dispatch(h, dest_flat, pad_rows)
    yb = _experts(xs, blk_expert, n_valid, src_blk, layer, w_gu, b_gu, w_down, b_down)
    gate = gate_t.T
    out = None
    tc = t // COMBINE_CHUNKS
    for c in range(COMBINE_CHUNKS):
        y4 = _sc_gather(yb, dest_t[:, c * tc:(c + 1) * tc].reshape(TOP_K * tc))
        out = _combine(y4, gate, x, g2, out, seq=seq, tok0=c * tc)
    return out


def _final_norm_kernel(x_ref, g_ref, o_ref):
    o_ref[...] = _rms(x_ref[...]) * g_ref[...]


def _final_norm(x, g, tm=1024):
    t, d = x.shape
    return pl.pallas_call(
        _final_norm_kernel,
        grid=(t // tm,),
        in_specs=[pl.BlockSpec((tm, d), lambda i: (i, 0)), pl.BlockSpec((1, d), lambda i: (0, 0))],
        out_specs=pl.BlockSpec((tm, d), lambda i: (i, 0)),
        out_shape=jax.ShapeDtypeStruct((t, d), F32),
        compiler_params=_cparams(("parallel",)),
        name="final_norm",
    )(x, g.reshape(1, d))


def _dup_heads(w, n_heads):
    d = w.shape[0]
    return jnp.repeat(w.reshape(d, n_heads, 1, HEAD_DIM), 2, axis=2).reshape(d, n_heads * LANES)


def _mixer_mla_swa(x, norm, g1, cos, sin, w_in, g_q, w_qb, g_kv, w_kvb, sink, w_out, *, bsz, seq):
    d = x.shape[1]
    o = np.cumsum((0, MLA_Q_LORA, MLA_KV_LORA, MLA_ROPE, SWA_Q_HEADS * HEAD_DIM,
                   SWA_KV_HEADS * HEAD_DIM, SWA_KV_HEADS * HEAD_DIM))
    w_cq, w_ckv, w_kr, w_qs, w_ks, w_vs = (w_in[:, o[i]:o[i + 1]] for i in range(6))
    w_lat = jnp.concatenate([w_cq, jnp.zeros((d, LANES), F32), w_ckv], axis=1).astype(BF16)
    lat = _proj([x], [w_lat], prologue="ada", norm=norm, seq=seq, tm=512, tn=w_lat.shape[1],
                out_dtype=F32, name="proj_latent")
    w_b = jnp.concatenate([_pair_layout(w_qs), _pair_layout(_dup_heads(w_ks, SWA_KV_HEADS)),
                           _dup_heads(w_vs, SWA_KV_HEADS),
                           w_kr, jnp.zeros((d, LANES - MLA_ROPE), F32)], axis=1).astype(BF16)
    qkv_b = _proj([x], [w_b], prologue="ada", norm=norm, seq=seq, tm=512, tn=w_b.shape[1], out_dtype=BF16,
                  rope=(cos, sin), rope_pattern=(ROPE_PAIR,) * 10 + (0, 0) + (ROPE_ROTATE,), name="proj_swa")
    kr_col = (w_b.shape[1] - LANES) // LANES
    qd = MLA_NOPE + MLA_ROPE
    w_q = w_qb.reshape(MLA_Q_LORA, MLA_HEADS, qd)
    w_q = jnp.concatenate([w_q, jnp.zeros((MLA_Q_LORA, MLA_HEADS, 2 * LANES - qd), F32)], axis=2)
    w_q = w_q.reshape(MLA_Q_LORA, MLA_HEADS * 2 * LANES).astype(BF16)
    q_cat = _proj([lat], [w_q], prologue="rms", norm=(g_q.reshape(1, -1),), lhs_cols=(MLA_Q_LORA, 0),
                  seq=seq, tm=512, tn=w_q.shape[1], out_dtype=BF16, rope=(cos, sin),
                  rope_pattern=(0, 1) * MLA_HEADS, out_scale=qd ** -0.5, name="proj_mla_q")
    w_kv = w_kvb.reshape(MLA_KV_LORA, MLA_HEADS, MLA_NOPE + MLA_V)
    w_kv = jnp.concatenate([w_kv[:, :, :MLA_NOPE].reshape(MLA_KV_LORA, -1),
                            w_kv[:, :, MLA_NOPE:].reshape(MLA_KV_LORA, -1)], axis=1).astype(BF16)
    kv = _proj([lat], [w_kv], prologue="rms", norm=(g_kv.reshape(1, -1),), lhs_cols=(MLA_KV_LORA, 2),
               seq=seq, tm=512, tn=w_kv.shape[1], out_dtype=BF16, name="proj_mla_kv")
    o_a = _mla_attention(q_cat, kv, qkv_b, kr_col, bsz=bsz, seq=seq, tq=256)
    qkv_b4 = qkv_b.reshape(bsz, 1, seq, qkv_b.shape[1])
    (o_b,) = _banded_attention(qkv_b4, qkv_b4, qkv_b4, bsz=bsz, seq=seq, dil=1, radius=SWA_RADIUS,
                               q_col=0, k_col=4, v_col=5, kv_width=2 * LANES,
                               pairs_per_kv=SWA_Q_HEADS // SWA_KV_HEADS // 2, sink=sink, want_lse=False,
                               name="swa_attention")
    o_b = o_b.reshape(bsz * seq, SWA_Q_HEADS * HEAD_DIM)
    na = MLA_HEADS * MLA_V
    return _proj([o_a, o_b], [w_out[:na].astype(BF16), w_out[na:].astype(BF16)], prologue="plain",
                 seq=seq, tm=512, tn=d, out_dtype=F32, residual=(x, g1), name="proj_out_ab")


def _mixer_dilated(x, norm, g1, cos, sin, w_in, w_out, *, bsz, seq):
    d = x.shape[1]
    qw = DIL_HEADS * HEAD_DIM
    w_bf = w_in.astype(BF16)
    outs, lses = [], []
    for g, (window, dil) in enumerate(DIL_PATTERN):
        w_g = w_bf[:, g * 3 * qw:(g + 1) * 3 * qw]
        w_g = jnp.concatenate([_pair_layout(w_g[:, :2 * qw]), w_g[:, 2 * qw:]], axis=1)
        qkv = _proj([x], [w_g], prologue="ada", norm=norm, seq=seq, tm=512, tn=3 * qw, sub_tn=qw, out_dtype=BF16,
                    rope=(cos, sin), rope_pattern=(ROPE_PAIR,) * (2 * qw // LANES) + (0,) * (qw // LANES),
                    out_dil=dil, name=f"proj_dil_{dil}")
        qkv = qkv.reshape(bsz, dil, seq // dil, 3 * qw)
        o, lse = _banded_attention(qkv, qkv, qkv, bsz=bsz, seq=seq, dil=dil, radius=window // (2 * dil),
                                   q_col=0, k_col=1, v_col=2, kv_width=qw, pairs_per_kv=1,
                                   want_lse=True, name=f"dilated_attention_{dil}")
        outs.append(o)
        lses.append(lse)
    return _proj(outs + lses, [w_out.astype(BF16)], prologue="dil", seq=seq, tm=512, tn=d, out_dtype=F32,
                 residual=(x, g1), in_dils=tuple(dil for _, dil in DIL_PATTERN), name="proj_out_c")


def kernel(x, c, positions, w_mod, b_mod, g_norm_mix, g_norm_ffn, w_in_ab, mla_g_q, mla_w_qb, mla_g_kv,
           mla_w_kvb, swa_sink, w_out_ab, w_in_c, w_out_c, w_router, b_router, w_gu, b_gu, w_down, b_down,
           g_final):
    bsz, seq, d = x.shape
    depth = w_mod.shape[0]
    cos, sin = _rope_tables(positions)
    mod = _modulation(c, w_mod, b_mod)
    xt = x.reshape(bsz * seq, d)
    for layer in range(depth):
        sh1, sc1, g1, sh2, sc2, g2 = (mod[layer, :, i * d:(i + 1) * d].reshape(bsz, 1, d) for i in range(6))
        li = layer // 2
        norm = (g_norm_mix[layer].reshape(1, d), sc1, sh1)
        if layer % 2 == 0:
            xt = _mixer_mla_swa(xt, norm, g1, cos, sin, w_in_ab[li], mla_g_q[li], mla_w_qb[li], mla_g_kv[li],
                                mla_w_kvb[li], swa_sink[li], w_out_ab[li], bsz=bsz, seq=seq)
        else:
            xt = _mixer_dilated(xt, norm, g1, cos, sin, w_in_c[li], w_out_c[li], bsz=bsz, seq=seq)
        norm = (g_norm_ffn[layer].reshape(1, d), sc2, sh2)
        xt = _moe(xt, norm, g2, w_router[layer], b_router[layer], layer, w_gu, b_gu, w_down, b_down, seq=seq)
    return _final_norm(xt, g_final).reshape(bsz, seq, d)
```

```python
import functools

import jax
import jax.numpy as jnp
import numpy as np
from jax import lax
from jax.experimental import pallas as pl
from jax.experimental.pallas import tpu as pltpu
from jax.experimental.pallas import tpu_sc as plsc

F32 = jnp.float32
BF16 = jnp.bfloat16

EPS = 1e-6
ROPE_THETA = 10000.0
LANES = 128
HEAD_DIM = 64
ROPE_HALF = HEAD_DIM // 2

MLA_HEADS = 8
MLA_Q_LORA = 384
MLA_KV_LORA = 256
MLA_NOPE = 128
MLA_ROPE = 64
MLA_V = 128
SWA_Q_HEADS = 16
SWA_KV_HEADS = 2
SWA_RADIUS = 128
DIL_PATTERN = ((128, 1), (512, 4), (2048, 16))
DIL_HEADS = 16
N_EXPERTS = 32
TOP_K = 4
SWIGLU_LIMIT = 7.0
SWIGLU_ALPHA = 1.702
MOE_ROWS = 512
BAND_SUB_ROWS = 128
LOG2_E = float(np.log2(np.e))
QK_LOG2_SCALE = HEAD_DIM ** -0.5 * LOG2_E

VMEM_LIMIT = 48 * 1024 * 1024
EXPERT_VMEM_LIMIT = 56 * 1024 * 1024


def _cparams(sem):
    return pltpu.CompilerParams(dimension_semantics=sem, vmem_limit_bytes=VMEM_LIMIT)


def _mod_kernel(c_ref, w_ref, b_ref, o_ref):
    c = c_ref[...]
    ca = c * (1.0 / (1.0 + jnp.exp(-c)))
    o_ref[...] = jnp.dot(ca, w_ref[...], preferred_element_type=F32,
                         precision=lax.Precision.HIGHEST) + b_ref[...]


def _modulation(c, w_mod, b_mod):
    depth, d, n = w_mod.shape
    bsz = c.shape[0]
    tn = 1536
    return pl.pallas_call(
        _mod_kernel,
        grid=(depth, n // tn),
        in_specs=[pl.BlockSpec((bsz, d), lambda l, j: (0, 0)),
                  pl.BlockSpec((None, d, tn), lambda l, j: (l, 0, j)),
                  pl.BlockSpec((None, 1, tn), lambda l, j: (l, 0, j))],
        out_specs=pl.BlockSpec((None, bsz, tn), lambda l, j: (l, 0, j)),
        out_shape=jax.ShapeDtypeStruct((depth, bsz, n), F32),
        compiler_params=_cparams(("parallel", "parallel")),
        name="modulation",
    )(c, w_mod, b_mod.reshape(depth, 1, n))


def _rope_table_kernel(pos_ref, inv_ref, sign_ref, cos_ref, sin_ref):
    ang = inv_ref[...] * pos_ref[...].astype(F32)
    cos_ref[...] = jnp.transpose(jnp.cos(ang))
    sin_ref[...] = jnp.transpose(jnp.sin(ang) * sign_ref[...])


def _rope_tables(positions):
    bsz, s = positions.shape
    inv = ROPE_THETA ** (-jnp.arange(0, HEAD_DIM, 2, dtype=F32) / HEAD_DIM)
    inv128 = jnp.tile(inv, LANES // ROPE_HALF).reshape(LANES, 1)
    sign = jnp.where((jnp.arange(LANES) % HEAD_DIM) < ROPE_HALF, -1.0, 1.0).astype(F32).reshape(LANES, 1)
    out = jax.ShapeDtypeStruct((bsz * s, LANES), F32)
    return pl.pallas_call(
        _rope_table_kernel,
        grid=(bsz,),
        in_specs=[pl.BlockSpec((None, 1, s), lambda b: (b, 0, 0)),
                  pl.BlockSpec((LANES, 1), lambda b: (0, 0)),
                  pl.BlockSpec((LANES, 1), lambda b: (0, 0))],
        out_specs=[pl.BlockSpec((s, LANES), lambda b: (b, 0)),
                   pl.BlockSpec((s, LANES), lambda b: (b, 0))],
        out_shape=[out, out],
        compiler_params=_cparams(("parallel",)),
        name="rope_tables",
    )(positions.reshape(bsz, 1, s), inv128, sign)


def _rms(x):
    return x * lax.rsqrt(jnp.mean(x * x, axis=-1, keepdims=True) + EPS)


def _ada_norm(x, g, sc, sh):
    return (_rms(x) * g) * (1.0 + sc) + sh


def _pack_bf16_pairs(x):
    w = x.shape[1] // 2
    hi = pltpu.bitcast(x[:, :w].astype(BF16).astype(F32), jnp.int32)
    lo = pltpu.bitcast(x[:, w:].astype(BF16).astype(F32), jnp.int32)
    return hi | lax.shift_right_logical(lo, 16)


def _unpack_bf16_pairs(p):
    hi = pltpu.bitcast(p & jnp.int32(-65536), F32)
    lo = pltpu.bitcast(lax.shift_left(p, 16), F32)
    return hi, lo


ROPE_ROTATE, ROPE_PAIR = 1, 2


def _pair_layout(w):
    k, n = w.shape
    w = w.reshape(k, n // LANES, 2, 2, ROPE_HALF)
    return jnp.swapaxes(w, 2, 3).reshape(k, n)


def _rope_group(a, cos, sin, first_half):
    rot = jnp.where(first_half, pltpu.roll(a, LANES - ROPE_HALF, 1), pltpu.roll(a, ROPE_HALF, 1))
    return a * cos + rot * sin


def _proj_kernel(*refs, prologue, n_lhs, rope_pattern, group_scales, sub_tn, out_scale, residual, in_dils,
                 out_dil):
    pos = 0
    if prologue == "ada":
        x_ref, g_ref, sc_ref, sh_ref = refs[:4]
        pos = 4
    elif prologue == "rms":
        x_ref, g_ref = refs[:2]
        pos = 2
    elif prologue == "dil":
        o_refs = refs[0:3]
        l_refs = refs[3:6]
        pos = 6
    else:
        x_refs = refs[:n_lhs]
        pos = n_lhs
    w_refs = refs[pos:pos + n_lhs]
    pos += n_lhs
    if rope_pattern is not None:
        cos_ref, sin_ref = refs[pos:pos + 2]
        pos += 2
    if residual:
        res_ref, gate_ref = refs[pos:pos + 2]
        pos += 2
    o_ref = refs[pos]
    pos += 1
    h_ref = refs[pos] if prologue != "plain" else None
    slab_ref = refs[pos + 1] if (prologue == "dil" or out_dil > 1) else None

    if prologue != "plain":
        @pl.when(pl.program_id(1) == 0)
        def _():
            if prologue == "ada":
                h_ref[...] = _ada_norm(x_ref[...], g_ref[...], sc_ref[...], sh_ref[...]).astype(BF16)
            elif prologue == "rms":
                h_ref[...] = (_rms(x_ref[...]) * g_ref[...]).astype(BF16)
            else:
                tm = h_ref.shape[0]
                for c in range(h_ref.shape[1] // LANES):
                    cols = slice(c * LANES, (c + 1) * LANES)
                    vals = []
                    for i, (ref, dil) in enumerate(zip(o_refs + l_refs, in_dils + in_dils)):
                        if dil == 1:
                            vals.append(ref[0, :, cols].astype(F32))
                        else:
                            for r in range(dil):
                                slab_ref[i, pl.ds(r, tm // dil, stride=dil), :] = ref[r, :, cols].astype(F32)
                            vals.append(slab_ref[i])
                    o0, o1, o2, l0, l1, l2 = vals
                    m = jnp.maximum(jnp.maximum(l0, l1), l2)
                    e0, e1, e2 = jnp.exp2(l0 - m), jnp.exp2(l1 - m), jnp.exp2(l2 - m)
                    h_ref[:, cols] = ((e0 * o0 + e1 * o1 + e2 * o2) / (e0 + e1 + e2)).astype(BF16)

    n_total = w_refs[0].shape[1]
    tn = n_total if sub_tn is None else sub_tn

    def matmul(t):
        cols = slice(t * tn, (t + 1) * tn)
        if prologue != "plain":
            acc = jnp.dot(h_ref[...], w_refs[0][:, cols], preferred_element_type=F32)
        else:
            acc = jnp.dot(x_refs[0][...], w_refs[0][:, cols], preferred_element_type=F32)
            for xr, wr in zip(x_refs[1:], w_refs[1:]):
                acc = acc + jnp.dot(xr[...], wr[:, cols], preferred_element_type=F32)
        return acc * out_scale if out_scale != 1.0 else acc

    if residual:
        o_ref[...] = (res_ref[...] + gate_ref[...] * matmul(0)).astype(o_ref.dtype)
        return

    pattern = rope_pattern if rope_pattern is not None else (0,) * (n_total // LANES)
    if any(pattern):
        cos = cos_ref[...]
        sin = sin_ref[...]
        lane = lax.broadcasted_iota(jnp.int32, cos.shape, 1)
        first_half = (lane % HEAD_DIM) < ROPE_HALF
        if ROPE_PAIR in pattern:
            sin_pair = jnp.where(lane < HEAD_DIM, -1.0, 1.0) * jnp.where(first_half, -sin, sin)

    def epilogue(acc, t):
        tm = acc.shape[0]
        for c in range(tn // LANES):
            g = t * (tn // LANES) + c
            cols = slice(g * LANES, (g + 1) * LANES)
            a = acc[:, c * LANES:(c + 1) * LANES]
            if pattern[g] == ROPE_ROTATE:
                a = _rope_group(a, cos, sin, first_half)
            elif pattern[g] == ROPE_PAIR:
                a = a * cos + pltpu.roll(a, HEAD_DIM, 1) * sin_pair
            if group_scales is not None and group_scales[g] != 1.0:
                a = a * group_scales[g]
            if out_dil == 1:
                o_ref[:, cols] = a.astype(o_ref.dtype)
            else:
                slab_ref[0] = a
                for r in range(out_dil):
                    o_ref[r, :, cols] = slab_ref[0, pl.ds(r, tm // out_dil, stride=out_dil), :].astype(o_ref.dtype)

    acc_next = matmul(0)
    for t in range(n_total // tn):
        acc = acc_next
        if t + 1 < n_total // tn:
            acc_next = matmul(t + 1)
        epilogue(acc, t)


def _proj(lhs, ws, *, prologue, seq, tm, tn, out_dtype, norm=None, lhs_cols=None, rope=None,
          rope_pattern=None, group_scales=None, sub_tn=None, out_scale=1.0, residual=None, in_dils=None,
          out_dil=1, name):
    k, n = ws[0].shape
    bsz_seq = lhs[0].shape[0] if prologue != "dil" else lhs[0].shape[0] * seq
    t = bsz_seq
    assert t % tm == 0 and n % tn == 0 and seq % tm == 0
    bpb = seq // tm
    n_lhs = len(ws)
    args, specs = [], []
    row = lambda i, j: (i, 0)
    if prologue in ("ada", "rms"):
        kw, kc = lhs_cols if lhs_cols is not None else (k, 0)
        args.append(lhs[0])
        specs.append(pl.BlockSpec((tm, kw), lambda i, j: (i, kc)))
        args.append(norm[0])
        specs.append(pl.BlockSpec((1, k), lambda i, j: (0, 0)))
        if prologue == "ada":
            for v in norm[1:3]:
                args.append(v)
                specs.append(pl.BlockSpec((None, 1, k), lambda i, j: (i // bpb, 0, 0)))
    elif prologue == "dil":
        for a, dil in zip(lhs, in_dils + in_dils):
            args.append(a)
            specs.append(pl.BlockSpec((None, dil, tm // dil, k), lambda i, j: (i // bpb, 0, i % bpb, 0)))
    else:
        for a, w in zip(lhs, ws):
            args.append(a)
            specs.append(pl.BlockSpec((tm, w.shape[0]), row))
    for w in ws:
        args.append(w)
        specs.append(pl.BlockSpec((w.shape[0], tn), lambda i, j: (0, j)))
    if rope_pattern is not None:
        assert len(rope_pattern) == tn // LANES
        for tab in rope:
            args.append(tab)
            specs.append(pl.BlockSpec((tm, LANES), row))
    if residual is not None:
        res, gate = residual
        args.append(res)
        specs.append(pl.BlockSpec((tm, tn), lambda i, j: (i, j)))
        args.append(gate)
        specs.append(pl.BlockSpec((None, 1, tn), lambda i, j: (i // bpb, 0, j)))
    scratch = [] if prologue == "plain" else [pltpu.VMEM((tm, k), BF16)]
    if prologue == "dil":
        scratch.append(pltpu.VMEM((2 * len(in_dils), tm, LANES), F32))
    elif out_dil > 1:
        scratch.append(pltpu.VMEM((1, tm, LANES), F32))
    if out_dil == 1:
        out_spec = pl.BlockSpec((tm, tn), lambda i, j: (i, j))
        out_shape = jax.ShapeDtypeStruct((t, n), out_dtype)
    else:
        out_spec = pl.BlockSpec((None, out_dil, tm // out_dil, tn), lambda i, j: (i // bpb, 0, i % bpb, j))
        out_shape = jax.ShapeDtypeStruct((t // seq, out_dil, seq // out_dil, n), out_dtype)
    kern = functools.partial(_proj_kernel, prologue=prologue, n_lhs=n_lhs, rope_pattern=rope_pattern,
                             group_scales=group_scales, sub_tn=sub_tn, out_scale=out_scale,
                             residual=residual is not None,
                             in_dils=in_dils, out_dil=out_dil)
    return pl.pallas_call(
        kern,
        grid=(t // tm, n // tn),
        in_specs=specs,
        out_specs=out_spec,
        out_shape=out_shape,
        scratch_shapes=scratch,
        compiler_params=_cparams(("parallel", "arbitrary")),
        name=name,
    )(*args)


MLA_HEADS_PER_STEP = 4


def _mla_scores(q_ref, kn_ref, kr, h):
    k = jnp.concatenate([kn_ref[:, h * LANES:(h + 1) * LANES], kr], axis=1)
    q = q_ref[:, h * 2 * LANES:(h + 1) * 2 * LANES]
    return lax.dot_general(q, k, (((1,), (1,)), ((), ())), preferred_element_type=F32)


def _mla_kernel(q_ref, kn_ref, kr_ref, v_ref, o_ref):
    kr = kr_ref[...]
    s_next = _mla_scores(q_ref, kn_ref, kr, 0)
    for h in range(MLA_HEADS_PER_STEP):
        s = s_next
        if h + 1 < MLA_HEADS_PER_STEP:
            s_next = _mla_scores(q_ref, kn_ref, kr, h + 1)
        m = jnp.max(s, axis=1, keepdims=True)
        p = jnp.exp(s - m).astype(BF16)
        v = v_ref[:, h * LANES:(h + 1) * LANES]
        oe = jnp.dot(p, jnp.concatenate([v, jnp.ones_like(v)], axis=1), preferred_element_type=F32)
        o_ref[:, h * LANES:(h + 1) * LANES] = (oe[:, :LANES] / oe[:, LANES:]).astype(o_ref.dtype)


def _mla_attention(q_cat, kv, qkv_b, kr_col, *, bsz, seq, tq):
    t = bsz * seq
    nq = seq // tq
    hps = MLA_HEADS_PER_STEP
    groups = MLA_HEADS // hps
    return pl.pallas_call(
        _mla_kernel,
        grid=(bsz, groups, nq),
        in_specs=[pl.BlockSpec((tq, hps * 2 * LANES), lambda b, h, i: (b * nq + i, h)),
                  pl.BlockSpec((seq, hps * LANES), lambda b, h, i: (b, h)),
                  pl.BlockSpec((seq, LANES), lambda b, h, i: (b, kr_col)),
                  pl.BlockSpec((seq, hps * LANES), lambda b, h, i: (b, groups + h))],
        out_specs=pl.BlockSpec((tq, hps * LANES), lambda b, h, i: (b * nq + i, h)),
        out_shape=jax.ShapeDtypeStruct((t, MLA_HEADS * MLA_V), BF16),
        compiler_params=_cparams(("parallel", "parallel", "parallel")),
        name="mla_attention",
    )(q_cat, kv, qkv_b, kv)


def _banded_kernel(*refs, radius, tq, hb, length, n_pairs, pairs_per_kv, has_sink, want_lse):
    pos = 0
    if has_sink:
        sink_ref = refs[0]
        pos = 1
    q_ref, kp_ref, kc_ref, kn_ref, vp_ref, vc_ref, vn_ref = refs[pos:pos + 7]
    pos += 7
    o_ref = refs[pos]
    lse_ref = refs[pos + 1] if want_lse else None

    j = pl.program_id(2)
    kw = jnp.concatenate([kp_ref[...], kc_ref[...], kn_ref[...]], axis=0)
    vw = jnp.concatenate([vp_ref[...], vc_ref[...], vn_ref[...]], axis=0)
    qn = min(tq, BAND_SUB_ROWS)
    kn = qn + 2 * hb
    lane = lax.broadcasted_iota(jnp.int32, (qn, LANES), 1)
    low = lane < HEAD_DIM
    first = (lane % HEAD_DIM) < ROPE_HALF

    for q0 in range(0, tq, qn):
        qpos = j * tq + q0 + lax.broadcasted_iota(jnp.int32, (qn, kn), 0)
        kpos = j * tq - hb + q0 + lax.broadcasted_iota(jnp.int32, (qn, kn), 1)
        valid = (jnp.abs(qpos - kpos) <= radius) & (kpos >= 0) & (kpos < length)
        for p in range(n_pairs):
            c = p // pairs_per_kv
            qp = q_ref[q0:q0 + qn, p * LANES:(p + 1) * LANES]
            kp = kw[q0:q0 + kn, c * LANES:(c + 1) * LANES]
            vp = vw[q0:q0 + kn, c * LANES:(c + 1) * LANES]
            vp = jnp.concatenate([vp, jnp.ones_like(vp)], axis=1)
            halves = []
            for half in range(2):
                qh = jnp.where(first if half == 0 else ~first, qp, jnp.zeros_like(qp))
                s = lax.dot_general(qh, kp, (((1,), (1,)), ((), ())), preferred_element_type=F32)
                s = jnp.where(valid, s, -jnp.inf)
                m = jnp.max(s, axis=1, keepdims=True)
                if has_sink:
                    sk = sink_ref[2 * p + half] * LOG2_E
                    m = jnp.maximum(m, sk)
                oe = jnp.dot(jnp.exp2(s - m).astype(BF16), vp, preferred_element_type=F32)
                den = oe[:, LANES:]
                if has_sink:
                    den = den + jnp.exp2(sk - m)
                halves.append((oe[:, :LANES] / den, m + jnp.log2(den) if want_lse else None))
            o_ref[q0:q0 + qn, p * LANES:(p + 1) * LANES] = jnp.where(
                low, halves[0][0], halves[1][0]).astype(o_ref.dtype)
            if want_lse:
                lse_ref[q0:q0 + qn, p * LANES:(p + 1) * LANES] = jnp.where(low, halves[0][1], halves[1][1])


def _banded_attention(q_arr, k_arr, v_arr, *, bsz, seq, dil, radius, q_col, k_col, v_col, kv_width,
                      pairs_per_kv, sink=None, want_lse, name):
    length = seq // dil
    hb = radius
    tq = min(256, length)
    assert tq % hb == 0 and length % tq == 0
    qw = DIL_HEADS * HEAD_DIM
    n_pairs = qw // LANES
    rep = tq // hb
    last_hb = length // hb - 1

    def prev_spec(col):
        return pl.BlockSpec((None, None, hb, kv_width),
                            lambda b, r, j: (b, r, jnp.maximum(j * rep - 1, 0), col))

    def cur_spec(col):
        return pl.BlockSpec((None, None, tq, kv_width), lambda b, r, j: (b, r, j, col))

    def next_spec(col):
        return pl.BlockSpec((None, None, hb, kv_width),
                            lambda b, r, j: (b, r, jnp.minimum((j + 1) * rep, last_hb), col))

    in_specs = [pl.BlockSpec((None, None, tq, qw), lambda b, r, j: (b, r, j, q_col)),
                prev_spec(k_col), cur_spec(k_col), next_spec(k_col),
                prev_spec(v_col), cur_spec(v_col), next_spec(v_col)]
    args = [q_arr, k_arr, k_arr, k_arr, v_arr, v_arr, v_arr]
    if sink is not None:
        in_specs = [pl.BlockSpec(memory_space=pltpu.SMEM)] + in_specs
        args = [sink] + args
    out_spec = pl.BlockSpec((None, None, tq, qw), lambda b, r, j: (b, r, j, 0))
    out_specs = [out_spec]
    out_shape = [jax.ShapeDtypeStruct((bsz, dil, length, qw), BF16)]
    if want_lse:
        out_specs.append(out_spec)
        out_shape.append(jax.ShapeDtypeStruct((bsz, dil, length, qw), F32))
    kern = functools.partial(_banded_kernel, radius=radius, tq=tq, hb=hb, length=length, n_pairs=n_pairs,
                             pairs_per_kv=pairs_per_kv, has_sink=sink is not None, want_lse=want_lse)
    return pl.pallas_call(
        kern,
        grid=(bsz, dil, length // tq),
        in_specs=in_specs,
        out_specs=out_specs,
        out_shape=out_shape,
        compiler_params=_cparams(("parallel", "parallel", "parallel")),
        name=name,
    )(*args)


def _router_kernel(x_ref, g_ref, sc_ref, sh_ref, wt_ref, b_ref, tri_ref,
                   idx_ref, gate_ref, rank_ref, cnt_ref, h_ref, carry_ref):
    @pl.when(pl.program_id(0) == 0)
    def _():
        carry_ref[...] = jnp.zeros_like(carry_ref)

    h = _ada_norm(x_ref[...], g_ref[...], sc_ref[...], sh_ref[...])
    h_ref[...] = _pack_bf16_pairs(h)
    lt = lax.dot_general(wt_ref[...], h, (((1,), (1,)), ((), ())), preferred_element_type=F32,
                         precision=lax.Precision.HIGHEST) + b_ref[...]
    n_e, tm = lt.shape
    e_iota = lax.broadcasted_iota(jnp.int32, (n_e, tm), 0)
    vals, sels = [], []
    for k in range(TOP_K):
        m = jnp.max(lt, axis=0, keepdims=True)
        idx = jnp.min(jnp.where(lt == m, e_iota, n_e), axis=0, keepdims=True)
        sel = e_iota == idx
        idx_ref[k:k + 1, :] = idx
        vals.append(m)
        sels.append(sel)
        lt = jnp.where(sel, -jnp.inf, lt)
    exps = [jnp.exp(v - vals[0]) for v in vals]
    den = exps[0] + exps[1] + exps[2] + exps[3]
    for k in range(TOP_K):
        gate_ref[k:k + 1, :] = exps[k] / den

    onehot = jnp.where(sels[0] | sels[1] | sels[2] | sels[3], 1.0, 0.0)
    carry = carry_ref[:, 0:1]
    before = jnp.dot(onehot.astype(BF16), tri_ref[...], preferred_element_type=F32) + carry
    for k in range(TOP_K):
        rank_ref[k:k + 1, :] = jnp.sum(jnp.where(sels[k], before, 0.0), axis=0, keepdims=True).astype(jnp.int32)
    total = carry + jnp.sum(onehot, axis=1, keepdims=True)
    carry_ref[...] = jnp.broadcast_to(total, carry_ref.shape)
    cnt_ref[...] = jnp.broadcast_to(total, cnt_ref.shape)


def _router(x, norm, w_router, b_router, *, seq, tm=512):
    t, d = x.shape
    n_e = w_router.shape[1]
    bpb = seq // tm
    tri = (jnp.arange(tm)[:, None] < jnp.arange(tm)[None, :]).astype(BF16)
    g, sc, sh = norm
    out_i = jax.ShapeDtypeStruct((TOP_K, t), jnp.int32)
    tok = pl.BlockSpec((TOP_K, tm), lambda i: (0, i))
    return pl.pallas_call(
        _router_kernel,
        grid=(t // tm,),
        in_specs=[pl.BlockSpec((tm, d), lambda i: (i, 0)),
                  pl.BlockSpec((1, d), lambda i: (0, 0)),
                  pl.BlockSpec((None, 1, d), lambda i: (i // bpb, 0, 0)),
                  pl.BlockSpec((None, 1, d), lambda i: (i // bpb, 0, 0)),
                  pl.BlockSpec((n_e, d), lambda i: (0, 0)),
                  pl.BlockSpec((n_e, 1), lambda i: (0, 0)),
                  pl.BlockSpec((tm, tm), lambda i: (0, 0))],
        out_specs=[tok, tok, tok, pl.BlockSpec((n_e, LANES), lambda i: (0, 0)),
                   pl.BlockSpec((tm, d // 2), lambda i: (i, 0))],
        out_shape=[out_i, jax.ShapeDtypeStruct((TOP_K, t), F32), out_i,
                   jax.ShapeDtypeStruct((n_e, LANES), F32), jax.ShapeDtypeStruct((t, d // 2), jnp.int32)],
        scratch_shapes=[pltpu.VMEM((n_e, LANES), F32)],
        compiler_params=_cparams(("arbitrary",)),
        name="router",
    )(x, g, sc, sh, w_router.T, b_router.reshape(n_e, 1), tri)


SC_CORES = 2
SC_SUBCORES = 16
SC_WORKERS = SC_CORES * SC_SUBCORES
SC_ROWS = 64


def _sc_worker_id():
    return lax.axis_index("subcore") * SC_CORES + lax.axis_index("core")


def _sc_mesh():
    return plsc.VectorSubcoreMesh(core_axis_name="core", subcore_axis_name="subcore")


def _sc_dispatch(h, dest_flat, pad_rows):
    t, d = h.shape
    n_pad = pad_rows.shape[0]
    n_rows = TOP_K * t + n_pad
    tok_w = t // SC_WORKERS
    pad_w = n_pad // SC_WORKERS
    nb = tok_w // SC_ROWS
    assert tok_w % (2 * SC_ROWS) == 0 and pad_w % SC_ROWS == 0

    @functools.partial(
        pl.kernel, out_type=jax.ShapeDtypeStruct((n_rows, d), h.dtype), mesh=_sc_mesh(),
        scratch_types=[pltpu.VMEM((SC_ROWS,), jnp.int32), pltpu.VMEM((2, SC_ROWS, d), h.dtype),
                       pltpu.SemaphoreType.DMA((2,))])
    def scatter_rows(h_hbm, dest_hbm, pad_hbm, zero_hbm, xs_hbm, idx_v, rows_v, sem):
        wid = _sc_worker_id()
        base = wid * tok_w

        def load(b, slot):
            return pltpu.make_async_copy(h_hbm.at[pl.ds(base + b * SC_ROWS, SC_ROWS)], rows_v.at[slot],
                                         sem.at[slot])

        load(0, 0).start()

        @pl.loop(0, nb, step=2)
        def _(i):
            for slot in range(2):
                b = i + slot
                load(b, slot).wait()

                @pl.when(b + 1 < nb)
                def _():
                    load(b + 1, 1 - slot).start()

                for k in range(TOP_K):
                    pltpu.sync_copy(dest_hbm.at[pl.ds(k * t + base + b * SC_ROWS, SC_ROWS)], idx_v)
                    pltpu.sync_copy(rows_v.at[slot], xs_hbm.at[idx_v])

        pltpu.sync_copy(zero_hbm, rows_v.at[0])

        @pl.loop(0, pad_w // SC_ROWS)
        def _(b):
            pltpu.sync_copy(pad_hbm.at[pl.ds(wid * pad_w + b * SC_ROWS, SC_ROWS)], idx_v)
            pltpu.sync_copy(rows_v.at[0], xs_hbm.at[idx_v])

    return scatter_rows(h, dest_flat, pad_rows, jnp.zeros((SC_ROWS, d), h.dtype))


def _sc_gather(table, idx):
    n = idx.shape[0]
    d = table.shape[1]
    per_w = n // SC_WORKERS
    nb = per_w // SC_ROWS
    assert per_w % (2 * SC_ROWS) == 0

    @functools.partial(
        pl.kernel, out_type=jax.ShapeDtypeStruct((n, d), table.dtype), mesh=_sc_mesh(),
        scratch_types=[pltpu.VMEM((2, SC_ROWS), jnp.int32), pltpu.VMEM((2, SC_ROWS, d), table.dtype),
                       pltpu.SemaphoreType.DMA((2,))])
    def gather_rows(table_hbm, idx_hbm, out_hbm, idx_v, rows_v, sem):
        base = _sc_worker_id() * per_w

        def gather(slot):
            return pltpu.make_async_copy(table_hbm.at[idx_v.at[slot]], rows_v.at[slot], sem.at[slot])

        def start(b, slot):
            pltpu.sync_copy(idx_hbm.at[pl.ds(base + b * SC_ROWS, SC_ROWS)], idx_v.at[slot])
            gather(slot).start()

        start(0, 0)
        start(1, 1)

        @pl.loop(0, nb, step=2)
        def _(i):
            for slot in range(2):
                b = i + slot
                gather(slot).wait()
                pltpu.sync_copy(rows_v.at[slot], out_hbm.at[pl.ds(base + b * SC_ROWS, SC_ROWS)])

                @pl.when(b + 2 < nb)
                def _():
                    start(b + 2, slot)

    return gather_rows(table, idx)


def _expert_kernel(be_ref, nv_ref, sb_ref, xs_ref, wgu_ref, bgu_ref, wd_ref, bd_ref, y_ref, wgu_bf, wd_bf):
    i = pl.program_id(0)
    used = nv_ref[i] > 0

    @pl.when(used & ((i == 0) | (be_ref[i] != be_ref[jnp.maximum(i - 1, 0)])))
    def _():
        wgu_bf[...] = wgu_ref[...].astype(BF16)
        wd_bf[...] = wd_ref[...].astype(BF16)

    @pl.when(used)
    def _():
        d_ff = wd_ref.shape[0]
        x_hi, x_lo = _unpack_bf16_pairs(xs_ref[...])
        x = jnp.concatenate([x_hi.astype(BF16), x_lo.astype(BF16)], axis=1)
        gu = jnp.dot(x, wgu_bf[...], preferred_element_type=F32) + bgu_ref[...]
        glu = jnp.minimum(gu[:, :d_ff], SWIGLU_LIMIT)
        lin = jnp.clip(gu[:, d_ff:], -SWIGLU_LIMIT, SWIGLU_LIMIT)
        act = glu * (1.0 / (1.0 + jnp.exp(-SWIGLU_ALPHA * glu))) * (lin + 1.0)
        y = jnp.dot(act.astype(BF16), wd_bf[...], preferred_element_type=F32) + bd_ref[...]
        y_ref[...] = _pack_bf16_pairs(y)

    @pl.when(jnp.logical_not(used))
    def _():
        y_ref[...] = jnp.zeros_like(y_ref)


def _experts(xs, blk_expert, n_valid, src_blk, layer, w_gu, b_gu, w_down, b_down):
    n_rows, dw = xs.shape
    _, n_e, d, f2 = w_gu.shape
    n_blocks = n_rows // MOE_ROWS

    def wmap(i, be, nv, sb):
        return (layer, be[i], 0, 0)

    grid_spec = pltpu.PrefetchScalarGridSpec(
        num_scalar_prefetch=3,
        grid=(n_blocks,),
        in_specs=[pl.BlockSpec((MOE_ROWS, dw), lambda i, be, nv, sb: (sb[i], 0)),
                  pl.BlockSpec((None, None, d, f2), wmap),
                  pl.BlockSpec((None, None, 1, f2), wmap),
                  pl.BlockSpec((None, None, f2 // 2, d), wmap),
                  pl.BlockSpec((None, None, 1, d), wmap)],
        out_specs=pl.BlockSpec((MOE_ROWS, dw), lambda i, be, nv, sb: (i, 0)),
        scratch_shapes=[pltpu.VMEM((d, f2), BF16), pltpu.VMEM((f2 // 2, d), BF16)],
    )
    depth = w_gu.shape[0]
    return pl.pallas_call(
        _expert_kernel,
        grid_spec=grid_spec,
        out_shape=jax.ShapeDtypeStruct((n_rows, dw), jnp.int32),
        compiler_params=pltpu.CompilerParams(dimension_semantics=("arbitrary",),
                                             vmem_limit_bytes=EXPERT_VMEM_LIMIT),
        name="moe_experts",
    )(blk_expert, n_valid, src_blk, xs, w_gu, b_gu.reshape(depth, n_e, 1, f2), w_down, b_down.reshape(depth, n_e, 1, d))


def _combine_kernel(y0_ref, y1_ref, y2_ref, y3_ref, gate_ref, x_ref, g2_ref, *rest):
    o_ref = rest[-1]
    gate = gate_ref[...]
    w = y0_ref.shape[1]
    y_hi, y_lo = None, None
    for k, y_ref in enumerate((y0_ref, y1_ref, y2_ref, y3_ref)):
        hi, lo = _unpack_bf16_pairs(y_ref[...])
        gk = gate[:, k:k + 1]
        y_hi = gk * hi if y_hi is None else y_hi + gk * hi
        y_lo = gk * lo if y_lo is None else y_lo + gk * lo
    out_hi = x_ref[:, :w] + g2_ref[:, :w] * y_hi
    out_lo = x_ref[:, w:] + g2_ref[:, w:] * y_lo
    if len(rest) == 2:
        gain = rest[0][...]
        ms = (jnp.sum(out_hi * out_hi, axis=-1, keepdims=True)
              + jnp.sum(out_lo * out_lo, axis=-1, keepdims=True)) / (2 * w)
        r = lax.rsqrt(ms + EPS)
        out_hi = out_hi * r * gain[:, :w]
        out_lo = out_lo * r * gain[:, w:]
    o_ref[:, :w] = out_hi
    o_ref[:, w:] = out_lo


def _combine(y4, gate, x, g2, final_gain, *, seq, tm=512):
    t, d = x.shape
    bpb = seq // tm
    nblk = t // tm
    y_specs = [pl.BlockSpec((tm, d // 2), functools.partial(lambda i, k: (k * nblk + i, 0), k=k))
               for k in range(TOP_K)]
    in_specs = y_specs + [pl.BlockSpec((tm, TOP_K), lambda i: (i, 0)),
                          pl.BlockSpec((tm, d), lambda i: (i, 0)),
                          pl.BlockSpec((None, 1, d), lambda i: (i // bpb, 0, 0))]
    args = [y4, y4, y4, y4, gate, x, g2]
    if final_gain is not None:
        in_specs.append(pl.BlockSpec((1, d), lambda i: (0, 0)))
        args.append(final_gain.reshape(1, d))
    return pl.pallas_call(
        _combine_kernel,
        grid=(nblk,),
        in_specs=in_specs,
        out_specs=pl.BlockSpec((tm, d), lambda i: (i, 0)),
        out_shape=jax.ShapeDtypeStruct((t, d), F32),
        compiler_params=_cparams(("parallel",)),
        name="moe_combine",
    )(*args)


def _moe(x, norm, g2, w_router, b_router, layer, w_gu, b_gu, w_down, b_down, final_gain, *, seq):
    t = x.shape[0]
    n_e = w_router.shape[1]
    idx_t, gate_t, rank_t, counts, h = _router(x, norm, w_router, b_router, seq=seq)
    e_ids = jnp.arange(n_e, dtype=jnp.int32)
    counts = counts[:, 0].astype(jnp.int32)
    padded = (counts + MOE_ROWS - 1) // MOE_ROWS * MOE_ROWS
    pend = jnp.cumsum(padded)
    pstart = pend - padded
    n_pad = n_e * MOE_ROWS
    n_rows = t * TOP_K + n_pad
    n_blocks = n_rows // MOE_ROWS
    dest_t = jnp.sum(jnp.where(idx_t[:, :, None] == e_ids, pstart, 0), axis=-1) + rank_t
    dest_flat = dest_t.reshape(TOP_K * t)
    blk_start = jnp.arange(n_blocks, dtype=jnp.int32) * MOE_ROWS
    n_used = pend[-1] // MOE_ROWS
    src_blk = jnp.minimum(jnp.arange(n_blocks, dtype=jnp.int32), n_used - 1)
    blk_expert = jnp.minimum(jnp.sum((src_blk * MOE_ROWS)[:, None] >= pend[None, :], axis=1), n_e - 1)
    blk_expert = blk_expert.astype(jnp.int32)
    own = blk_expert[:, None] == e_ids
    rows_left = jnp.sum(jnp.where(own, pstart + counts, 0), axis=-1) - blk_start
    n_valid = jnp.where(blk_start < pend[-1], jnp.clip(rows_left, 0, MOE_ROWS), 0).astype(jnp.int32)
    pad_len = padded - counts
    pad_end = jnp.cumsum(pad_len)
    j = jnp.arange(n_pad, dtype=jnp.int32)
    pe = jnp.sum(j[:, None] >= pad_end[None, :], axis=1)
    sel = jnp.minimum(pe, n_e - 1)[:, None] == e_ids
    in_expert = jnp.sum(jnp.where(sel, pstart + counts - (pad_end - pad_len), 0), axis=-1) + j
    pad_rows = jnp.where(pe < n_e, in_expert, pend[-1] + j - pad_end[-1]).astype(jnp.int32)

    xs = _sc_dispatch(h, dest_flat, pad_rows)
    yb = _experts(xs, blk_expert, n_valid, src_blk, layer, w_gu, b_gu, w_down, b_down)
    y4 = _sc_gather(yb, dest_flat)
    return _combine(y4, gate_t.T, x, g2, final_gain, seq=seq)


def _dup_heads(w, n_heads):
    d = w.shape[0]
    return jnp.repeat(w.reshape(d, n_heads, 1, HEAD_DIM), 2, axis=2).reshape(d, n_heads * LANES)


def _mixer_mla_swa(x, norm, g1, cos, sin, w_in, g_q, w_qb, g_kv, w_kvb, sink, w_out, *, bsz, seq):
    d = x.shape[1]
    o = np.cumsum((0, MLA_Q_LORA, MLA_KV_LORA, MLA_ROPE, SWA_Q_HEADS * HEAD_DIM,
                   SWA_KV_HEADS * HEAD_DIM, SWA_KV_HEADS * HEAD_DIM))
    w_cq, w_ckv, w_kr, w_qs, w_ks, w_vs = (w_in[:, o[i]:o[i + 1]] for i in range(6))
    w_lat = jnp.concatenate([w_cq, jnp.zeros((d, LANES), F32), w_ckv], axis=1).astype(BF16)
    lat = _proj([x], [w_lat], prologue="ada", norm=norm, seq=seq, tm=512, tn=w_lat.shape[1],
                out_dtype=F32, name="proj_latent")
    w_b = jnp.concatenate([_pair_layout(w_qs), _pair_layout(_dup_heads(w_ks, SWA_KV_HEADS)),
                           _dup_heads(w_vs, SWA_KV_HEADS),
                           w_kr, jnp.zeros((d, LANES - MLA_ROPE), F32)], axis=1).astype(BF16)
    qkv_b = _proj([x], [w_b], prologue="ada", norm=norm, seq=seq, tm=512, tn=w_b.shape[1], out_dtype=BF16,
                  rope=(cos, sin), rope_pattern=(ROPE_PAIR,) * 10 + (0, 0) + (ROPE_ROTATE,),
                  group_scales=(QK_LOG2_SCALE,) * 8 + (1.0,) * 5, name="proj_swa")
    kr_col = (w_b.shape[1] - LANES) // LANES
    qd = MLA_NOPE + MLA_ROPE
    w_q = w_qb.reshape(MLA_Q_LORA, MLA_HEADS, qd)
    w_q = jnp.concatenate([w_q, jnp.zeros((MLA_Q_LORA, MLA_HEADS, 2 * LANES - qd), F32)], axis=2)
    w_q = w_q.reshape(MLA_Q_LORA, MLA_HEADS * 2 * LANES).astype(BF16)
    q_cat = _proj([lat], [w_q], prologue="rms", norm=(g_q.reshape(1, -1),), lhs_cols=(MLA_Q_LORA, 0),
                  seq=seq, tm=512, tn=w_q.shape[1], out_dtype=BF16, rope=(cos, sin),
                  rope_pattern=(0, 1) * MLA_HEADS, out_scale=qd ** -0.5, name="proj_mla_q")
    w_kv = w_kvb.reshape(MLA_KV_LORA, MLA_HEADS, MLA_NOPE + MLA_V)
    w_kv = jnp.concatenate([w_kv[:, :, :MLA_NOPE].reshape(MLA_KV_LORA, -1),
                            w_kv[:, :, MLA_NOPE:].reshape(MLA_KV_LORA, -1)], axis=1).astype(BF16)
    kv = _proj([lat], [w_kv], prologue="rms", norm=(g_kv.reshape(1, -1),), lhs_cols=(MLA_KV_LORA, 2),
               seq=seq, tm=512, tn=w_kv.shape[1], out_dtype=BF16, name="proj_mla_kv")
    o_a = _mla_attention(q_cat, kv, qkv_b, kr_col, bsz=bsz, seq=seq, tq=256)
    qkv_b4 = qkv_b.reshape(bsz, 1, seq, qkv_b.shape[1])
    (o_b,) = _banded_attention(qkv_b4, qkv_b4, qkv_b4, bsz=bsz, seq=seq, dil=1, radius=SWA_RADIUS,
                               q_col=0, k_col=4, v_col=5, kv_width=2 * LANES,
                               pairs_per_kv=SWA_Q_HEADS // SWA_KV_HEADS // 2, sink=sink, want_lse=False,
                               name="swa_attention")
    o_b = o_b.reshape(bsz * seq, SWA_Q_HEADS * HEAD_DIM)
    na = MLA_HEADS * MLA_V
    return _proj([o_a, o_b], [w_out[:na].astype(BF16), w_out[na:].astype(BF16)], prologue="plain",
                 seq=seq, tm=512, tn=d, out_dtype=F32, residual=(x, g1), name="proj_out_ab")


def _mixer_dilated(x, norm, g1, cos, sin, w_in, w_out, *, bsz, seq):
    d = x.shape[1]
    qw = DIL_HEADS * HEAD_DIM
    w_bf = w_in.astype(BF16)
    outs, lses = [], []
    for g, (window, dil) in enumerate(DIL_PATTERN):
        w_g = w_bf[:, g * 3 * qw:(g + 1) * 3 * qw]
        w_g = jnp.concatenate([_pair_layout(w_g[:, :2 * qw]), w_g[:, 2 * qw:]], axis=1)
        qkv = _proj([x], [w_g], prologue="ada", norm=norm, seq=seq, tm=512, tn=3 * qw, sub_tn=qw, out_dtype=BF16,
                    rope=(cos, sin), rope_pattern=(ROPE_PAIR,) * (2 * qw // LANES) + (0,) * (qw // LANES),
                    group_scales=(QK_LOG2_SCALE,) * (qw // LANES) + (1.0,) * (2 * qw // LANES),
                    out_dil=dil, name=f"proj_dil_{dil}")
        qkv = qkv.reshape(bsz, dil, seq // dil, 3 * qw)
        o, lse = _banded_attention(qkv, qkv, qkv, bsz=bsz, seq=seq, dil=dil, radius=window // (2 * dil),
                                   q_col=0, k_col=1, v_col=2, kv_width=qw, pairs_per_kv=1,
                                   want_lse=True, name=f"dilated_attention_{dil}")
        outs.append(o)
        lses.append(lse)
    return _proj(outs + lses, [w_out.astype(BF16)], prologue="dil", seq=seq, tm=512, tn=d, out_dtype=F32,
                 residual=(x, g1), in_dils=tuple(dil for _, dil in DIL_PATTERN), name="proj_out_c")


def kernel(x, c, positions, w_mod, b_mod, g_norm_mix, g_norm_ffn, w_in_ab, mla_g_q, mla_w_qb, mla_g_kv,
           mla_w_kvb, swa_sink, w_out_ab, w_in_c, w_out_c, w_router, b_router, w_gu, b_gu, w_down, b_down,
           g_final):
    bsz, seq, d = x.shape
    depth = w_mod.shape[0]
    cos, sin = _rope_tables(positions)
    mod = _modulation(c, w_mod, b_mod)
    xt = x.reshape(bsz * seq, d)
    for layer in range(depth):
        sh1, sc1, g1, sh2, sc2, g2 = (mod[layer, :, i * d:(i + 1) * d].reshape(bsz, 1, d) for i in range(6))
        li = layer // 2
        norm = (g_norm_mix[layer].reshape(1, d), sc1, sh1)
        if layer % 2 == 0:
            xt = _mixer_mla_swa(xt, norm, g1, cos, sin, w_in_ab[li], mla_g_q[li], mla_w_qb[li], mla_g_kv[li],
                                mla_w_kvb[li], swa_sink[li], w_out_ab[li], bsz=bsz, seq=seq)
        else:
            xt = _mixer_dilated(xt, norm, g1, cos, sin, w_in_c[li], w_out_c[li], bsz=bsz, seq=seq)
        norm = (g_norm_ffn[layer].reshape(1, d), sc2, sh2)
        xt = _moe(xt, norm, g2, w_router[layer], b_router[layer], layer, w_gu, b_gu, w_down, b_down,
                  g_final if layer == depth - 1 else None, seq=seq)
    return xt.reshape(bsz, seq, d)
```

```python
import functools

import jax
import jax.numpy as jnp
import numpy as np
from jax import lax
from jax.experimental import pallas as pl
from jax.experimental.pallas import tpu as pltpu
from jax.experimental.pallas import tpu_sc as plsc

F32 = jnp.float32
BF16 = jnp.bfloat16

EPS = 1e-6
ROPE_THETA = 10000.0
LANES = 128
HEAD_DIM = 64
ROPE_HALF = HEAD_DIM // 2

MLA_HEADS = 8
MLA_Q_LORA = 384
MLA_KV_LORA = 256
MLA_NOPE = 128
MLA_ROPE = 64
MLA_V = 128
SWA_Q_HEADS = 16
SWA_KV_HEADS = 2
SWA_RADIUS = 128
DIL_PATTERN = ((128, 1), (512, 4), (2048, 16))
DIL_HEADS = 16
N_EXPERTS = 32
TOP_K = 4
SWIGLU_LIMIT = 7.0
SWIGLU_ALPHA = 1.702
MOE_ROWS = 512
BAND_STEP_ROWS = 512
BAND_SUB_ROWS = 128
LOG2_E = float(np.log2(np.e))
QK_LOG2_SCALE = HEAD_DIM ** -0.5 * LOG2_E

VMEM_LIMIT = 48 * 1024 * 1024
EXPERT_VMEM_LIMIT = 56 * 1024 * 1024


def _cparams(sem):
    return pltpu.CompilerParams(dimension_semantics=sem, vmem_limit_bytes=VMEM_LIMIT)


def _mod_kernel(c_ref, w_ref, b_ref, o_ref):
    c = c_ref[...]
    ca = c * (1.0 / (1.0 + jnp.exp(-c)))
    o_ref[...] = jnp.dot(ca, w_ref[...], preferred_element_type=F32,
                         precision=lax.Precision.HIGHEST) + b_ref[...]


def _modulation(c, w_mod, b_mod):
    depth, d, n = w_mod.shape
    bsz = c.shape[0]
    tn = 1536
    return pl.pallas_call(
        _mod_kernel,
        grid=(depth, n // tn),
        in_specs=[pl.BlockSpec((bsz, d), lambda l, j: (0, 0)),
                  pl.BlockSpec((None, d, tn), lambda l, j: (l, 0, j)),
                  pl.BlockSpec((None, 1, tn), lambda l, j: (l, 0, j))],
        out_specs=pl.BlockSpec((None, bsz, tn), lambda l, j: (l, 0, j)),
        out_shape=jax.ShapeDtypeStruct((depth, bsz, n), F32),
        compiler_params=_cparams(("parallel", "parallel")),
        name="modulation",
    )(c, w_mod, b_mod.reshape(depth, 1, n))


def _rope_table_kernel(pos_ref, inv_ref, sign_ref, cos_ref, sin_ref):
    ang = inv_ref[...] * pos_ref[...].astype(F32)
    reps = LANES // ROPE_HALF
    cos_ref[...] = jnp.transpose(jnp.concatenate([jnp.cos(ang)] * reps, axis=0))
    sin_ref[...] = jnp.transpose(jnp.concatenate([jnp.sin(ang)] * reps, axis=0) * sign_ref[...])


def _rope_tables(positions):
    bsz, s = positions.shape
    inv = ROPE_THETA ** (-jnp.arange(0, HEAD_DIM, 2, dtype=F32) / HEAD_DIM)
    sign = jnp.where((jnp.arange(LANES) % HEAD_DIM) < ROPE_HALF, -1.0, 1.0).astype(F32).reshape(LANES, 1)
    out = jax.ShapeDtypeStruct((bsz * s, LANES), F32)
    return pl.pallas_call(
        _rope_table_kernel,
        grid=(bsz,),
        in_specs=[pl.BlockSpec((None, 1, s), lambda b: (b, 0, 0)),
                  pl.BlockSpec((ROPE_HALF, 1), lambda b: (0, 0)),
                  pl.BlockSpec((LANES, 1), lambda b: (0, 0))],
        out_specs=[pl.BlockSpec((s, LANES), lambda b: (b, 0)),
                   pl.BlockSpec((s, LANES), lambda b: (b, 0))],
        out_shape=[out, out],
        compiler_params=_cparams(("parallel",)),
        name="rope_tables",
    )(positions.reshape(bsz, 1, s), inv.reshape(ROPE_HALF, 1), sign)


def _rms(x):
    return x * lax.rsqrt(jnp.mean(x * x, axis=-1, keepdims=True) + EPS)


def _ada_norm(x, g, sc, sh):
    return (_rms(x) * g) * (1.0 + sc) + sh


def _pack_bf16_pairs(x):
    w = x.shape[1] // 2
    hi = pltpu.bitcast(x[:, :w].astype(BF16).astype(F32), jnp.int32)
    lo = pltpu.bitcast(x[:, w:].astype(BF16).astype(F32), jnp.int32)
    return hi | lax.shift_right_logical(lo, 16)


def _unpack_bf16_pairs(p):
    hi = pltpu.bitcast(p & jnp.int32(-65536), F32)
    lo = pltpu.bitcast(lax.shift_left(p, 16), F32)
    return hi, lo


ROPE_ROTATE, ROPE_PAIR = 1, 2


def _pair_layout(w):
    k, n = w.shape
    w = w.reshape(k, n // LANES, 2, 2, ROPE_HALF)
    return jnp.swapaxes(w, 2, 3).reshape(k, n)


def _rope_group(a, cos, sin, first_half):
    rot = jnp.where(first_half, pltpu.roll(a, LANES - ROPE_HALF, 1), pltpu.roll(a, ROPE_HALF, 1))
    return a * cos + rot * sin


def _proj_kernel(*refs, prologue, n_lhs, rope_pattern, group_scales, sub_tn, out_scale, residual, in_dils,
                 out_dil):
    pos = 0
    if prologue == "ada":
        x_ref, g_ref, sc_ref, sh_ref = refs[:4]
        pos = 4
    elif prologue == "rms":
        x_ref, g_ref = refs[:2]
        pos = 2
    elif prologue == "dil":
        o_refs = refs[0:3]
        l_refs = refs[3:6]
        pos = 6
    else:
        x_refs = refs[:n_lhs]
        pos = n_lhs
    w_refs = refs[pos:pos + n_lhs]
    pos += n_lhs
    if rope_pattern is not None:
        cos_ref, sin_ref = refs[pos:pos + 2]
        pos += 2
    if residual:
        res_ref, gate_ref = refs[pos:pos + 2]
        pos += 2
    o_ref = refs[pos]
    pos += 1
    h_ref = refs[pos] if prologue != "plain" else None
    slab_ref = refs[pos + 1] if (prologue == "dil" or out_dil > 1) else None

    if prologue != "plain":
        @pl.when(pl.program_id(1) == 0)
        def _():
            if prologue == "ada":
                h_ref[...] = _ada_norm(x_ref[...], g_ref[...], sc_ref[...], sh_ref[...]).astype(BF16)
            elif prologue == "rms":
                h_ref[...] = (_rms(x_ref[...]) * g_ref[...]).astype(BF16)
            else:
                tm = h_ref.shape[0]
                for c in range(h_ref.shape[1] // LANES):
                    cols = slice(c * LANES, (c + 1) * LANES)
                    vals = []
                    for i, (ref, dil) in enumerate(zip(o_refs + l_refs, in_dils + in_dils)):
                        if dil == 1:
                            vals.append(ref[0, :, cols].astype(F32))
                        else:
                            for r in range(dil):
                                slab_ref[i, pl.ds(r, tm // dil, stride=dil), :] = ref[r, :, cols].astype(F32)
                            vals.append(slab_ref[i])
                    o0, o1, o2, l0, l1, l2 = vals
                    m = jnp.maximum(jnp.maximum(l0, l1), l2)
                    e0, e1, e2 = jnp.exp2(l0 - m), jnp.exp2(l1 - m), jnp.exp2(l2 - m)
                    h_ref[:, cols] = ((e0 * o0 + e1 * o1 + e2 * o2) / (e0 + e1 + e2)).astype(BF16)

    n_total = w_refs[0].shape[1]
    tn = n_total if sub_tn is None else sub_tn

    def matmul(t):
        cols = slice(t * tn, (t + 1) * tn)
        if prologue != "plain":
            acc = jnp.dot(h_ref[...], w_refs[0][:, cols], preferred_element_type=F32)
        else:
            acc = jnp.dot(x_refs[0][...], w_refs[0][:, cols], preferred_element_type=F32)
            for xr, wr in zip(x_refs[1:], w_refs[1:]):
                acc = acc + jnp.dot(xr[...], wr[:, cols], preferred_element_type=F32)
        return acc * out_scale if out_scale != 1.0 else acc

    if residual:
        o_ref[...] = (res_ref[...] + gate_ref[...] * matmul(0)).astype(o_ref.dtype)
        return

    pattern = rope_pattern if rope_pattern is not None else (0,) * (n_total // LANES)
    if any(pattern):
        cos = cos_ref[...]
        sin = sin_ref[...]
        lane = lax.broadcasted_iota(jnp.int32, cos.shape, 1)
        first_half = (lane % HEAD_DIM) < ROPE_HALF
        if ROPE_PAIR in pattern:
            sin_pair = jnp.where(lane < HEAD_DIM, -1.0, 1.0) * jnp.where(first_half, -sin, sin)

    def epilogue(acc, t):
        tm = acc.shape[0]
        for c in range(tn // LANES):
            g = t * (tn // LANES) + c
            cols = slice(g * LANES, (g + 1) * LANES)
            a = acc[:, c * LANES:(c + 1) * LANES]
            if pattern[g] == ROPE_ROTATE:
                a = _rope_group(a, cos, sin, first_half)
            elif pattern[g] == ROPE_PAIR:
                a = a * cos + pltpu.roll(a, HEAD_DIM, 1) * sin_pair
            if group_scales is not None and group_scales[g] != 1.0:
                a = a * group_scales[g]
            if out_dil == 1:
                o_ref[:, cols] = a.astype(o_ref.dtype)
            else:
                slab_ref[0] = a
                for r in range(out_dil):
                    o_ref[r, :, cols] = slab_ref[0, pl.ds(r, tm // out_dil, stride=out_dil), :].astype(o_ref.dtype)

    acc_next = matmul(0)
    for t in range(n_total // tn):
        acc = acc_next
        if t + 1 < n_total // tn:
            acc_next = matmul(t + 1)
        epilogue(acc, t)


def _proj(lhs, ws, *, prologue, seq, tm, tn, out_dtype, norm=None, lhs_cols=None, rope=None,
          rope_pattern=None, group_scales=None, sub_tn=None, out_scale=1.0, residual=None, in_dils=None,
          out_dil=1, name):
    k, n = ws[0].shape
    bsz_seq = lhs[0].shape[0] if prologue != "dil" else lhs[0].shape[0] * seq
    t = bsz_seq
    assert t % tm == 0 and n % tn == 0 and seq % tm == 0
    bpb = seq // tm
    n_lhs = len(ws)
    args, specs = [], []
    row = lambda i, j: (i, 0)
    if prologue in ("ada", "rms"):
        kw, kc = lhs_cols if lhs_cols is not None else (k, 0)
        args.append(lhs[0])
        specs.append(pl.BlockSpec((tm, kw), lambda i, j: (i, kc)))
        args.append(norm[0])
        specs.append(pl.BlockSpec((1, k), lambda i, j: (0, 0)))
        if prologue == "ada":
            for v in norm[1:3]:
                args.append(v)
                specs.append(pl.BlockSpec((None, 1, k), lambda i, j: (i // bpb, 0, 0)))
    elif prologue == "dil":
        for a, dil in zip(lhs, in_dils + in_dils):
            args.append(a)
            specs.append(pl.BlockSpec((None, dil, tm // dil, k), lambda i, j: (i // bpb, 0, i % bpb, 0)))
    else:
        for a, w in zip(lhs, ws):
            args.append(a)
            specs.append(pl.BlockSpec((tm, w.shape[0]), row))
    for w in ws:
        args.append(w)
        specs.append(pl.BlockSpec((w.shape[0], tn), lambda i, j: (0, j)))
    if rope_pattern is not None:
        assert len(rope_pattern) == tn // LANES
        for tab in rope:
            args.append(tab)
            specs.append(pl.BlockSpec((tm, LANES), row))
    if residual is not None:
        res, gate = residual
        args.append(res)
        specs.append(pl.BlockSpec((tm, tn), lambda i, j: (i, j)))
        args.append(gate)
        specs.append(pl.BlockSpec((None, 1, tn), lambda i, j: (i // bpb, 0, j)))
    scratch = [] if prologue == "plain" else [pltpu.VMEM((tm, k), BF16)]
    if prologue == "dil":
        scratch.append(pltpu.VMEM((2 * len(in_dils), tm, LANES), F32))
    elif out_dil > 1:
        scratch.append(pltpu.VMEM((1, tm, LANES), F32))
    if out_dil == 1:
        out_spec = pl.BlockSpec((tm, tn), lambda i, j: (i, j))
        out_shape = jax.ShapeDtypeStruct((t, n), out_dtype)
    else:
        out_spec = pl.BlockSpec((None, out_dil, tm // out_dil, tn), lambda i, j: (i // bpb, 0, i % bpb, j))
        out_shape = jax.ShapeDtypeStruct((t // seq, out_dil, seq // out_dil, n), out_dtype)
    kern = functools.partial(_proj_kernel, prologue=prologue, n_lhs=n_lhs, rope_pattern=rope_pattern,
                             group_scales=group_scales, sub_tn=sub_tn, out_scale=out_scale,
                             residual=residual is not None,
                             in_dils=in_dils, out_dil=out_dil)
    return pl.pallas_call(
        kern,
        grid=(t // tm, n // tn),
        in_specs=specs,
        out_specs=out_spec,
        out_shape=out_shape,
        scratch_shapes=scratch,
        compiler_params=_cparams(("parallel", "arbitrary")),
        name=name,
    )(*args)


MLA_HEADS_PER_STEP = 4


def _mla_scores(q_ref, kn_ref, kr, h):
    k = jnp.concatenate([kn_ref[:, h * LANES:(h + 1) * LANES], kr], axis=1)
    q = q_ref[:, h * 2 * LANES:(h + 1) * 2 * LANES]
    return lax.dot_general(q, k, (((1,), (1,)), ((), ())), preferred_element_type=F32)


def _mla_kernel(q_ref, kn_ref, kr_ref, v_ref, o_ref):
    kr = kr_ref[...]
    s_next = _mla_scores(q_ref, kn_ref, kr, 0)
    for h in range(MLA_HEADS_PER_STEP):
        s = s_next
        if h + 1 < MLA_HEADS_PER_STEP:
            s_next = _mla_scores(q_ref, kn_ref, kr, h + 1)
        m = jnp.max(s, axis=1, keepdims=True)
        p = jnp.exp(s - m).astype(BF16)
        v = v_ref[:, h * LANES:(h + 1) * LANES]
        oe = jnp.dot(p, jnp.concatenate([v, jnp.ones_like(v)], axis=1), preferred_element_type=F32)
        o_ref[:, h * LANES:(h + 1) * LANES] = (oe[:, :LANES] / oe[:, LANES:]).astype(o_ref.dtype)


def _mla_attention(q_cat, kv, qkv_b, kr_col, *, bsz, seq, tq):
    t = bsz * seq
    nq = seq // tq
    hps = MLA_HEADS_PER_STEP
    groups = MLA_HEADS // hps
    return pl.pallas_call(
        _mla_kernel,
        grid=(bsz, groups, nq),
        in_specs=[pl.BlockSpec((tq, hps * 2 * LANES), lambda b, h, i: (b * nq + i, h)),
                  pl.BlockSpec((seq, hps * LANES), lambda b, h, i: (b, h)),
                  pl.BlockSpec((seq, LANES), lambda b, h, i: (b, kr_col)),
                  pl.BlockSpec((seq, hps * LANES), lambda b, h, i: (b, groups + h))],
        out_specs=pl.BlockSpec((tq, hps * LANES), lambda b, h, i: (b * nq + i, h)),
        out_shape=jax.ShapeDtypeStruct((t, MLA_HEADS * MLA_V), BF16),
        compiler_params=_cparams(("parallel", "parallel", "parallel")),
        name="mla_attention",
    )(q_cat, kv, qkv_b, kv)


def _banded_kernel(*refs, radius, tq, hb, length, n_pairs, pairs_per_kv, has_sink, want_lse):
    pos = 0
    if has_sink:
        sink_ref = refs[0]
        pos = 1
    q_ref, kp_ref, kc_ref, kn_ref, vp_ref, vc_ref, vn_ref = refs[pos:pos + 7]
    pos += 7
    o_ref = refs[pos]
    lse_ref = refs[pos + 1] if want_lse else None

    j = pl.program_id(2)
    kw = jnp.concatenate([kp_ref[...], kc_ref[...], kn_ref[...]], axis=0)
    vw = jnp.concatenate([vp_ref[...], vc_ref[...], vn_ref[...]], axis=0)
    qn = min(tq, BAND_SUB_ROWS)
    kn = qn + 2 * hb
    lane = lax.broadcasted_iota(jnp.int32, (qn, LANES), 1)
    low = lane < HEAD_DIM
    first = (lane % HEAD_DIM) < ROPE_HALF

    for q0 in range(0, tq, qn):
        qpos = j * tq + q0 + lax.broadcasted_iota(jnp.int32, (qn, kn), 0)
        kpos = j * tq - hb + q0 + lax.broadcasted_iota(jnp.int32, (qn, kn), 1)
        valid = (jnp.abs(qpos - kpos) <= radius) & (kpos >= 0) & (kpos < length)
        for p in range(n_pairs):
            c = p // pairs_per_kv
            qp = q_ref[q0:q0 + qn, p * LANES:(p + 1) * LANES]
            kp = kw[q0:q0 + kn, c * LANES:(c + 1) * LANES]
            vp = vw[q0:q0 + kn, c * LANES:(c + 1) * LANES]
            vp = jnp.concatenate([vp, jnp.ones_like(vp)], axis=1)
            halves = []
            for half in range(2):
                qh = jnp.where(first if half == 0 else ~first, qp, jnp.zeros_like(qp))
                s = lax.dot_general(qh, kp, (((1,), (1,)), ((), ())), preferred_element_type=F32)
                s = jnp.where(valid, s, -jnp.inf)
                m = jnp.max(s, axis=1, keepdims=True)
                if has_sink:
                    sk = sink_ref[2 * p + half] * LOG2_E
                    m = jnp.maximum(m, sk)
                oe = jnp.dot(jnp.exp2(s - m).astype(BF16), vp, preferred_element_type=F32)
                den = oe[:, LANES:]
                if has_sink:
                    den = den + jnp.exp2(sk - m)
                halves.append((oe[:, :LANES] / den, m + jnp.log2(den) if want_lse else None))
            o_ref[q0:q0 + qn, p * LANES:(p + 1) * LANES] = jnp.where(
                low, halves[0][0], halves[1][0]).astype(o_ref.dtype)
            if want_lse:
                lse_ref[q0:q0 + qn, p * LANES:(p + 1) * LANES] = jnp.where(low, halves[0][1], halves[1][1])


def _banded_attention(q_arr, k_arr, v_arr, *, bsz, seq, dil, radius, q_col, k_col, v_col, kv_width,
                      pairs_per_kv, sink=None, want_lse, name):
    length = seq // dil
    hb = radius
    tq = min(BAND_STEP_ROWS, length)
    assert tq % hb == 0 and length % tq == 0
    qw = DIL_HEADS * HEAD_DIM
    n_pairs = qw // LANES
    rep = tq // hb
    last_hb = length // hb - 1

    def prev_spec(col):
        return pl.BlockSpec((None, None, hb, kv_width),
                            lambda b, r, j: (b, r, jnp.maximum(j * rep - 1, 0), col))

    def cur_spec(col):
        return pl.BlockSpec((None, None, tq, kv_width), lambda b, r, j: (b, r, j, col))

    def next_spec(col):
        return pl.BlockSpec((None, None, hb, kv_width),
                            lambda b, r, j: (b, r, jnp.minimum((j + 1) * rep, last_hb), col))

    in_specs = [pl.BlockSpec((None, None, tq, qw), lambda b, r, j: (b, r, j, q_col)),
                prev_spec(k_col), cur_spec(k_col), next_spec(k_col),
                prev_spec(v_col), cur_spec(v_col), next_spec(v_col)]
    args = [q_arr, k_arr, k_arr, k_arr, v_arr, v_arr, v_arr]
    if sink is not None:
        in_specs = [pl.BlockSpec(memory_space=pltpu.SMEM)] + in_specs
        args = [sink] + args
    out_spec = pl.BlockSpec((None, None, tq, qw), lambda b, r, j: (b, r, j, 0))
    out_specs = [out_spec]
    out_shape = [jax.ShapeDtypeStruct((bsz, dil, length, qw), BF16)]
    if want_lse:
        out_specs.append(out_spec)
        out_shape.append(jax.ShapeDtypeStruct((bsz, dil, length, qw), F32))
    kern = functools.partial(_banded_kernel, radius=radius, tq=tq, hb=hb, length=length, n_pairs=n_pairs,
                             pairs_per_kv=pairs_per_kv, has_sink=sink is not None, want_lse=want_lse)
    return pl.pallas_call(
        kern,
        grid=(bsz, dil, length // tq),
        in_specs=in_specs,
        out_specs=out_specs,
        out_shape=out_shape,
        compiler_params=_cparams(("parallel", "parallel", "parallel")),
        name=name,
    )(*args)


def _router_kernel(x_ref, g_ref, sc_ref, sh_ref, wt_ref, b_ref, tri_ref,
                   idx_ref, gate_ref, rank_ref, cnt_ref, h_ref, carry_ref):
    @pl.when(pl.program_id(0) == 0)
    def _():
        carry_ref[...] = jnp.zeros_like(carry_ref)

    h = _ada_norm(x_ref[...], g_ref[...], sc_ref[...], sh_ref[...])
    h_ref[...] = _pack_bf16_pairs(h)
    lt = lax.dot_general(wt_ref[...], h, (((1,), (1,)), ((), ())), preferred_element_type=F32,
                         precision=lax.Precision.HIGHEST) + b_ref[...]
    n_e, tm = lt.shape
    e_iota = lax.broadcasted_iota(jnp.int32, (n_e, tm), 0)
    vals, sels = [], []
    for k in range(TOP_K):
        m = jnp.max(lt, axis=0, keepdims=True)
        idx = jnp.min(jnp.where(lt == m, e_iota, n_e), axis=0, keepdims=True)
        sel = e_iota == idx
        idx_ref[k:k + 1, :] = idx
        vals.append(m)
        sels.append(sel)
        lt = jnp.where(sel, -jnp.inf, lt)
    exps = [jnp.exp(v - vals[0]) for v in vals]
    den = exps[0] + exps[1] + exps[2] + exps[3]
    for k in range(TOP_K):
        gate_ref[k:k + 1, :] = exps[k] / den

    onehot = jnp.where(sels[0] | sels[1] | sels[2] | sels[3], 1.0, 0.0)
    carry = carry_ref[:, 0:1]
    before = jnp.dot(onehot.astype(BF16), tri_ref[...], preferred_element_type=F32) + carry
    for k in range(TOP_K):
        rank_ref[k:k + 1, :] = jnp.sum(jnp.where(sels[k], before, 0.0), axis=0, keepdims=True).astype(jnp.int32)
    total = carry + jnp.sum(onehot, axis=1, keepdims=True)
    carry_ref[...] = jnp.broadcast_to(total, carry_ref.shape)
    cnt_ref[...] = jnp.broadcast_to(total, cnt_ref.shape)


def _router(x, norm, w_router, b_router, *, seq, tm=512):
    t, d = x.shape
    n_e = w_router.shape[1]
    bpb = seq // tm
    tri = (jnp.arange(tm)[:, None] < jnp.arange(tm)[None, :]).astype(BF16)
    g, sc, sh = norm
    out_i = jax.ShapeDtypeStruct((TOP_K, t), jnp.int32)
    tok = pl.BlockSpec((TOP_K, tm), lambda i: (0, i))
    return pl.pallas_call(
        _router_kernel,
        grid=(t // tm,),
        in_specs=[pl.BlockSpec((tm, d), lambda i: (i, 0)),
                  pl.BlockSpec((1, d), lambda i: (0, 0)),
                  pl.BlockSpec((None, 1, d), lambda i: (i // bpb, 0, 0)),
                  pl.BlockSpec((None, 1, d), lambda i: (i // bpb, 0, 0)),
                  pl.BlockSpec((n_e, d), lambda i: (0, 0)),
                  pl.BlockSpec((n_e, 1), lambda i: (0, 0)),
                  pl.BlockSpec((tm, tm), lambda i: (0, 0))],
        out_specs=[tok, tok, tok, pl.BlockSpec((n_e, LANES), lambda i: (0, 0)),
                   pl.BlockSpec((tm, d // 2), lambda i: (i, 0))],
        out_shape=[out_i, jax.ShapeDtypeStruct((TOP_K, t), F32), out_i,
                   jax.ShapeDtypeStruct((n_e, LANES), F32), jax.ShapeDtypeStruct((t, d // 2), jnp.int32)],
        scratch_shapes=[pltpu.VMEM((n_e, LANES), F32)],
        compiler_params=_cparams(("arbitrary",)),
        name="router",
    )(x, g, sc, sh, w_router.T, b_router.reshape(n_e, 1), tri)


SC_CORES = 2
SC_SUBCORES = 16
SC_WORKERS = SC_CORES * SC_SUBCORES
SC_ROWS = 64


def _sc_worker_id():
    return lax.axis_index("subcore") * SC_CORES + lax.axis_index("core")


def _sc_mesh():
    return plsc.VectorSubcoreMesh(core_axis_name="core", subcore_axis_name="subcore")


def _sc_dispatch(h, dest_flat, pad_rows):
    t, d = h.shape
    n_pad = pad_rows.shape[0]
    n_rows = TOP_K * t + n_pad
    tok_w = t // SC_WORKERS
    pad_w = n_pad // SC_WORKERS
    nb = tok_w // SC_ROWS
    assert tok_w % (2 * SC_ROWS) == 0 and pad_w % SC_ROWS == 0

    @functools.partial(
        pl.kernel, out_type=jax.ShapeDtypeStruct((n_rows, d), h.dtype), mesh=_sc_mesh(),
        scratch_types=[pltpu.VMEM((SC_ROWS,), jnp.int32), pltpu.VMEM((2, SC_ROWS, d), h.dtype),
                       pltpu.SemaphoreType.DMA((2,))])
    def scatter_rows(h_hbm, dest_hbm, pad_hbm, zero_hbm, xs_hbm, idx_v, rows_v, sem):
        wid = _sc_worker_id()
        base = wid * tok_w

        def load(b, slot):
            return pltpu.make_async_copy(h_hbm.at[pl.ds(base + b * SC_ROWS, SC_ROWS)], rows_v.at[slot],
                                         sem.at[slot])

        load(0, 0).start()

        @pl.loop(0, nb, step=2)
        def _(i):
            for slot in range(2):
                b = i + slot
                load(b, slot).wait()

                @pl.when(b + 1 < nb)
                def _():
                    load(b + 1, 1 - slot).start()

                for k in range(TOP_K):
                    pltpu.sync_copy(dest_hbm.at[pl.ds(k * t + base + b * SC_ROWS, SC_ROWS)], idx_v)
                    pltpu.sync_copy(rows_v.at[slot], xs_hbm.at[idx_v])

        pltpu.sync_copy(zero_hbm, rows_v.at[0])

        @pl.loop(0, pad_w // SC_ROWS)
        def _(b):
            pltpu.sync_copy(pad_hbm.at[pl.ds(wid * pad_w + b * SC_ROWS, SC_ROWS)], idx_v)
            pltpu.sync_copy(rows_v.at[0], xs_hbm.at[idx_v])

    return scatter_rows(h, dest_flat, pad_rows, jnp.zeros((SC_ROWS, d), h.dtype))


def _sc_gather(table, idx):
    n = idx.shape[0]
    d = table.shape[1]
    per_w = n // SC_WORKERS
    nb = per_w // SC_ROWS
    assert per_w % (2 * SC_ROWS) == 0

    @functools.partial(
        pl.kernel, out_type=jax.ShapeDtypeStruct((n, d), table.dtype), mesh=_sc_mesh(),
        scratch_types=[pltpu.VMEM((2, SC_ROWS), jnp.int32), pltpu.VMEM((2, SC_ROWS, d), table.dtype),
                       pltpu.SemaphoreType.DMA((2,))])
    def gather_rows(table_hbm, idx_hbm, out_hbm, idx_v, rows_v, sem):
        base = _sc_worker_id() * per_w

        def gather(slot):
            return pltpu.make_async_copy(table_hbm.at[idx_v.at[slot]], rows_v.at[slot], sem.at[slot])

        def start(b, slot):
            pltpu.sync_copy(idx_hbm.at[pl.ds(base + b * SC_ROWS, SC_ROWS)], idx_v.at[slot])
            gather(slot).start()

        start(0, 0)
        start(1, 1)

        @pl.loop(0, nb, step=2)
        def _(i):
            for slot in range(2):
                b = i + slot
                gather(slot).wait()
                pltpu.sync_copy(rows_v.at[slot], out_hbm.at[pl.ds(base + b * SC_ROWS, SC_ROWS)])

                @pl.when(b + 2 < nb)
                def _():
                    start(b + 2, slot)

    return gather_rows(table, idx)


def _expert_kernel(be_ref, nv_ref, sb_ref, xs_ref, wgu_ref, bgu_ref, wd_ref, bd_ref, y_ref, wgu_bf, wd_bf):
    i = pl.program_id(0)
    used = nv_ref[i] > 0

    @pl.when(used & ((i == 0) | (be_ref[i] != be_ref[jnp.maximum(i - 1, 0)])))
    def _():
        wgu_bf[...] = wgu_ref[...].astype(BF16)
        wd_bf[...] = wd_ref[...].astype(BF16)

    @pl.when(used)
    def _():
        d_ff = wd_ref.shape[0]
        x_hi, x_lo = _unpack_bf16_pairs(xs_ref[...])
        x = jnp.concatenate([x_hi.astype(BF16), x_lo.astype(BF16)], axis=1)
        gu = jnp.dot(x, wgu_bf[...], preferred_element_type=F32) + bgu_ref[...]
        glu = jnp.minimum(gu[:, :d_ff], SWIGLU_LIMIT)
        lin = jnp.clip(gu[:, d_ff:], -SWIGLU_LIMIT, SWIGLU_LIMIT)
        act = glu * (1.0 / (1.0 + jnp.exp(-SWIGLU_ALPHA * glu))) * (lin + 1.0)
        y = jnp.dot(act.astype(BF16), wd_bf[...], preferred_element_type=F32) + bd_ref[...]
        y_ref[...] = _pack_bf16_pairs(y)

    @pl.when(jnp.logical_not(used))
    def _():
        y_ref[...] = jnp.zeros_like(y_ref)


def _experts(xs, blk_expert, n_valid, src_blk, layer, w_gu, b_gu, w_down, b_down):
    n_rows, dw = xs.shape
    _, n_e, d, f2 = w_gu.shape
    n_blocks = n_rows // MOE_ROWS

    def wmap(i, be, nv, sb):
        return (layer, be[i], 0, 0)

    grid_spec = pltpu.PrefetchScalarGridSpec(
        num_scalar_prefetch=3,
        grid=(n_blocks,),
        in_specs=[pl.BlockSpec((MOE_ROWS, dw), lambda i, be, nv, sb: (sb[i], 0)),
                  pl.BlockSpec((None, None, d, f2), wmap),
                  pl.BlockSpec((None, None, 1, f2), wmap),
                  pl.BlockSpec((None, None, f2 // 2, d), wmap),
                  pl.BlockSpec((None, None, 1, d), wmap)],
        out_specs=pl.BlockSpec((MOE_ROWS, dw), lambda i, be, nv, sb: (i, 0)),
        scratch_shapes=[pltpu.VMEM((d, f2), BF16), pltpu.VMEM((f2 // 2, d), BF16)],
    )
    depth = w_gu.shape[0]
    return pl.pallas_call(
        _expert_kernel,
        grid_spec=grid_spec,
        out_shape=jax.ShapeDtypeStruct((n_rows, dw), jnp.int32),
        compiler_params=pltpu.CompilerParams(dimension_semantics=("arbitrary",),
                                             vmem_limit_bytes=EXPERT_VMEM_LIMIT),
        name="moe_experts",
    )(blk_expert, n_valid, src_blk, xs, w_gu, b_gu.reshape(depth, n_e, 1, f2), w_down, b_down.reshape(depth, n_e, 1, d))


def _combine_kernel(y0_ref, y1_ref, y2_ref, y3_ref, gate_ref, x_ref, g2_ref, *rest):
    o_ref = rest[-1]
    gate = gate_ref[...]
    w = y0_ref.shape[1]
    y_hi, y_lo = None, None
    for k, y_ref in enumerate((y0_ref, y1_ref, y2_ref, y3_ref)):
        hi, lo = _unpack_bf16_pairs(y_ref[...])
        gk = gate[:, k:k + 1]
        y_hi = gk * hi if y_hi is None else y_hi + gk * hi
        y_lo = gk * lo if y_lo is None else y_lo + gk * lo
    out_hi = x_ref[:, :w] + g2_ref[:, :w] * y_hi
    out_lo = x_ref[:, w:] + g2_ref[:, w:] * y_lo
    if len(rest) == 2:
        gain = rest[0][...]
        ms = (jnp.sum(out_hi * out_hi, axis=-1, keepdims=True)
              + jnp.sum(out_lo * out_lo, axis=-1, keepdims=True)) / (2 * w)
        r = lax.rsqrt(ms + EPS)
        out_hi = out_hi * r * gain[:, :w]
        out_lo = out_lo * r * gain[:, w:]
    o_ref[:, :w] = out_hi
    o_ref[:, w:] = out_lo


def _combine(y4, gate, x, g2, final_gain, *, seq, tm=512):
    t, d = x.shape
    bpb = seq // tm
    nblk = t // tm
    y_specs = [pl.BlockSpec((tm, d // 2), functools.partial(lambda i, k: (k * nblk + i, 0), k=k))
               for k in range(TOP_K)]
    in_specs = y_specs + [pl.BlockSpec((tm, TOP_K), lambda i: (i, 0)),
                          pl.BlockSpec((tm, d), lambda i: (i, 0)),
                          pl.BlockSpec((None, 1, d), lambda i: (i // bpb, 0, 0))]
    args = [y4, y4, y4, y4, gate, x, g2]
    if final_gain is not None:
        in_specs.append(pl.BlockSpec((1, d), lambda i: (0, 0)))
        args.append(final_gain.reshape(1, d))
    return pl.pallas_call(
        _combine_kernel,
        grid=(nblk,),
        in_specs=in_specs,
        out_specs=pl.BlockSpec((tm, d), lambda i: (i, 0)),
        out_shape=jax.ShapeDtypeStruct((t, d), F32),
        compiler_params=_cparams(("parallel",)),
        name="moe_combine",
    )(*args)


def _moe(x, norm, g2, w_router, b_router, layer, w_gu, b_gu, w_down, b_down, final_gain, *, seq):
    t = x.shape[0]
    n_e = w_router.shape[1]
    idx_t, gate_t, rank_t, counts, h = _router(x, norm, w_router, b_router, seq=seq)
    e_ids = jnp.arange(n_e, dtype=jnp.int32)
    counts = counts[:, 0].astype(jnp.int32)
    padded = (counts + MOE_ROWS - 1) // MOE_ROWS * MOE_ROWS
    pend = jnp.cumsum(padded)
    pstart = pend - padded
    n_pad = n_e * MOE_ROWS
    n_rows = t * TOP_K + n_pad
    n_blocks = n_rows // MOE_ROWS
    dest_t = jnp.sum(jnp.where(idx_t[:, :, None] == e_ids, pstart, 0), axis=-1) + rank_t
    dest_flat = dest_t.reshape(TOP_K * t)
    blk_start = jnp.arange(n_blocks, dtype=jnp.int32) * MOE_ROWS
    n_used = pend[-1] // MOE_ROWS
    src_blk = jnp.minimum(jnp.arange(n_blocks, dtype=jnp.int32), n_used - 1)
    blk_expert = jnp.minimum(jnp.sum((src_blk * MOE_ROWS)[:, None] >= pend[None, :], axis=1), n_e - 1)
    blk_expert = blk_expert.astype(jnp.int32)
    own = blk_expert[:, None] == e_ids
    rows_left = jnp.sum(jnp.where(own, pstart + counts, 0), axis=-1) - blk_start
    n_valid = jnp.where(blk_start < pend[-1], jnp.clip(rows_left, 0, MOE_ROWS), 0).astype(jnp.int32)
    pad_len = padded - counts
    pad_end = jnp.cumsum(pad_len)
    j = jnp.arange(n_pad, dtype=jnp.int32)
    pe = jnp.sum(j[:, None] >= pad_end[None, :], axis=1)
    sel = jnp.minimum(pe, n_e - 1)[:, None] == e_ids
    in_expert = jnp.sum(jnp.where(sel, pstart + counts - (pad_end - pad_len), 0), axis=-1) + j
    pad_rows = jnp.where(pe < n_e, in_expert, pend[-1] + j - pad_end[-1]).astype(jnp.int32)

    xs = _sc_dispatch(h, dest_flat, pad_rows)
    yb = _experts(xs, blk_expert, n_valid, src_blk, layer, w_gu, b_gu, w_down, b_down)
    y4 = _sc_gather(yb, dest_flat)
    return _combine(y4, gate_t.T, x, g2, final_gain, seq=seq)


def _dup_heads(w, n_heads):
    d = w.shape[0]
    return jnp.repeat(w.reshape(d, n_heads, 1, HEAD_DIM), 2, axis=2).reshape(d, n_heads * LANES)


def _mixer_mla_swa(x, norm, g1, cos, sin, w_in, g_q, w_qb, g_kv, w_kvb, sink, w_out, *, bsz, seq):
    d = x.shape[1]
    o = np.cumsum((0, MLA_Q_LORA, MLA_KV_LORA, MLA_ROPE, SWA_Q_HEADS * HEAD_DIM,
                   SWA_KV_HEADS * HEAD_DIM, SWA_KV_HEADS * HEAD_DIM))
    w_cq, w_ckv, w_kr, w_qs, w_ks, w_vs = (w_in[:, o[i]:o[i + 1]] for i in range(6))
    w_lat = jnp.concatenate([w_cq, jnp.zeros((d, LANES), F32), w_ckv], axis=1).astype(BF16)
    lat = _proj([x], [w_lat], prologue="ada", norm=norm, seq=seq, tm=512, tn=w_lat.shape[1],
                out_dtype=F32, name="proj_latent")
    w_b = jnp.concatenate([_pair_layout(w_qs), _pair_layout(_dup_heads(w_ks, SWA_KV_HEADS)),
                           _dup_heads(w_vs, SWA_KV_HEADS),
                           w_kr, jnp.zeros((d, LANES - MLA_ROPE), F32)], axis=1).astype(BF16)
    qkv_b = _proj([x], [w_b], prologue="ada", norm=norm, seq=seq, tm=512, tn=w_b.shape[1], out_dtype=BF16,
                  rope=(cos, sin), rope_pattern=(ROPE_PAIR,) * 10 + (0, 0) + (ROPE_ROTATE,),
                  group_scales=(QK_LOG2_SCALE,) * 8 + (1.0,) * 5, name="proj_swa")
    kr_col = (w_b.shape[1] - LANES) // LANES
    qd = MLA_NOPE + MLA_ROPE
    w_q = w_qb.reshape(MLA_Q_LORA, MLA_HEADS, qd)
    w_q = jnp.concatenate([w_q, jnp.zeros((MLA_Q_LORA, MLA_HEADS, 2 * LANES - qd), F32)], axis=2)
    w_q = w_q.reshape(MLA_Q_LORA, MLA_HEADS * 2 * LANES).astype(BF16)
    q_cat = _proj([lat], [w_q], prologue="rms", norm=(g_q.reshape(1, -1),), lhs_cols=(MLA_Q_LORA, 0),
                  seq=seq, tm=512, tn=w_q.shape[1], out_dtype=BF16, rope=(cos, sin),
                  rope_pattern=(0, 1) * MLA_HEADS, out_scale=qd ** -0.5, name="proj_mla_q")
    w_kv = w_kvb.reshape(MLA_KV_LORA, MLA_HEADS, MLA_NOPE + MLA_V)
    w_kv = jnp.concatenate([w_kv[:, :, :MLA_NOPE].reshape(MLA_KV_LORA, -1),
                            w_kv[:, :, MLA_NOPE:].reshape(MLA_KV_LORA, -1)], axis=1).astype(BF16)
    kv = _proj([lat], [w_kv], prologue="rms", norm=(g_kv.reshape(1, -1),), lhs_cols=(MLA_KV_LORA, 2),
               seq=seq, tm=512, tn=w_kv.shape[1], out_dtype=BF16, name="proj_mla_kv")
    o_a = _mla_attention(q_cat, kv, qkv_b, kr_col, bsz=bsz, seq=seq, tq=256)
    qkv_b4 = qkv_b.reshape(bsz, 1, seq, qkv_b.shape[1])
    (o_b,) = _banded_attention(qkv_b4, qkv_b4, qkv_b4, bsz=bsz, seq=seq, dil=1, radius=SWA_RADIUS,
                               q_col=0, k_col=4, v_col=5, kv_width=2 * LANES,
                               pairs_per_kv=SWA_Q_HEADS // SWA_KV_HEADS // 2, sink=sink, want_lse=False,
                               name="swa_attention")
    o_b = o_b.reshape(bsz * seq, SWA_Q_HEADS * HEAD_DIM)
    na = MLA_HEADS * MLA_V
    return _proj([o_a, o_b], [w_out[:na].astype(BF16), w_out[na:].astype(BF16)], prologue="plain",
                 seq=seq, tm=512, tn=d, out_dtype=F32, residual=(x, g1), name="proj_out_ab")


def _mixer_dilated(x, norm, g1, cos, sin, w_in, w_out, *, bsz, seq):
    d = x.shape[1]
    qw = DIL_HEADS * HEAD_DIM
    w_bf = w_in.astype(BF16)
    outs, lses = [], []
    for g, (window, dil) in enumerate(DIL_PATTERN):
        w_g = w_bf[:, g * 3 * qw:(g + 1) * 3 * qw]
        w_g = jnp.concatenate([_pair_layout(w_g[:, :2 * qw]), w_g[:, 2 * qw:]], axis=1)
        qkv = _proj([x], [w_g], prologue="ada", norm=norm, seq=seq, tm=512, tn=3 * qw, sub_tn=qw, out_dtype=BF16,
                    rope=(cos, sin), rope_pattern=(ROPE_PAIR,) * (2 * qw // LANES) + (0,) * (qw // LANES),
                    group_scales=(QK_LOG2_SCALE,) * (qw // LANES) + (1.0,) * (2 * qw // LANES),
                    out_dil=dil, name=f"proj_dil_{dil}")
        qkv = qkv.reshape(bsz, dil, seq // dil, 3 * qw)
        o, lse = _banded_attention(qkv, qkv, qkv, bsz=bsz, seq=seq, dil=dil, radius=window // (2 * dil),
                                   q_col=0, k_col=1, v_col=2, kv_width=qw, pairs_per_kv=1,
                                   want_lse=True, name=f"dilated_attention_{dil}")
        outs.append(o)
        lses.append(lse)
    return _proj(outs + lses, [w_out.astype(BF16)], prologue="dil", seq=seq, tm=512, tn=d, out_dtype=F32,
                 residual=(x, g1), in_dils=tuple(dil for _, dil in DIL_PATTERN), name="proj_out_c")


def kernel(x, c, positions, w_mod, b_mod, g_norm_mix, g_norm_ffn, w_in_ab, mla_g_q, mla_w_qb, mla_g_kv,
           mla_w_kvb, swa_sink, w_out_ab, w_in_c, w_out_c, w_router, b_router, w_gu, b_gu, w_down, b_down,
           g_final):
    bsz, seq, d = x.shape
    depth = w_mod.shape[0]
    cos, sin = _rope_tables(positions)
    mod = _modulation(c, w_mod, b_mod)
    xt = x.reshape(bsz * seq, d)
    for layer in range(depth):
        sh1, sc1, g1, sh2, sc2, g2 = (mod[layer, :, i * d:(i + 1) * d].reshape(bsz, 1, d) for i in range(6))
        li = layer // 2
        norm = (g_norm_mix[layer].reshape(1, d), sc1, sh1)
        if layer % 2 == 0:
            xt = _mixer_mla_swa(xt, norm, g1, cos, sin, w_in_ab[li], mla_g_q[li], mla_w_qb[li], mla_g_kv[li],
                                mla_w_kvb[li], swa_sink[li], w_out_ab[li], bsz=bsz, seq=seq)
        else:
            xt = _mixer_dilated(xt, norm, g1, cos, sin, w_in_c[li], w_out_c[li], bsz=bsz, seq=seq)
        norm = (g_norm_ffn[layer].reshape(1, d), sc2, sh2)
        xt = _moe(xt, norm, g2, w_router[layer], b_router[layer], layer, w_gu, b_gu, w_down, b_down,
                  g_final if layer == depth - 1 else None, seq=seq)
    return xt.reshape(bsz, seq, d)
```

```python
import functools

import jax
import jax.numpy as jnp
import numpy as np
from jax import lax
from jax.experimental import pallas as pl
from jax.experimental.pallas import tpu as pltpu
from jax.experimental.pallas import tpu_sc as plsc

F32 = jnp.float32
BF16 = jnp.bfloat16

EPS = 1e-6
ROPE_THETA = 10000.0
LANES = 128
HEAD_DIM = 64
ROPE_HALF = HEAD_DIM // 2

MLA_HEADS = 8
MLA_Q_LORA = 384
MLA_KV_LORA = 256
MLA_NOPE = 128
MLA_ROPE = 64
MLA_V = 128
SWA_Q_HEADS = 16
SWA_KV_HEADS = 2
SWA_RADIUS = 128
DIL_PATTERN = ((128, 1), (512, 4), (2048, 16))
DIL_HEADS = 16
N_EXPERTS = 32
TOP_K = 4
SWIGLU_LIMIT = 7.0
SWIGLU_ALPHA = 1.702
MOE_ROWS = 512
PROJ_ROWS = 1024
BAND_STEP_ROWS = 512
BAND_SUB_ROWS = 128
LOG2_E = float(np.log2(np.e))
QK_LOG2_SCALE = HEAD_DIM ** -0.5 * LOG2_E

VMEM_LIMIT = 48 * 1024 * 1024
EXPERT_VMEM_LIMIT = 56 * 1024 * 1024


def _cparams(sem):
    return pltpu.CompilerParams(dimension_semantics=sem, vmem_limit_bytes=VMEM_LIMIT)


def _mod_kernel(c_ref, w_ref, b_ref, o_ref):
    c = c_ref[...]
    ca = c * (1.0 / (1.0 + jnp.exp(-c)))
    o_ref[...] = jnp.dot(ca, w_ref[...], preferred_element_type=F32,
                         precision=lax.Precision.HIGHEST) + b_ref[...]


def _modulation(c, w_mod, b_mod):
    depth, d, n = w_mod.shape
    bsz = c.shape[0]
    tn = 1536
    return pl.pallas_call(
        _mod_kernel,
        grid=(depth, n // tn),
        in_specs=[pl.BlockSpec((bsz, d), lambda l, j: (0, 0)),
                  pl.BlockSpec((None, d, tn), lambda l, j: (l, 0, j)),
                  pl.BlockSpec((None, 1, tn), lambda l, j: (l, 0, j))],
        out_specs=pl.BlockSpec((None, bsz, tn), lambda l, j: (l, 0, j)),
        out_shape=jax.ShapeDtypeStruct((depth, bsz, n), F32),
        compiler_params=_cparams(("parallel", "parallel")),
        name="modulation",
    )(c, w_mod, b_mod.reshape(depth, 1, n))


def _rope_table_kernel(pos_ref, inv_ref, sign_ref, cos_ref, sin_ref):
    ang = inv_ref[...] * pos_ref[...].astype(F32)
    reps = LANES // ROPE_HALF
    cos_ref[...] = jnp.transpose(jnp.concatenate([jnp.cos(ang)] * reps, axis=0))
    sin_ref[...] = jnp.transpose(jnp.concatenate([jnp.sin(ang)] * reps, axis=0) * sign_ref[...])


def _rope_tables(positions):
    bsz, s = positions.shape
    inv = ROPE_THETA ** (-jnp.arange(0, HEAD_DIM, 2, dtype=F32) / HEAD_DIM)
    sign = jnp.where((jnp.arange(LANES) % HEAD_DIM) < ROPE_HALF, -1.0, 1.0).astype(F32).reshape(LANES, 1)
    out = jax.ShapeDtypeStruct((bsz * s, LANES), F32)
    return pl.pallas_call(
        _rope_table_kernel,
        grid=(bsz,),
        in_specs=[pl.BlockSpec((None, 1, s), lambda b: (b, 0, 0)),
                  pl.BlockSpec((ROPE_HALF, 1), lambda b: (0, 0)),
                  pl.BlockSpec((LANES, 1), lambda b: (0, 0))],
        out_specs=[pl.BlockSpec((s, LANES), lambda b: (b, 0)),
                   pl.BlockSpec((s, LANES), lambda b: (b, 0))],
        out_shape=[out, out],
        compiler_params=_cparams(("parallel",)),
        name="rope_tables",
    )(positions.reshape(bsz, 1, s), inv.reshape(ROPE_HALF, 1), sign)


def _rms(x):
    return x * lax.rsqrt(jnp.mean(x * x, axis=-1, keepdims=True) + EPS)


def _ada_norm(x, g, sc, sh):
    return (_rms(x) * g) * (1.0 + sc) + sh


def _pack_bf16_pairs(x):
    w = x.shape[1] // 2
    hi = pltpu.bitcast(x[:, :w].astype(BF16).astype(F32), jnp.int32)
    lo = pltpu.bitcast(x[:, w:].astype(BF16).astype(F32), jnp.int32)
    return hi | lax.shift_right_logical(lo, 16)


def _unpack_bf16_pairs(p):
    hi = pltpu.bitcast(p & jnp.int32(-65536), F32)
    lo = pltpu.bitcast(lax.shift_left(p, 16), F32)
    return hi, lo


ROPE_ROTATE, ROPE_PAIR = 1, 2


def _pair_layout(w):
    k, n = w.shape
    w = w.reshape(k, n // LANES, 2, 2, ROPE_HALF)
    return jnp.swapaxes(w, 2, 3).reshape(k, n)


def _rope_group(a, cos, sin, first_half):
    rot = jnp.where(first_half, pltpu.roll(a, LANES - ROPE_HALF, 1), pltpu.roll(a, ROPE_HALF, 1))
    return a * cos + rot * sin


def _proj_kernel(*refs, prologue, n_lhs, rope_pattern, group_scales, sub_tn, out_scale, residual, in_dils,
                 out_dil):
    pos = 0
    if prologue == "ada":
        x_ref, g_ref, sc_ref, sh_ref = refs[:4]
        pos = 4
    elif prologue == "rms":
        x_ref, g_ref = refs[:2]
        pos = 2
    elif prologue == "dil":
        o_refs = refs[0:3]
        l_refs = refs[3:6]
        pos = 6
    else:
        x_refs = refs[:n_lhs]
        pos = n_lhs
    w_refs = refs[pos:pos + n_lhs]
    pos += n_lhs
    if rope_pattern is not None:
        cos_ref, sin_ref = refs[pos:pos + 2]
        pos += 2
    if residual:
        res_ref, gate_ref = refs[pos:pos + 2]
        pos += 2
    o_ref = refs[pos]
    pos += 1
    h_ref = refs[pos] if prologue != "plain" else None
    slab_ref = refs[pos + 1] if (prologue == "dil" or out_dil > 1) else None

    if prologue != "plain":
        @pl.when(pl.program_id(1) == 0)
        def _():
            if prologue == "ada":
                h_ref[...] = _ada_norm(x_ref[...], g_ref[...], sc_ref[...], sh_ref[...]).astype(BF16)
            elif prologue == "rms":
                h_ref[...] = (_rms(x_ref[...]) * g_ref[...]).astype(BF16)
            else:
                tm = h_ref.shape[0]
                for c in range(h_ref.shape[1] // LANES):
                    cols = slice(c * LANES, (c + 1) * LANES)
                    vals = []
                    for i, (ref, dil) in enumerate(zip(o_refs + l_refs, in_dils + in_dils)):
                        if dil == 1:
                            vals.append(ref[0, :, cols].astype(F32))
                        else:
                            for r in range(dil):
                                slab_ref[i, pl.ds(r, tm // dil, stride=dil), :] = ref[r, :, cols].astype(F32)
                            vals.append(slab_ref[i])
                    o0, o1, o2, l0, l1, l2 = vals
                    m = jnp.maximum(jnp.maximum(l0, l1), l2)
                    e0, e1, e2 = jnp.exp2(l0 - m), jnp.exp2(l1 - m), jnp.exp2(l2 - m)
                    h_ref[:, cols] = ((e0 * o0 + e1 * o1 + e2 * o2) / (e0 + e1 + e2)).astype(BF16)

    n_total = w_refs[0].shape[1]
    tn = n_total if sub_tn is None else sub_tn

    def matmul(t):
        cols = slice(t * tn, (t + 1) * tn)
        if prologue != "plain":
            acc = jnp.dot(h_ref[...], w_refs[0][:, cols], preferred_element_type=F32)
        else:
            acc = jnp.dot(x_refs[0][...], w_refs[0][:, cols], preferred_element_type=F32)
            for xr, wr in zip(x_refs[1:], w_refs[1:]):
                acc = acc + jnp.dot(xr[...], wr[:, cols], preferred_element_type=F32)
        return acc * out_scale if out_scale != 1.0 else acc

    if residual:
        o_ref[...] = (res_ref[...] + gate_ref[...] * matmul(0)).astype(o_ref.dtype)
        return

    pattern = rope_pattern if rope_pattern is not None else (0,) * (n_total // LANES)
    if any(pattern):
        cos = cos_ref[...]
        sin = sin_ref[...]
        lane = lax.broadcasted_iota(jnp.int32, cos.shape, 1)
        first_half = (lane % HEAD_DIM) < ROPE_HALF
        if ROPE_PAIR in pattern:
            sin_pair = jnp.where(lane < HEAD_DIM, -1.0, 1.0) * jnp.where(first_half, -sin, sin)

    def epilogue(acc, t):
        tm = acc.shape[0]
        for c in range(tn // LANES):
            g = t * (tn // LANES) + c
            cols = slice(g * LANES, (g + 1) * LANES)
            a = acc[:, c * LANES:(c + 1) * LANES]
            if pattern[g] == ROPE_ROTATE:
                a = _rope_group(a, cos, sin, first_half)
            elif pattern[g] == ROPE_PAIR:
                a = a * cos + pltpu.roll(a, HEAD_DIM, 1) * sin_pair
            if group_scales is not None and group_scales[g] != 1.0:
                a = a * group_scales[g]
            if out_dil == 1:
                o_ref[:, cols] = a.astype(o_ref.dtype)
            else:
                slab_ref[0] = a
                for r in range(out_dil):
                    o_ref[r, :, cols] = slab_ref[0, pl.ds(r, tm // out_dil, stride=out_dil), :].astype(o_ref.dtype)

    acc_next = matmul(0)
    for t in range(n_total // tn):
        acc = acc_next
        if t + 1 < n_total // tn:
            acc_next = matmul(t + 1)
        epilogue(acc, t)


def _proj(lhs, ws, *, prologue, seq, tm, tn, out_dtype, norm=None, lhs_cols=None, rope=None,
          rope_pattern=None, group_scales=None, sub_tn=None, out_scale=1.0, residual=None, in_dils=None,
          out_dil=1, name):
    k, n = ws[0].shape
    bsz_seq = lhs[0].shape[0] if prologue != "dil" else lhs[0].shape[0] * seq
    t = bsz_seq
    assert t % tm == 0 and n % tn == 0 and seq % tm == 0
    bpb = seq // tm
    n_lhs = len(ws)
    args, specs = [], []
    row = lambda i, j: (i, 0)
    if prologue in ("ada", "rms"):
        kw, kc = lhs_cols if lhs_cols is not None else (k, 0)
        args.append(lhs[0])
        specs.append(pl.BlockSpec((tm, kw), lambda i, j: (i, kc)))
        args.append(norm[0])
        specs.append(pl.BlockSpec((1, k), lambda i, j: (0, 0)))
        if prologue == "ada":
            for v in norm[1:3]:
                args.append(v)
                specs.append(pl.BlockSpec((None, 1, k), lambda i, j: (i // bpb, 0, 0)))
    elif prologue == "dil":
        for a, dil in zip(lhs, in_dils + in_dils):
            args.append(a)
            specs.append(pl.BlockSpec((None, dil, tm // dil, k), lambda i, j: (i // bpb, 0, i % bpb, 0)))
    else:
        for a, w in zip(lhs, ws):
            args.append(a)
            specs.append(pl.BlockSpec((tm, w.shape[0]), row))
    for w in ws:
        args.append(w)
        specs.append(pl.BlockSpec((w.shape[0], tn), lambda i, j: (0, j)))
    if rope_pattern is not None:
        assert len(rope_pattern) == tn // LANES
        for tab in rope:
            args.append(tab)
            specs.append(pl.BlockSpec((tm, LANES), row))
    if residual is not None:
        res, gate = residual
        args.append(res)
        specs.append(pl.BlockSpec((tm, tn), lambda i, j: (i, j)))
        args.append(gate)
        specs.append(pl.BlockSpec((None, 1, tn), lambda i, j: (i // bpb, 0, j)))
    scratch = [] if prologue == "plain" else [pltpu.VMEM((tm, k), BF16)]
    if prologue == "dil":
        scratch.append(pltpu.VMEM((2 * len(in_dils), tm, LANES), F32))
    elif out_dil > 1:
        scratch.append(pltpu.VMEM((1, tm, LANES), F32))
    if out_dil == 1:
        out_spec = pl.BlockSpec((tm, tn), lambda i, j: (i, j))
        out_shape = jax.ShapeDtypeStruct((t, n), out_dtype)
    else:
        out_spec = pl.BlockSpec((None, out_dil, tm // out_dil, tn), lambda i, j: (i // bpb, 0, i % bpb, j))
        out_shape = jax.ShapeDtypeStruct((t // seq, out_dil, seq // out_dil, n), out_dtype)
    kern = functools.partial(_proj_kernel, prologue=prologue, n_lhs=n_lhs, rope_pattern=rope_pattern,
                             group_scales=group_scales, sub_tn=sub_tn, out_scale=out_scale,
                             residual=residual is not None,
                             in_dils=in_dils, out_dil=out_dil)
    return pl.pallas_call(
        kern,
        grid=(t // tm, n // tn),
        in_specs=specs,
        out_specs=out_spec,
        out_shape=out_shape,
        scratch_shapes=scratch,
        compiler_params=_cparams(("parallel", "arbitrary")),
        name=name,
    )(*args)


MLA_HEADS_PER_STEP = 4


def _mla_scores(q_ref, kn_ref, kr, h):
    k = jnp.concatenate([kn_ref[:, h * LANES:(h + 1) * LANES], kr], axis=1)
    q = q_ref[:, h * 2 * LANES:(h + 1) * 2 * LANES]
    return lax.dot_general(q, k, (((1,), (1,)), ((), ())), preferred_element_type=F32)


def _mla_kernel(q_ref, kn_ref, kr_ref, v_ref, o_ref):
    kr = kr_ref[...]
    s_next = _mla_scores(q_ref, kn_ref, kr, 0)
    for h in range(MLA_HEADS_PER_STEP):
        s = s_next
        if h + 1 < MLA_HEADS_PER_STEP:
            s_next = _mla_scores(q_ref, kn_ref, kr, h + 1)
        m = jnp.max(s, axis=1, keepdims=True)
        p = jnp.exp(s - m).astype(BF16)
        v = v_ref[:, h * LANES:(h + 1) * LANES]
        oe = jnp.dot(p, jnp.concatenate([v, jnp.ones_like(v)], axis=1), preferred_element_type=F32)
        o_ref[:, h * LANES:(h + 1) * LANES] = (oe[:, :LANES] / oe[:, LANES:]).astype(o_ref.dtype)


def _mla_attention(q_cat, kv, qkv_b, kr_col, *, bsz, seq, tq):
    t = bsz * seq
    nq = seq // tq
    hps = MLA_HEADS_PER_STEP
    groups = MLA_HEADS // hps
    return pl.pallas_call(
        _mla_kernel,
        grid=(bsz, groups, nq),
        in_specs=[pl.BlockSpec((tq, hps * 2 * LANES), lambda b, h, i: (b * nq + i, h)),
                  pl.BlockSpec((seq, hps * LANES), lambda b, h, i: (b, h)),
                  pl.BlockSpec((seq, LANES), lambda b, h, i: (b, kr_col)),
                  pl.BlockSpec((seq, hps * LANES), lambda b, h, i: (b, groups + h))],
        out_specs=pl.BlockSpec((tq, hps * LANES), lambda b, h, i: (b * nq + i, h)),
        out_shape=jax.ShapeDtypeStruct((t, MLA_HEADS * MLA_V), BF16),
        compiler_params=_cparams(("parallel", "parallel", "parallel")),
        name="mla_attention",
    )(q_cat, kv, qkv_b, kv)


def _banded_kernel(*refs, n_seqs, has_sink, want_lse, **kw):
    sink_ref = refs[0] if has_sink else None
    refs = refs[1:] if has_sink else refs
    for rr in range(n_seqs):
        ins = [r.at[rr] for r in refs[:7]]
        o_ref = refs[7].at[rr]
        lse_ref = refs[8].at[rr] if want_lse else None
        _banded_sequence(sink_ref, *ins, o_ref, lse_ref, has_sink=has_sink, want_lse=want_lse, **kw)


def _banded_sequence(sink_ref, q_ref, kp_ref, kc_ref, kn_ref, vp_ref, vc_ref, vn_ref, o_ref, lse_ref, *,
                     radius, tq, hb, length, n_pairs, pairs_per_kv, has_sink, want_lse):
    j = pl.program_id(2)
    kw = jnp.concatenate([kp_ref[...], kc_ref[...], kn_ref[...]], axis=0)
    vw = jnp.concatenate([vp_ref[...], vc_ref[...], vn_ref[...]], axis=0)
    qn = min(tq, BAND_SUB_ROWS)
    kn = qn + 2 * hb
    lane = lax.broadcasted_iota(jnp.int32, (qn, LANES), 1)
    low = lane < HEAD_DIM
    first = (lane % HEAD_DIM) < ROPE_HALF

    for q0 in range(0, tq, qn):
        qpos = j * tq + q0 + lax.broadcasted_iota(jnp.int32, (qn, kn), 0)
        kpos = j * tq - hb + q0 + lax.broadcasted_iota(jnp.int32, (qn, kn), 1)
        valid = (jnp.abs(qpos - kpos) <= radius) & (kpos >= 0) & (kpos < length)
        for p in range(n_pairs):
            c = p // pairs_per_kv
            qp = q_ref[q0:q0 + qn, p * LANES:(p + 1) * LANES]
            kp = kw[q0:q0 + kn, c * LANES:(c + 1) * LANES]
            vp = vw[q0:q0 + kn, c * LANES:(c + 1) * LANES]
            vp = jnp.concatenate([vp, jnp.ones_like(vp)], axis=1)
            halves = []
            for half in range(2):
                qh = jnp.where(first if half == 0 else ~first, qp, jnp.zeros_like(qp))
                s = lax.dot_general(qh, kp, (((1,), (1,)), ((), ())), preferred_element_type=F32)
                s = jnp.where(valid, s, -jnp.inf)
                m = jnp.max(s, axis=1, keepdims=True)
                if has_sink:
                    sk = sink_ref[2 * p + half] * LOG2_E
                    m = jnp.maximum(m, sk)
                oe = jnp.dot(jnp.exp2(s - m).astype(BF16), vp, preferred_element_type=F32)
                den = oe[:, LANES:]
                if has_sink:
                    den = den + jnp.exp2(sk - m)
                halves.append((oe[:, :LANES] / den, m + jnp.log2(den) if want_lse else None))
            o_ref[q0:q0 + qn, p * LANES:(p + 1) * LANES] = jnp.where(
                low, halves[0][0], halves[1][0]).astype(o_ref.dtype)
            if want_lse:
                lse_ref[q0:q0 + qn, p * LANES:(p + 1) * LANES] = jnp.where(low, halves[0][1], halves[1][1])


def _banded_attention(q_arr, k_arr, v_arr, *, bsz, seq, dil, radius, q_col, k_col, v_col, kv_width,
                      pairs_per_kv, sink=None, want_lse, name):
    length = seq // dil
    hb = radius
    tq = min(BAND_STEP_ROWS, length)
    assert tq % hb == 0 and length % tq == 0
    qw = DIL_HEADS * HEAD_DIM
    n_pairs = qw // LANES
    rep = tq // hb
    last_hb = length // hb - 1
    n_seqs = min(dil, max(1, BAND_STEP_ROWS // length))
    assert dil % n_seqs == 0

    def prev_spec(col):
        return pl.BlockSpec((None, n_seqs, hb, kv_width),
                            lambda b, r, j: (b, r, jnp.maximum(j * rep - 1, 0), col))

    def cur_spec(col):
        return pl.BlockSpec((None, n_seqs, tq, kv_width), lambda b, r, j: (b, r, j, col))

    def next_spec(col):
        return pl.BlockSpec((None, n_seqs, hb, kv_width),
                            lambda b, r, j: (b, r, jnp.minimum((j + 1) * rep, last_hb), col))

    in_specs = [pl.BlockSpec((None, n_seqs, tq, qw), lambda b, r, j: (b, r, j, q_col)),
                prev_spec(k_col), cur_spec(k_col), next_spec(k_col),
                prev_spec(v_col), cur_spec(v_col), next_spec(v_col)]
    args = [q_arr, k_arr, k_arr, k_arr, v_arr, v_arr, v_arr]
    if sink is not None:
        in_specs = [pl.BlockSpec(memory_space=pltpu.SMEM)] + in_specs
        args = [sink] + args
    out_spec = pl.BlockSpec((None, n_seqs, tq, qw), lambda b, r, j: (b, r, j, 0))
    out_specs = [out_spec]
    out_shape = [jax.ShapeDtypeStruct((bsz, dil, length, qw), BF16)]
    if want_lse:
        out_specs.append(out_spec)
        out_shape.append(jax.ShapeDtypeStruct((bsz, dil, length, qw), F32))
    kern = functools.partial(_banded_kernel, n_seqs=n_seqs, radius=radius, tq=tq, hb=hb, length=length,
                             n_pairs=n_pairs, pairs_per_kv=pairs_per_kv, has_sink=sink is not None,
                             want_lse=want_lse)
    return pl.pallas_call(
        kern,
        grid=(bsz, dil // n_seqs, length // tq),
        in_specs=in_specs,
        out_specs=out_specs,
        out_shape=out_shape,
        compiler_params=_cparams(("parallel", "parallel", "parallel")),
        name=name,
    )(*args)


def _router_kernel(x_ref, g_ref, sc_ref, sh_ref, wt_ref, b_ref, tri_ref,
                   idx_ref, gate_ref, rank_ref, cnt_ref, h_ref, carry_ref):
    @pl.when(pl.program_id(0) == 0)
    def _():
        carry_ref[...] = jnp.zeros_like(carry_ref)

    h = _ada_norm(x_ref[...], g_ref[...], sc_ref[...], sh_ref[...])
    h_ref[...] = _pack_bf16_pairs(h)
    lt = lax.dot_general(wt_ref[...], h, (((1,), (1,)), ((), ())), preferred_element_type=F32,
                         precision=lax.Precision.HIGHEST) + b_ref[...]
    n_e, tm = lt.shape
    e_iota = lax.broadcasted_iota(jnp.int32, (n_e, tm), 0)
    vals, sels = [], []
    for k in range(TOP_K):
        m = jnp.max(lt, axis=0, keepdims=True)
        idx = jnp.min(jnp.where(lt == m, e_iota, n_e), axis=0, keepdims=True)
        sel = e_iota == idx
        idx_ref[k:k + 1, :] = idx
        vals.append(m)
        sels.append(sel)
        lt = jnp.where(sel, -jnp.inf, lt)
    exps = [jnp.exp(v - vals[0]) for v in vals]
    den = exps[0] + exps[1] + exps[2] + exps[3]
    for k in range(TOP_K):
        gate_ref[k:k + 1, :] = exps[k] / den

    onehot = jnp.where(sels[0] | sels[1] | sels[2] | sels[3], 1.0, 0.0)
    carry = carry_ref[:, 0:1]
    before = jnp.dot(onehot.astype(BF16), tri_ref[...], preferred_element_type=F32) + carry
    for k in range(TOP_K):
        rank_ref[k:k + 1, :] = jnp.sum(jnp.where(sels[k], before, 0.0), axis=0, keepdims=True).astype(jnp.int32)
    total = carry + jnp.sum(onehot, axis=1, keepdims=True)
    carry_ref[...] = jnp.broadcast_to(total, carry_ref.shape)
    cnt_ref[...] = jnp.broadcast_to(total, cnt_ref.shape)


def _router(x, norm, w_router, b_router, *, seq, tm=512):
    t, d = x.shape
    n_e = w_router.shape[1]
    bpb = seq // tm
    tri = (jnp.arange(tm)[:, None] < jnp.arange(tm)[None, :]).astype(BF16)
    g, sc, sh = norm
    out_i = jax.ShapeDtypeStruct((TOP_K, t), jnp.int32)
    tok = pl.BlockSpec((TOP_K, tm), lambda i: (0, i))
    return pl.pallas_call(
        _router_kernel,
        grid=(t // tm,),
        in_specs=[pl.BlockSpec((tm, d), lambda i: (i, 0)),
                  pl.BlockSpec((1, d), lambda i: (0, 0)),
                  pl.BlockSpec((None, 1, d), lambda i: (i // bpb, 0, 0)),
                  pl.BlockSpec((None, 1, d), lambda i: (i // bpb, 0, 0)),
                  pl.BlockSpec((n_e, d), lambda i: (0, 0)),
                  pl.BlockSpec((n_e, 1), lambda i: (0, 0)),
                  pl.BlockSpec((tm, tm), lambda i: (0, 0))],
        out_specs=[tok, tok, tok, pl.BlockSpec((n_e, LANES), lambda i: (0, 0)),
                   pl.BlockSpec((tm, d // 2), lambda i: (i, 0))],
        out_shape=[out_i, jax.ShapeDtypeStruct((TOP_K, t), F32), out_i,
                   jax.ShapeDtypeStruct((n_e, LANES), F32), jax.ShapeDtypeStruct((t, d // 2), jnp.int32)],
        scratch_shapes=[pltpu.VMEM((n_e, LANES), F32)],
        compiler_params=_cparams(("arbitrary",)),
        name="router",
    )(x, g, sc, sh, w_router.T, b_router.reshape(n_e, 1), tri)


SC_CORES = 2
SC_SUBCORES = 16
SC_WORKERS = SC_CORES * SC_SUBCORES
SC_ROWS = 64


def _sc_worker_id():
    return lax.axis_index("subcore") * SC_CORES + lax.axis_index("core")


def _sc_mesh():
    return plsc.VectorSubcoreMesh(core_axis_name="core", subcore_axis_name="subcore")


def _sc_dispatch(h, dest_flat, pad_rows):
    t, d = h.shape
    n_pad = pad_rows.shape[0]
    n_rows = TOP_K * t + n_pad
    tok_w = t // SC_WORKERS
    pad_w = n_pad // SC_WORKERS
    nb = tok_w // SC_ROWS
    assert tok_w % (2 * SC_ROWS) == 0 and pad_w % SC_ROWS == 0

    @functools.partial(
        pl.kernel, out_type=jax.ShapeDtypeStruct((n_rows, d), h.dtype), mesh=_sc_mesh(),
        scratch_types=[pltpu.VMEM((SC_ROWS,), jnp.int32), pltpu.VMEM((2, SC_ROWS, d), h.dtype),
                       pltpu.SemaphoreType.DMA((2,))])
    def scatter_rows(h_hbm, dest_hbm, pad_hbm, zero_hbm, xs_hbm, idx_v, rows_v, sem):
        wid = _sc_worker_id()
        base = wid * tok_w

        def load(b, slot):
            return pltpu.make_async_copy(h_hbm.at[pl.ds(base + b * SC_ROWS, SC_ROWS)], rows_v.at[slot],
                                         sem.at[slot])

        load(0, 0).start()

        @pl.loop(0, nb, step=2)
        def _(i):
            for slot in range(2):
                b = i + slot
                load(b, slot).wait()

                @pl.when(b + 1 < nb)
                def _():
                    load(b + 1, 1 - slot).start()

                for k in range(TOP_K):
                    pltpu.sync_copy(dest_hbm.at[pl.ds(k * t + base + b * SC_ROWS, SC_ROWS)], idx_v)
                    pltpu.sync_copy(rows_v.at[slot], xs_hbm.at[idx_v])

        pltpu.sync_copy(zero_hbm, rows_v.at[0])

        @pl.loop(0, pad_w // SC_ROWS)
        def _(b):
            pltpu.sync_copy(pad_hbm.at[pl.ds(wid * pad_w + b * SC_ROWS, SC_ROWS)], idx_v)
            pltpu.sync_copy(rows_v.at[0], xs_hbm.at[idx_v])

    return scatter_rows(h, dest_flat, pad_rows, jnp.zeros((SC_ROWS, d), h.dtype))


def _sc_gather(table, idx):
    n = idx.shape[0]
    d = table.shape[1]
    per_w = n // SC_WORKERS
    nb = per_w // SC_ROWS
    assert per_w % (2 * SC_ROWS) == 0

    @functools.partial(
        pl.kernel, out_type=jax.ShapeDtypeStruct((n, d), table.dtype), mesh=_sc_mesh(),
        scratch_types=[pltpu.VMEM((2, SC_ROWS), jnp.int32), pltpu.VMEM((2, SC_ROWS, d), table.dtype),
                       pltpu.SemaphoreType.DMA((2,))])
    def gather_rows(table_hbm, idx_hbm, out_hbm, idx_v, rows_v, sem):
        base = _sc_worker_id() * per_w

        def gather(slot):
            return pltpu.make_async_copy(table_hbm.at[idx_v.at[slot]], rows_v.at[slot], sem.at[slot])

        def start(b, slot):
            pltpu.sync_copy(idx_hbm.at[pl.ds(base + b * SC_ROWS, SC_ROWS)], idx_v.at[slot])
            gather(slot).start()

        start(0, 0)
        start(1, 1)

        @pl.loop(0, nb, step=2)
        def _(i):
            for slot in range(2):
                b = i + slot
                gather(slot).wait()
                pltpu.sync_copy(rows_v.at[slot], out_hbm.at[pl.ds(base + b * SC_ROWS, SC_ROWS)])

                @pl.when(b + 2 < nb)
                def _():
                    start(b + 2, slot)

    return gather_rows(table, idx)


def _expert_kernel(be_ref, nv_ref, sb_ref, xs_ref, wgu_ref, bgu_ref, wd_ref, bd_ref, y_ref, wgu_bf, wd_bf):
    i = pl.program_id(0)
    used = nv_ref[i] > 0

    @pl.when(used & ((i == 0) | (be_ref[i] != be_ref[jnp.maximum(i - 1, 0)])))
    def _():
        wgu_bf[...] = wgu_ref[...].astype(BF16)
        wd_bf[...] = wd_ref[...].astype(BF16)

    @pl.when(used)
    def _():
        d_ff = wd_ref.shape[0]
        x_hi, x_lo = _unpack_bf16_pairs(xs_ref[...])
        x = jnp.concatenate([x_hi.astype(BF16), x_lo.astype(BF16)], axis=1)
        gu = jnp.dot(x, wgu_bf[...], preferred_element_type=F32) + bgu_ref[...]
        glu = jnp.minimum(gu[:, :d_ff], SWIGLU_LIMIT)
        lin = jnp.clip(gu[:, d_ff:], -SWIGLU_LIMIT, SWIGLU_LIMIT)
        act = glu * (1.0 / (1.0 + jnp.exp(-SWIGLU_ALPHA * glu))) * (lin + 1.0)
        y = jnp.dot(act.astype(BF16), wd_bf[...], preferred_element_type=F32) + bd_ref[...]
        y_ref[...] = _pack_bf16_pairs(y)

    @pl.when(jnp.logical_not(used))
    def _():
        y_ref[...] = jnp.zeros_like(y_ref)


def _experts(xs, blk_expert, n_valid, src_blk, layer, w_gu, b_gu, w_down, b_down):
    n_rows, dw = xs.shape
    _, n_e, d, f2 = w_gu.shape
    n_blocks = n_rows // MOE_ROWS

    def wmap(i, be, nv, sb):
        return (layer, be[i], 0, 0)

    grid_spec = pltpu.PrefetchScalarGridSpec(
        num_scalar_prefetch=3,
        grid=(n_blocks,),
        in_specs=[pl.BlockSpec((MOE_ROWS, dw), lambda i, be, nv, sb: (sb[i], 0)),
                  pl.BlockSpec((None, None, d, f2), wmap),
                  pl.BlockSpec((None, None, 1, f2), wmap),
                  pl.BlockSpec((None, None, f2 // 2, d), wmap),
                  pl.BlockSpec((None, None, 1, d), wmap)],
        out_specs=pl.BlockSpec((MOE_ROWS, dw), lambda i, be, nv, sb: (i, 0)),
        scratch_shapes=[pltpu.VMEM((d, f2), BF16), pltpu.VMEM((f2 // 2, d), BF16)],
    )
    depth = w_gu.shape[0]
    return pl.pallas_call(
        _expert_kernel,
        grid_spec=grid_spec,
        out_shape=jax.ShapeDtypeStruct((n_rows, dw), jnp.int32),
        compiler_params=pltpu.CompilerParams(dimension_semantics=("arbitrary",),
                                             vmem_limit_bytes=EXPERT_VMEM_LIMIT),
        name="moe_experts",
    )(blk_expert, n_valid, src_blk, xs, w_gu, b_gu.reshape(depth, n_e, 1, f2), w_down, b_down.reshape(depth, n_e, 1, d))


def _combine_kernel(y0_ref, y1_ref, y2_ref, y3_ref, gate_ref, x_ref, g2_ref, *rest):
    o_ref = rest[-1]
    gate = gate_ref[...]
    w = y0_ref.shape[1]
    y_hi, y_lo = None, None
    for k, y_ref in enumerate((y0_ref, y1_ref, y2_ref, y3_ref)):
        hi, lo = _unpack_bf16_pairs(y_ref[...])
        gk = gate[:, k:k + 1]
        y_hi = gk * hi if y_hi is None else y_hi + gk * hi
        y_lo = gk * lo if y_lo is None else y_lo + gk * lo
    out_hi = x_ref[:, :w] + g2_ref[:, :w] * y_hi
    out_lo = x_ref[:, w:] + g2_ref[:, w:] * y_lo
    if len(rest) == 2:
        gain = rest[0][...]
        ms = (jnp.sum(out_hi * out_hi, axis=-1, keepdims=True)
              + jnp.sum(out_lo * out_lo, axis=-1, keepdims=True)) / (2 * w)
        r = lax.rsqrt(ms + EPS)
        out_hi = out_hi * r * gain[:, :w]
        out_lo = out_lo * r * gain[:, w:]
    o_ref[:, :w] = out_hi
    o_ref[:, w:] = out_lo


def _combine(y4, gate, x, g2, final_gain, *, seq, tm=PROJ_ROWS):
    t, d = x.shape
    bpb = seq // tm
    nblk = t // tm
    y_specs = [pl.BlockSpec((tm, d // 2), functools.partial(lambda i, k: (k * nblk + i, 0), k=k))
               for k in range(TOP_K)]
    in_specs = y_specs + [pl.BlockSpec((tm, TOP_K), lambda i: (i, 0)),
                          pl.BlockSpec((tm, d), lambda i: (i, 0)),
                          pl.BlockSpec((None, 1, d), lambda i: (i // bpb, 0, 0))]
    args = [y4, y4, y4, y4, gate, x, g2]
    if final_gain is not None:
        in_specs.append(pl.BlockSpec((1, d), lambda i: (0, 0)))
        args.append(final_gain.reshape(1, d))
    return pl.pallas_call(
        _combine_kernel,
        grid=(nblk,),
        in_specs=in_specs,
        out_specs=pl.BlockSpec((tm, d), lambda i: (i, 0)),
        out_shape=jax.ShapeDtypeStruct((t, d), F32),
        compiler_params=_cparams(("parallel",)),
        name="moe_combine",
    )(*args)


def _moe(x, norm, g2, w_router, b_router, layer, w_gu, b_gu, w_down, b_down, final_gain, *, seq):
    t = x.shape[0]
    n_e = w_router.shape[1]
    idx_t, gate_t, rank_t, counts, h = _router(x, norm, w_router, b_router, seq=seq)
    e_ids = jnp.arange(n_e, dtype=jnp.int32)
    counts = counts[:, 0].astype(jnp.int32)
    padded = (counts + MOE_ROWS - 1) // MOE_ROWS * MOE_ROWS
    pend = jnp.cumsum(padded)
    pstart = pend - padded
    n_pad = n_e * MOE_ROWS
    n_rows = t * TOP_K + n_pad
    n_blocks = n_rows // MOE_ROWS
    dest_t = jnp.sum(jnp.where(idx_t[:, :, None] == e_ids, pstart, 0), axis=-1) + rank_t
    dest_flat = dest_t.reshape(TOP_K * t)
    blk_start = jnp.arange(n_blocks, dtype=jnp.int32) * MOE_ROWS
    n_used = pend[-1] // MOE_ROWS
    src_blk = jnp.minimum(jnp.arange(n_blocks, dtype=jnp.int32), n_used - 1)
    blk_expert = jnp.minimum(jnp.sum((src_blk * MOE_ROWS)[:, None] >= pend[None, :], axis=1), n_e - 1)
    blk_expert = blk_expert.astype(jnp.int32)
    own = blk_expert[:, None] == e_ids
    rows_left = jnp.sum(jnp.where(own, pstart + counts, 0), axis=-1) - blk_start
    n_valid = jnp.where(blk_start < pend[-1], jnp.clip(rows_left, 0, MOE_ROWS), 0).astype(jnp.int32)
    pad_len = padded - counts
    pad_end = jnp.cumsum(pad_len)
    j = jnp.arange(n_pad, dtype=jnp.int32)
    pe = jnp.sum(j[:, None] >= pad_end[None, :], axis=1)
    sel = jnp.minimum(pe, n_e - 1)[:, None] == e_ids
    in_expert = jnp.sum(jnp.where(sel, pstart + counts - (pad_end - pad_len), 0), axis=-1) + j
    pad_rows = jnp.where(pe < n_e, in_expert, pend[-1] + j - pad_end[-1]).astype(jnp.int32)

    xs = _sc_dispatch(h, dest_flat, pad_rows)
    yb = _experts(xs, blk_expert, n_valid, src_blk, layer, w_gu, b_gu, w_down, b_down)
    y4 = _sc_gather(yb, dest_flat)
    return _combine(y4, gate_t.T, x, g2, final_gain, seq=seq)


def _dup_heads(w, n_heads):
    d = w.shape[0]
    return jnp.repeat(w.reshape(d, n_heads, 1, HEAD_DIM), 2, axis=2).reshape(d, n_heads * LANES)


def _mixer_mla_swa(x, norm, g1, cos, sin, w_in, g_q, w_qb, g_kv, w_kvb, sink, w_out, *, bsz, seq):
    d = x.shape[1]
    o = np.cumsum((0, MLA_Q_LORA, MLA_KV_LORA, MLA_ROPE, SWA_Q_HEADS * HEAD_DIM,
                   SWA_KV_HEADS * HEAD_DIM, SWA_KV_HEADS * HEAD_DIM))
    w_cq, w_ckv, w_kr, w_qs, w_ks, w_vs = (w_in[:, o[i]:o[i + 1]] for i in range(6))
    w_lat = jnp.concatenate([w_cq, jnp.zeros((d, LANES), F32), w_ckv], axis=1).astype(BF16)
    lat = _proj([x], [w_lat], prologue="ada", norm=norm, seq=seq, tm=PROJ_ROWS, tn=w_lat.shape[1],
                out_dtype=F32, name="proj_latent")
    w_b = jnp.concatenate([_pair_layout(w_qs), _pair_layout(_dup_heads(w_ks, SWA_KV_HEADS)),
                           _dup_heads(w_vs, SWA_KV_HEADS),
                           w_kr, jnp.zeros((d, LANES - MLA_ROPE), F32)], axis=1).astype(BF16)
    qkv_b = _proj([x], [w_b], prologue="ada", norm=norm, seq=seq, tm=PROJ_ROWS, tn=w_b.shape[1], out_dtype=BF16,
                  rope=(cos, sin), rope_pattern=(ROPE_PAIR,) * 10 + (0, 0) + (ROPE_ROTATE,),
                  group_scales=(QK_LOG2_SCALE,) * 8 + (1.0,) * 5, name="proj_swa")
    kr_col = (w_b.shape[1] - LANES) // LANES
    qd = MLA_NOPE + MLA_ROPE
    w_q = w_qb.reshape(MLA_Q_LORA, MLA_HEADS, qd)
    w_q = jnp.concatenate([w_q, jnp.zeros((MLA_Q_LORA, MLA_HEADS, 2 * LANES - qd), F32)], axis=2)
    w_q = w_q.reshape(MLA_Q_LORA, MLA_HEADS * 2 * LANES).astype(BF16)
    q_cat = _proj([lat], [w_q], prologue="rms", norm=(g_q.reshape(1, -1),), lhs_cols=(MLA_Q_LORA, 0),
                  seq=seq, tm=PROJ_ROWS, tn=w_q.shape[1], out_dtype=BF16, rope=(cos, sin),
                  rope_pattern=(0, 1) * MLA_HEADS, out_scale=qd ** -0.5, name="proj_mla_q")
    w_kv = w_kvb.reshape(MLA_KV_LORA, MLA_HEADS, MLA_NOPE + MLA_V)
    w_kv = jnp.concatenate([w_kv[:, :, :MLA_NOPE].reshape(MLA_KV_LORA, -1),
                            w_kv[:, :, MLA_NOPE:].reshape(MLA_KV_LORA, -1)], axis=1).astype(BF16)
    kv = _proj([lat], [w_kv], prologue="rms", norm=(g_kv.reshape(1, -1),), lhs_cols=(MLA_KV_LORA, 2),
               seq=seq, tm=PROJ_ROWS, tn=w_kv.shape[1], out_dtype=BF16, name="proj_mla_kv")
    o_a = _mla_attention(q_cat, kv, qkv_b, kr_col, bsz=bsz, seq=seq, tq=256)
    qkv_b4 = qkv_b.reshape(bsz, 1, seq, qkv_b.shape[1])
    (o_b,) = _banded_attention(qkv_b4, qkv_b4, qkv_b4, bsz=bsz, seq=seq, dil=1, radius=SWA_RADIUS,
                               q_col=0, k_col=4, v_col=5, kv_width=2 * LANES,
                               pairs_per_kv=SWA_Q_HEADS // SWA_KV_HEADS // 2, sink=sink, want_lse=False,
                               name="swa_attention")
    o_b = o_b.reshape(bsz * seq, SWA_Q_HEADS * HEAD_DIM)
    na = MLA_HEADS * MLA_V
    return _proj([o_a, o_b], [w_out[:na].astype(BF16), w_out[na:].astype(BF16)], prologue="plain",
                 seq=seq, tm=PROJ_ROWS, tn=d, out_dtype=F32, residual=(x, g1), name="proj_out_ab")


def _mixer_dilated(x, norm, g1, cos, sin, w_in, w_out, *, bsz, seq):
    d = x.shape[1]
    qw = DIL_HEADS * HEAD_DIM
    w_bf = w_in.astype(BF16)
    outs, lses = [], []
    for g, (window, dil) in enumerate(DIL_PATTERN):
        w_g = w_bf[:, g * 3 * qw:(g + 1) * 3 * qw]
        w_g = jnp.concatenate([_pair_layout(w_g[:, :2 * qw]), w_g[:, 2 * qw:]], axis=1)
        qkv = _proj([x], [w_g], prologue="ada", norm=norm, seq=seq, tm=512, tn=3 * qw, sub_tn=qw, out_dtype=BF16,
                    rope=(cos, sin), rope_pattern=(ROPE_PAIR,) * (2 * qw // LANES) + (0,) * (qw // LANES),
                    group_scales=(QK_LOG2_SCALE,) * (qw // LANES) + (1.0,) * (2 * qw // LANES),
                    out_dil=dil, name=f"proj_dil_{dil}")
        qkv = qkv.reshape(bsz, dil, seq // dil, 3 * qw)
        o, lse = _banded_attention(qkv, qkv, qkv, bsz=bsz, seq=seq, dil=dil, radius=window // (2 * dil),
                                   q_col=0, k_col=1, v_col=2, kv_width=qw, pairs_per_kv=1,
                                   want_lse=True, name=f"dilated_attention_{dil}")
        outs.append(o)
        lses.append(lse)
    return _proj(outs + lses, [w_out.astype(BF16)], prologue="dil", seq=seq, tm=512, tn=d, out_dtype=F32,
                 residual=(x, g1), in_dils=tuple(dil for _, dil in DIL_PATTERN), name="proj_out_c")


def kernel(x, c, positions, w_mod, b_mod, g_norm_mix, g_norm_ffn, w_in_ab, mla_g_q, mla_w_qb, mla_g_kv,
           mla_w_kvb, swa_sink, w_out_ab, w_in_c, w_out_c, w_router, b_router, w_gu, b_gu, w_down, b_down,
           g_final):
    bsz, seq, d = x.shape
    depth = w_mod.shape[0]
    cos, sin = _rope_tables(positions)
    mod = _modulation(c, w_mod, b_mod)
    xt = x.reshape(bsz * seq, d)
    for layer in range(depth):
        sh1, sc1, g1, sh2, sc2, g2 = (mod[layer, :, i * d:(i + 1) * d].reshape(bsz, 1, d) for i in range(6))
        li = layer // 2
        norm = (g_norm_mix[layer].reshape(1, d), sc1, sh1)
        if layer % 2 == 0:
            xt = _mixer_mla_swa(xt, norm, g1, cos, sin, w_in_ab[li], mla_g_q[li], mla_w_qb[li], mla_g_kv[li],
                                mla_w_kvb[li], swa_sink[li], w_out_ab[li], bsz=bsz, seq=seq)
        else:
            xt = _mixer_dilated(xt, norm, g1, cos, sin, w_in_c[li], w_out_c[li], bsz=bsz, seq=seq)
        norm = (g_norm_ffn[layer].reshape(1, d), sc2, sh2)
        xt = _moe(xt, norm, g2, w_router[layer], b_router[layer], layer, w_gu, b_gu, w_down, b_down,
                  g_final if layer == depth - 1 else None, seq=seq)
    return xt.reshape(bsz, seq, d)
```

```python
import functools

import jax
import jax.numpy as jnp
import numpy as np
from jax import lax
from jax.experimental import pallas as pl
from jax.experimental.pallas import tpu as pltpu
from jax.experimental.pallas import tpu_sc as plsc

F32 = jnp.float32
BF16 = jnp.bfloat16

EPS = 1e-6
ROPE_THETA = 10000.0
LANES = 128
HEAD_DIM = 64
ROPE_HALF = HEAD_DIM // 2

MLA_HEADS = 8
MLA_Q_LORA = 384
MLA_KV_LORA = 256
MLA_NOPE = 128
MLA_ROPE = 64
MLA_V = 128
SWA_Q_HEADS = 16
SWA_KV_HEADS = 2
SWA_RADIUS = 128
DIL_PATTERN = ((128, 1), (512, 4), (2048, 16))
DIL_HEADS = 16
N_EXPERTS = 32
TOP_K = 4
SWIGLU_LIMIT = 7.0
SWIGLU_ALPHA = 1.702
MOE_ROWS = 512
PROJ_ROWS = 1024
BAND_STEP_ROWS = 1024
BAND_SUB_ROWS = 128
LOG2_E = float(np.log2(np.e))
QK_LOG2_SCALE = HEAD_DIM ** -0.5 * LOG2_E

VMEM_LIMIT = 48 * 1024 * 1024
EXPERT_VMEM_LIMIT = 56 * 1024 * 1024


def _cparams(sem):
    return pltpu.CompilerParams(dimension_semantics=sem, vmem_limit_bytes=VMEM_LIMIT)


def _mod_kernel(c_ref, w_ref, b_ref, o_ref):
    c = c_ref[...]
    ca = c * (1.0 / (1.0 + jnp.exp(-c)))
    o_ref[...] = jnp.dot(ca, w_ref[...], preferred_element_type=F32,
                         precision=lax.Precision.HIGHEST) + b_ref[...]


def _modulation(c, w_mod, b_mod):
    depth, d, n = w_mod.shape
    bsz = c.shape[0]
    tn = 1536
    return pl.pallas_call(
        _mod_kernel,
        grid=(depth, n // tn),
        in_specs=[pl.BlockSpec((bsz, d), lambda l, j: (0, 0)),
                  pl.BlockSpec((None, d, tn), lambda l, j: (l, 0, j)),
                  pl.BlockSpec((None, 1, tn), lambda l, j: (l, 0, j))],
        out_specs=pl.BlockSpec((None, bsz, tn), lambda l, j: (l, 0, j)),
        out_shape=jax.ShapeDtypeStruct((depth, bsz, n), F32),
        compiler_params=_cparams(("parallel", "parallel")),
        name="modulation",
    )(c, w_mod, b_mod.reshape(depth, 1, n))


def _rope_table_kernel(pos_ref, inv_ref, sign_ref, cos_ref, sin_ref):
    ang = inv_ref[...] * pos_ref[...].astype(F32)
    reps = LANES // ROPE_HALF
    cos_ref[...] = jnp.transpose(jnp.concatenate([jnp.cos(ang)] * reps, axis=0))
    sin_ref[...] = jnp.transpose(jnp.concatenate([jnp.sin(ang)] * reps, axis=0) * sign_ref[...])


def _rope_tables(positions):
    bsz, s = positions.shape
    inv = ROPE_THETA ** (-jnp.arange(0, HEAD_DIM, 2, dtype=F32) / HEAD_DIM)
    sign = jnp.where((jnp.arange(LANES) % HEAD_DIM) < ROPE_HALF, -1.0, 1.0).astype(F32).reshape(LANES, 1)
    out = jax.ShapeDtypeStruct((bsz * s, LANES), F32)
    return pl.pallas_call(
        _rope_table_kernel,
        grid=(bsz,),
        in_specs=[pl.BlockSpec((None, 1, s), lambda b: (b, 0, 0)),
                  pl.BlockSpec((ROPE_HALF, 1), lambda b: (0, 0)),
                  pl.BlockSpec((LANES, 1), lambda b: (0, 0))],
        out_specs=[pl.BlockSpec((s, LANES), lambda b: (b, 0)),
                   pl.BlockSpec((s, LANES), lambda b: (b, 0))],
        out_shape=[out, out],
        compiler_params=_cparams(("parallel",)),
        name="rope_tables",
    )(positions.reshape(bsz, 1, s), inv.reshape(ROPE_HALF, 1), sign)


def _rms(x):
    return x * lax.rsqrt(jnp.mean(x * x, axis=-1, keepdims=True) + EPS)


def _ada_norm(x, g, sc, sh):
    return (_rms(x) * g) * (1.0 + sc) + sh


def _pack_bf16_pairs(x):
    w = x.shape[1] // 2
    hi = pltpu.bitcast(x[:, :w].astype(BF16).astype(F32), jnp.int32)
    lo = pltpu.bitcast(x[:, w:].astype(BF16).astype(F32), jnp.int32)
    return hi | lax.shift_right_logical(lo, 16)


def _unpack_bf16_pairs(p):
    hi = pltpu.bitcast(p & jnp.int32(-65536), F32)
    lo = pltpu.bitcast(lax.shift_left(p, 16), F32)
    return hi, lo


ROPE_ROTATE, ROPE_PAIR = 1, 2


def _pair_layout(w):
    k, n = w.shape
    w = w.reshape(k, n // LANES, 2, 2, ROPE_HALF)
    return jnp.swapaxes(w, 2, 3).reshape(k, n)


def _rope_group(a, cos, sin, first_half):
    rot = jnp.where(first_half, pltpu.roll(a, LANES - ROPE_HALF, 1), pltpu.roll(a, ROPE_HALF, 1))
    return a * cos + rot * sin


def _proj_kernel(*refs, prologue, n_lhs, rope_pattern, group_scales, sub_tn, out_scale, residual, in_dils,
                 out_dil):
    pos = 0
    if prologue == "ada":
        x_ref, g_ref, sc_ref, sh_ref = refs[:4]
        pos = 4
    elif prologue == "rms":
        x_ref, g_ref = refs[:2]
        pos = 2
    elif prologue == "dil":
        o_refs = refs[0:3]
        l_refs = refs[3:6]
        pos = 6
    else:
        x_refs = refs[:n_lhs]
        pos = n_lhs
    w_refs = refs[pos:pos + n_lhs]
    pos += n_lhs
    if rope_pattern is not None:
        cos_ref, sin_ref = refs[pos:pos + 2]
        pos += 2
    if residual:
        res_ref, gate_ref = refs[pos:pos + 2]
        pos += 2
    o_ref = refs[pos]
    pos += 1
    h_ref = refs[pos] if prologue != "plain" else None
    slab_ref = refs[pos + 1] if (prologue == "dil" or out_dil > 1) else None

    if prologue != "plain":
        @pl.when(pl.program_id(1) == 0)
        def _():
            if prologue == "ada":
                h_ref[...] = _ada_norm(x_ref[...], g_ref[...], sc_ref[...], sh_ref[...]).astype(BF16)
            elif prologue == "rms":
                h_ref[...] = (_rms(x_ref[...]) * g_ref[...]).astype(BF16)
            else:
                tm = h_ref.shape[0]
                for c in range(h_ref.shape[1] // LANES):
                    cols = slice(c * LANES, (c + 1) * LANES)
                    vals = []
                    for i, (ref, dil) in enumerate(zip(o_refs + l_refs, in_dils + in_dils)):
                        if dil == 1:
                            vals.append(ref[0, :, cols].astype(F32))
                        else:
                            for r in range(dil):
                                slab_ref[i, pl.ds(r, tm // dil, stride=dil), :] = ref[r, :, cols].astype(F32)
                            vals.append(slab_ref[i])
                    o0, o1, o2, l0, l1, l2 = vals
                    m = jnp.maximum(jnp.maximum(l0, l1), l2)
                    e0, e1, e2 = jnp.exp2(l0 - m), jnp.exp2(l1 - m), jnp.exp2(l2 - m)
                    h_ref[:, cols] = ((e0 * o0 + e1 * o1 + e2 * o2) / (e0 + e1 + e2)).astype(BF16)

    n_total = w_refs[0].shape[1]
    tn = n_total if sub_tn is None else sub_tn

    def matmul(t):
        cols = slice(t * tn, (t + 1) * tn)
        if prologue != "plain":
            acc = jnp.dot(h_ref[...], w_refs[0][:, cols], preferred_element_type=F32)
        else:
            acc = jnp.dot(x_refs[0][...], w_refs[0][:, cols], preferred_element_type=F32)
            for xr, wr in zip(x_refs[1:], w_refs[1:]):
                acc = acc + jnp.dot(xr[...], wr[:, cols], preferred_element_type=F32)
        return acc * out_scale if out_scale != 1.0 else acc

    if residual:
        o_ref[...] = (res_ref[...] + gate_ref[...] * matmul(0)).astype(o_ref.dtype)
        return

    pattern = rope_pattern if rope_pattern is not None else (0,) * (n_total // LANES)
    if any(pattern):
        cos = cos_ref[...]
        sin = sin_ref[...]
        lane = lax.broadcasted_iota(jnp.int32, cos.shape, 1)
        first_half = (lane % HEAD_DIM) < ROPE_HALF
        if ROPE_PAIR in pattern:
            sin_pair = jnp.where(lane < HEAD_DIM, -1.0, 1.0) * jnp.where(first_half, -sin, sin)

    def epilogue(acc, t):
        tm = acc.shape[0]
        for c in range(tn // LANES):
            g = t * (tn // LANES) + c
            cols = slice(g * LANES, (g + 1) * LANES)
            a = acc[:, c * LANES:(c + 1) * LANES]
            if pattern[g] == ROPE_ROTATE:
                a = _rope_group(a, cos, sin, first_half)
            elif pattern[g] == ROPE_PAIR:
                a = a * cos + pltpu.roll(a, HEAD_DIM, 1) * sin_pair
            if group_scales is not None and group_scales[g] != 1.0:
                a = a * group_scales[g]
            if out_dil == 1:
                o_ref[:, cols] = a.astype(o_ref.dtype)
            else:
                slab_ref[0] = a
                for r in range(out_dil):
                    o_ref[r, :, cols] = slab_ref[0, pl.ds(r, tm // out_dil, stride=out_dil), :].astype(o_ref.dtype)

    acc_next = matmul(0)
    for t in range(n_total // tn):
        acc = acc_next
        if t + 1 < n_total // tn:
            acc_next = matmul(t + 1)
        epilogue(acc, t)


def _proj(lhs, ws, *, prologue, seq, tm, tn, out_dtype, norm=None, lhs_cols=None, rope=None,
          rope_pattern=None, group_scales=None, sub_tn=None, out_scale=1.0, residual=None, in_dils=None,
          out_dil=1, name):
    k, n = ws[0].shape
    bsz_seq = lhs[0].shape[0] if prologue != "dil" else lhs[0].shape[0] * seq
    t = bsz_seq
    assert t % tm == 0 and n % tn == 0 and seq % tm == 0
    bpb = seq // tm
    n_lhs = len(ws)
    args, specs = [], []
    row = lambda i, j: (i, 0)
    if prologue in ("ada", "rms"):
        kw, kc = lhs_cols if lhs_cols is not None else (k, 0)
        args.append(lhs[0])
        specs.append(pl.BlockSpec((tm, kw), lambda i, j: (i, kc)))
        args.append(norm[0])
        specs.append(pl.BlockSpec((1, k), lambda i, j: (0, 0)))
        if prologue == "ada":
            for v in norm[1:3]:
                args.append(v)
                specs.append(pl.BlockSpec((None, 1, k), lambda i, j: (i // bpb, 0, 0)))
    elif prologue == "dil":
        for a, dil in zip(lhs, in_dils + in_dils):
            args.append(a)
            specs.append(pl.BlockSpec((None, dil, tm // dil, k), lambda i, j: (i // bpb, 0, i % bpb, 0)))
    else:
        for a, w in zip(lhs, ws):
            args.append(a)
            specs.append(pl.BlockSpec((tm, w.shape[0]), row))
    for w in ws:
        args.append(w)
        specs.append(pl.BlockSpec((w.shape[0], tn), lambda i, j: (0, j)))
    if rope_pattern is not None:
        assert len(rope_pattern) == tn // LANES
        for tab in rope:
            args.append(tab)
            specs.append(pl.BlockSpec((tm, LANES), row))
    if residual is not None:
        res, gate = residual
        args.append(res)
        specs.append(pl.BlockSpec((tm, tn), lambda i, j: (i, j)))
        args.append(gate)
        specs.append(pl.BlockSpec((None, 1, tn), lambda i, j: (i // bpb, 0, j)))
    scratch = [] if prologue == "plain" else [pltpu.VMEM((tm, k), BF16)]
    if prologue == "dil":
        scratch.append(pltpu.VMEM((2 * len(in_dils), tm, LANES), F32))
    elif out_dil > 1:
        scratch.append(pltpu.VMEM((1, tm, LANES), F32))
    if out_dil == 1:
        out_spec = pl.BlockSpec((tm, tn), lambda i, j: (i, j))
        out_shape = jax.ShapeDtypeStruct((t, n), out_dtype)
    else:
        out_spec = pl.BlockSpec((None, out_dil, tm // out_dil, tn), lambda i, j: (i // bpb, 0, i % bpb, j))
        out_shape = jax.ShapeDtypeStruct((t // seq, out_dil, seq // out_dil, n), out_dtype)
    kern = functools.partial(_proj_kernel, prologue=prologue, n_lhs=n_lhs, rope_pattern=rope_pattern,
                             group_scales=group_scales, sub_tn=sub_tn, out_scale=out_scale,
                             residual=residual is not None,
                             in_dils=in_dils, out_dil=out_dil)
    return pl.pallas_call(
        kern,
        grid=(t // tm, n // tn),
        in_specs=specs,
        out_specs=out_spec,
        out_shape=out_shape,
        scratch_shapes=scratch,
        compiler_params=_cparams(("parallel", "arbitrary")),
        name=name,
    )(*args)


MLA_HEADS_PER_STEP = 4


def _mla_scores(q_ref, kn_ref, kr, h):
    k = jnp.concatenate([kn_ref[:, h * LANES:(h + 1) * LANES], kr], axis=1)
    q = q_ref[:, h * 2 * LANES:(h + 1) * 2 * LANES]
    return lax.dot_general(q, k, (((1,), (1,)), ((), ())), preferred_element_type=F32)


def _mla_kernel(q_ref, kn_ref, kr_ref, v_ref, o_ref):
    kr = kr_ref[...]
    s_next = _mla_scores(q_ref, kn_ref, kr, 0)
    for h in range(MLA_HEADS_PER_STEP):
        s = s_next
        if h + 1 < MLA_HEADS_PER_STEP:
            s_next = _mla_scores(q_ref, kn_ref, kr, h + 1)
        m = jnp.max(s, axis=1, keepdims=True)
        p = jnp.exp(s - m).astype(BF16)
        v = v_ref[:, h * LANES:(h + 1) * LANES]
        oe = jnp.dot(p, jnp.concatenate([v, jnp.ones_like(v)], axis=1), preferred_element_type=F32)
        o_ref[:, h * LANES:(h + 1) * LANES] = (oe[:, :LANES] / oe[:, LANES:]).astype(o_ref.dtype)


def _mla_attention(q_cat, kv, qkv_b, kr_col, *, bsz, seq, tq):
    t = bsz * seq
    nq = seq // tq
    hps = MLA_HEADS_PER_STEP
    groups = MLA_HEADS // hps
    return pl.pallas_call(
        _mla_kernel,
        grid=(bsz, groups, nq),
        in_specs=[pl.BlockSpec((tq, hps * 2 * LANES), lambda b, h, i: (b * nq + i, h)),
                  pl.BlockSpec((seq, hps * LANES), lambda b, h, i: (b, h)),
                  pl.BlockSpec((seq, LANES), lambda b, h, i: (b, kr_col)),
                  pl.BlockSpec((seq, hps * LANES), lambda b, h, i: (b, groups + h))],
        out_specs=pl.BlockSpec((tq, hps * LANES), lambda b, h, i: (b * nq + i, h)),
        out_shape=jax.ShapeDtypeStruct((t, MLA_HEADS * MLA_V), BF16),
        compiler_params=_cparams(("parallel", "parallel", "parallel")),
        name="mla_attention",
    )(q_cat, kv, qkv_b, kv)


def _banded_kernel(*refs, n_seqs, has_sink, want_lse, **kw):
    sink_ref = refs[0] if has_sink else None
    refs = refs[1:] if has_sink else refs
    for rr in range(n_seqs):
        ins = [r.at[rr] for r in refs[:7]]
        o_ref = refs[7].at[rr]
        lse_ref = refs[8].at[rr] if want_lse else None
        _banded_sequence(sink_ref, *ins, o_ref, lse_ref, has_sink=has_sink, want_lse=want_lse, **kw)


def _banded_sequence(sink_ref, q_ref, kp_ref, kc_ref, kn_ref, vp_ref, vc_ref, vn_ref, o_ref, lse_ref, *,
                     radius, tq, hb, length, n_pairs, pairs_per_kv, has_sink, want_lse):
    j = pl.program_id(2)
    kw = jnp.concatenate([kp_ref[...], kc_ref[...], kn_ref[...]], axis=0)
    vw = jnp.concatenate([vp_ref[...], vc_ref[...], vn_ref[...]], axis=0)
    qn = min(tq, BAND_SUB_ROWS)
    kn = qn + 2 * hb
    lane = lax.broadcasted_iota(jnp.int32, (qn, LANES), 1)
    low = lane < HEAD_DIM
    first = (lane % HEAD_DIM) < ROPE_HALF

    for q0 in range(0, tq, qn):
        qpos = j * tq + q0 + lax.broadcasted_iota(jnp.int32, (qn, kn), 0)
        kpos = j * tq - hb + q0 + lax.broadcasted_iota(jnp.int32, (qn, kn), 1)
        valid = (jnp.abs(qpos - kpos) <= radius) & (kpos >= 0) & (kpos < length)
        for p in range(n_pairs):
            c = p // pairs_per_kv
            qp = q_ref[q0:q0 + qn, p * LANES:(p + 1) * LANES]
            kp = kw[q0:q0 + kn, c * LANES:(c + 1) * LANES]
            vp = vw[q0:q0 + kn, c * LANES:(c + 1) * LANES]
            vp = jnp.concatenate([vp, jnp.ones_like(vp)], axis=1)
            halves = []
            for half in range(2):
                qh = jnp.where(first if half == 0 else ~first, qp, jnp.zeros_like(qp))
                s = lax.dot_general(qh, kp, (((1,), (1,)), ((), ())), preferred_element_type=F32)
                s = jnp.where(valid, s, -jnp.inf)
                m = jnp.max(s, axis=1, keepdims=True)
                if has_sink:
                    sk = sink_ref[2 * p + half] * LOG2_E
                    m = jnp.maximum(m, sk)
                oe = jnp.dot(jnp.exp2(s - m).astype(BF16), vp, preferred_element_type=F32)
                den = oe[:, LANES:]
                if has_sink:
                    den = den + jnp.exp2(sk - m)
                halves.append((oe[:, :LANES] / den, m + jnp.log2(den) if want_lse else None))
            o_ref[q0:q0 + qn, p * LANES:(p + 1) * LANES] = jnp.where(
                low, halves[0][0], halves[1][0]).astype(o_ref.dtype)
            if want_lse:
                lse_ref[q0:q0 + qn, p * LANES:(p + 1) * LANES] = jnp.where(low, halves[0][1], halves[1][1])


def _banded_attention(q_arr, k_arr, v_arr, *, bsz, seq, dil, radius, q_col, k_col, v_col, kv_width,
                      pairs_per_kv, sink=None, want_lse, name):
    length = seq // dil
    hb = radius
    tq = min(BAND_STEP_ROWS, length)
    assert tq % hb == 0 and length % tq == 0
    qw = DIL_HEADS * HEAD_DIM
    n_pairs = qw // LANES
    rep = tq // hb
    last_hb = length // hb - 1
    n_seqs = min(dil, max(1, BAND_STEP_ROWS // length))
    assert dil % n_seqs == 0

    def prev_spec(col):
        return pl.BlockSpec((None, n_seqs, hb, kv_width),
                            lambda b, r, j: (b, r, jnp.maximum(j * rep - 1, 0), col))

    def cur_spec(col):
        return pl.BlockSpec((None, n_seqs, tq, kv_width), lambda b, r, j: (b, r, j, col))

    def next_spec(col):
        return pl.BlockSpec((None, n_seqs, hb, kv_width),
                            lambda b, r, j: (b, r, jnp.minimum((j + 1) * rep, last_hb), col))

    in_specs = [pl.BlockSpec((None, n_seqs, tq, qw), lambda b, r, j: (b, r, j, q_col)),
                prev_spec(k_col), cur_spec(k_col), next_spec(k_col),
                prev_spec(v_col), cur_spec(v_col), next_spec(v_col)]
    args = [q_arr, k_arr, k_arr, k_arr, v_arr, v_arr, v_arr]
    if sink is not None:
        in_specs = [pl.BlockSpec(memory_space=pltpu.SMEM)] + in_specs
        args = [sink] + args
    out_spec = pl.BlockSpec((None, n_seqs, tq, qw), lambda b, r, j: (b, r, j, 0))
    out_specs = [out_spec]
    out_shape = [jax.ShapeDtypeStruct((bsz, dil, length, qw), BF16)]
    if want_lse:
        out_specs.append(out_spec)
        out_shape.append(jax.ShapeDtypeStruct((bsz, dil, length, qw), F32))
    kern = functools.partial(_banded_kernel, n_seqs=n_seqs, radius=radius, tq=tq, hb=hb, length=length,
                             n_pairs=n_pairs, pairs_per_kv=pairs_per_kv, has_sink=sink is not None,
                             want_lse=want_lse)
    return pl.pallas_call(
        kern,
        grid=(bsz, dil // n_seqs, length // tq),
        in_specs=in_specs,
        out_specs=out_specs,
        out_shape=out_shape,
        compiler_params=_cparams(("parallel", "parallel", "parallel")),
        name=name,
    )(*args)


def _router_kernel(x_ref, g_ref, sc_ref, sh_ref, wt_ref, b_ref, tri_ref,
                   idx_ref, gate_ref, rank_ref, cnt_ref, h_ref, carry_ref):
    @pl.when(pl.program_id(0) == 0)
    def _():
        carry_ref[...] = jnp.zeros_like(carry_ref)

    h = _ada_norm(x_ref[...], g_ref[...], sc_ref[...], sh_ref[...])
    h_ref[...] = _pack_bf16_pairs(h)
    lt = lax.dot_general(wt_ref[...], h, (((1,), (1,)), ((), ())), preferred_element_type=F32,
                         precision=lax.Precision.HIGHEST) + b_ref[...]
    n_e, tm = lt.shape
    e_iota = lax.broadcasted_iota(jnp.int32, (n_e, tm), 0)
    vals, sels = [], []
    for k in range(TOP_K):
        m = jnp.max(lt, axis=0, keepdims=True)
        idx = jnp.min(jnp.where(lt == m, e_iota, n_e), axis=0, keepdims=True)
        sel = e_iota == idx
        idx_ref[k:k + 1, :] = idx
        vals.append(m)
        sels.append(sel)
        lt = jnp.where(sel, -jnp.inf, lt)
    exps = [jnp.exp(v - vals[0]) for v in vals]
    den = exps[0] + exps[1] + exps[2] + exps[3]
    for k in range(TOP_K):
        gate_ref[k:k + 1, :] = exps[k] / den

    onehot = jnp.where(sels[0] | sels[1] | sels[2] | sels[3], 1.0, 0.0)
    carry = carry_ref[:, 0:1]
    before = jnp.dot(onehot.astype(BF16), tri_ref[...], preferred_element_type=F32) + carry
    for k in range(TOP_K):
        rank_ref[k:k + 1, :] = jnp.sum(jnp.where(sels[k], before, 0.0), axis=0, keepdims=True).astype(jnp.int32)
    total = carry + jnp.sum(onehot, axis=1, keepdims=True)
    carry_ref[...] = jnp.broadcast_to(total, carry_ref.shape)
    cnt_ref[...] = jnp.broadcast_to(total, cnt_ref.shape)


def _router(x, norm, w_router, b_router, *, seq, tm=PROJ_ROWS):
    t, d = x.shape
    n_e = w_router.shape[1]
    bpb = seq // tm
    tri = (jnp.arange(tm)[:, None] < jnp.arange(tm)[None, :]).astype(BF16)
    g, sc, sh = norm
    out_i = jax.ShapeDtypeStruct((TOP_K, t), jnp.int32)
    tok = pl.BlockSpec((TOP_K, tm), lambda i: (0, i))
    return pl.pallas_call(
        _router_kernel,
        grid=(t // tm,),
        in_specs=[pl.BlockSpec((tm, d), lambda i: (i, 0)),
                  pl.BlockSpec((1, d), lambda i: (0, 0)),
                  pl.BlockSpec((None, 1, d), lambda i: (i // bpb, 0, 0)),
                  pl.BlockSpec((None, 1, d), lambda i: (i // bpb, 0, 0)),
                  pl.BlockSpec((n_e, d), lambda i: (0, 0)),
                  pl.BlockSpec((n_e, 1), lambda i: (0, 0)),
                  pl.BlockSpec((tm, tm), lambda i: (0, 0))],
        out_specs=[tok, tok, tok, pl.BlockSpec((n_e, LANES), lambda i: (0, 0)),
                   pl.BlockSpec((tm, d // 2), lambda i: (i, 0))],
        out_shape=[out_i, jax.ShapeDtypeStruct((TOP_K, t), F32), out_i,
                   jax.ShapeDtypeStruct((n_e, LANES), F32), jax.ShapeDtypeStruct((t, d // 2), jnp.int32)],
        scratch_shapes=[pltpu.VMEM((n_e, LANES), F32)],
        compiler_params=_cparams(("arbitrary",)),
        name="router",
    )(x, g, sc, sh, w_router.T, b_router.reshape(n_e, 1), tri)


SC_CORES = 2
SC_SUBCORES = 16
SC_WORKERS = SC_CORES * SC_SUBCORES
SC_ROWS = 64


def _sc_worker_id():
    return lax.axis_index("subcore") * SC_CORES + lax.axis_index("core")


def _sc_mesh():
    return plsc.VectorSubcoreMesh(core_axis_name="core", subcore_axis_name="subcore")


def _sc_dispatch(h, dest_flat, pad_rows):
    t, d = h.shape
    n_pad = pad_rows.shape[0]
    n_rows = TOP_K * t + n_pad
    tok_w = t // SC_WORKERS
    pad_w = n_pad // SC_WORKERS
    nb = tok_w // SC_ROWS
    assert tok_w % (2 * SC_ROWS) == 0 and pad_w % SC_ROWS == 0

    @functools.partial(
        pl.kernel, out_type=jax.ShapeDtypeStruct((n_rows, d), h.dtype), mesh=_sc_mesh(),
        scratch_types=[pltpu.VMEM((SC_ROWS,), jnp.int32), pltpu.VMEM((2, SC_ROWS, d), h.dtype),
                       pltpu.SemaphoreType.DMA((2,))])
    def scatter_rows(h_hbm, dest_hbm, pad_hbm, zero_hbm, xs_hbm, idx_v, rows_v, sem):
        wid = _sc_worker_id()
        base = wid * tok_w

        def load(b, slot):
            return pltpu.make_async_copy(h_hbm.at[pl.ds(base + b * SC_ROWS, SC_ROWS)], rows_v.at[slot],
                                         sem.at[slot])

        load(0, 0).start()

        @pl.loop(0, nb, step=2)
        def _(i):
            for slot in range(2):
                b = i + slot
                load(b, slot).wait()

                @pl.when(b + 1 < nb)
                def _():
                    load(b + 1, 1 - slot).start()

                for k in range(TOP_K):
                    pltpu.sync_copy(dest_hbm.at[pl.ds(k * t + base + b * SC_ROWS, SC_ROWS)], idx_v)
                    pltpu.sync_copy(rows_v.at[slot], xs_hbm.at[idx_v])

        pltpu.sync_copy(zero_hbm, rows_v.at[0])

        @pl.loop(0, pad_w // SC_ROWS)
        def _(b):
            pltpu.sync_copy(pad_hbm.at[pl.ds(wid * pad_w + b * SC_ROWS, SC_ROWS)], idx_v)
            pltpu.sync_copy(rows_v.at[0], xs_hbm.at[idx_v])

    return scatter_rows(h, dest_flat, pad_rows, jnp.zeros((SC_ROWS, d), h.dtype))


def _sc_gather(table, idx):
    n = idx.shape[0]
    d = table.shape[1]
    per_w = n // SC_WORKERS
    nb = per_w // SC_ROWS
    assert per_w % (2 * SC_ROWS) == 0

    @functools.partial(
        pl.kernel, out_type=jax.ShapeDtypeStruct((n, d), table.dtype), mesh=_sc_mesh(),
        scratch_types=[pltpu.VMEM((2, SC_ROWS), jnp.int32), pltpu.VMEM((2, SC_ROWS, d), table.dtype),
                       pltpu.SemaphoreType.DMA((2,))])
    def gather_rows(table_hbm, idx_hbm, out_hbm, idx_v, rows_v, sem):
        base = _sc_worker_id() * per_w

        def gather(slot):
            return pltpu.make_async_copy(table_hbm.at[idx_v.at[slot]], rows_v.at[slot], sem.at[slot])

        def start(b, slot):
            pltpu.sync_copy(idx_hbm.at[pl.ds(base + b * SC_ROWS, SC_ROWS)], idx_v.at[slot])
            gather(slot).start()

        start(0, 0)
        start(1, 1)

        @pl.loop(0, nb, step=2)
        def _(i):
            for slot in range(2):
                b = i + slot
                gather(slot).wait()
                pltpu.sync_copy(rows_v.at[slot], out_hbm.at[pl.ds(base + b * SC_ROWS, SC_ROWS)])

                @pl.when(b + 2 < nb)
                def _():
                    start(b + 2, slot)

    return gather_rows(table, idx)


def _expert_kernel(be_ref, nv_ref, sb_ref, xs_ref, wgu_ref, bgu_ref, wd_ref, bd_ref, y_ref, wgu_bf, wd_bf):
    i = pl.program_id(0)
    used = nv_ref[i] > 0

    @pl.when(used & ((i == 0) | (be_ref[i] != be_ref[jnp.maximum(i - 1, 0)])))
    def _():
        wgu_bf[...] = wgu_ref[...].astype(BF16)
        wd_bf[...] = wd_ref[...].astype(BF16)

    @pl.when(used)
    def _():
        d_ff = wd_ref.shape[0]
        x_hi, x_lo = _unpack_bf16_pairs(xs_ref[...])
        x = jnp.concatenate([x_hi.astype(BF16), x_lo.astype(BF16)], axis=1)
        gu = jnp.dot(x, wgu_bf[...], preferred_element_type=F32) + bgu_ref[...]
        glu = jnp.minimum(gu[:, :d_ff], SWIGLU_LIMIT)
        lin = jnp.clip(gu[:, d_ff:], -SWIGLU_LIMIT, SWIGLU_LIMIT)
        act = glu * (1.0 / (1.0 + jnp.exp(-SWIGLU_ALPHA * glu))) * (lin + 1.0)
        y = jnp.dot(act.astype(BF16), wd_bf[...], preferred_element_type=F32) + bd_ref[...]
        y_ref[...] = _pack_bf16_pairs(y)

    @pl.when(jnp.logical_not(used))
    def _():
        y_ref[...] = jnp.zeros_like(y_ref)


def _experts(xs, blk_expert, n_valid, src_blk, layer, w_gu, b_gu, w_down, b_down):
    n_rows, dw = xs.shape
    _, n_e, d, f2 = w_gu.shape
    n_blocks = n_rows // MOE_ROWS

    def wmap(i, be, nv, sb):
        return (layer, be[i], 0, 0)

    grid_spec = pltpu.PrefetchScalarGridSpec(
        num_scalar_prefetch=3,
        grid=(n_blocks,),
        in_specs=[pl.BlockSpec((MOE_ROWS, dw), lambda i, be, nv, sb: (sb[i], 0)),
                  pl.BlockSpec((None, None, d, f2), wmap),
                  pl.BlockSpec((None, None, 1, f2), wmap),
                  pl.BlockSpec((None, None, f2 // 2, d), wmap),
                  pl.BlockSpec((None, None, 1, d), wmap)],
        out_specs=pl.BlockSpec((MOE_ROWS, dw), lambda i, be, nv, sb: (i, 0)),
        scratch_shapes=[pltpu.VMEM((d, f2), BF16), pltpu.VMEM((f2 // 2, d), BF16)],
    )
    depth = w_gu.shape[0]
    return pl.pallas_call(
        _expert_kernel,
        grid_spec=grid_spec,
        out_shape=jax.ShapeDtypeStruct((n_rows, dw), jnp.int32),
        compiler_params=pltpu.CompilerParams(dimension_semantics=("arbitrary",),
                                             vmem_limit_bytes=EXPERT_VMEM_LIMIT),
        name="moe_experts",
    )(blk_expert, n_valid, src_blk, xs, w_gu, b_gu.reshape(depth, n_e, 1, f2), w_down, b_down.reshape(depth, n_e, 1, d))


def _combine_kernel(y0_ref, y1_ref, y2_ref, y3_ref, gate_ref, x_ref, g2_ref, *rest):
    o_ref = rest[-1]
    gate = gate_ref[...]
    w = y0_ref.shape[1]
    y_hi, y_lo = None, None
    for k, y_ref in enumerate((y0_ref, y1_ref, y2_ref, y3_ref)):
        hi, lo = _unpack_bf16_pairs(y_ref[...])
        gk = gate[:, k:k + 1]
        y_hi = gk * hi if y_hi is None else y_hi + gk * hi
        y_lo = gk * lo if y_lo is None else y_lo + gk * lo
    out_hi = x_ref[:, :w] + g2_ref[:, :w] * y_hi
    out_lo = x_ref[:, w:] + g2_ref[:, w:] * y_lo
    if len(rest) == 2:
        gain = rest[0][...]
        ms = (jnp.sum(out_hi * out_hi, axis=-1, keepdims=True)
              + jnp.sum(out_lo * out_lo, axis=-1, keepdims=True)) / (2 * w)
        r = lax.rsqrt(ms + EPS)
        out_hi = out_hi * r * gain[:, :w]
        out_lo = out_lo * r * gain[:, w:]
    o_ref[:, :w] = out_hi
    o_ref[:, w:] = out_lo


def _combine(y4, gate, x, g2, final_gain, *, seq, tm=PROJ_ROWS):
    t, d = x.shape
    bpb = seq // tm
    nblk = t // tm
    y_specs = [pl.BlockSpec((tm, d // 2), functools.partial(lambda i, k: (k * nblk + i, 0), k=k))
               for k in range(TOP_K)]
    in_specs = y_specs + [pl.BlockSpec((tm, TOP_K), lambda i: (i, 0)),
                          pl.BlockSpec((tm, d), lambda i: (i, 0)),
                          pl.BlockSpec((None, 1, d), lambda i: (i // bpb, 0, 0))]
    args = [y4, y4, y4, y4, gate, x, g2]
    if final_gain is not None:
        in_specs.append(pl.BlockSpec((1, d), lambda i: (0, 0)))
        args.append(final_gain.reshape(1, d))
    return pl.pallas_call(
        _combine_kernel,
        grid=(nblk,),
        in_specs=in_specs,
        out_specs=pl.BlockSpec((tm, d), lambda i: (i, 0)),
        out_shape=jax.ShapeDtypeStruct((t, d), F32),
        compiler_params=_cparams(("parallel",)),
        name="moe_combine",
    )(*args)


def _moe(x, norm, g2, w_router, b_router, layer, w_gu, b_gu, w_down, b_down, final_gain, *, seq):
    t = x.shape[0]
    n_e = w_router.shape[1]
    idx_t, gate_t, rank_t, counts, h = _router(x, norm, w_router, b_router, seq=seq)
    e_ids = jnp.arange(n_e, dtype=jnp.int32)
    counts = counts[:, 0].astype(jnp.int32)
    padded = (counts + MOE_ROWS - 1) // MOE_ROWS * MOE_ROWS
    pend = jnp.cumsum(padded)
    pstart = pend - padded
    n_pad = n_e * MOE_ROWS
    n_rows = t * TOP_K + n_pad
    n_blocks = n_rows // MOE_ROWS
    dest_t = jnp.sum(jnp.where(idx_t[:, :, None] == e_ids, pstart, 0), axis=-1) + rank_t
    dest_flat = dest_t.reshape(TOP_K * t)
    blk_start = jnp.arange(n_blocks, dtype=jnp.int32) * MOE_ROWS
    n_used = pend[-1] // MOE_ROWS
    src_blk = jnp.minimum(jnp.arange(n_blocks, dtype=jnp.int32), n_used - 1)
    blk_expert = jnp.minimum(jnp.sum((src_blk * MOE_ROWS)[:, None] >= pend[None, :], axis=1), n_e - 1)
    blk_expert = blk_expert.astype(jnp.int32)
    own = blk_expert[:, None] == e_ids
    rows_left = jnp.sum(jnp.where(own, pstart + counts, 0), axis=-1) - blk_start
    n_valid = jnp.where(blk_start < pend[-1], jnp.clip(rows_left, 0, MOE_ROWS), 0).astype(jnp.int32)
    pad_len = padded - counts
    pad_end = jnp.cumsum(pad_len)
    j = jnp.arange(n_pad, dtype=jnp.int32)
    pe = jnp.sum(j[:, None] >= pad_end[None, :], axis=1)
    sel = jnp.minimum(pe, n_e - 1)[:, None] == e_ids
    in_expert = jnp.sum(jnp.where(sel, pstart + counts - (pad_end - pad_len), 0), axis=-1) + j
    pad_rows = jnp.where(pe < n_e, in_expert, pend[-1] + j - pad_end[-1]).astype(jnp.int32)

    xs = _sc_dispatch(h, dest_flat, pad_rows)
    yb = _experts(xs, blk_expert, n_valid, src_blk, layer, w_gu, b_gu, w_down, b_down)
    y4 = _sc_gather(yb, dest_flat)
    return _combine(y4, gate_t.T, x, g2, final_gain, seq=seq)


def _dup_heads(w, n_heads):
    d = w.shape[0]
    return jnp.repeat(w.reshape(d, n_heads, 1, HEAD_DIM), 2, axis=2).reshape(d, n_heads * LANES)


def _mixer_mla_swa(x, norm, g1, cos, sin, w_in, g_q, w_qb, g_kv, w_kvb, sink, w_out, *, bsz, seq):
    d = x.shape[1]
    o = np.cumsum((0, MLA_Q_LORA, MLA_KV_LORA, MLA_ROPE, SWA_Q_HEADS * HEAD_DIM,
                   SWA_KV_HEADS * HEAD_DIM, SWA_KV_HEADS * HEAD_DIM))
    w_cq, w_ckv, w_kr, w_qs, w_ks, w_vs = (w_in[:, o[i]:o[i + 1]] for i in range(6))
    w_lat = jnp.concatenate([w_cq, jnp.zeros((d, LANES), F32), w_ckv], axis=1).astype(BF16)
    lat = _proj([x], [w_lat], prologue="ada", norm=norm, seq=seq, tm=PROJ_ROWS, tn=w_lat.shape[1],
                out_dtype=F32, name="proj_latent")
    w_b = jnp.concatenate([_pair_layout(w_qs), _pair_layout(_dup_heads(w_ks, SWA_KV_HEADS)),
                           _dup_heads(w_vs, SWA_KV_HEADS),
                           w_kr, jnp.zeros((d, LANES - MLA_ROPE), F32)], axis=1).astype(BF16)
    qkv_b = _proj([x], [w_b], prologue="ada", norm=norm, seq=seq, tm=PROJ_ROWS, tn=w_b.shape[1], out_dtype=BF16,
                  rope=(cos, sin), rope_pattern=(ROPE_PAIR,) * 10 + (0, 0) + (ROPE_ROTATE,),
                  group_scales=(QK_LOG2_SCALE,) * 8 + (1.0,) * 5, name="proj_swa")
    kr_col = (w_b.shape[1] - LANES) // LANES
    qd = MLA_NOPE + MLA_ROPE
    w_q = w_qb.reshape(MLA_Q_LORA, MLA_HEADS, qd)
    w_q = jnp.concatenate([w_q, jnp.zeros((MLA_Q_LORA, MLA_HEADS, 2 * LANES - qd), F32)], axis=2)
    w_q = w_q.reshape(MLA_Q_LORA, MLA_HEADS * 2 * LANES).astype(BF16)
    q_cat = _proj([lat], [w_q], prologue="rms", norm=(g_q.reshape(1, -1),), lhs_cols=(MLA_Q_LORA, 0),
                  seq=seq, tm=PROJ_ROWS, tn=w_q.shape[1], out_dtype=BF16, rope=(cos, sin),
                  rope_pattern=(0, 1) * MLA_HEADS, out_scale=qd ** -0.5, name="proj_mla_q")
    w_kv = w_kvb.reshape(MLA_KV_LORA, MLA_HEADS, MLA_NOPE + MLA_V)
    w_kv = jnp.concatenate([w_kv[:, :, :MLA_NOPE].reshape(MLA_KV_LORA, -1),
                            w_kv[:, :, MLA_NOPE:].reshape(MLA_KV_LORA, -1)], axis=1).astype(BF16)
    kv = _proj([lat], [w_kv], prologue="rms", norm=(g_kv.reshape(1, -1),), lhs_cols=(MLA_KV_LORA, 2),
               seq=seq, tm=PROJ_ROWS, tn=w_kv.shape[1], out_dtype=BF16, name="proj_mla_kv")
    o_a = _mla_attention(q_cat, kv, qkv_b, kr_col, bsz=bsz, seq=seq, tq=256)
    qkv_b4 = qkv_b.reshape(bsz, 1, seq, qkv_b.shape[1])
    (o_b,) = _banded_attention(qkv_b4, qkv_b4, qkv_b4, bsz=bsz, seq=seq, dil=1, radius=SWA_RADIUS,
                               q_col=0, k_col=4, v_col=5, kv_width=2 * LANES,
                               pairs_per_kv=SWA_Q_HEADS // SWA_KV_HEADS // 2, sink=sink, want_lse=False,
                               name="swa_attention")
    o_b = o_b.reshape(bsz * seq, SWA_Q_HEADS * HEAD_DIM)
    na = MLA_HEADS * MLA_V
    return _proj([o_a, o_b], [w_out[:na].astype(BF16), w_out[na:].astype(BF16)], prologue="plain",
                 seq=seq, tm=PROJ_ROWS, tn=d, out_dtype=F32, residual=(x, g1), name="proj_out_ab")


def _mixer_dilated(x, norm, g1, cos, sin, w_in, w_out, *, bsz, seq):
    d = x.shape[1]
    qw = DIL_HEADS * HEAD_DIM
    w_bf = w_in.astype(BF16)
    outs, lses = [], []
    for g, (window, dil) in enumerate(DIL_PATTERN):
        w_g = w_bf[:, g * 3 * qw:(g + 1) * 3 * qw]
        w_g = jnp.concatenate([_pair_layout(w_g[:, :2 * qw]), w_g[:, 2 * qw:]], axis=1)
        qkv = _proj([x], [w_g], prologue="ada", norm=norm, seq=seq, tm=PROJ_ROWS, tn=3 * qw, sub_tn=qw, out_dtype=BF16,
                    rope=(cos, sin), rope_pattern=(ROPE_PAIR,) * (2 * qw // LANES) + (0,) * (qw // LANES),
                    group_scales=(QK_LOG2_SCALE,) * (qw // LANES) + (1.0,) * (2 * qw // LANES),
                    out_dil=dil, name=f"proj_dil_{dil}")
        qkv = qkv.reshape(bsz, dil, seq // dil, 3 * qw)
        o, lse = _banded_attention(qkv, qkv, qkv, bsz=bsz, seq=seq, dil=dil, radius=window // (2 * dil),
                                   q_col=0, k_col=1, v_col=2, kv_width=qw, pairs_per_kv=1,
                                   want_lse=True, name=f"dilated_attention_{dil}")
        outs.append(o)
        lses.append(lse)
    return _proj(outs + lses, [w_out.astype(BF16)], prologue="dil", seq=seq, tm=512, tn=d, out_dtype=F32,
                 residual=(x, g1), in_dils=tuple(dil for _, dil in DIL_PATTERN), name="proj_out_c")


def kernel(x, c, positions, w_mod, b_mod, g_norm_mix, g_norm_ffn, w_in_ab, mla_g_q, mla_w_qb, mla_g_kv,
           mla_w_kvb, swa_sink, w_out_ab, w_in_c, w_out_c, w_router, b_router, w_gu, b_gu, w_down, b_down,
           g_final):
    bsz, seq, d = x.shape
    depth = w_mod.shape[0]
    cos, sin = _rope_tables(positions)
    mod = _modulation(c, w_mod, b_mod)
    xt = x.reshape(bsz * seq, d)
    for layer in range(depth):
        sh1, sc1, g1, sh2, sc2, g2 = (mod[layer, :, i * d:(i + 1) * d].reshape(bsz, 1, d) for i in range(6))
        li = layer // 2
        norm = (g_norm_mix[layer].reshape(1, d), sc1, sh1)
        if layer % 2 == 0:
            xt = _mixer_mla_swa(xt, norm, g1, cos, sin, w_in_ab[li], mla_g_q[li], mla_w_qb[li], mla_g_kv[li],
                                mla_w_kvb[li], swa_sink[li], w_out_ab[li], bsz=bsz, seq=seq)
        else:
            xt = _mixer_dilated(xt, norm, g1, cos, sin, w_in_c[li], w_out_c[li], bsz=bsz, seq=seq)
        norm = (g_norm_ffn[layer].reshape(1, d), sc2, sh2)
        xt = _moe(xt, norm, g2, w_router[layer], b_router[layer], layer, w_gu, b_gu, w_down, b_down,
                  g_final if layer == depth - 1 else None, seq=seq)
    return xt.reshape(bsz, seq, d)
```

```python
import functools

import jax
import jax.numpy as jnp
import numpy as np
from jax import lax
from jax.experimental import pallas as pl
from jax.experimental.pallas import tpu as pltpu
from jax.experimental.pallas import tpu_sc as plsc

F32 = jnp.float32
BF16 = jnp.bfloat16

EPS = 1e-6
ROPE_THETA = 10000.0
LANES = 128
HEAD_DIM = 64
ROPE_HALF = HEAD_DIM // 2

MLA_HEADS = 8
MLA_Q_LORA = 384
MLA_KV_LORA = 256
MLA_NOPE = 128
MLA_ROPE = 64
MLA_V = 128
SWA_Q_HEADS = 16
SWA_KV_HEADS = 2
SWA_RADIUS = 128
DIL_PATTERN = ((128, 1), (512, 4), (2048, 16))
DIL_HEADS = 16
N_EXPERTS = 32
TOP_K = 4
SWIGLU_LIMIT = 7.0
SWIGLU_ALPHA = 1.702
MOE_ROWS = 1024
MOE_PASS_ROWS = 512
PROJ_ROWS = 1024
BAND_STEP_ROWS = 1024
BAND_SUB_ROWS = 128
LOG2_E = float(np.log2(np.e))
QK_LOG2_SCALE = HEAD_DIM ** -0.5 * LOG2_E

VMEM_LIMIT = 48 * 1024 * 1024
BIG_VMEM_LIMIT = 56 * 1024 * 1024


def _cparams(sem):
    return pltpu.CompilerParams(dimension_semantics=sem, vmem_limit_bytes=VMEM_LIMIT)


def _mod_kernel(c_ref, w_ref, b_ref, o_ref):
    c = c_ref[...]
    ca = c * (1.0 / (1.0 + jnp.exp(-c)))
    o_ref[...] = jnp.dot(ca, w_ref[...], preferred_element_type=F32,
                         precision=lax.Precision.HIGHEST) + b_ref[...]


def _modulation(c, w_mod, b_mod):
    depth, d, n = w_mod.shape
    bsz = c.shape[0]
    tn = 1536
    return pl.pallas_call(
        _mod_kernel,
        grid=(depth, n // tn),
        in_specs=[pl.BlockSpec((bsz, d), lambda l, j: (0, 0)),
                  pl.BlockSpec((None, d, tn), lambda l, j: (l, 0, j)),
                  pl.BlockSpec((None, 1, tn), lambda l, j: (l, 0, j))],
        out_specs=pl.BlockSpec((None, bsz, tn), lambda l, j: (l, 0, j)),
        out_shape=jax.ShapeDtypeStruct((depth, bsz, n), F32),
        compiler_params=_cparams(("parallel", "parallel")),
        name="modulation",
    )(c, w_mod, b_mod.reshape(depth, 1, n))


def _rope_table_kernel(pos_ref, inv_ref, sign_ref, cos_ref, sin_ref):
    ang = inv_ref[...] * pos_ref[...].astype(F32)
    reps = LANES // ROPE_HALF
    cos_ref[...] = jnp.transpose(jnp.concatenate([jnp.cos(ang)] * reps, axis=0))
    sin_ref[...] = jnp.transpose(jnp.concatenate([jnp.sin(ang)] * reps, axis=0) * sign_ref[...])


def _rope_tables(positions):
    bsz, s = positions.shape
    inv = ROPE_THETA ** (-jnp.arange(0, HEAD_DIM, 2, dtype=F32) / HEAD_DIM)
    sign = jnp.where((jnp.arange(LANES) % HEAD_DIM) < ROPE_HALF, -1.0, 1.0).astype(F32).reshape(LANES, 1)
    out = jax.ShapeDtypeStruct((bsz * s, LANES), F32)
    return pl.pallas_call(
        _rope_table_kernel,
        grid=(bsz,),
        in_specs=[pl.BlockSpec((None, 1, s), lambda b: (b, 0, 0)),
                  pl.BlockSpec((ROPE_HALF, 1), lambda b: (0, 0)),
                  pl.BlockSpec((LANES, 1), lambda b: (0, 0))],
        out_specs=[pl.BlockSpec((s, LANES), lambda b: (b, 0)),
                   pl.BlockSpec((s, LANES), lambda b: (b, 0))],
        out_shape=[out, out],
        compiler_params=_cparams(("parallel",)),
        name="rope_tables",
    )(positions.reshape(bsz, 1, s), inv.reshape(ROPE_HALF, 1), sign)


def _rms(x):
    return x * lax.rsqrt(jnp.mean(x * x, axis=-1, keepdims=True) + EPS)


def _ada_norm(x, g, sc, sh):
    return (_rms(x) * g) * (1.0 + sc) + sh


def _pack_bf16_pairs(x):
    w = x.shape[1] // 2
    hi = pltpu.bitcast(x[:, :w].astype(BF16).astype(F32), jnp.int32)
    lo = pltpu.bitcast(x[:, w:].astype(BF16).astype(F32), jnp.int32)
    return hi | lax.shift_right_logical(lo, 16)


def _unpack_bf16_pairs(p):
    hi = pltpu.bitcast(p & jnp.int32(-65536), F32)
    lo = pltpu.bitcast(lax.shift_left(p, 16), F32)
    return hi, lo


ROPE_ROTATE, ROPE_PAIR = 1, 2


def _pair_layout(w):
    k, n = w.shape
    w = w.reshape(k, n // LANES, 2, 2, ROPE_HALF)
    return jnp.swapaxes(w, 2, 3).reshape(k, n)


def _rope_group(a, cos, sin, first_half):
    rot = jnp.where(first_half, pltpu.roll(a, LANES - ROPE_HALF, 1), pltpu.roll(a, ROPE_HALF, 1))
    return a * cos + rot * sin


def _proj_kernel(*refs, prologue, n_lhs, rope_pattern, group_scales, sub_tn, out_scale, residual, in_dils,
                 out_dil):
    pos = 0
    if prologue == "ada":
        x_ref, g_ref, sc_ref, sh_ref = refs[:4]
        pos = 4
    elif prologue == "rms":
        x_ref, g_ref = refs[:2]
        pos = 2
    elif prologue == "dil":
        o_refs = refs[0:3]
        l_refs = refs[3:6]
        pos = 6
    else:
        x_refs = refs[:n_lhs]
        pos = n_lhs
    w_refs = refs[pos:pos + n_lhs]
    pos += n_lhs
    if rope_pattern is not None:
        cos_ref, sin_ref = refs[pos:pos + 2]
        pos += 2
    if residual:
        res_ref, gate_ref = refs[pos:pos + 2]
        pos += 2
    o_ref = refs[pos]
    pos += 1
    h_ref = refs[pos] if prologue != "plain" else None
    slab_ref = refs[pos + 1] if (prologue == "dil" or out_dil > 1) else None

    if prologue != "plain":
        @pl.when(pl.program_id(1) == 0)
        def _():
            if prologue == "ada":
                h_ref[...] = _ada_norm(x_ref[...], g_ref[...], sc_ref[...], sh_ref[...]).astype(BF16)
            elif prologue == "rms":
                h_ref[...] = (_rms(x_ref[...]) * g_ref[...]).astype(BF16)
            else:
                tm = h_ref.shape[0]
                for c in range(h_ref.shape[1] // LANES):
                    cols = slice(c * LANES, (c + 1) * LANES)
                    vals = []
                    for i, (ref, dil) in enumerate(zip(o_refs + l_refs, in_dils + in_dils)):
                        if dil == 1:
                            vals.append(ref[0, :, cols].astype(F32))
                        else:
                            for r in range(dil):
                                slab_ref[i, pl.ds(r, tm // dil, stride=dil), :] = ref[r, :, cols].astype(F32)
                            vals.append(slab_ref[i])
                    o0, o1, o2, l0, l1, l2 = vals
                    m = jnp.maximum(jnp.maximum(l0, l1), l2)
                    e0, e1, e2 = jnp.exp2(l0 - m), jnp.exp2(l1 - m), jnp.exp2(l2 - m)
                    h_ref[:, cols] = ((e0 * o0 + e1 * o1 + e2 * o2) / (e0 + e1 + e2)).astype(BF16)

    n_total = w_refs[0].shape[1]
    tn = n_total if sub_tn is None else sub_tn

    def matmul(t):
        cols = slice(t * tn, (t + 1) * tn)
        if prologue != "plain":
            acc = jnp.dot(h_ref[...], w_refs[0][:, cols], preferred_element_type=F32)
        else:
            acc = jnp.dot(x_refs[0][...], w_refs[0][:, cols], preferred_element_type=F32)
            for xr, wr in zip(x_refs[1:], w_refs[1:]):
                acc = acc + jnp.dot(xr[...], wr[:, cols], preferred_element_type=F32)
        return acc * out_scale if out_scale != 1.0 else acc

    if residual:
        o_ref[...] = (res_ref[...] + gate_ref[...] * matmul(0)).astype(o_ref.dtype)
        return

    pattern = rope_pattern if rope_pattern is not None else (0,) * (n_total // LANES)
    if any(pattern):
        cos = cos_ref[...]
        sin = sin_ref[...]
        lane = lax.broadcasted_iota(jnp.int32, cos.shape, 1)
        first_half = (lane % HEAD_DIM) < ROPE_HALF
        if ROPE_PAIR in pattern:
            sin_pair = jnp.where(lane < HEAD_DIM, -1.0, 1.0) * jnp.where(first_half, -sin, sin)

    def epilogue(acc, t):
        tm = acc.shape[0]
        for c in range(tn // LANES):
            g = t * (tn // LANES) + c
            cols = slice(g * LANES, (g + 1) * LANES)
            a = acc[:, c * LANES:(c + 1) * LANES]
            if pattern[g] == ROPE_ROTATE:
                a = _rope_group(a, cos, sin, first_half)
            elif pattern[g] == ROPE_PAIR:
                a = a * cos + pltpu.roll(a, HEAD_DIM, 1) * sin_pair
            if group_scales is not None and group_scales[g] != 1.0:
                a = a * group_scales[g]
            if out_dil == 1:
                o_ref[:, cols] = a.astype(o_ref.dtype)
            else:
                slab_ref[0] = a
                for r in range(out_dil):
                    o_ref[r, :, cols] = slab_ref[0, pl.ds(r, tm // out_dil, stride=out_dil), :].astype(o_ref.dtype)

    acc_next = matmul(0)
    for t in range(n_total // tn):
        acc = acc_next
        if t + 1 < n_total // tn:
            acc_next = matmul(t + 1)
        epilogue(acc, t)


def _proj(lhs, ws, *, prologue, seq, tm, tn, out_dtype, norm=None, lhs_cols=None, rope=None,
          rope_pattern=None, group_scales=None, sub_tn=None, out_scale=1.0, residual=None, in_dils=None,
          out_dil=1, name):
    k, n = ws[0].shape
    bsz_seq = lhs[0].shape[0] if prologue != "dil" else lhs[0].shape[0] * seq
    t = bsz_seq
    assert t % tm == 0 and n % tn == 0 and seq % tm == 0
    bpb = seq // tm
    n_lhs = len(ws)
    args, specs = [], []
    row = lambda i, j: (i, 0)
    if prologue in ("ada", "rms"):
        kw, kc = lhs_cols if lhs_cols is not None else (k, 0)
        args.append(lhs[0])
        specs.append(pl.BlockSpec((tm, kw), lambda i, j: (i, kc)))
        args.append(norm[0])
        specs.append(pl.BlockSpec((1, k), lambda i, j: (0, 0)))
        if prologue == "ada":
            for v in norm[1:3]:
                args.append(v)
                specs.append(pl.BlockSpec((None, 1, k), lambda i, j: (i // bpb, 0, 0)))
    elif prologue == "dil":
        for a, dil in zip(lhs, in_dils + in_dils):
            args.append(a)
            specs.append(pl.BlockSpec((None, dil, tm // dil, k), lambda i, j: (i // bpb, 0, i % bpb, 0)))
    else:
        for a, w in zip(lhs, ws):
            args.append(a)
            specs.append(pl.BlockSpec((tm, w.shape[0]), row))
    for w in ws:
        args.append(w)
        specs.append(pl.BlockSpec((w.shape[0], tn), lambda i, j: (0, j)))
    if rope_pattern is not None:
        assert len(rope_pattern) == tn // LANES
        for tab in rope:
            args.append(tab)
            specs.append(pl.BlockSpec((tm, LANES), row))
    if residual is not None:
        res, gate = residual
        args.append(res)
        specs.append(pl.BlockSpec((tm, tn), lambda i, j: (i, j)))
        args.append(gate)
        specs.append(pl.BlockSpec((None, 1, tn), lambda i, j: (i // bpb, 0, j)))
    scratch = [] if prologue == "plain" else [pltpu.VMEM((tm, k), BF16)]
    if prologue == "dil":
        scratch.append(pltpu.VMEM((2 * len(in_dils), tm, LANES), F32))
    elif out_dil > 1:
        scratch.append(pltpu.VMEM((1, tm, LANES), F32))
    if out_dil == 1:
        out_spec = pl.BlockSpec((tm, tn), lambda i, j: (i, j))
        out_shape = jax.ShapeDtypeStruct((t, n), out_dtype)
    else:
        out_spec = pl.BlockSpec((None, out_dil, tm // out_dil, tn), lambda i, j: (i // bpb, 0, i % bpb, j))
        out_shape = jax.ShapeDtypeStruct((t // seq, out_dil, seq // out_dil, n), out_dtype)
    kern = functools.partial(_proj_kernel, prologue=prologue, n_lhs=n_lhs, rope_pattern=rope_pattern,
                             group_scales=group_scales, sub_tn=sub_tn, out_scale=out_scale,
                             residual=residual is not None,
                             in_dils=in_dils, out_dil=out_dil)
    return pl.pallas_call(
        kern,
        grid=(t // tm, n // tn),
        in_specs=specs,
        out_specs=out_spec,
        out_shape=out_shape,
        scratch_shapes=scratch,
        compiler_params=_cparams(("parallel", "arbitrary")),
        name=name,
    )(*args)


MLA_HEADS_PER_STEP = 4


def _mla_scores(q_ref, kn_ref, kr, h):
    k = jnp.concatenate([kn_ref[:, h * LANES:(h + 1) * LANES], kr], axis=1)
    q = q_ref[:, h * 2 * LANES:(h + 1) * 2 * LANES]
    return lax.dot_general(q, k, (((1,), (1,)), ((), ())), preferred_element_type=F32)


def _mla_kernel(q_ref, kn_ref, kr_ref, v_ref, o_ref):
    kr = kr_ref[...]
    s_next = _mla_scores(q_ref, kn_ref, kr, 0)
    for h in range(MLA_HEADS_PER_STEP):
        s = s_next
        if h + 1 < MLA_HEADS_PER_STEP:
            s_next = _mla_scores(q_ref, kn_ref, kr, h + 1)
        m = jnp.max(s, axis=1, keepdims=True)
        p = jnp.exp(s - m).astype(BF16)
        v = v_ref[:, h * LANES:(h + 1) * LANES]
        oe = jnp.dot(p, jnp.concatenate([v, jnp.ones_like(v)], axis=1), preferred_element_type=F32)
        o_ref[:, h * LANES:(h + 1) * LANES] = (oe[:, :LANES] / oe[:, LANES:]).astype(o_ref.dtype)


def _mla_attention(q_cat, kv, qkv_b, kr_col, *, bsz, seq, tq):
    t = bsz * seq
    nq = seq // tq
    hps = MLA_HEADS_PER_STEP
    groups = MLA_HEADS // hps
    return pl.pallas_call(
        _mla_kernel,
        grid=(bsz, groups, nq),
        in_specs=[pl.BlockSpec((tq, hps * 2 * LANES), lambda b, h, i: (b * nq + i, h)),
                  pl.BlockSpec((seq, hps * LANES), lambda b, h, i: (b, h)),
                  pl.BlockSpec((seq, LANES), lambda b, h, i: (b, kr_col)),
                  pl.BlockSpec((seq, hps * LANES), lambda b, h, i: (b, groups + h))],
        out_specs=pl.BlockSpec((tq, hps * LANES), lambda b, h, i: (b * nq + i, h)),
        out_shape=jax.ShapeDtypeStruct((t, MLA_HEADS * MLA_V), BF16),
        compiler_params=pltpu.CompilerParams(dimension_semantics=("parallel", "parallel", "parallel"),
                                             vmem_limit_bytes=BIG_VMEM_LIMIT),
        name="mla_attention",
    )(q_cat, kv, qkv_b, kv)


def _banded_kernel(*refs, n_seqs, has_sink, want_lse, **kw):
    sink_ref = refs[0] if has_sink else None
    refs = refs[1:] if has_sink else refs
    for rr in range(n_seqs):
        ins = [r.at[rr] for r in refs[:7]]
        o_ref = refs[7].at[rr]
        lse_ref = refs[8].at[rr] if want_lse else None
        _banded_sequence(sink_ref, *ins, o_ref, lse_ref, has_sink=has_sink, want_lse=want_lse, **kw)


def _banded_sequence(sink_ref, q_ref, kp_ref, kc_ref, kn_ref, vp_ref, vc_ref, vn_ref, o_ref, lse_ref, *,
                     radius, tq, hb, length, n_pairs, pairs_per_kv, has_sink, want_lse):
    j = pl.program_id(2)
    kw = jnp.concatenate([kp_ref[...], kc_ref[...], kn_ref[...]], axis=0)
    vw = jnp.concatenate([vp_ref[...], vc_ref[...], vn_ref[...]], axis=0)
    qn = min(tq, BAND_SUB_ROWS)
    kn = qn + 2 * hb
    lane = lax.broadcasted_iota(jnp.int32, (qn, LANES), 1)
    low = lane < HEAD_DIM
    first = (lane % HEAD_DIM) < ROPE_HALF

    for q0 in range(0, tq, qn):
        qpos = j * tq + q0 + lax.broadcasted_iota(jnp.int32, (qn, kn), 0)
        kpos = j * tq - hb + q0 + lax.broadcasted_iota(jnp.int32, (qn, kn), 1)
        valid = (jnp.abs(qpos - kpos) <= radius) & (kpos >= 0) & (kpos < length)
        for p in range(n_pairs):
            c = p // pairs_per_kv
            qp = q_ref[q0:q0 + qn, p * LANES:(p + 1) * LANES]
            kp = kw[q0:q0 + kn, c * LANES:(c + 1) * LANES]
            vp = vw[q0:q0 + kn, c * LANES:(c + 1) * LANES]
            vp = jnp.concatenate([vp, jnp.ones_like(vp)], axis=1)
            halves = []
            for half in range(2):
                qh = jnp.where(first if half == 0 else ~first, qp, jnp.zeros_like(qp))
                s = lax.dot_general(qh, kp, (((1,), (1,)), ((), ())), preferred_element_type=F32)
                s = jnp.where(valid, s, -jnp.inf)
                m = jnp.max(s, axis=1, keepdims=True)
                if has_sink:
                    sk = sink_ref[2 * p + half] * LOG2_E
                    m = jnp.maximum(m, sk)
                oe = jnp.dot(jnp.exp2(s - m).astype(BF16), vp, preferred_element_type=F32)
                den = oe[:, LANES:]
                if has_sink:
                    den = den + jnp.exp2(sk - m)
                halves.append((oe[:, :LANES] / den, m + jnp.log2(den) if want_lse else None))
            o_ref[q0:q0 + qn, p * LANES:(p + 1) * LANES] = jnp.where(
                low, halves[0][0], halves[1][0]).astype(o_ref.dtype)
            if want_lse:
                lse_ref[q0:q0 + qn, p * LANES:(p + 1) * LANES] = jnp.where(low, halves[0][1], halves[1][1])


def _banded_attention(q_arr, k_arr, v_arr, *, bsz, seq, dil, radius, q_col, k_col, v_col, kv_width,
                      pairs_per_kv, sink=None, want_lse, name):
    length = seq // dil
    hb = radius
    tq = min(BAND_STEP_ROWS, length)
    assert tq % hb == 0 and length % tq == 0
    qw = DIL_HEADS * HEAD_DIM
    n_pairs = qw // LANES
    rep = tq // hb
    last_hb = length // hb - 1
    n_seqs = min(dil, max(1, BAND_STEP_ROWS // length))
    assert dil % n_seqs == 0

    def prev_spec(col):
        return pl.BlockSpec((None, n_seqs, hb, kv_width),
                            lambda b, r, j: (b, r, jnp.maximum(j * rep - 1, 0), col))

    def cur_spec(col):
        return pl.BlockSpec((None, n_seqs, tq, kv_width), lambda b, r, j: (b, r, j, col))

    def next_spec(col):
        return pl.BlockSpec((None, n_seqs, hb, kv_width),
                            lambda b, r, j: (b, r, jnp.minimum((j + 1) * rep, last_hb), col))

    in_specs = [pl.BlockSpec((None, n_seqs, tq, qw), lambda b, r, j: (b, r, j, q_col)),
                prev_spec(k_col), cur_spec(k_col), next_spec(k_col),
                prev_spec(v_col), cur_spec(v_col), next_spec(v_col)]
    args = [q_arr, k_arr, k_arr, k_arr, v_arr, v_arr, v_arr]
    if sink is not None:
        in_specs = [pl.BlockSpec(memory_space=pltpu.SMEM)] + in_specs
        args = [sink] + args
    out_spec = pl.BlockSpec((None, n_seqs, tq, qw), lambda b, r, j: (b, r, j, 0))
    out_specs = [out_spec]
    out_shape = [jax.ShapeDtypeStruct((bsz, dil, length, qw), BF16)]
    if want_lse:
        out_specs.append(out_spec)
        out_shape.append(jax.ShapeDtypeStruct((bsz, dil, length, qw), F32))
    kern = functools.partial(_banded_kernel, n_seqs=n_seqs, radius=radius, tq=tq, hb=hb, length=length,
                             n_pairs=n_pairs, pairs_per_kv=pairs_per_kv, has_sink=sink is not None,
                             want_lse=want_lse)
    return pl.pallas_call(
        kern,
        grid=(bsz, dil // n_seqs, length // tq),
        in_specs=in_specs,
        out_specs=out_specs,
        out_shape=out_shape,
        compiler_params=_cparams(("parallel", "parallel", "parallel")),
        name=name,
    )(*args)


def _router_kernel(x_ref, g_ref, sc_ref, sh_ref, wt_ref, b_ref, tri_ref,
                   idx_ref, gate_ref, rank_ref, cnt_ref, h_ref, carry_ref):
    @pl.when(pl.program_id(0) == 0)
    def _():
        carry_ref[...] = jnp.zeros_like(carry_ref)

    h = _ada_norm(x_ref[...], g_ref[...], sc_ref[...], sh_ref[...])
    h_ref[...] = _pack_bf16_pairs(h)
    lt = lax.dot_general(wt_ref[...], h, (((1,), (1,)), ((), ())), preferred_element_type=F32,
                         precision=lax.Precision.HIGHEST) + b_ref[...]
    n_e, tm = lt.shape
    e_iota = lax.broadcasted_iota(jnp.int32, (n_e, tm), 0)
    vals, sels = [], []
    for k in range(TOP_K):
        m = jnp.max(lt, axis=0, keepdims=True)
        idx = jnp.min(jnp.where(lt == m, e_iota, n_e), axis=0, keepdims=True)
        sel = e_iota == idx
        idx_ref[k:k + 1, :] = idx
        vals.append(m)
        sels.append(sel)
        lt = jnp.where(sel, -jnp.inf, lt)
    exps = [jnp.exp(v - vals[0]) for v in vals]
    den = exps[0] + exps[1] + exps[2] + exps[3]
    for k in range(TOP_K):
        gate_ref[k:k + 1, :] = exps[k] / den

    onehot = jnp.where(sels[0] | sels[1] | sels[2] | sels[3], 1.0, 0.0)
    carry = carry_ref[:, 0:1]
    before = jnp.dot(onehot.astype(BF16), tri_ref[...], preferred_element_type=F32) + carry
    for k in range(TOP_K):
        rank_ref[k:k + 1, :] = jnp.sum(jnp.where(sels[k], before, 0.0), axis=0, keepdims=True).astype(jnp.int32)
    total = carry + jnp.sum(onehot, axis=1, keepdims=True)
    carry_ref[...] = jnp.broadcast_to(total, carry_ref.shape)
    cnt_ref[...] = jnp.broadcast_to(total, cnt_ref.shape)


def _router(x, norm, w_router, b_router, *, seq, tm=PROJ_ROWS):
    t, d = x.shape
    n_e = w_router.shape[1]
    bpb = seq // tm
    tri = (jnp.arange(tm)[:, None] < jnp.arange(tm)[None, :]).astype(BF16)
    g, sc, sh = norm
    out_i = jax.ShapeDtypeStruct((TOP_K, t), jnp.int32)
    tok = pl.BlockSpec((TOP_K, tm), lambda i: (0, i))
    return pl.pallas_call(
        _router_kernel,
        grid=(t // tm,),
        in_specs=[pl.BlockSpec((tm, d), lambda i: (i, 0)),
                  pl.BlockSpec((1, d), lambda i: (0, 0)),
                  pl.BlockSpec((None, 1, d), lambda i: (i // bpb, 0, 0)),
                  pl.BlockSpec((None, 1, d), lambda i: (i // bpb, 0, 0)),
                  pl.BlockSpec((n_e, d), lambda i: (0, 0)),
                  pl.BlockSpec((n_e, 1), lambda i: (0, 0)),
                  pl.BlockSpec((tm, tm), lambda i: (0, 0))],
        out_specs=[tok, tok, tok, pl.BlockSpec((n_e, LANES), lambda i: (0, 0)),
                   pl.BlockSpec((tm, d // 2), lambda i: (i, 0))],
        out_shape=[out_i, jax.ShapeDtypeStruct((TOP_K, t), F32), out_i,
                   jax.ShapeDtypeStruct((n_e, LANES), F32), jax.ShapeDtypeStruct((t, d // 2), jnp.int32)],
        scratch_shapes=[pltpu.VMEM((n_e, LANES), F32)],
        compiler_params=_cparams(("arbitrary",)),
        name="router",
    )(x, g, sc, sh, w_router.T, b_router.reshape(n_e, 1), tri)


SC_CORES = 2
SC_SUBCORES = 16
SC_WORKERS = SC_CORES * SC_SUBCORES
SC_ROWS = 64


def _sc_worker_id():
    return lax.axis_index("subcore") * SC_CORES + lax.axis_index("core")


def _sc_mesh():
    return plsc.VectorSubcoreMesh(core_axis_name="core", subcore_axis_name="subcore")


def _sc_dispatch(h, dest_flat, pad_rows):
    t, d = h.shape
    n_pad = pad_rows.shape[0]
    n_rows = TOP_K * t + n_pad
    tok_w = t // SC_WORKERS
    pad_w = n_pad // SC_WORKERS
    nb = tok_w // SC_ROWS
    assert tok_w % (2 * SC_ROWS) == 0 and pad_w % SC_ROWS == 0

    @functools.partial(
        pl.kernel, out_type=jax.ShapeDtypeStruct((n_rows, d), h.dtype), mesh=_sc_mesh(),
        scratch_types=[pltpu.VMEM((SC_ROWS,), jnp.int32), pltpu.VMEM((2, SC_ROWS, d), h.dtype),
                       pltpu.SemaphoreType.DMA((2,))])
    def scatter_rows(h_hbm, dest_hbm, pad_hbm, zero_hbm, xs_hbm, idx_v, rows_v, sem):
        wid = _sc_worker_id()
        base = wid * tok_w

        def load(b, slot):
            return pltpu.make_async_copy(h_hbm.at[pl.ds(base + b * SC_ROWS, SC_ROWS)], rows_v.at[slot],
                                         sem.at[slot])

        load(0, 0).start()

        @pl.loop(0, nb, step=2)
        def _(i):
            for slot in range(2):
                b = i + slot
                load(b, slot).wait()

                @pl.when(b + 1 < nb)
                def _():
                    load(b + 1, 1 - slot).start()

                for k in range(TOP_K):
                    pltpu.sync_copy(dest_hbm.at[pl.ds(k * t + base + b * SC_ROWS, SC_ROWS)], idx_v)
                    pltpu.sync_copy(rows_v.at[slot], xs_hbm.at[idx_v])

        pltpu.sync_copy(zero_hbm, rows_v.at[0])

        @pl.loop(0, pad_w // SC_ROWS)
        def _(b):
            pltpu.sync_copy(pad_hbm.at[pl.ds(wid * pad_w + b * SC_ROWS, SC_ROWS)], idx_v)
            pltpu.sync_copy(rows_v.at[0], xs_hbm.at[idx_v])

    return scatter_rows(h, dest_flat, pad_rows, jnp.zeros((SC_ROWS, d), h.dtype))


def _sc_gather(table, idx):
    n = idx.shape[0]
    d = table.shape[1]
    per_w = n // SC_WORKERS
    nb = per_w // SC_ROWS
    assert per_w % (2 * SC_ROWS) == 0

    @functools.partial(
        pl.kernel, out_type=jax.ShapeDtypeStruct((n, d), table.dtype), mesh=_sc_mesh(),
        scratch_types=[pltpu.VMEM((2, SC_ROWS), jnp.int32), pltpu.VMEM((2, SC_ROWS, d), table.dtype),
                       pltpu.SemaphoreType.DMA((2,))])
    def gather_rows(table_hbm, idx_hbm, out_hbm, idx_v, rows_v, sem):
        base = _sc_worker_id() * per_w

        def gather(slot):
            return pltpu.make_async_copy(table_hbm.at[idx_v.at[slot]], rows_v.at[slot], sem.at[slot])

        def start(b, slot):
            pltpu.sync_copy(idx_hbm.at[pl.ds(base + b * SC_ROWS, SC_ROWS)], idx_v.at[slot])
            gather(slot).start()

        start(0, 0)
        start(1, 1)

        @pl.loop(0, nb, step=2)
        def _(i):
            for slot in range(2):
                b = i + slot
                gather(slot).wait()
                pltpu.sync_copy(rows_v.at[slot], out_hbm.at[pl.ds(base + b * SC_ROWS, SC_ROWS)])

                @pl.when(b + 2 < nb)
                def _():
                    start(b + 2, slot)

    return gather_rows(table, idx)


def _expert_kernel(be_ref, nv_ref, sb_ref, xs_ref, wgu_ref, bgu_ref, wd_ref, bd_ref, y_ref, wgu_bf, wd_bf):
    i = pl.program_id(0)
    n_valid = nv_ref[i]

    @pl.when((n_valid > 0) & ((i == 0) | (be_ref[i] != be_ref[jnp.maximum(i - 1, 0)])))
    def _():
        wgu_bf[...] = wgu_ref[...].astype(BF16)
        wd_bf[...] = wd_ref[...].astype(BF16)

    d_ff = wd_ref.shape[0]
    for r0 in range(0, MOE_ROWS, MOE_PASS_ROWS):
        rows = slice(r0, r0 + MOE_PASS_ROWS)

        @pl.when(n_valid > r0)
        def _():
            x_hi, x_lo = _unpack_bf16_pairs(xs_ref[rows, :])
            x = jnp.concatenate([x_hi.astype(BF16), x_lo.astype(BF16)], axis=1)
            gu = jnp.dot(x, wgu_bf[...], preferred_element_type=F32) + bgu_ref[...]
            glu = jnp.minimum(gu[:, :d_ff], SWIGLU_LIMIT)
            lin = jnp.clip(gu[:, d_ff:], -SWIGLU_LIMIT, SWIGLU_LIMIT)
            act = glu * (1.0 / (1.0 + jnp.exp(-SWIGLU_ALPHA * glu))) * (lin + 1.0)
            y = jnp.dot(act.astype(BF16), wd_bf[...], preferred_element_type=F32) + bd_ref[...]
            y_ref[rows, :] = _pack_bf16_pairs(y)

        @pl.when(n_valid <= r0)
        def _():
            y_ref[rows, :] = jnp.zeros((MOE_PASS_ROWS, y_ref.shape[1]), y_ref.dtype)


def _experts(xs, blk_expert, n_valid, src_blk, layer, w_gu, b_gu, w_down, b_down):
    n_rows, dw = xs.shape
    _, n_e, d, f2 = w_gu.shape
    n_blocks = n_rows // MOE_ROWS

    def wmap(i, be, nv, sb):
        return (layer, be[i], 0, 0)

    grid_spec = pltpu.PrefetchScalarGridSpec(
        num_scalar_prefetch=3,
        grid=(n_blocks,),
        in_specs=[pl.BlockSpec((MOE_ROWS, dw), lambda i, be, nv, sb: (sb[i], 0)),
                  pl.BlockSpec((None, None, d, f2), wmap),
                  pl.BlockSpec((None, None, 1, f2), wmap),
                  pl.BlockSpec((None, None, f2 // 2, d), wmap),
                  pl.BlockSpec((None, None, 1, d), wmap)],
        out_specs=pl.BlockSpec((MOE_ROWS, dw), lambda i, be, nv, sb: (i, 0)),
        scratch_shapes=[pltpu.VMEM((d, f2), BF16), pltpu.VMEM((f2 // 2, d), BF16)],
    )
    depth = w_gu.shape[0]
    return pl.pallas_call(
        _expert_kernel,
        grid_spec=grid_spec,
        out_shape=jax.ShapeDtypeStruct((n_rows, dw), jnp.int32),
        compiler_params=pltpu.CompilerParams(dimension_semantics=("arbitrary",),
                                             vmem_limit_bytes=BIG_VMEM_LIMIT),
        name="moe_experts",
    )(blk_expert, n_valid, src_blk, xs, w_gu, b_gu.reshape(depth, n_e, 1, f2), w_down, b_down.reshape(depth, n_e, 1, d))


def _combine_kernel(y0_ref, y1_ref, y2_ref, y3_ref, gate_ref, x_ref, g2_ref, *rest):
    o_ref = rest[-1]
    gate = gate_ref[...]
    w = y0_ref.shape[1]
    y_hi, y_lo = None, None
    for k, y_ref in enumerate((y0_ref, y1_ref, y2_ref, y3_ref)):
        hi, lo = _unpack_bf16_pairs(y_ref[...])
        gk = gate[:, k:k + 1]
        y_hi = gk * hi if y_hi is None else y_hi + gk * hi
        y_lo = gk * lo if y_lo is None else y_lo + gk * lo
    out_hi = x_ref[:, :w] + g2_ref[:, :w] * y_hi
    out_lo = x_ref[:, w:] + g2_ref[:, w:] * y_lo
    if len(rest) == 2:
        gain = rest[0][...]
        ms = (jnp.sum(out_hi * out_hi, axis=-1, keepdims=True)
              + jnp.sum(out_lo * out_lo, axis=-1, keepdims=True)) / (2 * w)
        r = lax.rsqrt(ms + EPS)
        out_hi = out_hi * r * gain[:, :w]
        out_lo = out_lo * r * gain[:, w:]
    o_ref[:, :w] = out_hi
    o_ref[:, w:] = out_lo


def _combine(y4, gate, x, g2, final_gain, *, seq, tm=PROJ_ROWS):
    t, d = x.shape
    bpb = seq // tm
    nblk = t // tm
    y_specs = [pl.BlockSpec((tm, d // 2), functools.partial(lambda i, k: (k * nblk + i, 0), k=k))
               for k in range(TOP_K)]
    in_specs = y_specs + [pl.BlockSpec((tm, TOP_K), lambda i: (i, 0)),
                          pl.BlockSpec((tm, d), lambda i: (i, 0)),
                          pl.BlockSpec((None, 1, d), lambda i: (i // bpb, 0, 0))]
    args = [y4, y4, y4, y4, gate, x, g2]
    if final_gain is not None:
        in_specs.append(pl.BlockSpec((1, d), lambda i: (0, 0)))
        args.append(final_gain.reshape(1, d))
    return pl.pallas_call(
        _combine_kernel,
        grid=(nblk,),
        in_specs=in_specs,
        out_specs=pl.BlockSpec((tm, d), lambda i: (i, 0)),
        out_shape=jax.ShapeDtypeStruct((t, d), F32),
        compiler_params=_cparams(("parallel",)),
        name="moe_combine",
    )(*args)


def _moe(x, norm, g2, w_router, b_router, layer, w_gu, b_gu, w_down, b_down, final_gain, *, seq):
    t = x.shape[0]
    n_e = w_router.shape[1]
    idx_t, gate_t, rank_t, counts, h = _router(x, norm, w_router, b_router, seq=seq)
    e_ids = jnp.arange(n_e, dtype=jnp.int32)
    counts = counts[:, 0].astype(jnp.int32)
    padded = (counts + MOE_ROWS - 1) // MOE_ROWS * MOE_ROWS
    pend = jnp.cumsum(padded)
    pstart = pend - padded
    n_pad = n_e * MOE_ROWS
    n_rows = t * TOP_K + n_pad
    n_blocks = n_rows // MOE_ROWS
    dest_t = jnp.sum(jnp.where(idx_t[:, :, None] == e_ids, pstart, 0), axis=-1) + rank_t
    dest_flat = dest_t.reshape(TOP_K * t)
    blk_start = jnp.arange(n_blocks, dtype=jnp.int32) * MOE_ROWS
    n_used = pend[-1] // MOE_ROWS
    src_blk = jnp.minimum(jnp.arange(n_blocks, dtype=jnp.int32), n_used - 1)
    blk_expert = jnp.minimum(jnp.sum((src_blk * MOE_ROWS)[:, None] >= pend[None, :], axis=1), n_e - 1)
    blk_expert = blk_expert.astype(jnp.int32)
    own = blk_expert[:, None] == e_ids
    rows_left = jnp.sum(jnp.where(own, pstart + counts, 0), axis=-1) - blk_start
    n_valid = jnp.where(blk_start < pend[-1], jnp.clip(rows_left, 0, MOE_ROWS), 0).astype(jnp.int32)
    pad_len = padded - counts
    pad_end = jnp.cumsum(pad_len)
    j = jnp.arange(n_pad, dtype=jnp.int32)
    pe = jnp.sum(j[:, None] >= pad_end[None, :], axis=1)
    sel = jnp.minimum(pe, n_e - 1)[:, None] == e_ids
    in_expert = jnp.sum(jnp.where(sel, pstart + counts - (pad_end - pad_len), 0), axis=-1) + j
    pad_rows = jnp.where(pe < n_e, in_expert, pend[-1] + j - pad_end[-1]).astype(jnp.int32)

    xs = _sc_dispatch(h, dest_flat, pad_rows)
    yb = _experts(xs, blk_expert, n_valid, src_blk, layer, w_gu, b_gu, w_down, b_down)
    y4 = _sc_gather(yb, dest_flat)
    return _combine(y4, gate_t.T, x, g2, final_gain, seq=seq)


def _dup_heads(w, n_heads):
    d = w.shape[0]
    return jnp.repeat(w.reshape(d, n_heads, 1, HEAD_DIM), 2, axis=2).reshape(d, n_heads * LANES)


def _mixer_mla_swa(x, norm, g1, cos, sin, w_in, g_q, w_qb, g_kv, w_kvb, sink, w_out, *, bsz, seq):
    d = x.shape[1]
    o = np.cumsum((0, MLA_Q_LORA, MLA_KV_LORA, MLA_ROPE, SWA_Q_HEADS * HEAD_DIM,
                   SWA_KV_HEADS * HEAD_DIM, SWA_KV_HEADS * HEAD_DIM))
    w_cq, w_ckv, w_kr, w_qs, w_ks, w_vs = (w_in[:, o[i]:o[i + 1]] for i in range(6))
    w_lat = jnp.concatenate([w_cq, jnp.zeros((d, LANES), F32), w_ckv], axis=1).astype(BF16)
    lat = _proj([x], [w_lat], prologue="ada", norm=norm, seq=seq, tm=PROJ_ROWS, tn=w_lat.shape[1],
                out_dtype=F32, name="proj_latent")
    w_b = jnp.concatenate([_pair_layout(w_qs), _pair_layout(_dup_heads(w_ks, SWA_KV_HEADS)),
                           _dup_heads(w_vs, SWA_KV_HEADS),
                           w_kr, jnp.zeros((d, LANES - MLA_ROPE), F32)], axis=1).astype(BF16)
    qkv_b = _proj([x], [w_b], prologue="ada", norm=norm, seq=seq, tm=PROJ_ROWS, tn=w_b.shape[1], out_dtype=BF16,
                  rope=(cos, sin), rope_pattern=(ROPE_PAIR,) * 10 + (0, 0) + (ROPE_ROTATE,),
                  group_scales=(QK_LOG2_SCALE,) * 8 + (1.0,) * 5, name="proj_swa")
    kr_col = (w_b.shape[1] - LANES) // LANES
    qd = MLA_NOPE + MLA_ROPE
    w_q = w_qb.reshape(MLA_Q_LORA, MLA_HEADS, qd)
    w_q = jnp.concatenate([w_q, jnp.zeros((MLA_Q_LORA, MLA_HEADS, 2 * LANES - qd), F32)], axis=2)
    w_q = w_q.reshape(MLA_Q_LORA, MLA_HEADS * 2 * LANES).astype(BF16)
    q_cat = _proj([lat], [w_q], prologue="rms", norm=(g_q.reshape(1, -1),), lhs_cols=(MLA_Q_LORA, 0),
                  seq=seq, tm=PROJ_ROWS, tn=w_q.shape[1], out_dtype=BF16, rope=(cos, sin),
                  rope_pattern=(0, 1) * MLA_HEADS, out_scale=qd ** -0.5, name="proj_mla_q")
    w_kv = w_kvb.reshape(MLA_KV_LORA, MLA_HEADS, MLA_NOPE + MLA_V)
    w_kv = jnp.concatenate([w_kv[:, :, :MLA_NOPE].reshape(MLA_KV_LORA, -1),
                            w_kv[:, :, MLA_NOPE:].reshape(MLA_KV_LORA, -1)], axis=1).astype(BF16)
    kv = _proj([lat], [w_kv], prologue="rms", norm=(g_kv.reshape(1, -1),), lhs_cols=(MLA_KV_LORA, 2),
               seq=seq, tm=PROJ_ROWS, tn=w_kv.shape[1], out_dtype=BF16, name="proj_mla_kv")
    o_a = _mla_attention(q_cat, kv, qkv_b, kr_col, bsz=bsz, seq=seq, tq=512)
    qkv_b4 = qkv_b.reshape(bsz, 1, seq, qkv_b.shape[1])
    (o_b,) = _banded_attention(qkv_b4, qkv_b4, qkv_b4, bsz=bsz, seq=seq, dil=1, radius=SWA_RADIUS,
                               q_col=0, k_col=4, v_col=5, kv_width=2 * LANES,
                               pairs_per_kv=SWA_Q_HEADS // SWA_KV_HEADS // 2, sink=sink, want_lse=False,
                               name="swa_attention")
    o_b = o_b.reshape(bsz * seq, SWA_Q_HEADS * HEAD_DIM)
    na = MLA_HEADS * MLA_V
    return _proj([o_a, o_b], [w_out[:na].astype(BF16), w_out[na:].astype(BF16)], prologue="plain",
                 seq=seq, tm=PROJ_ROWS, tn=d, out_dtype=F32, residual=(x, g1), name="proj_out_ab")


def _mixer_dilated(x, norm, g1, cos, sin, w_in, w_out, *, bsz, seq):
    d = x.shape[1]
    qw = DIL_HEADS * HEAD_DIM
    w_bf = w_in.astype(BF16)
    outs, lses = [], []
    for g, (window, dil) in enumerate(DIL_PATTERN):
        w_g = w_bf[:, g * 3 * qw:(g + 1) * 3 * qw]
        w_g = jnp.concatenate([_pair_layout(w_g[:, :2 * qw]), w_g[:, 2 * qw:]], axis=1)
        qkv = _proj([x], [w_g], prologue="ada", norm=norm, seq=seq, tm=PROJ_ROWS, tn=3 * qw, sub_tn=qw, out_dtype=BF16,
                    rope=(cos, sin), rope_pattern=(ROPE_PAIR,) * (2 * qw // LANES) + (0,) * (qw // LANES),
                    group_scales=(QK_LOG2_SCALE,) * (qw // LANES) + (1.0,) * (2 * qw // LANES),
                    out_dil=dil, name=f"proj_dil_{dil}")
        qkv = qkv.reshape(bsz, dil, seq // dil, 3 * qw)
        o, lse = _banded_attention(qkv, qkv, qkv, bsz=bsz, seq=seq, dil=dil, radius=window // (2 * dil),
                                   q_col=0, k_col=1, v_col=2, kv_width=qw, pairs_per_kv=1,
                                   want_lse=True, name=f"dilated_attention_{dil}")
        outs.append(o)
        lses.append(lse)
    return _proj(outs + lses, [w_out.astype(BF16)], prologue="dil", seq=seq, tm=512, tn=d, out_dtype=F32,
                 residual=(x, g1), in_dils=tuple(dil for _, dil in DIL_PATTERN), name="proj_out_c")


def kernel(x, c, positions, w_mod, b_mod, g_norm_mix, g_norm_ffn, w_in_ab, mla_g_q, mla_w_qb, mla_g_kv,
           mla_w_kvb, swa_sink, w_out_ab, w_in_c, w_out_c, w_router, b_router, w_gu, b_gu, w_down, b_down,
           g_final):
    bsz, seq, d = x.shape
    depth = w_mod.shape[0]
    cos, sin = _rope_tables(positions)
    mod = _modulation(c, w_mod, b_mod)
    xt = x.reshape(bsz * seq, d)
    for layer in range(depth):
        sh1, sc1, g1, sh2, sc2, g2 = (mod[layer, :, i * d:(i + 1) * d].reshape(bsz, 1, d) for i in range(6))
        li = layer // 2
        norm = (g_norm_mix[layer].reshape(1, d), sc1, sh1)
        if layer % 2 == 0:
            xt = _mixer_mla_swa(xt, norm, g1, cos, sin, w_in_ab[li], mla_g_q[li], mla_w_qb[li], mla_g_kv[li],
                                mla_w_kvb[li], swa_sink[li], w_out_ab[li], bsz=bsz, seq=seq)
        else:
            xt = _mixer_dilated(xt, norm, g1, cos, sin, w_in_c[li], w_out_c[li], bsz=bsz, seq=seq)
        norm = (g_norm_ffn[layer].reshape(1, d), sc2, sh2)
        xt = _moe(xt, norm, g2, w_router[layer], b_router[layer], layer, w_gu, b_gu, w_down, b_down,
                  g_final if layer == depth - 1 else None, seq=seq)
    return xt.reshape(bsz, seq, d)
```

```python
import functools

import jax
import jax.numpy as jnp
import numpy as np
from jax import lax
from jax.experimental import pallas as pl
from jax.experimental.pallas import tpu as pltpu
from jax.experimental.pallas import tpu_sc as plsc

F32 = jnp.float32
BF16 = jnp.bfloat16

EPS = 1e-6
ROPE_THETA = 10000.0
LANES = 128
HEAD_DIM = 64
ROPE_HALF = HEAD_DIM // 2

MLA_HEADS = 8
MLA_Q_LORA = 384
MLA_KV_LORA = 256
MLA_NOPE = 128
MLA_ROPE = 64
MLA_V = 128
SWA_Q_HEADS = 16
SWA_KV_HEADS = 2
SWA_RADIUS = 128
DIL_PATTERN = ((128, 1), (512, 4), (2048, 16))
DIL_HEADS = 16
N_EXPERTS = 32
TOP_K = 4
SWIGLU_LIMIT = 7.0
SWIGLU_ALPHA = 1.702
MOE_ROWS = 512
PROJ_ROWS = 1024
BAND_STEP_ROWS = 1024
BAND_SUB_ROWS = 128
LOG2_E = float(np.log2(np.e))
QK_LOG2_SCALE = HEAD_DIM ** -0.5 * LOG2_E

VMEM_LIMIT = 48 * 1024 * 1024
BIG_VMEM_LIMIT = 56 * 1024 * 1024


def _cparams(sem):
    return pltpu.CompilerParams(dimension_semantics=sem, vmem_limit_bytes=VMEM_LIMIT)


def _mod_kernel(c_ref, w_ref, b_ref, o_ref):
    c = c_ref[...]
    ca = c * (1.0 / (1.0 + jnp.exp(-c)))
    o_ref[...] = jnp.dot(ca, w_ref[...], preferred_element_type=F32,
                         precision=lax.Precision.HIGHEST) + b_ref[...]


def _modulation(c, w_mod, b_mod):
    depth, d, n = w_mod.shape
    bsz = c.shape[0]
    tn = 1536
    return pl.pallas_call(
        _mod_kernel,
        grid=(depth, n // tn),
        in_specs=[pl.BlockSpec((bsz, d), lambda l, j: (0, 0)),
                  pl.BlockSpec((None, d, tn), lambda l, j: (l, 0, j)),
                  pl.BlockSpec((None, 1, tn), lambda l, j: (l, 0, j))],
        out_specs=pl.BlockSpec((None, bsz, tn), lambda l, j: (l, 0, j)),
        out_shape=jax.ShapeDtypeStruct((depth, bsz, n), F32),
        compiler_params=_cparams(("parallel", "parallel")),
        name="modulation",
    )(c, w_mod, b_mod.reshape(depth, 1, n))


def _rope_table_kernel(pos_ref, inv_ref, sign_ref, cos_ref, sin_ref):
    ang = inv_ref[...] * pos_ref[...].astype(F32)
    reps = LANES // ROPE_HALF
    cos_ref[...] = jnp.transpose(jnp.concatenate([jnp.cos(ang)] * reps, axis=0))
    sin_ref[...] = jnp.transpose(jnp.concatenate([jnp.sin(ang)] * reps, axis=0) * sign_ref[...])


def _rope_tables(positions):
    bsz, s = positions.shape
    inv = ROPE_THETA ** (-jnp.arange(0, HEAD_DIM, 2, dtype=F32) / HEAD_DIM)
    sign = jnp.where((jnp.arange(LANES) % HEAD_DIM) < ROPE_HALF, -1.0, 1.0).astype(F32).reshape(LANES, 1)
    out = jax.ShapeDtypeStruct((bsz * s, LANES), F32)
    return pl.pallas_call(
        _rope_table_kernel,
        grid=(bsz,),
        in_specs=[pl.BlockSpec((None, 1, s), lambda b: (b, 0, 0)),
                  pl.BlockSpec((ROPE_HALF, 1), lambda b: (0, 0)),
                  pl.BlockSpec((LANES, 1), lambda b: (0, 0))],
        out_specs=[pl.BlockSpec((s, LANES), lambda b: (b, 0)),
                   pl.BlockSpec((s, LANES), lambda b: (b, 0))],
        out_shape=[out, out],
        compiler_params=_cparams(("parallel",)),
        name="rope_tables",
    )(positions.reshape(bsz, 1, s), inv.reshape(ROPE_HALF, 1), sign)


def _rms(x):
    return x * lax.rsqrt(jnp.mean(x * x, axis=-1, keepdims=True) + EPS)


def _ada_norm(x, g, sc, sh):
    return (_rms(x) * g) * (1.0 + sc) + sh


def _pack_bf16_pairs(x):
    w = x.shape[1] // 2
    hi = pltpu.bitcast(x[:, :w].astype(BF16).astype(F32), jnp.int32)
    lo = pltpu.bitcast(x[:, w:].astype(BF16).astype(F32), jnp.int32)
    return hi | lax.shift_right_logical(lo, 16)


def _unpack_bf16_pairs(p):
    hi = pltpu.bitcast(p & jnp.int32(-65536), F32)
    lo = pltpu.bitcast(lax.shift_left(p, 16), F32)
    return hi, lo


ROPE_ROTATE, ROPE_PAIR = 1, 2


def _pair_layout(w):
    k, n = w.shape
    w = w.reshape(k, n // LANES, 2, 2, ROPE_HALF)
    return jnp.swapaxes(w, 2, 3).reshape(k, n)


def _rope_group(a, cos, sin, first_half):
    rot = jnp.where(first_half, pltpu.roll(a, LANES - ROPE_HALF, 1), pltpu.roll(a, ROPE_HALF, 1))
    return a * cos + rot * sin


def _proj_kernel(*refs, prologue, n_lhs, rope_pattern, group_scales, sub_tn, out_scale, residual, in_dils,
                 out_dil):
    pos = 0
    if prologue == "ada":
        x_ref, g_ref, sc_ref, sh_ref = refs[:4]
        pos = 4
    elif prologue == "rms":
        x_ref, g_ref = refs[:2]
        pos = 2
    elif prologue == "dil":
        o_refs = refs[0:3]
        l_refs = refs[3:6]
        pos = 6
    else:
        x_refs = refs[:n_lhs]
        pos = n_lhs
    w_refs = refs[pos:pos + n_lhs]
    pos += n_lhs
    if rope_pattern is not None:
        cos_ref, sin_ref = refs[pos:pos + 2]
        pos += 2
    if residual:
        res_ref, gate_ref = refs[pos:pos + 2]
        pos += 2
    o_ref = refs[pos]
    pos += 1
    h_ref = refs[pos] if prologue != "plain" else None
    slab_ref = refs[pos + 1] if (prologue == "dil" or out_dil > 1) else None

    if prologue != "plain":
        @pl.when(pl.program_id(1) == 0)
        def _():
            if prologue == "ada":
                h_ref[...] = _ada_norm(x_ref[...], g_ref[...], sc_ref[...], sh_ref[...]).astype(BF16)
            elif prologue == "rms":
                h_ref[...] = (_rms(x_ref[...]) * g_ref[...]).astype(BF16)
            else:
                tm = h_ref.shape[0]
                for c in range(h_ref.shape[1] // LANES):
                    cols = slice(c * LANES, (c + 1) * LANES)
                    vals = []
                    for i, (ref, dil) in enumerate(zip(o_refs + l_refs, in_dils + in_dils)):
                        if dil == 1:
                            vals.append(ref[0, :, cols].astype(F32))
                        else:
                            for r in range(dil):
                                slab_ref[i, pl.ds(r, tm // dil, stride=dil), :] = ref[r, :, cols].astype(F32)
                            vals.append(slab_ref[i])
                    o0, o1, o2, l0, l1, l2 = vals
                    m = jnp.maximum(jnp.maximum(l0, l1), l2)
                    e0, e1, e2 = jnp.exp2(l0 - m), jnp.exp2(l1 - m), jnp.exp2(l2 - m)
                    h_ref[:, cols] = ((e0 * o0 + e1 * o1 + e2 * o2) / (e0 + e1 + e2)).astype(BF16)

    n_total = w_refs[0].shape[1]
    tn = n_total if sub_tn is None else sub_tn

    def matmul(t):
        cols = slice(t * tn, (t + 1) * tn)
        if prologue != "plain":
            acc = jnp.dot(h_ref[...], w_refs[0][:, cols], preferred_element_type=F32)
        else:
            acc = jnp.dot(x_refs[0][...], w_refs[0][:, cols], preferred_element_type=F32)
            for xr, wr in zip(x_refs[1:], w_refs[1:]):
                acc = acc + jnp.dot(xr[...], wr[:, cols], preferred_element_type=F32)
        return acc * out_scale if out_scale != 1.0 else acc

    if residual:
        o_ref[...] = (res_ref[...] + gate_ref[...] * matmul(0)).astype(o_ref.dtype)
        return

    pattern = rope_pattern if rope_pattern is not None else (0,) * (n_total // LANES)
    if any(pattern):
        cos = cos_ref[...]
        sin = sin_ref[...]
        lane = lax.broadcasted_iota(jnp.int32, cos.shape, 1)
        first_half = (lane % HEAD_DIM) < ROPE_HALF
        if ROPE_PAIR in pattern:
            sin_pair = jnp.where(lane < HEAD_DIM, -1.0, 1.0) * jnp.where(first_half, -sin, sin)

    def epilogue(acc, t):
        tm = acc.shape[0]
        for c in range(tn // LANES):
            g = t * (tn // LANES) + c
            cols = slice(g * LANES, (g + 1) * LANES)
            a = acc[:, c * LANES:(c + 1) * LANES]
            if pattern[g] == ROPE_ROTATE:
                a = _rope_group(a, cos, sin, first_half)
            elif pattern[g] == ROPE_PAIR:
                a = a * cos + pltpu.roll(a, HEAD_DIM, 1) * sin_pair
            if group_scales is not None and group_scales[g] != 1.0:
                a = a * group_scales[g]
            if out_dil == 1:
                o_ref[:, cols] = a.astype(o_ref.dtype)
            else:
                slab_ref[0] = a
                for r in range(out_dil):
                    o_ref[r, :, cols] = slab_ref[0, pl.ds(r, tm // out_dil, stride=out_dil), :].astype(o_ref.dtype)

    acc_next = matmul(0)
    for t in range(n_total // tn):
        acc = acc_next
        if t + 1 < n_total // tn:
            acc_next = matmul(t + 1)
        epilogue(acc, t)


def _proj(lhs, ws, *, prologue, seq, tm, tn, out_dtype, norm=None, lhs_cols=None, rope=None,
          rope_pattern=None, group_scales=None, sub_tn=None, out_scale=1.0, residual=None, in_dils=None,
          out_dil=1, name):
    k, n = ws[0].shape
    bsz_seq = lhs[0].shape[0] if prologue != "dil" else lhs[0].shape[0] * seq
    t = bsz_seq
    assert t % tm == 0 and n % tn == 0 and seq % tm == 0
    bpb = seq // tm
    n_lhs = len(ws)
    args, specs = [], []
    row = lambda i, j: (i, 0)
    if prologue in ("ada", "rms"):
        kw, kc = lhs_cols if lhs_cols is not None else (k, 0)
        args.append(lhs[0])
        specs.append(pl.BlockSpec((tm, kw), lambda i, j: (i, kc)))
        args.append(norm[0])
        specs.append(pl.BlockSpec((1, k), lambda i, j: (0, 0)))
        if prologue == "ada":
            for v in norm[1:3]:
                args.append(v)
                specs.append(pl.BlockSpec((None, 1, k), lambda i, j: (i // bpb, 0, 0)))
    elif prologue == "dil":
        for a, dil in zip(lhs, in_dils + in_dils):
            args.append(a)
            specs.append(pl.BlockSpec((None, dil, tm // dil, k), lambda i, j: (i // bpb, 0, i % bpb, 0)))
    else:
        for a, w in zip(lhs, ws):
            args.append(a)
            specs.append(pl.BlockSpec((tm, w.shape[0]), row))
    for w in ws:
        args.append(w)
        specs.append(pl.BlockSpec((w.shape[0], tn), lambda i, j: (0, j)))
    if rope_pattern is not None:
        assert len(rope_pattern) == tn // LANES
        for tab in rope:
            args.append(tab)
            specs.append(pl.BlockSpec((tm, LANES), row))
    if residual is not None:
        res, gate = residual
        args.append(res)
        specs.append(pl.BlockSpec((tm, tn), lambda i, j: (i, j)))
        args.append(gate)
        specs.append(pl.BlockSpec((None, 1, tn), lambda i, j: (i // bpb, 0, j)))
    scratch = [] if prologue == "plain" else [pltpu.VMEM((tm, k), BF16)]
    if prologue == "dil":
        scratch.append(pltpu.VMEM((2 * len(in_dils), tm, LANES), F32))
    elif out_dil > 1:
        scratch.append(pltpu.VMEM((1, tm, LANES), F32))
    if out_dil == 1:
        out_spec = pl.BlockSpec((tm, tn), lambda i, j: (i, j))
        out_shape = jax.ShapeDtypeStruct((t, n), out_dtype)
    else:
        out_spec = pl.BlockSpec((None, out_dil, tm // out_dil, tn), lambda i, j: (i // bpb, 0, i % bpb, j))
        out_shape = jax.ShapeDtypeStruct((t // seq, out_dil, seq // out_dil, n), out_dtype)
    kern = functools.partial(_proj_kernel, prologue=prologue, n_lhs=n_lhs, rope_pattern=rope_pattern,
                             group_scales=group_scales, sub_tn=sub_tn, out_scale=out_scale,
                             residual=residual is not None,
                             in_dils=in_dils, out_dil=out_dil)
    return pl.pallas_call(
        kern,
        grid=(t // tm, n // tn),
        in_specs=specs,
        out_specs=out_spec,
        out_shape=out_shape,
        scratch_shapes=scratch,
        compiler_params=_cparams(("parallel", "arbitrary")),
        name=name,
    )(*args)


MLA_HEADS_PER_STEP = 4


def _mla_scores(q_ref, kn_ref, kr, h):
    k = jnp.concatenate([kn_ref[:, h * LANES:(h + 1) * LANES], kr], axis=1)
    q = q_ref[:, h * 2 * LANES:(h + 1) * 2 * LANES]
    return lax.dot_general(q, k, (((1,), (1,)), ((), ())), preferred_element_type=F32)


def _mla_kernel(q_ref, kn_ref, kr_ref, v_ref, o_ref):
    kr = kr_ref[...]
    s_next = _mla_scores(q_ref, kn_ref, kr, 0)
    for h in range(MLA_HEADS_PER_STEP):
        s = s_next
        if h + 1 < MLA_HEADS_PER_STEP:
            s_next = _mla_scores(q_ref, kn_ref, kr, h + 1)
        m = jnp.max(s, axis=1, keepdims=True)
        p = jnp.exp(s - m).astype(BF16)
        v = v_ref[:, h * LANES:(h + 1) * LANES]
        oe = jnp.dot(p, jnp.concatenate([v, jnp.ones_like(v)], axis=1), preferred_element_type=F32)
        o_ref[:, h * LANES:(h + 1) * LANES] = (oe[:, :LANES] / oe[:, LANES:]).astype(o_ref.dtype)


def _mla_attention(q_cat, kv, qkv_b, kr_col, *, bsz, seq, tq):
    t = bsz * seq
    nq = seq // tq
    hps = MLA_HEADS_PER_STEP
    groups = MLA_HEADS // hps
    return pl.pallas_call(
        _mla_kernel,
        grid=(bsz, groups, nq),
        in_specs=[pl.BlockSpec((tq, hps * 2 * LANES), lambda b, h, i: (b * nq + i, h)),
                  pl.BlockSpec((seq, hps * LANES), lambda b, h, i: (b, h)),
                  pl.BlockSpec((seq, LANES), lambda b, h, i: (b, kr_col)),
                  pl.BlockSpec((seq, hps * LANES), lambda b, h, i: (b, groups + h))],
        out_specs=pl.BlockSpec((tq, hps * LANES), lambda b, h, i: (b * nq + i, h)),
        out_shape=jax.ShapeDtypeStruct((t, MLA_HEADS * MLA_V), BF16),
        compiler_params=pltpu.CompilerParams(dimension_semantics=("parallel", "parallel", "parallel"),
                                             vmem_limit_bytes=BIG_VMEM_LIMIT),
        name="mla_attention",
    )(q_cat, kv, qkv_b, kv)


def _banded_kernel(*refs, n_seqs, has_sink, want_lse, **kw):
    sink_ref = refs[0] if has_sink else None
    refs = refs[1:] if has_sink else refs
    for rr in range(n_seqs):
        ins = [r.at[rr] for r in refs[:7]]
        o_ref = refs[7].at[rr]
        lse_ref = refs[8].at[rr] if want_lse else None
        _banded_sequence(sink_ref, *ins, o_ref, lse_ref, has_sink=has_sink, want_lse=want_lse, **kw)


def _banded_sequence(sink_ref, q_ref, kp_ref, kc_ref, kn_ref, vp_ref, vc_ref, vn_ref, o_ref, lse_ref, *,
                     radius, tq, hb, length, n_pairs, pairs_per_kv, has_sink, want_lse):
    j = pl.program_id(2)
    kw = jnp.concatenate([kp_ref[...], kc_ref[...], kn_ref[...]], axis=0)
    vw = jnp.concatenate([vp_ref[...], vc_ref[...], vn_ref[...]], axis=0)
    qn = min(tq, BAND_SUB_ROWS)
    kn = qn + 2 * hb
    lane = lax.broadcasted_iota(jnp.int32, (qn, LANES), 1)
    low = lane < HEAD_DIM
    first = (lane % HEAD_DIM) < ROPE_HALF

    for q0 in range(0, tq, qn):
        qpos = j * tq + q0 + lax.broadcasted_iota(jnp.int32, (qn, kn), 0)
        kpos = j * tq - hb + q0 + lax.broadcasted_iota(jnp.int32, (qn, kn), 1)
        valid = (jnp.abs(qpos - kpos) <= radius) & (kpos >= 0) & (kpos < length)
        for p in range(n_pairs):
            c = p // pairs_per_kv
            qp = q_ref[q0:q0 + qn, p * LANES:(p + 1) * LANES]
            kp = kw[q0:q0 + kn, c * LANES:(c + 1) * LANES]
            vp = vw[q0:q0 + kn, c * LANES:(c + 1) * LANES]
            vp = jnp.concatenate([vp, jnp.ones_like(vp)], axis=1)
            halves = []
            for half in range(2):
                qh = jnp.where(first if half == 0 else ~first, qp, jnp.zeros_like(qp))
                s = lax.dot_general(qh, kp, (((1,), (1,)), ((), ())), preferred_element_type=F32)
                s = jnp.where(valid, s, -jnp.inf)
                m = jnp.max(s, axis=1, keepdims=True)
                if has_sink:
                    sk = sink_ref[2 * p + half] * LOG2_E
                    m = jnp.maximum(m, sk)
                oe = jnp.dot(jnp.exp2(s - m).astype(BF16), vp, preferred_element_type=F32)
                den = oe[:, LANES:]
                if has_sink:
                    den = den + jnp.exp2(sk - m)
                halves.append((oe[:, :LANES] / den, m + jnp.log2(den) if want_lse else None))
            o_ref[q0:q0 + qn, p * LANES:(p + 1) * LANES] = jnp.where(
                low, halves[0][0], halves[1][0]).astype(o_ref.dtype)
            if want_lse:
                lse_ref[q0:q0 + qn, p * LANES:(p + 1) * LANES] = jnp.where(low, halves[0][1], halves[1][1])


def _banded_attention(q_arr, k_arr, v_arr, *, bsz, seq, dil, radius, q_col, k_col, v_col, kv_width,
                      pairs_per_kv, sink=None, want_lse, name):
    length = seq // dil
    hb = radius
    tq = min(BAND_STEP_ROWS, length)
    assert tq % hb == 0 and length % tq == 0
    qw = DIL_HEADS * HEAD_DIM
    n_pairs = qw // LANES
    rep = tq // hb
    last_hb = length // hb - 1
    n_seqs = min(dil, max(1, BAND_STEP_ROWS // length))
    assert dil % n_seqs == 0

    def prev_spec(col):
        return pl.BlockSpec((None, n_seqs, hb, kv_width),
                            lambda b, r, j: (b, r, jnp.maximum(j * rep - 1, 0), col))

    def cur_spec(col):
        return pl.BlockSpec((None, n_seqs, tq, kv_width), lambda b, r, j: (b, r, j, col))

    def next_spec(col):
        return pl.BlockSpec((None, n_seqs, hb, kv_width),
                            lambda b, r, j: (b, r, jnp.minimum((j + 1) * rep, last_hb), col))

    in_specs = [pl.BlockSpec((None, n_seqs, tq, qw), lambda b, r, j: (b, r, j, q_col)),
                prev_spec(k_col), cur_spec(k_col), next_spec(k_col),
                prev_spec(v_col), cur_spec(v_col), next_spec(v_col)]
    args = [q_arr, k_arr, k_arr, k_arr, v_arr, v_arr, v_arr]
    if sink is not None:
        in_specs = [pl.BlockSpec(memory_space=pltpu.SMEM)] + in_specs
        args = [sink] + args
    out_spec = pl.BlockSpec((None, n_seqs, tq, qw), lambda b, r, j: (b, r, j, 0))
    out_specs = [out_spec]
    out_shape = [jax.ShapeDtypeStruct((bsz, dil, length, qw), BF16)]
    if want_lse:
        out_specs.append(out_spec)
        out_shape.append(jax.ShapeDtypeStruct((bsz, dil, length, qw), F32))
    kern = functools.partial(_banded_kernel, n_seqs=n_seqs, radius=radius, tq=tq, hb=hb, length=length,
                             n_pairs=n_pairs, pairs_per_kv=pairs_per_kv, has_sink=sink is not None,
                             want_lse=want_lse)
    return pl.pallas_call(
        kern,
        grid=(bsz, dil // n_seqs, length // tq),
        in_specs=in_specs,
        out_specs=out_specs,
        out_shape=out_shape,
        compiler_params=_cparams(("parallel", "parallel", "parallel")),
        name=name,
    )(*args)


def _router_kernel(x_ref, g_ref, sc_ref, sh_ref, whi_ref, wlo_ref, b_ref, tri_ref,
                   idx_ref, gate_ref, rank_ref, cnt_ref, h_ref, carry_ref):
    @pl.when(pl.program_id(0) == 0)
    def _():
        carry_ref[...] = jnp.zeros_like(carry_ref)

    h = _ada_norm(x_ref[...], g_ref[...], sc_ref[...], sh_ref[...])
    h_ref[...] = _pack_bf16_pairs(h)
    h_hi = h.astype(BF16)
    h_lo = (h - h_hi.astype(F32)).astype(BF16)

    def nt_dot(a, b):
        return lax.dot_general(a, b, (((1,), (1,)), ((), ())), preferred_element_type=F32)

    w_hi = whi_ref[...]
    lt = nt_dot(w_hi, h_hi) + (nt_dot(w_hi, h_lo) + nt_dot(wlo_ref[...], h_hi)) + b_ref[...]
    n_e, tm = lt.shape
    e_iota = lax.broadcasted_iota(jnp.int32, (n_e, tm), 0)
    vals, sels = [], []
    for k in range(TOP_K):
        m = jnp.max(lt, axis=0, keepdims=True)
        idx = jnp.min(jnp.where(lt == m, e_iota, n_e), axis=0, keepdims=True)
        sel = e_iota == idx
        idx_ref[k:k + 1, :] = idx
        vals.append(m)
        sels.append(sel)
        lt = jnp.where(sel, -jnp.inf, lt)
    exps = [jnp.exp(v - vals[0]) for v in vals]
    den = exps[0] + exps[1] + exps[2] + exps[3]
    for k in range(TOP_K):
        gate_ref[k:k + 1, :] = exps[k] / den

    onehot = jnp.where(sels[0] | sels[1] | sels[2] | sels[3], 1.0, 0.0)
    carry = carry_ref[:, 0:1]
    before = jnp.dot(onehot.astype(BF16), tri_ref[...], preferred_element_type=F32) + carry
    for k in range(TOP_K):
        rank_ref[k:k + 1, :] = jnp.sum(jnp.where(sels[k], before, 0.0), axis=0, keepdims=True).astype(jnp.int32)
    total = carry + jnp.sum(onehot, axis=1, keepdims=True)
    carry_ref[...] = jnp.broadcast_to(total, carry_ref.shape)
    cnt_ref[...] = jnp.broadcast_to(total, cnt_ref.shape)


def _router(x, norm, w_router, b_router, *, seq, tm=PROJ_ROWS):
    t, d = x.shape
    n_e = w_router.shape[1]
    bpb = seq // tm
    tri = (jnp.arange(tm)[:, None] < jnp.arange(tm)[None, :]).astype(BF16)
    w_t = w_router.T
    w_hi = w_t.astype(BF16)
    w_lo = (w_t - w_hi.astype(F32)).astype(BF16)
    g, sc, sh = norm
    out_i = jax.ShapeDtypeStruct((TOP_K, t), jnp.int32)
    tok = pl.BlockSpec((TOP_K, tm), lambda i: (0, i))
    return pl.pallas_call(
        _router_kernel,
        grid=(t // tm,),
        in_specs=[pl.BlockSpec((tm, d), lambda i: (i, 0)),
                  pl.BlockSpec((1, d), lambda i: (0, 0)),
                  pl.BlockSpec((None, 1, d), lambda i: (i // bpb, 0, 0)),
                  pl.BlockSpec((None, 1, d), lambda i: (i // bpb, 0, 0)),
                  pl.BlockSpec((n_e, d), lambda i: (0, 0)),
                  pl.BlockSpec((n_e, d), lambda i: (0, 0)),
                  pl.BlockSpec((n_e, 1), lambda i: (0, 0)),
                  pl.BlockSpec((tm, tm), lambda i: (0, 0))],
        out_specs=[tok, tok, tok, pl.BlockSpec((n_e, LANES), lambda i: (0, 0)),
                   pl.BlockSpec((tm, d // 2), lambda i: (i, 0))],
        out_shape=[out_i, jax.ShapeDtypeStruct((TOP_K, t), F32), out_i,
                   jax.ShapeDtypeStruct((n_e, LANES), F32), jax.ShapeDtypeStruct((t, d // 2), jnp.int32)],
        scratch_shapes=[pltpu.VMEM((n_e, LANES), F32)],
        compiler_params=_cparams(("arbitrary",)),
        name="router",
    )(x, g, sc, sh, w_hi, w_lo, b_router.reshape(n_e, 1), tri)


SC_CORES = 2
SC_SUBCORES = 16
SC_WORKERS = SC_CORES * SC_SUBCORES
SC_ROWS = 64


def _sc_worker_id():
    return lax.axis_index("subcore") * SC_CORES + lax.axis_index("core")


def _sc_mesh():
    return plsc.VectorSubcoreMesh(core_axis_name="core", subcore_axis_name="subcore")


def _sc_dispatch(h, dest_flat, pad_rows):
    t, d = h.shape
    n_pad = pad_rows.shape[0]
    n_rows = TOP_K * t + n_pad
    tok_w = t // SC_WORKERS
    pad_w = n_pad // SC_WORKERS
    nb = tok_w // SC_ROWS
    assert tok_w % (2 * SC_ROWS) == 0 and pad_w % SC_ROWS == 0

    @functools.partial(
        pl.kernel, out_type=jax.ShapeDtypeStruct((n_rows, d), h.dtype), mesh=_sc_mesh(),
        scratch_types=[pltpu.VMEM((SC_ROWS,), jnp.int32), pltpu.VMEM((2, SC_ROWS, d), h.dtype),
                       pltpu.SemaphoreType.DMA((2,))])
    def scatter_rows(h_hbm, dest_hbm, pad_hbm, zero_hbm, xs_hbm, idx_v, rows_v, sem):
        wid = _sc_worker_id()
        base = wid * tok_w

        def load(b, slot):
            return pltpu.make_async_copy(h_hbm.at[pl.ds(base + b * SC_ROWS, SC_ROWS)], rows_v.at[slot],
                                         sem.at[slot])

        load(0, 0).start()

        @pl.loop(0, nb, step=2)
        def _(i):
            for slot in range(2):
                b = i + slot
                load(b, slot).wait()

                @pl.when(b + 1 < nb)
                def _():
                    load(b + 1, 1 - slot).start()

                for k in range(TOP_K):
                    pltpu.sync_copy(dest_hbm.at[pl.ds(k * t + base + b * SC_ROWS, SC_ROWS)], idx_v)
                    pltpu.sync_copy(rows_v.at[slot], xs_hbm.at[idx_v])

        pltpu.sync_copy(zero_hbm, rows_v.at[0])

        @pl.loop(0, pad_w // SC_ROWS)
        def _(b):
            pltpu.sync_copy(pad_hbm.at[pl.ds(wid * pad_w + b * SC_ROWS, SC_ROWS)], idx_v)
            pltpu.sync_copy(rows_v.at[0], xs_hbm.at[idx_v])

    return scatter_rows(h, dest_flat, pad_rows, jnp.zeros((SC_ROWS, d), h.dtype))


def _sc_gather(table, idx):
    n = idx.shape[0]
    d = table.shape[1]
    per_w = n // SC_WORKERS
    nb = per_w // SC_ROWS
    assert per_w % (2 * SC_ROWS) == 0

    @functools.partial(
        pl.kernel, out_type=jax.ShapeDtypeStruct((n, d), table.dtype), mesh=_sc_mesh(),
        scratch_types=[pltpu.VMEM((2, SC_ROWS), jnp.int32), pltpu.VMEM((2, SC_ROWS, d), table.dtype),
                       pltpu.SemaphoreType.DMA((2,))])
    def gather_rows(table_hbm, idx_hbm, out_hbm, idx_v, rows_v, sem):
        base = _sc_worker_id() * per_w

        def gather(slot):
            return pltpu.make_async_copy(table_hbm.at[idx_v.at[slot]], rows_v.at[slot], sem.at[slot])

        def start(b, slot):
            pltpu.sync_copy(idx_hbm.at[pl.ds(base + b * SC_ROWS, SC_ROWS)], idx_v.at[slot])
            gather(slot).start()

        start(0, 0)
        start(1, 1)

        @pl.loop(0, nb, step=2)
        def _(i):
            for slot in range(2):
                b = i + slot
                gather(slot).wait()
                pltpu.sync_copy(rows_v.at[slot], out_hbm.at[pl.ds(base + b * SC_ROWS, SC_ROWS)])

                @pl.when(b + 2 < nb)
                def _():
                    start(b + 2, slot)

    return gather_rows(table, idx)


def _expert_kernel(be_ref, nv_ref, sb_ref, xs_ref, wgu_ref, bgu_ref, wd_ref, bd_ref, y_ref, wgu_bf, wd_bf):
    i = pl.program_id(0)
    used = nv_ref[i] > 0

    @pl.when(used & ((i == 0) | (be_ref[i] != be_ref[jnp.maximum(i - 1, 0)])))
    def _():
        wgu_bf[...] = wgu_ref[...].astype(BF16)
        wd_bf[...] = wd_ref[...].astype(BF16)

    @pl.when(used)
    def _():
        d_ff = wd_ref.shape[0]
        x_hi, x_lo = _unpack_bf16_pairs(xs_ref[...])
        x = jnp.concatenate([x_hi.astype(BF16), x_lo.astype(BF16)], axis=1)
        gu = jnp.dot(x, wgu_bf[...], preferred_element_type=F32) + bgu_ref[...]
        glu = jnp.minimum(gu[:, :d_ff], SWIGLU_LIMIT)
        lin = jnp.clip(gu[:, d_ff:], -SWIGLU_LIMIT, SWIGLU_LIMIT)
        act = glu * (1.0 / (1.0 + jnp.exp(-SWIGLU_ALPHA * glu))) * (lin + 1.0)
        y = jnp.dot(act.astype(BF16), wd_bf[...], preferred_element_type=F32) + bd_ref[...]
        y_ref[...] = _pack_bf16_pairs(y)

    @pl.when(jnp.logical_not(used))
    def _():
        y_ref[...] = jnp.zeros_like(y_ref)


def _experts(xs, blk_expert, n_valid, src_blk, layer, w_gu, b_gu, w_down, b_down):
    n_rows, dw = xs.shape
    _, n_e, d, f2 = w_gu.shape
    n_blocks = n_rows // MOE_ROWS

    def wmap(i, be, nv, sb):
        return (layer, be[i], 0, 0)

    grid_spec = pltpu.PrefetchScalarGridSpec(
        num_scalar_prefetch=3,
        grid=(n_blocks,),
        in_specs=[pl.BlockSpec((MOE_ROWS, dw), lambda i, be, nv, sb: (sb[i], 0)),
                  pl.BlockSpec((None, None, d, f2), wmap),
                  pl.BlockSpec((None, None, 1, f2), wmap),
                  pl.BlockSpec((None, None, f2 // 2, d), wmap),
                  pl.BlockSpec((None, None, 1, d), wmap)],
        out_specs=pl.BlockSpec((MOE_ROWS, dw), lambda i, be, nv, sb: (i, 0)),
        scratch_shapes=[pltpu.VMEM((d, f2), BF16), pltpu.VMEM((f2 // 2, d), BF16)],
    )
    depth = w_gu.shape[0]
    return pl.pallas_call(
        _expert_kernel,
        grid_spec=grid_spec,
        out_shape=jax.ShapeDtypeStruct((n_rows, dw), jnp.int32),
        compiler_params=pltpu.CompilerParams(dimension_semantics=("arbitrary",),
                                             vmem_limit_bytes=BIG_VMEM_LIMIT),
        name="moe_experts",
    )(blk_expert, n_valid, src_blk, xs, w_gu, b_gu.reshape(depth, n_e, 1, f2), w_down, b_down.reshape(depth, n_e, 1, d))


def _combine_kernel(y0_ref, y1_ref, y2_ref, y3_ref, gate_ref, x_ref, g2_ref, *rest):
    o_ref = rest[-1]
    gate = gate_ref[...]
    w = y0_ref.shape[1]
    y_hi, y_lo = None, None
    for k, y_ref in enumerate((y0_ref, y1_ref, y2_ref, y3_ref)):
        hi, lo = _unpack_bf16_pairs(y_ref[...])
        gk = gate[:, k:k + 1]
        y_hi = gk * hi if y_hi is None else y_hi + gk * hi
        y_lo = gk * lo if y_lo is None else y_lo + gk * lo
    out_hi = x_ref[:, :w] + g2_ref[:, :w] * y_hi
    out_lo = x_ref[:, w:] + g2_ref[:, w:] * y_lo
    if len(rest) == 2:
        gain = rest[0][...]
        ms = (jnp.sum(out_hi * out_hi, axis=-1, keepdims=True)
              + jnp.sum(out_lo * out_lo, axis=-1, keepdims=True)) / (2 * w)
        r = lax.rsqrt(ms + EPS)
        out_hi = out_hi * r * gain[:, :w]
        out_lo = out_lo * r * gain[:, w:]
    o_ref[:, :w] = out_hi
    o_ref[:, w:] = out_lo


def _combine(y4, gate, x, g2, final_gain, *, seq, tm=PROJ_ROWS):
    t, d = x.shape
    bpb = seq // tm
    nblk = t // tm
    y_specs = [pl.BlockSpec((tm, d // 2), functools.partial(lambda i, k: (k * nblk + i, 0), k=k))
               for k in range(TOP_K)]
    in_specs = y_specs + [pl.BlockSpec((tm, TOP_K), lambda i: (i, 0)),
                          pl.BlockSpec((tm, d), lambda i: (i, 0)),
                          pl.BlockSpec((None, 1, d), lambda i: (i // bpb, 0, 0))]
    args = [y4, y4, y4, y4, gate, x, g2]
    if final_gain is not None:
        in_specs.append(pl.BlockSpec((1, d), lambda i: (0, 0)))
        args.append(final_gain.reshape(1, d))
    return pl.pallas_call(
        _combine_kernel,
        grid=(nblk,),
        in_specs=in_specs,
        out_specs=pl.BlockSpec((tm, d), lambda i: (i, 0)),
        out_shape=jax.ShapeDtypeStruct((t, d), F32),
        compiler_params=_cparams(("parallel",)),
        name="moe_combine",
    )(*args)


def _moe(x, norm, g2, w_router, b_router, layer, w_gu, b_gu, w_down, b_down, final_gain, *, seq):
    t = x.shape[0]
    n_e = w_router.shape[1]
    idx_t, gate_t, rank_t, counts, h = _router(x, norm, w_router, b_router, seq=seq)
    e_ids = jnp.arange(n_e, dtype=jnp.int32)
    counts = counts[:, 0].astype(jnp.int32)
    padded = (counts + MOE_ROWS - 1) // MOE_ROWS * MOE_ROWS
    pend = jnp.cumsum(padded)
    pstart = pend - padded
    n_pad = n_e * MOE_ROWS
    n_rows = t * TOP_K + n_pad
    n_blocks = n_rows // MOE_ROWS
    dest_t = jnp.sum(jnp.where(idx_t[:, :, None] == e_ids, pstart, 0), axis=-1) + rank_t
    dest_flat = dest_t.reshape(TOP_K * t)
    blk_start = jnp.arange(n_blocks, dtype=jnp.int32) * MOE_ROWS
    n_used = pend[-1] // MOE_ROWS
    src_blk = jnp.minimum(jnp.arange(n_blocks, dtype=jnp.int32), n_used - 1)
    blk_expert = jnp.minimum(jnp.sum((src_blk * MOE_ROWS)[:, None] >= pend[None, :], axis=1), n_e - 1)
    blk_expert = blk_expert.astype(jnp.int32)
    own = blk_expert[:, None] == e_ids
    rows_left = jnp.sum(jnp.where(own, pstart + counts, 0), axis=-1) - blk_start
    n_valid = jnp.where(blk_start < pend[-1], jnp.clip(rows_left, 0, MOE_ROWS), 0).astype(jnp.int32)
    pad_len = padded - counts
    pad_end = jnp.cumsum(pad_len)
    j = jnp.arange(n_pad, dtype=jnp.int32)
    pe = jnp.sum(j[:, None] >= pad_end[None, :], axis=1)
    sel = jnp.minimum(pe, n_e - 1)[:, None] == e_ids
    in_expert = jnp.sum(jnp.where(sel, pstart + counts - (pad_end - pad_len), 0), axis=-1) + j
    pad_rows = jnp.where(pe < n_e, in_expert, pend[-1] + j - pad_end[-1]).astype(jnp.int32)

    xs = _sc_dispatch(h, dest_flat, pad_rows)
    yb = _experts(xs, blk_expert, n_valid, src_blk, layer, w_gu, b_gu, w_down, b_down)
    y4 = _sc_gather(yb, dest_flat)
    return _combine(y4, gate_t.T, x, g2, final_gain, seq=seq)


def _dup_heads(w, n_heads):
    d = w.shape[0]
    return jnp.repeat(w.reshape(d, n_heads, 1, HEAD_DIM), 2, axis=2).reshape(d, n_heads * LANES)


def _mixer_mla_swa(x, norm, g1, cos, sin, w_in, g_q, w_qb, g_kv, w_kvb, sink, w_out, *, bsz, seq):
    d = x.shape[1]
    o = np.cumsum((0, MLA_Q_LORA, MLA_KV_LORA, MLA_ROPE, SWA_Q_HEADS * HEAD_DIM,
                   SWA_KV_HEADS * HEAD_DIM, SWA_KV_HEADS * HEAD_DIM))
    w_cq, w_ckv, w_kr, w_qs, w_ks, w_vs = (w_in[:, o[i]:o[i + 1]] for i in range(6))
    w_lat = jnp.concatenate([w_cq, jnp.zeros((d, LANES), F32), w_ckv], axis=1).astype(BF16)
    lat = _proj([x], [w_lat], prologue="ada", norm=norm, seq=seq, tm=PROJ_ROWS, tn=w_lat.shape[1],
                out_dtype=F32, name="proj_latent")
    w_b = jnp.concatenate([_pair_layout(w_qs), _pair_layout(_dup_heads(w_ks, SWA_KV_HEADS)),
                           _dup_heads(w_vs, SWA_KV_HEADS),
                           w_kr, jnp.zeros((d, LANES - MLA_ROPE), F32)], axis=1).astype(BF16)
    qkv_b = _proj([x], [w_b], prologue="ada", norm=norm, seq=seq, tm=PROJ_ROWS, tn=w_b.shape[1], out_dtype=BF16,
                  rope=(cos, sin), rope_pattern=(ROPE_PAIR,) * 10 + (0, 0) + (ROPE_ROTATE,),
                  group_scales=(QK_LOG2_SCALE,) * 8 + (1.0,) * 5, name="proj_swa")
    kr_col = (w_b.shape[1] - LANES) // LANES
    qd = MLA_NOPE + MLA_ROPE
    w_q = w_qb.reshape(MLA_Q_LORA, MLA_HEADS, qd)
    w_q = jnp.concatenate([w_q, jnp.zeros((MLA_Q_LORA, MLA_HEADS, 2 * LANES - qd), F32)], axis=2)
    w_q = w_q.reshape(MLA_Q_LORA, MLA_HEADS * 2 * LANES).astype(BF16)
    q_cat = _proj([lat], [w_q], prologue="rms", norm=(g_q.reshape(1, -1),), lhs_cols=(MLA_Q_LORA, 0),
                  seq=seq, tm=PROJ_ROWS, tn=w_q.shape[1], out_dtype=BF16, rope=(cos, sin),
                  rope_pattern=(0, 1) * MLA_HEADS, out_scale=qd ** -0.5, name="proj_mla_q")
    w_kv = w_kvb.reshape(MLA_KV_LORA, MLA_HEADS, MLA_NOPE + MLA_V)
    w_kv = jnp.concatenate([w_kv[:, :, :MLA_NOPE].reshape(MLA_KV_LORA, -1),
                            w_kv[:, :, MLA_NOPE:].reshape(MLA_KV_LORA, -1)], axis=1).astype(BF16)
    kv = _proj([lat], [w_kv], prologue="rms", norm=(g_kv.reshape(1, -1),), lhs_cols=(MLA_KV_LORA, 2),
               seq=seq, tm=PROJ_ROWS, tn=w_kv.shape[1], out_dtype=BF16, name="proj_mla_kv")
    o_a = _mla_attention(q_cat, kv, qkv_b, kr_col, bsz=bsz, seq=seq, tq=512)
    qkv_b4 = qkv_b.reshape(bsz, 1, seq, qkv_b.shape[1])
    (o_b,) = _banded_attention(qkv_b4, qkv_b4, qkv_b4, bsz=bsz, seq=seq, dil=1, radius=SWA_RADIUS,
                               q_col=0, k_col=4, v_col=5, kv_width=2 * LANES,
                               pairs_per_kv=SWA_Q_HEADS // SWA_KV_HEADS // 2, sink=sink, want_lse=False,
                               name="swa_attention")
    o_b = o_b.reshape(bsz * seq, SWA_Q_HEADS * HEAD_DIM)
    na = MLA_HEADS * MLA_V
    return _proj([o_a, o_b], [w_out[:na].astype(BF16), w_out[na:].astype(BF16)], prologue="plain",
                 seq=seq, tm=PROJ_ROWS, tn=d, out_dtype=F32, residual=(x, g1), name="proj_out_ab")


def _mixer_dilated(x, norm, g1, cos, sin, w_in, w_out, *, bsz, seq):
    d = x.shape[1]
    qw = DIL_HEADS * HEAD_DIM
    w_bf = w_in.astype(BF16)
    outs, lses = [], []
    for g, (window, dil) in enumerate(DIL_PATTERN):
        w_g = w_bf[:, g * 3 * qw:(g + 1) * 3 * qw]
        w_g = jnp.concatenate([_pair_layout(w_g[:, :2 * qw]), w_g[:, 2 * qw:]], axis=1)
        qkv = _proj([x], [w_g], prologue="ada", norm=norm, seq=seq, tm=PROJ_ROWS, tn=3 * qw, sub_tn=qw, out_dtype=BF16,
                    rope=(cos, sin), rope_pattern=(ROPE_PAIR,) * (2 * qw // LANES) + (0,) * (qw // LANES),
                    group_scales=(QK_LOG2_SCALE,) * (qw // LANES) + (1.0,) * (2 * qw // LANES),
                    out_dil=dil, name=f"proj_dil_{dil}")
        qkv = qkv.reshape(bsz, dil, seq // dil, 3 * qw)
        o, lse = _banded_attention(qkv, qkv, qkv, bsz=bsz, seq=seq, dil=dil, radius=window // (2 * dil),
                                   q_col=0, k_col=1, v_col=2, kv_width=qw, pairs_per_kv=1,
                                   want_lse=True, name=f"dilated_attention_{dil}")
        outs.append(o)
        lses.append(lse)
    return _proj(outs + lses, [w_out.astype(BF16)], prologue="dil", seq=seq, tm=512, tn=d, out_dtype=F32,
                 residual=(x, g1), in_dils=tuple(dil for _, dil in DIL_PATTERN), name="proj_out_c")


def kernel(x, c, positions, w_mod, b_mod, g_norm_mix, g_norm_ffn, w_in_ab, mla_g_q, mla_w_qb, mla_g_kv,
           mla_w_kvb, swa_sink, w_out_ab, w_in_c, w_out_c, w_router, b_router, w_gu, b_gu, w_down, b_down,
           g_final):
    bsz, seq, d = x.shape
    depth = w_mod.shape[0]
    cos, sin = _rope_tables(positions)
    mod = _modulation(c, w_mod, b_mod)
    xt = x.reshape(bsz * seq, d)
    for layer in range(depth):
        sh1, sc1, g1, sh2, sc2, g2 = (mod[layer, :, i * d:(i + 1) * d].reshape(bsz, 1, d) for i in range(6))
        li = layer // 2
        norm = (g_norm_mix[layer].reshape(1, d), sc1, sh1)
        if layer % 2 == 0:
            xt = _mixer_mla_swa(xt, norm, g1, cos, sin, w_in_ab[li], mla_g_q[li], mla_w_qb[li], mla_g_kv[li],
                                mla_w_kvb[li], swa_sink[li], w_out_ab[li], bsz=bsz, seq=seq)
        else:
            xt = _mixer_dilated(xt, norm, g1, cos, sin, w_in_c[li], w_out_c[li], bsz=bsz, seq=seq)
        norm = (g_norm_ffn[layer].reshape(1, d), sc2, sh2)
        xt = _moe(xt, norm, g2, w_router[layer], b_router[layer], layer, w_gu, b_gu, w_down, b_down,
                  g_final if layer == depth - 1 else None, seq=seq)
    return xt.reshape(bsz, seq, d)
```

```python
import functools

import jax
import jax.numpy as jnp
import numpy as np
from jax import lax
from jax.experimental import pallas as pl
from jax.experimental.pallas import tpu as pltpu
from jax.experimental.pallas import tpu_sc as plsc

F32 = jnp.float32
BF16 = jnp.bfloat16

EPS = 1e-6
ROPE_THETA = 10000.0
LANES = 128
HEAD_DIM = 64
ROPE_HALF = HEAD_DIM // 2

MLA_HEADS = 8
MLA_Q_LORA = 384
MLA_KV_LORA = 256
MLA_NOPE = 128
MLA_ROPE = 64
MLA_V = 128
SWA_Q_HEADS = 16
SWA_KV_HEADS = 2
SWA_RADIUS = 128
DIL_PATTERN = ((128, 1), (512, 4), (2048, 16))
DIL_HEADS = 16
N_EXPERTS = 32
TOP_K = 4
SWIGLU_LIMIT = 7.0
SWIGLU_ALPHA = 1.702
MOE_ROWS = 512
PROJ_ROWS = 1024
BAND_STEP_ROWS = 1024
BAND_SUB_ROWS = 128
LOG2_E = float(np.log2(np.e))
QK_LOG2_SCALE = HEAD_DIM ** -0.5 * LOG2_E

VMEM_LIMIT = 48 * 1024 * 1024
BIG_VMEM_LIMIT = 56 * 1024 * 1024


def _cparams(sem):
    return pltpu.CompilerParams(dimension_semantics=sem, vmem_limit_bytes=VMEM_LIMIT)


def _mod_kernel(c_ref, w_ref, b_ref, o_ref):
    c = c_ref[...]
    ca = c * (1.0 / (1.0 + jnp.exp(-c)))
    o_ref[...] = jnp.dot(ca, w_ref[...], preferred_element_type=F32,
                         precision=lax.Precision.HIGHEST) + b_ref[...]


def _modulation(c, w_mod, b_mod):
    depth, d, n = w_mod.shape
    bsz = c.shape[0]
    tn = 1536
    return pl.pallas_call(
        _mod_kernel,
        grid=(depth, n // tn),
        in_specs=[pl.BlockSpec((bsz, d), lambda l, j: (0, 0)),
                  pl.BlockSpec((None, d, tn), lambda l, j: (l, 0, j)),
                  pl.BlockSpec((None, 1, tn), lambda l, j: (l, 0, j))],
        out_specs=pl.BlockSpec((None, bsz, tn), lambda l, j: (l, 0, j)),
        out_shape=jax.ShapeDtypeStruct((depth, bsz, n), F32),
        compiler_params=_cparams(("parallel", "parallel")),
        name="modulation",
    )(c, w_mod, b_mod.reshape(depth, 1, n))


def _rope_table_kernel(pos_ref, inv_ref, sign_ref, cos_ref, sin_ref):
    ang = inv_ref[...] * pos_ref[...].astype(F32)
    reps = LANES // ROPE_HALF
    cos_ref[...] = jnp.transpose(jnp.concatenate([jnp.cos(ang)] * reps, axis=0))
    sin_ref[...] = jnp.transpose(jnp.concatenate([jnp.sin(ang)] * reps, axis=0) * sign_ref[...])


def _rope_tables(positions):
    bsz, s = positions.shape
    inv = ROPE_THETA ** (-jnp.arange(0, HEAD_DIM, 2, dtype=F32) / HEAD_DIM)
    sign = jnp.where((jnp.arange(LANES) % HEAD_DIM) < ROPE_HALF, -1.0, 1.0).astype(F32).reshape(LANES, 1)
    out = jax.ShapeDtypeStruct((bsz * s, LANES), F32)
    return pl.pallas_call(
        _rope_table_kernel,
        grid=(bsz,),
        in_specs=[pl.BlockSpec((None, 1, s), lambda b: (b, 0, 0)),
                  pl.BlockSpec((ROPE_HALF, 1), lambda b: (0, 0)),
                  pl.BlockSpec((LANES, 1), lambda b: (0, 0))],
        out_specs=[pl.BlockSpec((s, LANES), lambda b: (b, 0)),
                   pl.BlockSpec((s, LANES), lambda b: (b, 0))],
        out_shape=[out, out],
        compiler_params=_cparams(("parallel",)),
        name="rope_tables",
    )(positions.reshape(bsz, 1, s), inv.reshape(ROPE_HALF, 1), sign)


def _rms(x):
    return x * lax.rsqrt(jnp.mean(x * x, axis=-1, keepdims=True) + EPS)


def _ada_norm(x, g, sc, sh):
    return (_rms(x) * g) * (1.0 + sc) + sh


def _pack_bf16_pairs(x):
    w = x.shape[1] // 2
    hi = pltpu.bitcast(x[:, :w].astype(BF16).astype(F32), jnp.int32)
    lo = pltpu.bitcast(x[:, w:].astype(BF16).astype(F32), jnp.int32)
    return hi | lax.shift_right_logical(lo, 16)


def _unpack_bf16_pairs(p):
    hi = pltpu.bitcast(p & jnp.int32(-65536), F32)
    lo = pltpu.bitcast(lax.shift_left(p, 16), F32)
    return hi, lo


ROPE_ROTATE, ROPE_PAIR = 1, 2


def _pair_layout(w):
    k, n = w.shape
    w = w.reshape(k, n // LANES, 2, 2, ROPE_HALF)
    return jnp.swapaxes(w, 2, 3).reshape(k, n)


def _rope_group(a, cos, sin, first_half):
    rot = jnp.where(first_half, pltpu.roll(a, LANES - ROPE_HALF, 1), pltpu.roll(a, ROPE_HALF, 1))
    return a * cos + rot * sin


def _proj_kernel(*refs, prologue, n_lhs, rope_pattern, group_scales, sub_tn, out_scale, residual, in_dils,
                 out_dil, emit_lhs):
    pos = 0
    if prologue == "ada":
        x_ref, g_ref, sc_ref, sh_ref = refs[:4]
        pos = 4
    elif prologue == "rms":
        x_ref, g_ref = refs[:2]
        pos = 2
    elif prologue == "dil":
        o_refs = refs[0:3]
        l_refs = refs[3:6]
        pos = 6
    else:
        x_refs = refs[:n_lhs]
        pos = n_lhs
    w_refs = refs[pos:pos + n_lhs]
    pos += n_lhs
    if rope_pattern is not None:
        cos_ref, sin_ref = refs[pos:pos + 2]
        pos += 2
    if residual:
        res_ref, gate_ref = refs[pos:pos + 2]
        pos += 2
    o_ref = refs[pos]
    pos += 1
    if emit_lhs:
        lhs_out_ref = refs[pos]
        pos += 1
    h_ref = refs[pos] if prologue != "plain" else None
    pos += 0 if prologue == "plain" else 1
    slab_ref = refs[pos] if (prologue == "dil" or out_dil > 1) else None

    if prologue != "plain":
        @pl.when(pl.program_id(1) == 0)
        def _():
            if prologue == "ada":
                h_ref[...] = _ada_norm(x_ref[...], g_ref[...], sc_ref[...], sh_ref[...]).astype(BF16)
                if emit_lhs:
                    lhs_out_ref[...] = h_ref[...]
            elif prologue == "rms":
                h_ref[...] = (_rms(x_ref[...]) * g_ref[...]).astype(BF16)
            else:
                tm = h_ref.shape[0]
                for c in range(h_ref.shape[1] // LANES):
                    cols = slice(c * LANES, (c + 1) * LANES)
                    vals = []
                    for i, (ref, dil) in enumerate(zip(o_refs + l_refs, in_dils + in_dils)):
                        if dil == 1:
                            vals.append(ref[0, :, cols].astype(F32))
                        else:
                            for r in range(dil):
                                slab_ref[i, pl.ds(r, tm // dil, stride=dil), :] = ref[r, :, cols].astype(F32)
                            vals.append(slab_ref[i])
                    o0, o1, o2, l0, l1, l2 = vals
                    m = jnp.maximum(jnp.maximum(l0, l1), l2)
                    e0, e1, e2 = jnp.exp2(l0 - m), jnp.exp2(l1 - m), jnp.exp2(l2 - m)
                    h_ref[:, cols] = ((e0 * o0 + e1 * o1 + e2 * o2) / (e0 + e1 + e2)).astype(BF16)

    n_total = w_refs[0].shape[1]
    tn = n_total if sub_tn is None else sub_tn

    def matmul(t):
        cols = slice(t * tn, (t + 1) * tn)
        if prologue != "plain":
            acc = jnp.dot(h_ref[...], w_refs[0][:, cols], preferred_element_type=F32)
        else:
            acc = jnp.dot(x_refs[0][...], w_refs[0][:, cols], preferred_element_type=F32)
            for xr, wr in zip(x_refs[1:], w_refs[1:]):
                acc = acc + jnp.dot(xr[...], wr[:, cols], preferred_element_type=F32)
        return acc * out_scale if out_scale != 1.0 else acc

    if residual:
        o_ref[...] = (res_ref[...] + gate_ref[...] * matmul(0)).astype(o_ref.dtype)
        return

    pattern = rope_pattern if rope_pattern is not None else (0,) * (n_total // LANES)
    if any(pattern):
        cos = cos_ref[...]
        sin = sin_ref[...]
        lane = lax.broadcasted_iota(jnp.int32, cos.shape, 1)
        first_half = (lane % HEAD_DIM) < ROPE_HALF
        if ROPE_PAIR in pattern:
            sin_pair = jnp.where(lane < HEAD_DIM, -1.0, 1.0) * jnp.where(first_half, -sin, sin)

    def epilogue(acc, t):
        tm = acc.shape[0]
        for c in range(tn // LANES):
            g = t * (tn // LANES) + c
            cols = slice(g * LANES, (g + 1) * LANES)
            a = acc[:, c * LANES:(c + 1) * LANES]
            if pattern[g] == ROPE_ROTATE:
                a = _rope_group(a, cos, sin, first_half)
            elif pattern[g] == ROPE_PAIR:
                a = a * cos + pltpu.roll(a, HEAD_DIM, 1) * sin_pair
            if group_scales is not None and group_scales[g] != 1.0:
                a = a * group_scales[g]
            if out_dil == 1:
                o_ref[:, cols] = a.astype(o_ref.dtype)
            else:
                slab_ref[0] = a
                for r in range(out_dil):
                    o_ref[r, :, cols] = slab_ref[0, pl.ds(r, tm // out_dil, stride=out_dil), :].astype(o_ref.dtype)

    acc_next = matmul(0)
    for t in range(n_total // tn):
        acc = acc_next
        if t + 1 < n_total // tn:
            acc_next = matmul(t + 1)
        epilogue(acc, t)


def _proj(lhs, ws, *, prologue, seq, tm, tn, out_dtype, norm=None, lhs_cols=None, rope=None,
          rope_pattern=None, group_scales=None, sub_tn=None, out_scale=1.0, residual=None, in_dils=None,
          out_dil=1, emit_lhs=False, name):
    k, n = ws[0].shape
    bsz_seq = lhs[0].shape[0] if prologue != "dil" else lhs[0].shape[0] * seq
    t = bsz_seq
    assert t % tm == 0 and n % tn == 0 and seq % tm == 0
    bpb = seq // tm
    n_lhs = len(ws)
    args, specs = [], []
    row = lambda i, j: (i, 0)
    if prologue in ("ada", "rms"):
        kw, kc = lhs_cols if lhs_cols is not None else (k, 0)
        args.append(lhs[0])
        specs.append(pl.BlockSpec((tm, kw), lambda i, j: (i, kc)))
        args.append(norm[0])
        specs.append(pl.BlockSpec((1, k), lambda i, j: (0, 0)))
        if prologue == "ada":
            for v in norm[1:3]:
                args.append(v)
                specs.append(pl.BlockSpec((None, 1, k), lambda i, j: (i // bpb, 0, 0)))
    elif prologue == "dil":
        for a, dil in zip(lhs, in_dils + in_dils):
            args.append(a)
            specs.append(pl.BlockSpec((None, dil, tm // dil, k), lambda i, j: (i // bpb, 0, i % bpb, 0)))
    else:
        for a, w in zip(lhs, ws):
            args.append(a)
            specs.append(pl.BlockSpec((tm, w.shape[0]), row))
    for w in ws:
        args.append(w)
        specs.append(pl.BlockSpec((w.shape[0], tn), lambda i, j: (0, j)))
    if rope_pattern is not None:
        assert len(rope_pattern) == tn // LANES
        for tab in rope:
            args.append(tab)
            specs.append(pl.BlockSpec((tm, LANES), row))
    if residual is not None:
        res, gate = residual
        args.append(res)
        specs.append(pl.BlockSpec((tm, tn), lambda i, j: (i, j)))
        args.append(gate)
        specs.append(pl.BlockSpec((None, 1, tn), lambda i, j: (i // bpb, 0, j)))
    scratch = [] if prologue == "plain" else [pltpu.VMEM((tm, k), BF16)]
    if prologue == "dil":
        scratch.append(pltpu.VMEM((2 * len(in_dils), tm, LANES), F32))
    elif out_dil > 1:
        scratch.append(pltpu.VMEM((1, tm, LANES), F32))
    if out_dil == 1:
        out_spec = pl.BlockSpec((tm, tn), lambda i, j: (i, j))
        out_shape = jax.ShapeDtypeStruct((t, n), out_dtype)
    else:
        out_spec = pl.BlockSpec((None, out_dil, tm // out_dil, tn), lambda i, j: (i // bpb, 0, i % bpb, j))
        out_shape = jax.ShapeDtypeStruct((t // seq, out_dil, seq // out_dil, n), out_dtype)
    kern = functools.partial(_proj_kernel, prologue=prologue, n_lhs=n_lhs, rope_pattern=rope_pattern,
                             group_scales=group_scales, sub_tn=sub_tn, out_scale=out_scale,
                             residual=residual is not None,
                             in_dils=in_dils, out_dil=out_dil, emit_lhs=emit_lhs)
    if emit_lhs:
        out_spec = [out_spec, pl.BlockSpec((tm, k), lambda i, j: (i, 0))]
        out_shape = [out_shape, jax.ShapeDtypeStruct((t, k), BF16)]
    return pl.pallas_call(
        kern,
        grid=(t // tm, n // tn),
        in_specs=specs,
        out_specs=out_spec,
        out_shape=out_shape,
        scratch_shapes=scratch,
        compiler_params=_cparams(("parallel", "arbitrary")),
        name=name,
    )(*args)


MLA_HEADS_PER_STEP = 4


def _mla_scores(q_ref, kn_ref, kr, h):
    k = jnp.concatenate([kn_ref[:, h * LANES:(h + 1) * LANES], kr], axis=1)
    q = q_ref[:, h * 2 * LANES:(h + 1) * 2 * LANES]
    return lax.dot_general(q, k, (((1,), (1,)), ((), ())), preferred_element_type=F32)


def _mla_kernel(q_ref, kn_ref, kr_ref, v_ref, o_ref):
    kr = kr_ref[...]
    s_next = _mla_scores(q_ref, kn_ref, kr, 0)
    for h in range(MLA_HEADS_PER_STEP):
        s = s_next
        if h + 1 < MLA_HEADS_PER_STEP:
            s_next = _mla_scores(q_ref, kn_ref, kr, h + 1)
        m = jnp.max(s, axis=1, keepdims=True)
        p = jnp.exp(s - m).astype(BF16)
        v = v_ref[:, h * LANES:(h + 1) * LANES]
        oe = jnp.dot(p, jnp.concatenate([v, jnp.ones_like(v)], axis=1), preferred_element_type=F32)
        o_ref[:, h * LANES:(h + 1) * LANES] = (oe[:, :LANES] / oe[:, LANES:]).astype(o_ref.dtype)


def _mla_attention(q_cat, kv, qkv_b, kr_col, *, bsz, seq, tq):
    t = bsz * seq
    nq = seq // tq
    hps = MLA_HEADS_PER_STEP
    groups = MLA_HEADS // hps
    return pl.pallas_call(
        _mla_kernel,
        grid=(bsz, groups, nq),
        in_specs=[pl.BlockSpec((tq, hps * 2 * LANES), lambda b, h, i: (b * nq + i, h)),
                  pl.BlockSpec((seq, hps * LANES), lambda b, h, i: (b, h)),
                  pl.BlockSpec((seq, LANES), lambda b, h, i: (b, kr_col)),
                  pl.BlockSpec((seq, hps * LANES), lambda b, h, i: (b, groups + h))],
        out_specs=pl.BlockSpec((tq, hps * LANES), lambda b, h, i: (b * nq + i, h)),
        out_shape=jax.ShapeDtypeStruct((t, MLA_HEADS * MLA_V), BF16),
        compiler_params=pltpu.CompilerParams(dimension_semantics=("parallel", "parallel", "parallel"),
                                             vmem_limit_bytes=BIG_VMEM_LIMIT),
        name="mla_attention",
    )(q_cat, kv, qkv_b, kv)


def _banded_kernel(*refs, n_seqs, has_sink, want_lse, **kw):
    sink_ref = refs[0] if has_sink else None
    refs = refs[1:] if has_sink else refs
    for rr in range(n_seqs):
        ins = [r.at[rr] for r in refs[:7]]
        o_ref = refs[7].at[rr]
        lse_ref = refs[8].at[rr] if want_lse else None
        _banded_sequence(sink_ref, *ins, o_ref, lse_ref, has_sink=has_sink, want_lse=want_lse, **kw)


def _banded_sequence(sink_ref, q_ref, kp_ref, kc_ref, kn_ref, vp_ref, vc_ref, vn_ref, o_ref, lse_ref, *,
                     radius, tq, hb, length, n_pairs, pairs_per_kv, has_sink, want_lse):
    j = pl.program_id(2)
    kw = jnp.concatenate([kp_ref[...], kc_ref[...], kn_ref[...]], axis=0)
    vw = jnp.concatenate([vp_ref[...], vc_ref[...], vn_ref[...]], axis=0)
    qn = min(tq, BAND_SUB_ROWS)
    kn = qn + 2 * hb
    lane = lax.broadcasted_iota(jnp.int32, (qn, LANES), 1)
    low = lane < HEAD_DIM
    first = (lane % HEAD_DIM) < ROPE_HALF

    for q0 in range(0, tq, qn):
        qpos = j * tq + q0 + lax.broadcasted_iota(jnp.int32, (qn, kn), 0)
        kpos = j * tq - hb + q0 + lax.broadcasted_iota(jnp.int32, (qn, kn), 1)
        valid = (jnp.abs(qpos - kpos) <= radius) & (kpos >= 0) & (kpos < length)
        for p in range(n_pairs):
            c = p // pairs_per_kv
            qp = q_ref[q0:q0 + qn, p * LANES:(p + 1) * LANES]
            kp = kw[q0:q0 + kn, c * LANES:(c + 1) * LANES]
            vp = vw[q0:q0 + kn, c * LANES:(c + 1) * LANES]
            vp = jnp.concatenate([vp, jnp.ones_like(vp)], axis=1)
            halves = []
            for half in range(2):
                qh = jnp.where(first if half == 0 else ~first, qp, jnp.zeros_like(qp))
                s = lax.dot_general(qh, kp, (((1,), (1,)), ((), ())), preferred_element_type=F32)
                s = jnp.where(valid, s, -jnp.inf)
                m = jnp.max(s, axis=1, keepdims=True)
                if has_sink:
                    sk = sink_ref[2 * p + half] * LOG2_E
                    m = jnp.maximum(m, sk)
                oe = jnp.dot(jnp.exp2(s - m).astype(BF16), vp, preferred_element_type=F32)
                den = oe[:, LANES:]
                if has_sink:
                    den = den + jnp.exp2(sk - m)
                halves.append((oe[:, :LANES] / den, m + jnp.log2(den) if want_lse else None))
            o_ref[q0:q0 + qn, p * LANES:(p + 1) * LANES] = jnp.where(
                low, halves[0][0], halves[1][0]).astype(o_ref.dtype)
            if want_lse:
                lse_ref[q0:q0 + qn, p * LANES:(p + 1) * LANES] = jnp.where(low, halves[0][1], halves[1][1])


def _banded_attention(q_arr, k_arr, v_arr, *, bsz, seq, dil, radius, q_col, k_col, v_col, kv_width,
                      pairs_per_kv, sink=None, want_lse, name):
    length = seq // dil
    hb = radius
    tq = min(BAND_STEP_ROWS, length)
    assert tq % hb == 0 and length % tq == 0
    qw = DIL_HEADS * HEAD_DIM
    n_pairs = qw // LANES
    rep = tq // hb
    last_hb = length // hb - 1
    n_seqs = min(dil, max(1, BAND_STEP_ROWS // length))
    assert dil % n_seqs == 0

    def prev_spec(col):
        return pl.BlockSpec((None, n_seqs, hb, kv_width),
                            lambda b, r, j: (b, r, jnp.maximum(j * rep - 1, 0), col))

    def cur_spec(col):
        return pl.BlockSpec((None, n_seqs, tq, kv_width), lambda b, r, j: (b, r, j, col))

    def next_spec(col):
        return pl.BlockSpec((None, n_seqs, hb, kv_width),
                            lambda b, r, j: (b, r, jnp.minimum((j + 1) * rep, last_hb), col))

    in_specs = [pl.BlockSpec((None, n_seqs, tq, qw), lambda b, r, j: (b, r, j, q_col)),
                prev_spec(k_col), cur_spec(k_col), next_spec(k_col),
                prev_spec(v_col), cur_spec(v_col), next_spec(v_col)]
    args = [q_arr, k_arr, k_arr, k_arr, v_arr, v_arr, v_arr]
    if sink is not None:
        in_specs = [pl.BlockSpec(memory_space=pltpu.SMEM)] + in_specs
        args = [sink] + args
    out_spec = pl.BlockSpec((None, n_seqs, tq, qw), lambda b, r, j: (b, r, j, 0))
    out_specs = [out_spec]
    out_shape = [jax.ShapeDtypeStruct((bsz, dil, length, qw), BF16)]
    if want_lse:
        out_specs.append(out_spec)
        out_shape.append(jax.ShapeDtypeStruct((bsz, dil, length, qw), F32))
    kern = functools.partial(_banded_kernel, n_seqs=n_seqs, radius=radius, tq=tq, hb=hb, length=length,
                             n_pairs=n_pairs, pairs_per_kv=pairs_per_kv, has_sink=sink is not None,
                             want_lse=want_lse)
    return pl.pallas_call(
        kern,
        grid=(bsz, dil // n_seqs, length // tq),
        in_specs=in_specs,
        out_specs=out_specs,
        out_shape=out_shape,
        compiler_params=_cparams(("parallel", "parallel", "parallel")),
        name=name,
    )(*args)


def _router_kernel(x_ref, g_ref, sc_ref, sh_ref, whi_ref, wlo_ref, b_ref, tri_ref,
                   idx_ref, gate_ref, rank_ref, cnt_ref, h_ref, carry_ref):
    @pl.when(pl.program_id(0) == 0)
    def _():
        carry_ref[...] = jnp.zeros_like(carry_ref)

    h = _ada_norm(x_ref[...], g_ref[...], sc_ref[...], sh_ref[...])
    h_ref[...] = _pack_bf16_pairs(h)
    h_hi = h.astype(BF16)
    h_lo = (h - h_hi.astype(F32)).astype(BF16)

    def nt_dot(a, b):
        return lax.dot_general(a, b, (((1,), (1,)), ((), ())), preferred_element_type=F32)

    w_hi = whi_ref[...]
    lt = nt_dot(w_hi, h_hi) + (nt_dot(w_hi, h_lo) + nt_dot(wlo_ref[...], h_hi)) + b_ref[...]
    n_e, tm = lt.shape
    e_iota = lax.broadcasted_iota(jnp.int32, (n_e, tm), 0)
    vals, sels = [], []
    for k in range(TOP_K):
        m = jnp.max(lt, axis=0, keepdims=True)
        idx = jnp.min(jnp.where(lt == m, e_iota, n_e), axis=0, keepdims=True)
        sel = e_iota == idx
        idx_ref[k:k + 1, :] = idx
        vals.append(m)
        sels.append(sel)
        lt = jnp.where(sel, -jnp.inf, lt)
    exps = [jnp.exp(v - vals[0]) for v in vals]
    den = exps[0] + exps[1] + exps[2] + exps[3]
    for k in range(TOP_K):
        gate_ref[k:k + 1, :] = exps[k] / den

    onehot = jnp.where(sels[0] | sels[1] | sels[2] | sels[3], 1.0, 0.0)
    carry = carry_ref[:, 0:1]
    before = jnp.dot(onehot.astype(BF16), tri_ref[...], preferred_element_type=F32) + carry
    for k in range(TOP_K):
        rank_ref[k:k + 1, :] = jnp.sum(jnp.where(sels[k], before, 0.0), axis=0, keepdims=True).astype(jnp.int32)
    total = carry + jnp.sum(onehot, axis=1, keepdims=True)
    carry_ref[...] = jnp.broadcast_to(total, carry_ref.shape)
    cnt_ref[...] = jnp.broadcast_to(total, cnt_ref.shape)


def _router(x, norm, w_router, b_router, *, seq, tm=PROJ_ROWS):
    t, d = x.shape
    n_e = w_router.shape[1]
    bpb = seq // tm
    tri = (jnp.arange(tm)[:, None] < jnp.arange(tm)[None, :]).astype(BF16)
    w_t = w_router.T
    w_hi = w_t.astype(BF16)
    w_lo = (w_t - w_hi.astype(F32)).astype(BF16)
    g, sc, sh = norm
    out_i = jax.ShapeDtypeStruct((TOP_K, t), jnp.int32)
    tok = pl.BlockSpec((TOP_K, tm), lambda i: (0, i))
    return pl.pallas_call(
        _router_kernel,
        grid=(t // tm,),
        in_specs=[pl.BlockSpec((tm, d), lambda i: (i, 0)),
                  pl.BlockSpec((1, d), lambda i: (0, 0)),
                  pl.BlockSpec((None, 1, d), lambda i: (i // bpb, 0, 0)),
                  pl.BlockSpec((None, 1, d), lambda i: (i // bpb, 0, 0)),
                  pl.BlockSpec((n_e, d), lambda i: (0, 0)),
                  pl.BlockSpec((n_e, d), lambda i: (0, 0)),
                  pl.BlockSpec((n_e, 1), lambda i: (0, 0)),
                  pl.BlockSpec((tm, tm), lambda i: (0, 0))],
        out_specs=[tok, tok, tok, pl.BlockSpec((n_e, LANES), lambda i: (0, 0)),
                   pl.BlockSpec((tm, d // 2), lambda i: (i, 0))],
        out_shape=[out_i, jax.ShapeDtypeStruct((TOP_K, t), F32), out_i,
                   jax.ShapeDtypeStruct((n_e, LANES), F32), jax.ShapeDtypeStruct((t, d // 2), jnp.int32)],
        scratch_shapes=[pltpu.VMEM((n_e, LANES), F32)],
        compiler_params=_cparams(("arbitrary",)),
        name="router",
    )(x, g, sc, sh, w_hi, w_lo, b_router.reshape(n_e, 1), tri)


SC_CORES = 2
SC_SUBCORES = 16
SC_WORKERS = SC_CORES * SC_SUBCORES
SC_ROWS = 64


def _sc_worker_id():
    return lax.axis_index("subcore") * SC_CORES + lax.axis_index("core")


def _sc_mesh():
    return plsc.VectorSubcoreMesh(core_axis_name="core", subcore_axis_name="subcore")


def _sc_dispatch(h, dest_flat, pad_rows):
    t, d = h.shape
    n_pad = pad_rows.shape[0]
    n_rows = TOP_K * t + n_pad
    tok_w = t // SC_WORKERS
    pad_w = n_pad // SC_WORKERS
    nb = tok_w // SC_ROWS
    assert tok_w % (2 * SC_ROWS) == 0 and pad_w % SC_ROWS == 0

    @functools.partial(
        pl.kernel, out_type=jax.ShapeDtypeStruct((n_rows, d), h.dtype), mesh=_sc_mesh(),
        scratch_types=[pltpu.VMEM((SC_ROWS,), jnp.int32), pltpu.VMEM((2, SC_ROWS, d), h.dtype),
                       pltpu.SemaphoreType.DMA((2,))])
    def scatter_rows(h_hbm, dest_hbm, pad_hbm, zero_hbm, xs_hbm, idx_v, rows_v, sem):
        wid = _sc_worker_id()
        base = wid * tok_w

        def load(b, slot):
            return pltpu.make_async_copy(h_hbm.at[pl.ds(base + b * SC_ROWS, SC_ROWS)], rows_v.at[slot],
                                         sem.at[slot])

        load(0, 0).start()

        @pl.loop(0, nb, step=2)
        def _(i):
            for slot in range(2):
                b = i + slot
                load(b, slot).wait()

                @pl.when(b + 1 < nb)
                def _():
                    load(b + 1, 1 - slot).start()

                for k in range(TOP_K):
                    pltpu.sync_copy(dest_hbm.at[pl.ds(k * t + base + b * SC_ROWS, SC_ROWS)], idx_v)
                    pltpu.sync_copy(rows_v.at[slot], xs_hbm.at[idx_v])

        pltpu.sync_copy(zero_hbm, rows_v.at[0])

        @pl.loop(0, pad_w // SC_ROWS)
        def _(b):
            pltpu.sync_copy(pad_hbm.at[pl.ds(wid * pad_w + b * SC_ROWS, SC_ROWS)], idx_v)
            pltpu.sync_copy(rows_v.at[0], xs_hbm.at[idx_v])

    return scatter_rows(h, dest_flat, pad_rows, jnp.zeros((SC_ROWS, d), h.dtype))


def _sc_gather(table, idx):
    n = idx.shape[0]
    d = table.shape[1]
    per_w = n // SC_WORKERS
    nb = per_w // SC_ROWS
    assert per_w % (2 * SC_ROWS) == 0

    @functools.partial(
        pl.kernel, out_type=jax.ShapeDtypeStruct((n, d), table.dtype), mesh=_sc_mesh(),
        scratch_types=[pltpu.VMEM((2, SC_ROWS), jnp.int32), pltpu.VMEM((2, SC_ROWS, d), table.dtype),
                       pltpu.SemaphoreType.DMA((2,))])
    def gather_rows(table_hbm, idx_hbm, out_hbm, idx_v, rows_v, sem):
        base = _sc_worker_id() * per_w

        def gather(slot):
            return pltpu.make_async_copy(table_hbm.at[idx_v.at[slot]], rows_v.at[slot], sem.at[slot])

        def start(b, slot):
            pltpu.sync_copy(idx_hbm.at[pl.ds(base + b * SC_ROWS, SC_ROWS)], idx_v.at[slot])
            gather(slot).start()

        start(0, 0)
        start(1, 1)

        @pl.loop(0, nb, step=2)
        def _(i):
            for slot in range(2):
                b = i + slot
                gather(slot).wait()
                pltpu.sync_copy(rows_v.at[slot], out_hbm.at[pl.ds(base + b * SC_ROWS, SC_ROWS)])

                @pl.when(b + 2 < nb)
                def _():
                    start(b + 2, slot)

    return gather_rows(table, idx)


def _expert_kernel(be_ref, nv_ref, sb_ref, xs_ref, wgu_ref, bgu_ref, wd_ref, bd_ref, y_ref, wgu_bf, wd_bf):
    i = pl.program_id(0)
    used = nv_ref[i] > 0

    @pl.when(used & ((i == 0) | (be_ref[i] != be_ref[jnp.maximum(i - 1, 0)])))
    def _():
        wgu_bf[...] = wgu_ref[...].astype(BF16)
        wd_bf[...] = wd_ref[...].astype(BF16)

    @pl.when(used)
    def _():
        d_ff = wd_ref.shape[0]
        x_hi, x_lo = _unpack_bf16_pairs(xs_ref[...])
        x = jnp.concatenate([x_hi.astype(BF16), x_lo.astype(BF16)], axis=1)
        gu = jnp.dot(x, wgu_bf[...], preferred_element_type=F32) + bgu_ref[...]
        glu = jnp.minimum(gu[:, :d_ff], SWIGLU_LIMIT)
        lin = jnp.clip(gu[:, d_ff:], -SWIGLU_LIMIT, SWIGLU_LIMIT)
        act = glu * (1.0 / (1.0 + jnp.exp(-SWIGLU_ALPHA * glu))) * (lin + 1.0)
        y = jnp.dot(act.astype(BF16), wd_bf[...], preferred_element_type=F32) + bd_ref[...]
        y_ref[...] = _pack_bf16_pairs(y)

    @pl.when(jnp.logical_not(used))
    def _():
        y_ref[...] = jnp.zeros_like(y_ref)


def _experts(xs, blk_expert, n_valid, src_blk, layer, w_gu, b_gu, w_down, b_down):
    n_rows, dw = xs.shape
    _, n_e, d, f2 = w_gu.shape
    n_blocks = n_rows // MOE_ROWS

    def wmap(i, be, nv, sb):
        return (layer, be[i], 0, 0)

    grid_spec = pltpu.PrefetchScalarGridSpec(
        num_scalar_prefetch=3,
        grid=(n_blocks,),
        in_specs=[pl.BlockSpec((MOE_ROWS, dw), lambda i, be, nv, sb: (sb[i], 0)),
                  pl.BlockSpec((None, None, d, f2), wmap),
                  pl.BlockSpec((None, None, 1, f2), wmap),
                  pl.BlockSpec((None, None, f2 // 2, d), wmap),
                  pl.BlockSpec((None, None, 1, d), wmap)],
        out_specs=pl.BlockSpec((MOE_ROWS, dw), lambda i, be, nv, sb: (i, 0)),
        scratch_shapes=[pltpu.VMEM((d, f2), BF16), pltpu.VMEM((f2 // 2, d), BF16)],
    )
    depth = w_gu.shape[0]
    return pl.pallas_call(
        _expert_kernel,
        grid_spec=grid_spec,
        out_shape=jax.ShapeDtypeStruct((n_rows, dw), jnp.int32),
        compiler_params=pltpu.CompilerParams(dimension_semantics=("arbitrary",),
                                             vmem_limit_bytes=BIG_VMEM_LIMIT),
        name="moe_experts",
    )(blk_expert, n_valid, src_blk, xs, w_gu, b_gu.reshape(depth, n_e, 1, f2), w_down, b_down.reshape(depth, n_e, 1, d))


def _combine_kernel(y0_ref, y1_ref, y2_ref, y3_ref, gate_ref, x_ref, g2_ref, *rest):
    o_ref = rest[-1]
    gate = gate_ref[...]
    w = y0_ref.shape[1]
    y_hi, y_lo = None, None
    for k, y_ref in enumerate((y0_ref, y1_ref, y2_ref, y3_ref)):
        hi, lo = _unpack_bf16_pairs(y_ref[...])
        gk = gate[:, k:k + 1]
        y_hi = gk * hi if y_hi is None else y_hi + gk * hi
        y_lo = gk * lo if y_lo is None else y_lo + gk * lo
    out_hi = x_ref[:, :w] + g2_ref[:, :w] * y_hi
    out_lo = x_ref[:, w:] + g2_ref[:, w:] * y_lo
    if len(rest) == 2:
        gain = rest[0][...]
        ms = (jnp.sum(out_hi * out_hi, axis=-1, keepdims=True)
              + jnp.sum(out_lo * out_lo, axis=-1, keepdims=True)) / (2 * w)
        r = lax.rsqrt(ms + EPS)
        out_hi = out_hi * r * gain[:, :w]
        out_lo = out_lo * r * gain[:, w:]
    o_ref[:, :w] = out_hi
    o_ref[:, w:] = out_lo


def _combine(y4, gate, x, g2, final_gain, *, seq, tm=PROJ_ROWS):
    t, d = x.shape
    bpb = seq // tm
    nblk = t // tm
    y_specs = [pl.BlockSpec((tm, d // 2), functools.partial(lambda i, k: (k * nblk + i, 0), k=k))
               for k in range(TOP_K)]
    in_specs = y_specs + [pl.BlockSpec((tm, TOP_K), lambda i: (i, 0)),
                          pl.BlockSpec((tm, d), lambda i: (i, 0)),
                          pl.BlockSpec((None, 1, d), lambda i: (i // bpb, 0, 0))]
    args = [y4, y4, y4, y4, gate, x, g2]
    if final_gain is not None:
        in_specs.append(pl.BlockSpec((1, d), lambda i: (0, 0)))
        args.append(final_gain.reshape(1, d))
    return pl.pallas_call(
        _combine_kernel,
        grid=(nblk,),
        in_specs=in_specs,
        out_specs=pl.BlockSpec((tm, d), lambda i: (i, 0)),
        out_shape=jax.ShapeDtypeStruct((t, d), F32),
        compiler_params=_cparams(("parallel",)),
        name="moe_combine",
    )(*args)


def _moe(x, norm, g2, w_router, b_router, layer, w_gu, b_gu, w_down, b_down, final_gain, *, seq):
    t = x.shape[0]
    n_e = w_router.shape[1]
    idx_t, gate_t, rank_t, counts, h = _router(x, norm, w_router, b_router, seq=seq)
    e_ids = jnp.arange(n_e, dtype=jnp.int32)
    counts = counts[:, 0].astype(jnp.int32)
    padded = (counts + MOE_ROWS - 1) // MOE_ROWS * MOE_ROWS
    pend = jnp.cumsum(padded)
    pstart = pend - padded
    n_pad = n_e * MOE_ROWS
    n_rows = t * TOP_K + n_pad
    n_blocks = n_rows // MOE_ROWS
    dest_t = jnp.sum(jnp.where(idx_t[:, :, None] == e_ids, pstart, 0), axis=-1) + rank_t
    dest_flat = dest_t.reshape(TOP_K * t)
    blk_start = jnp.arange(n_blocks, dtype=jnp.int32) * MOE_ROWS
    n_used = pend[-1] // MOE_ROWS
    src_blk = jnp.minimum(jnp.arange(n_blocks, dtype=jnp.int32), n_used - 1)
    blk_expert = jnp.minimum(jnp.sum((src_blk * MOE_ROWS)[:, None] >= pend[None, :], axis=1), n_e - 1)
    blk_expert = blk_expert.astype(jnp.int32)
    own = blk_expert[:, None] == e_ids
    rows_left = jnp.sum(jnp.where(own, pstart + counts, 0), axis=-1) - blk_start
    n_valid = jnp.where(blk_start < pend[-1], jnp.clip(rows_left, 0, MOE_ROWS), 0).astype(jnp.int32)
    pad_len = padded - counts
    pad_end = jnp.cumsum(pad_len)
    j = jnp.arange(n_pad, dtype=jnp.int32)
    pe = jnp.sum(j[:, None] >= pad_end[None, :], axis=1)
    sel = jnp.minimum(pe, n_e - 1)[:, None] == e_ids
    in_expert = jnp.sum(jnp.where(sel, pstart + counts - (pad_end - pad_len), 0), axis=-1) + j
    pad_rows = jnp.where(pe < n_e, in_expert, pend[-1] + j - pad_end[-1]).astype(jnp.int32)

    xs = _sc_dispatch(h, dest_flat, pad_rows)
    yb = _experts(xs, blk_expert, n_valid, src_blk, layer, w_gu, b_gu, w_down, b_down)
    y4 = _sc_gather(yb, dest_flat)
    return _combine(y4, gate_t.T, x, g2, final_gain, seq=seq)


def _dup_heads(w, n_heads):
    d = w.shape[0]
    return jnp.repeat(w.reshape(d, n_heads, 1, HEAD_DIM), 2, axis=2).reshape(d, n_heads * LANES)


def _mixer_mla_swa(x, norm, g1, cos, sin, w_in, g_q, w_qb, g_kv, w_kvb, sink, w_out, *, bsz, seq):
    d = x.shape[1]
    o = np.cumsum((0, MLA_Q_LORA, MLA_KV_LORA, MLA_ROPE, SWA_Q_HEADS * HEAD_DIM,
                   SWA_KV_HEADS * HEAD_DIM, SWA_KV_HEADS * HEAD_DIM))
    w_cq, w_ckv, w_kr, w_qs, w_ks, w_vs = (w_in[:, o[i]:o[i + 1]] for i in range(6))
    w_lat = jnp.concatenate([w_cq, jnp.zeros((d, LANES), F32), w_ckv], axis=1).astype(BF16)
    lat, h = _proj([x], [w_lat], prologue="ada", norm=norm, seq=seq, tm=PROJ_ROWS, tn=w_lat.shape[1],
                   out_dtype=F32, emit_lhs=True, name="proj_latent")
    w_b = jnp.concatenate([_pair_layout(w_qs), _pair_layout(_dup_heads(w_ks, SWA_KV_HEADS)),
                           _dup_heads(w_vs, SWA_KV_HEADS),
                           w_kr, jnp.zeros((d, LANES - MLA_ROPE), F32)], axis=1).astype(BF16)
    qkv_b = _proj([h], [w_b], prologue="plain", seq=seq, tm=PROJ_ROWS, tn=w_b.shape[1], out_dtype=BF16,
                  rope=(cos, sin), rope_pattern=(ROPE_PAIR,) * 10 + (0, 0) + (ROPE_ROTATE,),
                  group_scales=(QK_LOG2_SCALE,) * 8 + (1.0,) * 5, name="proj_swa")
    kr_col = (w_b.shape[1] - LANES) // LANES
    qd = MLA_NOPE + MLA_ROPE
    w_q = w_qb.reshape(MLA_Q_LORA, MLA_HEADS, qd)
    w_q = jnp.concatenate([w_q, jnp.zeros((MLA_Q_LORA, MLA_HEADS, 2 * LANES - qd), F32)], axis=2)
    w_q = w_q.reshape(MLA_Q_LORA, MLA_HEADS * 2 * LANES).astype(BF16)
    q_cat = _proj([lat], [w_q], prologue="rms", norm=(g_q.reshape(1, -1),), lhs_cols=(MLA_Q_LORA, 0),
                  seq=seq, tm=PROJ_ROWS, tn=w_q.shape[1], out_dtype=BF16, rope=(cos, sin),
                  rope_pattern=(0, 1) * MLA_HEADS, out_scale=qd ** -0.5, name="proj_mla_q")
    w_kv = w_kvb.reshape(MLA_KV_LORA, MLA_HEADS, MLA_NOPE + MLA_V)
    w_kv = jnp.concatenate([w_kv[:, :, :MLA_NOPE].reshape(MLA_KV_LORA, -1),
                            w_kv[:, :, MLA_NOPE:].reshape(MLA_KV_LORA, -1)], axis=1).astype(BF16)
    kv = _proj([lat], [w_kv], prologue="rms", norm=(g_kv.reshape(1, -1),), lhs_cols=(MLA_KV_LORA, 2),
               seq=seq, tm=PROJ_ROWS, tn=w_kv.shape[1], out_dtype=BF16, name="proj_mla_kv")
    o_a = _mla_attention(q_cat, kv, qkv_b, kr_col, bsz=bsz, seq=seq, tq=512)
    qkv_b4 = qkv_b.reshape(bsz, 1, seq, qkv_b.shape[1])
    (o_b,) = _banded_attention(qkv_b4, qkv_b4, qkv_b4, bsz=bsz, seq=seq, dil=1, radius=SWA_RADIUS,
                               q_col=0, k_col=4, v_col=5, kv_width=2 * LANES,
                               pairs_per_kv=SWA_Q_HEADS // SWA_KV_HEADS // 2, sink=sink, want_lse=False,
                               name="swa_attention")
    o_b = o_b.reshape(bsz * seq, SWA_Q_HEADS * HEAD_DIM)
    na = MLA_HEADS * MLA_V
    return _proj([o_a, o_b], [w_out[:na].astype(BF16), w_out[na:].astype(BF16)], prologue="plain",
                 seq=seq, tm=PROJ_ROWS, tn=d, out_dtype=F32, residual=(x, g1), name="proj_out_ab")


def _mixer_dilated(x, norm, g1, cos, sin, w_in, w_out, *, bsz, seq):
    d = x.shape[1]
    qw = DIL_HEADS * HEAD_DIM
    w_bf = w_in.astype(BF16)
    outs, lses = [], []
    for g, (window, dil) in enumerate(DIL_PATTERN):
        w_g = w_bf[:, g * 3 * qw:(g + 1) * 3 * qw]
        w_g = jnp.concatenate([_pair_layout(w_g[:, :2 * qw]), w_g[:, 2 * qw:]], axis=1)
        common = dict(seq=seq, tm=PROJ_ROWS, tn=3 * qw, sub_tn=qw, out_dtype=BF16, rope=(cos, sin),
                      rope_pattern=(ROPE_PAIR,) * (2 * qw // LANES) + (0,) * (qw // LANES),
                      group_scales=(QK_LOG2_SCALE,) * (qw // LANES) + (1.0,) * (2 * qw // LANES),
                      out_dil=dil, name=f"proj_dil_{dil}")
        if g == 0:
            qkv, h = _proj([x], [w_g], prologue="ada", norm=norm, emit_lhs=True, **common)
        else:
            qkv = _proj([h], [w_g], prologue="plain", **common)
        qkv = qkv.reshape(bsz, dil, seq // dil, 3 * qw)
        o, lse = _banded_attention(qkv, qkv, qkv, bsz=bsz, seq=seq, dil=dil, radius=window // (2 * dil),
                                   q_col=0, k_col=1, v_col=2, kv_width=qw, pairs_per_kv=1,
                                   want_lse=True, name=f"dilated_attention_{dil}")
        outs.append(o)
        lses.append(lse)
    return _proj(outs + lses, [w_out.astype(BF16)], prologue="dil", seq=seq, tm=512, tn=d, out_dtype=F32,
                 residual=(x, g1), in_dils=tuple(dil for _, dil in DIL_PATTERN), name="proj_out_c")


def kernel(x, c, positions, w_mod, b_mod, g_norm_mix, g_norm_ffn, w_in_ab, mla_g_q, mla_w_qb, mla_g_kv,
           mla_w_kvb, swa_sink, w_out_ab, w_in_c, w_out_c, w_router, b_router, w_gu, b_gu, w_down, b_down,
           g_final):
    bsz, seq, d = x.shape
    depth = w_mod.shape[0]
    cos, sin = _rope_tables(positions)
    mod = _modulation(c, w_mod, b_mod)
    xt = x.reshape(bsz * seq, d)
    for layer in range(depth):
        sh1, sc1, g1, sh2, sc2, g2 = (mod[layer, :, i * d:(i + 1) * d].reshape(bsz, 1, d) for i in range(6))
        li = layer // 2
        norm = (g_norm_mix[layer].reshape(1, d), sc1, sh1)
        if layer % 2 == 0:
            xt = _mixer_mla_swa(xt, norm, g1, cos, sin, w_in_ab[li], mla_g_q[li], mla_w_qb[li], mla_g_kv[li],
                                mla_w_kvb[li], swa_sink[li], w_out_ab[li], bsz=bsz, seq=seq)
        else:
            xt = _mixer_dilated(xt, norm, g1, cos, sin, w_in_c[li], w_out_c[li], bsz=bsz, seq=seq)
        norm = (g_norm_ffn[layer].reshape(1, d), sc2, sh2)
        xt = _moe(xt, norm, g2, w_router[layer], b_router[layer], layer, w_gu, b_gu, w_down, b_down,
                  g_final if layer == depth - 1 else None, seq=seq)
    return xt.reshape(bsz, seq, d)
```
